```python
import math
import jax
import jax.numpy as jnp
from jax import lax
import numpy as np

D_MODEL = 2048
BATCH = 2
SEQ = 4096
DEPTH = 1
DEC_BATCH = 32
DEC_SEQ = 4
PAST_LEN = 8192
PAGE_SIZE = 128

N_HEADS = 8
N_KV_HEADS = 2
GQA = N_HEADS // N_KV_HEADS
HEAD_DIM = D_MODEL // (2 * N_HEADS)
D_ATTN = N_HEADS * HEAD_DIM
D_REC = D_MODEL - D_ATTN
N_REC_BLOCKS = 8
REC_BLOCK = D_REC // N_REC_BLOCKS
CONV_W = 4
LRU_C = 8.0
L_CMP = 32
L_SEL = 64
N_SEL = 16
WINDOW = 512
Q_BLOCK = 128
N_EXPERTS = 32
TOP_K = 4
D_EXPERT = D_MODEL
SWIGLU_LIMIT = 7.0
SWIGLU_ALPHA = 1.702
MOE_BLOCK = 128
EPS = 1e-6
NEG_INF = -1e30
FORCE_SCORE = 1e9
D_KV3 = 3 * 2 * N_KV_HEADS * HEAD_DIM
D_IN = D_ATTN + D_KV3 + 3 * N_HEADS + 2 * D_REC

kernel_name = 'nsa_rglru_moe_hybrid_step'


def rms_norm(x, w):
    xf = x.astype(jnp.float32)
    y = xf * lax.rsqrt(jnp.mean(xf * xf, axis=-1, keepdims=True) + EPS) * w.astype(jnp.float32)
    return y.astype(x.dtype)


def masked_softmax(s, mask, axis):
    s = jnp.where(mask, s, NEG_INF)
    e = jnp.where(mask, jnp.exp(s - jnp.max(s, axis=axis, keepdims=True)), 0.0)
    return e / jnp.maximum(jnp.sum(e, axis=axis, keepdims=True), 1e-30)


def alibi_slopes():
    return jnp.exp2(-8.0 * jnp.arange(1, N_HEADS + 1, dtype=jnp.float32) / N_HEADS)


def nsa_block(q, qpos, kc, vc, blk_end, ksel, vsel, kw, vw, kwpos, gates, slopes):
    f32 = jnp.float32
    bsz, qb = q.shape[0], q.shape[1]
    q = q.reshape(bsz, qb, N_KV_HEADS, GQA, HEAD_DIM)
    sl = slopes.reshape(N_KV_HEADS, GQA)
    dist_c = qpos[:, None] - blk_end[None, :]
    s_c = jnp.einsum('btgrd,bngd->bgrtn', q, kc) - sl[None, :, :, None, None] * dist_c.astype(f32)
    p_c = masked_softmax(s_c, (dist_c >= 0)[None, None, None], -1)
    o_c = jnp.einsum('bgrtn,bngd->btgrd', p_c, vc.astype(f32))
    nb = kc.shape[1]
    ns = ksel.shape[2]
    ratio = L_SEL // L_CMP
    imp = jnp.sum(p_c, axis=2)
    imp = jnp.pad(imp, ((0, 0), (0, 0), (0, 0), (0, ns * ratio - nb)))
    imp = imp.reshape(bsz, N_KV_HEADS, qb, ns, ratio).sum(-1)
    blk = jnp.arange(ns)[None, :]
    cur = (qpos // L_SEL)[:, None]
    forced = (blk == 0) | (blk == cur) | (blk == cur - 1)
    visible = blk * L_SEL <= qpos[:, None]
    imp = jnp.where(visible, jnp.where(forced, FORCE_SCORE, imp), NEG_INF)
    _, top_i = lax.top_k(imp, min(N_SEL, ns))
    bi = jnp.arange(bsz)[:, None, None, None]
    gi = jnp.arange(N_KV_HEADS)[None, :, None, None]
    kg = ksel[bi, gi, top_i]
    vg = vsel[bi, gi, top_i]
    spos = top_i[..., None] * L_SEL + jnp.arange(L_SEL)
    dist_s = (qpos[None, None, :, None, None] - spos)[:, :, None]
    s_s = jnp.einsum('btgrd,bgtnld->bgrtnl', q, kg) - sl[None, :, :, None, None, None] * dist_s.astype(f32)
    p_s = masked_softmax(s_s, dist_s >= 0, (-2, -1))
    o_s = jnp.einsum('bgrtnl,bgtnld->btgrd', p_s, vg.astype(f32))
    dist_w = qpos[:, None] - kwpos[None, :]
    mask_w = (dist_w >= 0) & (dist_w < WINDOW) & (kwpos[None, :] >= 0)
    s_w = jnp.einsum('btgrd,bsgd->bgrts', q, kw) - sl[None, :, :, None, None] * dist_w.astype(f32)
    p_w = masked_softmax(s_w, mask_w[None, None, None], -1)
    o_w = jnp.einsum('bgrts,bsgd->btgrd', p_w, vw.astype(f32))
    g = gates.reshape(bsz, qb, N_KV_HEADS, GQA, 3)
    o = g[..., 0:1] * o_c + g[..., 1:2] * o_s + g[..., 2:3] * o_w
    return o.reshape(bsz, qb, D_ATTN)


def nsa_attention(q, q_start, kv_cmp, kv_sel, k_win, v_win, gates, w_cmp_k, w_cmp_v, slopes):
    bsz, t_len = q.shape[0], q.shape[1]
    s_len = kv_cmp.shape[1]
    nb = s_len // L_CMP
    cb = kv_cmp[:, :nb * L_CMP].reshape(bsz, nb, L_CMP, 2, N_KV_HEADS, HEAD_DIM)
    kc = jnp.einsum('bnlgd,gl->bngd', cb[:, :, :, 0], w_cmp_k)
    vc = jnp.einsum('bnlgd,gl->bngd', cb[:, :, :, 1], w_cmp_v)
    blk_end = jnp.arange(nb) * L_CMP + (L_CMP - 1)
    ns = -(-s_len // L_SEL)
    sb = jnp.pad(kv_sel, ((0, 0), (0, ns * L_SEL - s_len), (0, 0), (0, 0), (0, 0)))
    sb = sb.reshape(bsz, ns, L_SEL, 2, N_KV_HEADS, HEAD_DIM)
    ksel = sb[:, :, :, 0].transpose(0, 3, 1, 2, 4)
    vsel = sb[:, :, :, 1].transpose(0, 3, 1, 2, 4)
    qb = min(Q_BLOCK, t_len)
    nq = t_len // qb

    def one_block(i):
        qs = i * qb
        qpos = q_start + qs + jnp.arange(qb)
        kwpos = q_start - WINDOW + qs + jnp.arange(WINDOW + qb)
        return nsa_block(lax.dynamic_slice_in_dim(q, qs, qb, 1), qpos, kc, vc, blk_end, ksel, vsel,
                         lax.dynamic_slice_in_dim(k_win, qs, WINDOW + qb, 1),
                         lax.dynamic_slice_in_dim(v_win, qs, WINDOW + qb, 1), kwpos,
                         lax.dynamic_slice_in_dim(gates, qs, qb, 1), slopes)

    o = lax.map(one_block, jnp.arange(nq))
    return o.transpose(1, 0, 2, 3).reshape(bsz, t_len, D_ATTN)


def lin_combine(e1, e2):
    a1, b1 = e1
    a2, b2 = e2
    return a1 * a2, a2 * b1 + b2


def rg_lru(xc, h0, w_gate_a, b_gate_a, w_gate_i, b_gate_i, lru_lambda):
    f32 = jnp.float32
    bsz, t_len = xc.shape[0], xc.shape[1]
    xf = xc.astype(f32)
    xb = xf.reshape(bsz, t_len, N_REC_BLOCKS, REC_BLOCK)
    r = jax.nn.sigmoid(jnp.einsum('btnc,ncd->btnd', xb, w_gate_a).reshape(bsz, t_len, D_REC) + b_gate_a)
    i = jax.nn.sigmoid(jnp.einsum('btnc,ncd->btnd', xb, w_gate_i).reshape(bsz, t_len, D_REC) + b_gate_i)
    log_a = -LRU_C * r * jax.nn.softplus(-lru_lambda.astype(f32))
    a = jnp.exp(log_a)
    b = jnp.sqrt(-jnp.expm1(2.0 * log_a)) * (i * xf)
    b = b.at[:, 0].add(a[:, 0] * h0.astype(f32))
    _, hs = lax.associative_scan(lin_combine, (a, b), axis=1)
    return hs, hs[:, -1]


def token_mixer(h, past_cmp, past_sel, past_win, conv_buf, h0, win_len, slopes, mix_w):
    (w_in, q_norm_w, k_norm_w, w_cmp_k, w_cmp_v, conv_w, conv_b, w_gate_a, b_gate_a,
     w_gate_i, b_gate_i, lru_lambda, out_norm_attn, out_norm_rec, w_out) = mix_w
    bsz, t_len = h.shape[0], h.shape[1]
    q_start = past_cmp.shape[1]
    proj = jnp.einsum('btd,de->bte', h, w_in)
    o1 = D_ATTN
    o2 = o1 + D_KV3
    o3 = o2 + 3 * N_HEADS
    o4 = o3 + D_REC
    q = rms_norm(proj[..., :o1].reshape(bsz, t_len, N_HEADS, HEAD_DIM), q_norm_w).astype(jnp.float32) * (HEAD_DIM ** -0.5)
    kv = proj[..., o1:o2].reshape(bsz, t_len, 3, 2, N_KV_HEADS, HEAD_DIM)
    k = rms_norm(kv[:, :, :, 0], k_norm_w[:, None, :])
    new_kv = jnp.stack([k, kv[:, :, :, 1]], axis=3)
    new_cmp, new_sel, new_win = new_kv[:, :, 0], new_kv[:, :, 1], new_kv[:, :, 2]
    gates = jax.nn.sigmoid(proj[..., o2:o3].astype(jnp.float32)).reshape(bsz, t_len, N_HEADS, 3)
    kv_cmp = jnp.concatenate([past_cmp, new_cmp], axis=1)
    kv_sel = jnp.concatenate([past_sel, new_sel], axis=1)
    win_all = jnp.concatenate([past_win, new_win], axis=1)
    win_full = jnp.pad(win_all, ((0, 0), (WINDOW - past_win.shape[1], 0), (0, 0), (0, 0), (0, 0)))
    attn = nsa_attention(q, q_start, kv_cmp, kv_sel, win_full[:, :, 0], win_full[:, :, 1], gates,
                         w_cmp_k, w_cmp_v, slopes)
    new_win_state = win_all[:, win_all.shape[1] - win_len:]
    xr = proj[..., o3:o4]
    yr = proj[..., o4:]
    xcat = jnp.concatenate([conv_buf.astype(xr.dtype), xr], axis=1)
    conv = conv_b
    for j in range(CONV_W):
        conv = conv + xcat[:, j:j + t_len] * conv_w[j]
    hs, h_last = rg_lru(conv, h0, w_gate_a, b_gate_a, w_gate_i, b_gate_i, lru_lambda)
    rec = hs * jax.nn.gelu(yr.astype(jnp.float32))
    merged = jnp.concatenate([rms_norm(attn, out_norm_attn), rms_norm(rec, out_norm_rec)], axis=-1)
    out = jnp.einsum('bte,ed->btd', merged, w_out).astype(h.dtype)
    return out, (new_cmp, new_sel, new_win_state, xcat[:, t_len:], h_last.astype(h.dtype))


def moe_ffn(h2, w_router, b_router, w_gate_up, b_gate_up, w_down, b_down):
    f32 = jnp.float32
    n_tok = h2.shape[0]
    logits = jnp.einsum('nd,de->ne', h2, w_router).astype(f32) + b_router.astype(f32)
    top_v, top_i = lax.top_k(logits, TOP_K)
    gates = jax.nn.softmax(top_v, axis=-1)
    n_asg = n_tok * TOP_K
    flat_e = top_i.reshape(n_asg)
    flat_g = gates.reshape(n_asg)
    flat_tok = jnp.arange(n_asg) // TOP_K
    order = jnp.argsort(flat_e)
    se = flat_e[order]
    counts = jnp.bincount(flat_e, length=N_EXPERTS)
    padded = (counts + MOE_BLOCK - 1) // MOE_BLOCK * MOE_BLOCK
    pad_end = jnp.cumsum(padded)
    pad_start = pad_end - padded
    start = jnp.cumsum(counts) - counts
    dest = pad_start[se] + (jnp.arange(n_asg) - start[se])
    n_blocks = -(-n_asg // MOE_BLOCK) + N_EXPERTS
    n_rows = n_blocks * MOE_BLOCK
    row_tok = jnp.zeros((n_rows,), jnp.int32).at[dest].set(flat_tok[order])
    row_gate = jnp.zeros((n_rows,), f32).at[dest].set(flat_g[order])
    block_e = jnp.minimum(jnp.searchsorted(pad_end, jnp.arange(n_blocks) * MOE_BLOCK, side='right'), N_EXPERTS - 1)
    xr = h2[row_tok].reshape(n_blocks, MOE_BLOCK, h2.shape[1])

    def expert_block(args):
        xb, e = args
        gu = (jnp.einsum('md,df->mf', xb, w_gate_up[e]) + b_gate_up[e]).astype(f32)
        glu = jnp.minimum(gu[..., 0::2], SWIGLU_LIMIT)
        lin = jnp.clip(gu[..., 1::2], -SWIGLU_LIMIT, SWIGLU_LIMIT)
        act = glu * jax.nn.sigmoid(SWIGLU_ALPHA * glu) * (lin + 1.0)
        return (jnp.einsum('mf,fd->md', act, w_down[e]) + b_down[e]).astype(f32)

    yr = lax.map(expert_block, (xr, block_e)).reshape(n_rows, h2.shape[1])
    y = jnp.zeros((n_tok, h2.shape[1]), f32).at[row_tok].add(yr * row_gate[:, None])
    return y.astype(h2.dtype)


def setup_inputs(seed: int = 0) -> dict:
    key = jax.random.key(seed)
    ks = jax.random.split(key, 32)
    f32 = jnp.float32
    n_pages = PAST_LEN // PAGE_SIZE
    n_used = DEC_BATCH * n_pages
    n_pool = n_used + max(1, n_used // 4)
    win_len = min(WINDOW, PAST_LEN)

    def nrm(k, shape, scale):
        return scale * jax.random.normal(k, shape, f32)

    def gain(k, shape):
        return 1.0 + 0.02 * jax.random.normal(k, shape, f32)

    u = jax.random.uniform(ks[20], (DEPTH, D_REC), f32, 0.9, 0.999)
    a = u ** (1.0 / LRU_C)
    lru_lambda = jnp.log(a) - jnp.log1p(-a)
    page_table = jax.random.permutation(ks[7], n_pool)[:n_used].reshape(DEC_BATCH, n_pages).astype(jnp.int32)
    return {
        'x_prompt': nrm(ks[0], (BATCH, SEQ, D_MODEL), 1.0),
        'x_sample': nrm(ks[1], (DEC_BATCH, DEC_SEQ, D_MODEL), 1.0),
        'cache_cmp_kv': nrm(ks[2], (DEPTH, n_pool, PAGE_SIZE, 2, N_KV_HEADS, HEAD_DIM), 1.0),
        'cache_sel_kv': nrm(ks[3], (DEPTH, n_pool, PAGE_SIZE, 2, N_KV_HEADS, HEAD_DIM), 1.0),
        'cache_win_kv': nrm(ks[4], (DEPTH, DEC_BATCH, win_len, 2, N_KV_HEADS, HEAD_DIM), 1.0),
        'state_conv': nrm(ks[5], (DEPTH, DEC_BATCH, CONV_W - 1, D_REC), 1.0),
        'state_h': nrm(ks[6], (DEPTH, DEC_BATCH, D_REC), 0.5),
        'page_table': page_table,
        'norm_mix_w': gain(ks[8], (DEPTH, D_MODEL)),
        'w_in': nrm(ks[9], (DEPTH, D_MODEL, D_IN), D_MODEL ** -0.5),
        'q_norm_w': gain(ks[10], (DEPTH, HEAD_DIM)),
        'k_norm_w': gain(ks[11], (DEPTH, 3, HEAD_DIM)),
        'w_cmp_k': (1.0 + 0.1 * jax.random.normal(ks[12], (DEPTH, N_KV_HEADS, L_CMP), f32)) / L_CMP,
        'w_cmp_v': (1.0 + 0.1 * jax.random.normal(ks[13], (DEPTH, N_KV_HEADS, L_CMP), f32)) / L_CMP,
        'conv_w': nrm(ks[14], (DEPTH, CONV_W, D_REC), CONV_W ** -0.5),
        'conv_b': nrm(ks[15], (DEPTH, D_REC), 0.01),
        'w_gate_a': nrm(ks[16], (DEPTH, N_REC_BLOCKS, REC_BLOCK, REC_BLOCK), REC_BLOCK ** -0.5),
        'b_gate_a': nrm(ks[17], (DEPTH, D_REC), 0.01),
        'w_gate_i': nrm(ks[18], (DEPTH, N_REC_BLOCKS, REC_BLOCK, REC_BLOCK), REC_BLOCK ** -0.5),
        'b_gate_i': nrm(ks[19], (DEPTH, D_REC), 0.01),
        'lru_lambda': lru_lambda,
        'out_norm_attn': gain(ks[21], (DEPTH, D_ATTN)),
        'out_norm_rec': gain(ks[22], (DEPTH, D_REC)),
        'w_out': nrm(ks[23], (DEPTH, D_ATTN + D_REC, D_MODEL), (D_ATTN + D_REC) ** -0.5),
        'norm_ffn_w': gain(ks[24], (DEPTH, D_MODEL)),
        'w_router': nrm(ks[25], (DEPTH, D_MODEL, N_EXPERTS), D_MODEL ** -0.5),
        'b_router': nrm(ks[26], (DEPTH, N_EXPERTS), 0.01),
        'w_gate_up': nrm(ks[27], (DEPTH, N_EXPERTS, D_MODEL, 2 * D_EXPERT), D_MODEL ** -0.5),
        'b_gate_up': nrm(ks[28], (DEPTH, N_EXPERTS, 2 * D_EXPERT), 0.01),
        'w_down': nrm(ks[29], (DEPTH, N_EXPERTS, D_EXPERT, D_MODEL), D_EXPERT ** -0.5),
        'b_down': nrm(ks[30], (DEPTH, N_EXPERTS, D_MODEL), 0.01),
    }


def reference(x_prompt, x_sample, cache_cmp_kv, cache_sel_kv, cache_win_kv, state_conv, state_h, page_table,
              norm_mix_w, w_in, q_norm_w, k_norm_w, w_cmp_k, w_cmp_v, conv_w, conv_b, w_gate_a, b_gate_a,
              w_gate_i, b_gate_i, lru_lambda, out_norm_attn, out_norm_rec, w_out, norm_ffn_w, w_router,
              b_router, w_gate_up, b_gate_up, w_down, b_down):
    slopes = alibi_slopes()
    bp, tp = x_prompt.shape[0], x_prompt.shape[1]
    bs, ts = x_sample.shape[0], x_sample.shape[1]
    n_pages = page_table.shape[1]
    win_len_p = min(WINDOW, tp)
    win_len_s = cache_win_kv.shape[2]
    xp = x_prompt
    xs = x_sample
    p_cmp, p_sel, p_win, p_conv, p_h = [], [], [], [], []
    s_cmp, s_sel, s_win, s_conv, s_h = [], [], [], [], []
    for l in range(DEPTH):
        mix_w = (w_in[l], q_norm_w[l], k_norm_w[l], w_cmp_k[l], w_cmp_v[l], conv_w[l], conv_b[l],
                 w_gate_a[l], b_gate_a[l], w_gate_i[l], b_gate_i[l], lru_lambda[l],
                 out_norm_attn[l], out_norm_rec[l], w_out[l])
        empty = jnp.zeros((bp, 0, 2, N_KV_HEADS, HEAD_DIM), xp.dtype)
        mp, st = token_mixer(rms_norm(xp, norm_mix_w[l]), empty, empty, empty,
                             jnp.zeros((bp, CONV_W - 1, D_REC), xp.dtype), jnp.zeros((bp, D_REC), xp.dtype),
                             win_len_p, slopes, mix_w)
        xp = xp + mp
        xp = xp + moe_ffn(rms_norm(xp, norm_ffn_w[l]).reshape(bp * tp, D_MODEL), w_router[l], b_router[l],
                          w_gate_up[l], b_gate_up[l], w_down[l], b_down[l]).reshape(bp, tp, D_MODEL)
        p_cmp.append(st[0]); p_sel.append(st[1]); p_win.append(st[2]); p_conv.append(st[3]); p_h.append(st[4])
        past_cmp = cache_cmp_kv[l][page_table].reshape(bs, n_pages * PAGE_SIZE, 2, N_KV_HEADS, HEAD_DIM)
        past_sel = cache_sel_kv[l][page_table].reshape(bs, n_pages * PAGE_SIZE, 2, N_KV_HEADS, HEAD_DIM)
        ms, st = token_mixer(rms_norm(xs, norm_mix_w[l]), past_cmp, past_sel, cache_win_kv[l],
                             state_conv[l], state_h[l], win_len_s, slopes, mix_w)
        xs = xs + ms
        xs = xs + moe_ffn(rms_norm(xs, norm_ffn_w[l]).reshape(bs * ts, D_MODEL), w_router[l], b_router[l],
                          w_gate_up[l], b_gate_up[l], w_down[l], b_down[l]).reshape(bs, ts, D_MODEL)
        s_cmp.append(st[0]); s_sel.append(st[1]); s_win.append(st[2]); s_conv.append(st[3]); s_h.append(st[4])
    return (xp, xs,
            jnp.stack(p_cmp), jnp.stack(p_sel), jnp.stack(p_win), jnp.stack(p_conv), jnp.stack(p_h),
            jnp.stack(s_cmp), jnp.stack(s_sel), jnp.stack(s_win), jnp.stack(s_conv), jnp.stack(s_h))
```

```python
import functools

import jax
import jax.numpy as jnp
from jax import lax
from jax.experimental import pallas as pl
from jax.experimental.pallas import tpu as pltpu

F32 = jnp.float32
BF16 = jnp.bfloat16
I32 = jnp.int32

N_HEADS = 8
N_KV = 2
GQA = N_HEADS // N_KV
DH = 128
D_ATTN = N_HEADS * DH
CONV_W = 4
LRU_C = 8.0
L_CMP = 32
L_SEL = 64
N_SEL = 16
WINDOW = 512
QB = 128
N_EXPERTS = 32
TOP_K = 4
SWIGLU_LIMIT = 7.0
SWIGLU_ALPHA = 1.702
EPS = 1e-6
NEG = -1e30
FORCE = 1e9
Q_SCALE = DH ** -0.5
LANES = 128
TP = 8
MOE_TM = 256
MOE_TN = 1024
PAGES_PER_STEP = 8
VMEM_LIMIT = 56 * 1024 * 1024


def _dot(a, b):
    return jnp.dot(a, b, preferred_element_type=F32)


def _dot_nt(a, b):
    return lax.dot_general(a, b, (((1,), (1,)), ((), ())), preferred_element_type=F32)


def _iota(shape, dim):
    return lax.broadcasted_iota(I32, shape, dim)


def _rms(x, w):
    return x * lax.rsqrt(jnp.mean(x * x, axis=-1, keepdims=True) + EPS) * w


def _masked_softmax_rows(s, mask):
    s = jnp.where(mask, s, NEG)
    e = jnp.where(mask, jnp.exp(s - jnp.max(s, axis=-1, keepdims=True)), 0.0)
    return e / jnp.maximum(jnp.sum(e, axis=-1, keepdims=True), 1e-30)


def _params(sem, vmem=None):
    return pltpu.CompilerParams(dimension_semantics=sem, vmem_limit_bytes=vmem or VMEM_LIMIT)


def _inproj_body(x_ref, nw_ref, wq_ref, wkv_ref, wg_ref, wxy_ref, qnw_ref, knw_ref,
                 q_ref, cmp_ref, sel_ref, win_ref, kvb_ref, gate_ref, xr_ref, yr_ref):
    x = x_ref[...]
    h = _rms(x, nw_ref[...]).astype(BF16)
    q = _dot(h, wq_ref[...])
    qnw = qnw_ref[...]
    nqb = q_ref.shape[0]
    for hd in range(N_HEADS):
        qn = (_rms(q[:, hd * DH:(hd + 1) * DH], qnw) * Q_SCALE).astype(BF16)
        for b in range(nqb):
            q_ref[b, hd] = qn[b * QB:(b + 1) * QB]
    kv = _dot(h, wkv_ref[...])
    outs = (cmp_ref, sel_ref, win_ref)
    for br in range(3):
        knw = knw_ref[br:br + 1, :]
        for g in range(N_KV):
            c0 = br * 4 * DH + g * DH
            kn = _rms(kv[:, c0:c0 + DH], knw)
            v = kv[:, c0 + 2 * DH:c0 + 3 * DH]
            outs[br][:, g * DH:(g + 1) * DH] = kn
            outs[br][:, (2 + g) * DH:(3 + g) * DH] = v
            kvb_ref[:, c0:c0 + DH] = kn.astype(BF16)
            kvb_ref[:, c0 + 2 * DH:c0 + 3 * DH] = v.astype(BF16)
    gate_ref[...] = jax.nn.sigmoid(_dot(h, wg_ref[...]))
    xy = _dot(h, wxy_ref[...])
    d_rec = xr_ref.shape[1]
    xr_ref[...] = xy[:, :d_rec]
    yr_ref[...] = xy[:, d_rec:]


def _inproj(x2, nw, wq, wkv, wg, wxy, qnw, knw, tm):
    n, d = x2.shape
    d_rec = wxy.shape[1] // 2
    nqb = tm // QB
    row = lambda i: (i, 0)
    const = lambda i: (0, 0)
    wspec = lambda a: pl.BlockSpec(a.shape, const, pipeline_mode=pl.Buffered(1))
    return pl.pallas_call(
        _inproj_body,
        grid=(n // tm,),
        in_specs=[pl.BlockSpec((tm, d), row), wspec(nw), wspec(wq), wspec(wkv), wspec(wg), wspec(wxy),
                  wspec(qnw), wspec(knw)],
        out_specs=[pl.BlockSpec((nqb, N_HEADS, QB, DH), lambda i: (i, 0, 0, 0)),
                   pl.BlockSpec((tm, 4 * DH), row), pl.BlockSpec((tm, 4 * DH), row), pl.BlockSpec((tm, 4 * DH), row),
                   pl.BlockSpec((tm, 12 * DH), row), pl.BlockSpec((tm, LANES), row),
                   pl.BlockSpec((tm, d_rec), row), pl.BlockSpec((tm, d_rec), row)],
        out_shape=[jax.ShapeDtypeStruct((n // QB, N_HEADS, QB, DH), BF16),
                   jax.ShapeDtypeStruct((n, 4 * DH), F32), jax.ShapeDtypeStruct((n, 4 * DH), F32),
                   jax.ShapeDtypeStruct((n, 4 * DH), F32), jax.ShapeDtypeStruct((n, 12 * DH), BF16),
                   jax.ShapeDtypeStruct((n, LANES), F32),
                   jax.ShapeDtypeStruct((n, d_rec), F32), jax.ShapeDtypeStruct((n, d_rec), F32)],
        compiler_params=_params(("parallel",)),
        name="inproj",
    )(x2, nw, wq, wkv, wg, wxy, qnw, knw)


def _pool_rows(x, wl_ref):
    r = x.shape[0] // (2 * L_CMP)
    x3 = x.reshape(r, 2 * L_CMP, x.shape[1])
    even = jnp.sum(x3 * wl_ref[0][None], axis=1)
    odd = jnp.sum(x3 * wl_ref[1][None], axis=1)
    return even, odd


def _pool_body(x_ref, wl_ref, e_ref, o_ref):
    even, odd = _pool_rows(x_ref[...], wl_ref)
    e_ref[...] = even.astype(BF16)
    o_ref[...] = odd.astype(BF16)


def _pool_prompt(cmp2, wl):
    n, c = cmp2.shape
    rows = 1024
    ob = rows // (2 * L_CMP)
    return pl.pallas_call(
        _pool_body,
        grid=(n // rows,),
        in_specs=[pl.BlockSpec((rows, c), lambda i: (i, 0)), pl.BlockSpec(wl.shape, lambda i: (0, 0, 0))],
        out_specs=[pl.BlockSpec((ob, c), lambda i: (i, 0)), pl.BlockSpec((ob, c), lambda i: (i, 0))],
        out_shape=[jax.ShapeDtypeStruct((n // (2 * L_CMP), c), BF16)] * 2,
        compiler_params=_params(("parallel",)),
        name="pool_prompt",
    )(cmp2, wl)


def _pool_pages_body(pt_ref, *refs):
    pages = refs[:PAGES_PER_STEP]
    wl_ref, e_ref, o_ref = refs[PAGES_PER_STEP:]
    x = jnp.concatenate([p[0] for p in pages], axis=0)
    even, odd = _pool_rows(x, wl_ref)
    e_ref[0] = even.astype(BF16)
    o_ref[0] = odd.astype(BF16)


def _pool_pages(cache3, page_table, wl):
    n_pool, page, c = cache3.shape
    bsz, n_pages = page_table.shape
    n_steps = n_pages // PAGES_PER_STEP
    ob = PAGES_PER_STEP * page // (2 * L_CMP)
    page_spec = lambda i: pl.BlockSpec((1, page, c), lambda b, s, pt: (pt[b * n_pages + s * PAGES_PER_STEP + i], 0, 0))
    grid_spec = pltpu.PrefetchScalarGridSpec(
        num_scalar_prefetch=1,
        grid=(bsz, n_steps),
        in_specs=[page_spec(i) for i in range(PAGES_PER_STEP)] + [pl.BlockSpec(wl.shape, lambda b, s, pt: (0, 0, 0))],
        out_specs=[pl.BlockSpec((1, ob, c), lambda b, s, pt: (b, s, 0))] * 2,
    )
    return pl.pallas_call(
        _pool_pages_body,
        grid_spec=grid_spec,
        out_shape=[jax.ShapeDtypeStruct((bsz, n_steps * ob, c), BF16)] * 2,
        compiler_params=_params(("parallel", "parallel")),
        name="pool_pages",
    )(page_table.reshape(-1), *([cache3] * PAGES_PER_STEP), wl)


def _select_blocks_cols(score, blk, n_pick, n_blk):
    sel = jnp.zeros(score.shape, F32)
    for _ in range(n_pick):
        m = jnp.max(score, axis=0, keepdims=True)
        idx = jnp.min(jnp.where(score == m, blk, n_blk), axis=0, keepdims=True)
        hit = blk == idx
        sel = jnp.where(hit, 1.0, sel)
        score = jnp.where(hit, -jnp.inf, score)
    return sel


def _pattn_body(slopes_ref, q_ref, ksel_ref, vsel_ref, kwin_ref, vwin_ref, kce_ref, kco_ref, vce_ref, vco_ref,
                gt_ref, o_ref, m_scr, l_scr, acc_scr):
    g = pl.program_id(1)
    i = pl.program_id(2)
    rows = GQA * QB
    q = q_ref[0].reshape(rows, DH)
    row = _iota((rows, 1), 0)
    qpos = i * QB + (row & (QB - 1))
    r_of_row = row >> 7
    slope = jnp.zeros((rows, 1), F32)
    for r in range(GQA):
        slope = jnp.where(r_of_row == r, slopes_ref[g * GQA + r], slope)

    kc = jnp.concatenate([kce_ref[...], kco_ref[...]], axis=0)
    vc = jnp.concatenate([vce_ref[...], vco_ref[...]], axis=0)
    nb = kc.shape[0]
    half = nb // 2
    lane = _iota((rows, nb), 1)
    blk_c = jnp.where(lane < half, 2 * lane, 2 * (lane - half) + 1)
    dist = qpos - (blk_c * L_CMP + (L_CMP - 1))
    mask = dist >= 0
    s = _dot_nt(q, kc) - slope * dist.astype(F32)
    p = _masked_softmax_rows(s, mask)
    o_c = _dot(p.astype(BF16), vc)
    imp = p[0:QB]
    for r in range(1, GQA):
        imp = imp + p[r * QB:(r + 1) * QB]
    imp_t = imp.T
    pair = imp_t[:half] + imp_t[half:]
    ns = half
    blk = _iota((ns, QB), 0)
    qp = i * QB + _iota((ns, QB), 1)
    cur = qp >> 6
    forced = (blk == 0) | (blk == cur) | (blk == cur - 1)
    score = jnp.where(blk * L_SEL <= qp, jnp.where(forced, FORCE, pair), NEG)
    sel = _select_blocks_cols(score, blk, min(N_SEL, ns), ns)
    sel = jnp.concatenate([sel, jnp.zeros((LANES - ns, QB), F32)], axis=0) if ns < LANES else sel
    sel_q = sel.T.astype(BF16)
    sel4 = jnp.concatenate([sel_q] * GQA, axis=0)

    m_scr[...] = jnp.full(m_scr.shape, NEG, F32)
    l_scr[...] = jnp.zeros(l_scr.shape, F32)
    acc_scr[...] = jnp.zeros(acc_scr.shape, F32)
    ck = 512

    def chunk(c, carry):
        k0 = pl.multiple_of(c * ck, ck)
        kch = ksel_ref[pl.ds(k0, ck), :]
        vch = vsel_ref[pl.ds(k0, ck), :]
        dist_s = qpos - (k0 + _iota((rows, ck), 1))
        expand = jnp.where((_iota((LANES, ck), 1) >> 6) + c * (ck // L_SEL) == _iota((LANES, ck), 0), 1.0, 0.0)
        picked = _dot(sel4, expand.astype(BF16))
        msk = jnp.where(dist_s >= 0, picked, 0.0) > 0.5
        sc = jnp.where(msk, _dot_nt(q, kch) - slope * dist_s.astype(F32), NEG)
        m_old = m_scr[...]
        m_new = jnp.maximum(m_old, jnp.max(sc, axis=-1, keepdims=True))
        alpha = jnp.exp(m_old - m_new)
        e = jnp.where(msk, jnp.exp(sc - m_new), 0.0)
        l_scr[...] = alpha * l_scr[...] + jnp.sum(e, axis=-1, keepdims=True)
        acc_scr[...] = alpha * acc_scr[...] + _dot(e.astype(BF16), vch)
        m_scr[...] = m_new
        return carry

    lax.fori_loop(0, (i >> 2) + 1, chunk, 0)
    o_s = acc_scr[...] / jnp.maximum(l_scr[...], 1e-30)

    span = WINDOW + QB
    start = pl.multiple_of(jnp.maximum(i * QB - WINDOW, 0), QB)
    kw = kwin_ref[pl.ds(start, span), :]
    vw = vwin_ref[pl.ds(start, span), :]
    dist_w = qpos - (start + _iota((rows, span), 1))
    mask_w = jnp.where(dist_w >= 0, dist_w, WINDOW) < WINDOW
    s_w = _dot_nt(q, kw) - slope * dist_w.astype(F32)
    o_w = _dot(_masked_softmax_rows(s_w, mask_w).astype(BF16), vw)

    gt = gt_ref[0, 0]
    for r in range(GQA):
        sl = slice(r * QB, (r + 1) * QB)
        o_ref[0, r] = (gt[:, 3 * r:3 * r + 1] * o_c[sl] + gt[:, 3 * r + 1:3 * r + 2] * o_s[sl]
                       + gt[:, 3 * r + 2:3 * r + 3] * o_w[sl])


def _prompt_attention(slopes, q_blk, kvb, kce, kco, gates_g, bsz, t_len):
    nq = t_len // QB
    nb = t_len // L_CMP
    assert nb % (2 * LANES) == 0 or nb == LANES, "compressed blocks must fill whole lane tiles"
    assert t_len >= WINDOW + QB
    half = nb // 2
    rows = GQA * QB
    kv_spec = lambda col: pl.BlockSpec((t_len, DH), lambda b, g, i, col=col: (b, col + g))
    kc_spec = lambda col: pl.BlockSpec((half, DH), lambda b, g, i, col=col: (b, col + g))
    return pl.pallas_call(
        _pattn_body,
        grid=(bsz, N_KV, nq),
        in_specs=[pl.BlockSpec(memory_space=pltpu.SMEM),
                  pl.BlockSpec((1, GQA, QB, DH), lambda b, g, i: (b * nq + i, g, 0, 0)),
                  kv_spec(4), kv_spec(6), kv_spec(8), kv_spec(10),
                  kc_spec(0), kc_spec(0), kc_spec(2), kc_spec(2),
                  pl.BlockSpec((1, 1, QB, 3 * GQA), lambda b, g, i: (b, g, i, 0))],
        out_specs=pl.BlockSpec((1, GQA, QB, DH), lambda b, g, i: (b * nq + i, g, 0, 0)),
        out_shape=jax.ShapeDtypeStruct((bsz * nq, N_HEADS, QB, DH), F32),
        scratch_shapes=[pltpu.VMEM((rows, 1), F32), pltpu.VMEM((rows, 1), F32), pltpu.VMEM((rows, DH), F32)],
        compiler_params=_params(("parallel", "parallel", "arbitrary")),
        name="prompt_attention",
    )(slopes, q_blk, kvb, kvb, kvb, kvb, kce, kco, kce, kco, gates_g)


def _sample_select_body(slopes_ref, q_ref, kce_ref, kco_ref, oc_ref, sel_ref, *, past):
    rows = GQA * TP
    row = _iota((rows, 1), 0)
    qpos = past + (row & (TP - 1))
    r_of_row = row >> 3
    half = kce_ref.shape[1]
    nb = 2 * half
    for g in range(N_KV):
        slope = jnp.zeros((rows, 1), F32)
        for r in range(GQA):
            slope = jnp.where(r_of_row == r, slopes_ref[g * GQA + r], slope)
        q = q_ref[0, g]
        kc = jnp.concatenate([kce_ref[0, :, g * DH:(g + 1) * DH], kco_ref[0, :, g * DH:(g + 1) * DH]], axis=0)
        vc = jnp.concatenate([kce_ref[0, :, (2 + g) * DH:(3 + g) * DH], kco_ref[0, :, (2 + g) * DH:(3 + g) * DH]], axis=0)
        lane = _iota((rows, nb), 1)
        blk_c = jnp.where(lane < half, 2 * lane, 2 * (lane - half) + 1)
        dist = qpos - (blk_c * L_CMP + (L_CMP - 1))
        s = _dot_nt(q, kc) - slope * dist.astype(F32)
        p = _masked_softmax_rows(s, dist >= 0)
        oc_ref[0, g] = _dot(p.astype(BF16), vc)
        imp = p[0:TP]
        for r in range(1, GQA):
            imp = imp + p[r * TP:(r + 1) * TP]
        pair = imp[:, :half] + imp[:, half:]
        blk = _iota((TP, half), 1)
        qp = past + _iota((TP, half), 0)
        cur = qp >> 6
        forced = (blk == 0) | (blk == cur) | (blk == cur - 1)
        score = jnp.where(blk * L_SEL <= qp, jnp.where(forced, FORCE, pair), NEG)
        sel = jnp.zeros((TP, half), F32)
        for _ in range(N_SEL - 1):
            m = jnp.max(score, axis=1, keepdims=True)
            idx = jnp.min(jnp.where(score == m, blk, half), axis=1, keepdims=True)
            hit = blk == idx
            sel = jnp.where(hit, 1.0, sel)
            score = jnp.where(hit, -jnp.inf, score)
        sel_ref[0, g] = jnp.concatenate([sel] * GQA, axis=0).astype(BF16)


def _sample_select(slopes, q_s, kce, kco, past):
    bsz = q_s.shape[0]
    half = kce.shape[1]
    assert half == LANES, "past selection blocks must fill one lane tile"
    rows = GQA * TP
    return pl.pallas_call(
        functools.partial(_sample_select_body, past=past),
        grid=(bsz,),
        in_specs=[pl.BlockSpec(memory_space=pltpu.SMEM),
                  pl.BlockSpec((1, N_KV, rows, DH), lambda b: (b, 0, 0, 0)),
                  pl.BlockSpec((1, half, 4 * DH), lambda b: (b, 0, 0)),
                  pl.BlockSpec((1, half, 4 * DH), lambda b: (b, 0, 0))],
        out_specs=[pl.BlockSpec((1, N_KV, rows, DH), lambda b: (b, 0, 0, 0)),
                   pl.BlockSpec((1, N_KV, rows, half), lambda b: (b, 0, 0, 0))],
        out_shape=[jax.ShapeDtypeStruct((bsz, N_KV, rows, DH), F32),
                   jax.ShapeDtypeStruct((bsz, N_KV, rows, half), BF16)],
        compiler_params=_params(("parallel",)),
        name="sample_select",
    )(slopes, q_s, kce, kco)


def _online_update(m_ref, l_ref, acc_ref, g, sc, msk, v):
    m_old = m_ref[g]
    m_new = jnp.maximum(m_old, jnp.max(sc, axis=-1, keepdims=True))
    alpha = jnp.exp(m_old - m_new)
    e = jnp.where(msk, jnp.exp(sc - m_new), 0.0)
    l_ref[g] = alpha * l_ref[g] + jnp.sum(e, axis=-1, keepdims=True)
    acc_ref[g] = alpha * acc_ref[g] + _dot(e.astype(BF16), v)
    m_ref[g] = m_new


def _sample_attn_body(pt_ref, slopes_ref, *refs, past, t_new):
    pages = refs[:PAGES_PER_STEP]
    q_ref, sel_ref, new_ref, win_ref, oc_ref, gt_ref, o_ref, m_scr, l_scr, acc_scr = refs[PAGES_PER_STEP:]
    c = pl.program_id(1)
    rows = GQA * TP
    row = _iota((rows, 1), 0)
    qpos = past + (row & (TP - 1))
    r_of_row = row >> 3
    page = pages[0].shape[1]
    ck = PAGES_PER_STEP * page

    @pl.when(c == 0)
    def _():
        m_scr[...] = jnp.full(m_scr.shape, NEG, F32)
        l_scr[...] = jnp.zeros(l_scr.shape, F32)
        acc_scr[...] = jnp.zeros(acc_scr.shape, F32)

    dist_s = qpos - (c * ck + _iota((rows, ck), 1))
    expand = jnp.where((_iota((LANES, ck), 1) >> 6) + c * (ck // L_SEL) == _iota((LANES, ck), 0), 1.0, 0.0).astype(BF16)
    slopes = []
    for g in range(N_KV):
        slope = jnp.zeros((rows, 1), F32)
        for r in range(GQA):
            slope = jnp.where(r_of_row == r, slopes_ref[g * GQA + r], slope)
        slopes.append(slope)
        q = q_ref[0, g]
        kch = jnp.concatenate([p[0, :, g * DH:(g + 1) * DH].astype(BF16) for p in pages], axis=0)
        vch = jnp.concatenate([p[0, :, (2 + g) * DH:(3 + g) * DH].astype(BF16) for p in pages], axis=0)
        picked = _dot(sel_ref[0, g], expand)
        msk = jnp.where(dist_s >= 0, picked, 0.0) > 0.5
        sc = jnp.where(msk, _dot_nt(q, kch) - slope * dist_s.astype(F32), NEG)
        _online_update(m_scr, l_scr, acc_scr, g, sc, msk, vch)

    @pl.when(c == pl.num_programs(1) - 1)
    def _():
        col = _iota((rows, TP), 1)
        dist_n = qpos - (past + col)
        mask_n = jnp.where(col < t_new, dist_n, -1) >= 0
        dist_c = qpos - (past - WINDOW + _iota((rows, WINDOW), 1))
        mask_c = jnp.where(dist_c >= 0, dist_c, WINDOW) < WINDOW
        mask_wn = jnp.where(mask_n, dist_n, WINDOW) < WINDOW
        for g in range(N_KV):
            slope = slopes[g]
            q = q_ref[0, g]
            kn = new_ref[0, :, (4 + g) * DH:(5 + g) * DH]
            vn = new_ref[0, :, (6 + g) * DH:(7 + g) * DH]
            sc = jnp.where(mask_n, _dot_nt(q, kn) - slope * dist_n.astype(F32), NEG)
            _online_update(m_scr, l_scr, acc_scr, g, sc, mask_n, vn)
            o_s = acc_scr[g] / jnp.maximum(l_scr[g], 1e-30)
            kwc = win_ref[0, :, g * DH:(g + 1) * DH].astype(BF16)
            vwc = win_ref[0, :, (2 + g) * DH:(3 + g) * DH].astype(BF16)
            kwn = new_ref[0, :, (8 + g) * DH:(9 + g) * DH]
            vwn = new_ref[0, :, (10 + g) * DH:(11 + g) * DH]
            s1 = jnp.where(mask_c, _dot_nt(q, kwc) - slope * dist_c.astype(F32), NEG)
            s2 = jnp.where(mask_wn, _dot_nt(q, kwn) - slope * dist_n.astype(F32), NEG)
            mx = jnp.maximum(jnp.max(s1, axis=-1, keepdims=True), jnp.max(s2, axis=-1, keepdims=True))
            e1 = jnp.where(mask_c, jnp.exp(s1 - mx), 0.0)
            e2 = jnp.where(mask_wn, jnp.exp(s2 - mx), 0.0)
            den = jnp.maximum(jnp.sum(e1, axis=-1, keepdims=True) + jnp.sum(e2, axis=-1, keepdims=True), 1e-30)
            o_w = _dot((e1 / den).astype(BF16), vwc) + _dot((e2 / den).astype(BF16), vwn)
            gt = gt_ref[0, g]
            o_ref[0, g] = gt[:, 0:1] * oc_ref[0, g] + gt[:, 1:2] * o_s + gt[:, 2:3] * o_w


def _sample_attention(slopes, page_table, cache_sel3, q_s, sel_s, new_kvb, cache_win3, o_c, gates_s, past, t_new):
    n_pool, page, c4 = cache_sel3.shape
    bsz, n_pages = page_table.shape
    n_steps = n_pages // PAGES_PER_STEP
    rows = GQA * TP
    win_len = cache_win3.shape[1]
    assert win_len == WINDOW and past % L_SEL == 0 and t_new <= TP
    page_spec = lambda i: pl.BlockSpec((1, page, c4), lambda b, s, pt: (pt[b * n_pages + s * PAGES_PER_STEP + i], 0, 0))
    per_b = lambda shape: pl.BlockSpec((1,) + shape, lambda b, s, pt: (b,) + (0,) * len(shape))
    grid_spec = pltpu.PrefetchScalarGridSpec(
        num_scalar_prefetch=1,
        grid=(bsz, n_steps),
        in_specs=[pl.BlockSpec(memory_space=pltpu.SMEM)] + [page_spec(i) for i in range(PAGES_PER_STEP)]
        + [per_b((N_KV, rows, DH)), per_b((N_KV, rows, LANES)), per_b((TP, 12 * DH)), per_b((win_len, c4)),
           per_b((N_KV, rows, DH)), per_b((N_KV, rows, 3))],
        out_specs=per_b((N_KV, rows, DH)),
        scratch_shapes=[pltpu.VMEM((N_KV, rows, 1), F32), pltpu.VMEM((N_KV, rows, 1), F32),
                        pltpu.VMEM((N_KV, rows, DH), F32)],
    )
    return pl.pallas_call(
        functools.partial(_sample_attn_body, past=past, t_new=t_new),
        grid_spec=grid_spec,
        out_shape=jax.ShapeDtypeStruct((bsz, N_KV, rows, DH), F32),
        compiler_params=_params(("parallel", "arbitrary")),
        name="sample_attention",
    )(page_table.reshape(-1), slopes, *([cache_sel3] * PAGES_PER_STEP), q_s, sel_s, new_kvb, cache_win3, o_c, gates_s)


def _gelu_tanh(x):
    return 0.5 * x * (1.0 + jnp.tanh(0.7978845608028654 * (x + 0.044715 * (x * x * x))))


def _lru_coeffs(conv, wa_ref, ba_ref, wi_ref, bi_ref, sp_ref):
    cb = conv.astype(BF16)
    n_blk, blk = wa_ref.shape[0], wa_ref.shape[1]
    ra = jnp.concatenate([_dot(cb[:, n * blk:(n + 1) * blk], wa_ref[n]) for n in range(n_blk)], axis=1)
    ri = jnp.concatenate([_dot(cb[:, n * blk:(n + 1) * blk], wi_ref[n]) for n in range(n_blk)], axis=1)
    r = jax.nn.sigmoid(ra + ba_ref[...])
    gi = jax.nn.sigmoid(ri + bi_ref[...])
    log_a = -LRU_C * r * sp_ref[...]
    a = jnp.exp(log_a)
    b = jnp.sqrt(-jnp.tanh(log_a) * (a * a + 1.0)) * (gi * conv)
    return a, b


def _lru_seq_body(xr_ref, yr_ref, cs_ref, h0_ref, cw_ref, cb_ref, wa_ref, ba_ref, wi_ref, bi_ref, sp_ref, onw_ref,
                  rec_ref, hl_ref, xbuf, h_scr):
    k = pl.program_id(1)
    tt = xr_ref.shape[1]
    pad = 8

    @pl.when(k == 0)
    def _():
        xbuf[0:pad, :] = jnp.zeros((pad, xbuf.shape[1]), F32)
        xbuf[pad - (CONV_W - 1):pad, :] = cs_ref[0]
        h_scr[...] = h0_ref[0]

    x = xr_ref[0]
    xbuf[pad:pad + tt, :] = x
    conv = cb_ref[...] + cw_ref[CONV_W - 1:CONV_W, :] * x
    for j in range(CONV_W - 1):
        conv = conv + cw_ref[j:j + 1, :] * xbuf[pad - (CONV_W - 1) + j:pad - (CONV_W - 1) + j + tt, :]
    xbuf[0:pad, :] = x[tt - pad:tt]
    a, b = _lru_coeffs(conv, wa_ref, ba_ref, wi_ref, bi_ref, sp_ref)
    row = _iota((tt, 1), 0)
    s = 1
    while s < tt:
        keep = row >= s
        a_sh = jnp.where(keep, pltpu.roll(a, s, 0), 1.0)
        b_sh = jnp.where(keep, pltpu.roll(b, s, 0), 0.0)
        b = a * b_sh + b
        a = a * a_sh
        s *= 2
    hs = a * h_scr[...] + b
    h_scr[...] = hs[tt - 1:tt]
    hl_ref[0] = hs[tt - 1:tt]
    rec = hs * _gelu_tanh(yr_ref[0])
    rec_ref[0] = _rms(rec, onw_ref[...]).astype(BF16)


def _lru_seq(xr3, yr3, cs, h0, cw, cb, wa, ba, wi, bi, sp, onw, tt):
    bsz, t_len, d = xr3.shape
    seq = pl.BlockSpec((1, tt, d), lambda b, k: (b, k, 0))
    full = lambda a: pl.BlockSpec(a.shape, lambda b, k: (0,) * a.ndim)
    return pl.pallas_call(
        _lru_seq_body,
        grid=(bsz, t_len // tt),
        in_specs=[seq, seq, pl.BlockSpec((1, CONV_W - 1, d), lambda b, k: (b, 0, 0)),
                  pl.BlockSpec((1, 1, d), lambda b, k: (b, 0, 0)),
                  full(cw), full(cb), full(wa), full(ba), full(wi), full(bi), full(sp), full(onw)],
        out_specs=[seq, pl.BlockSpec((1, 1, d), lambda b, k: (b, 0, 0))],
        out_shape=[jax.ShapeDtypeStruct((bsz, t_len, d), BF16), jax.ShapeDtypeStruct((bsz, 1, d), F32)],
        scratch_shapes=[pltpu.VMEM((tt + 8, d), F32), pltpu.VMEM((1, d), F32)],
        compiler_params=_params(("parallel", "arbitrary")),
        name="lru_seq",
    )(xr3, yr3, cs, h0, cw, cb, wa, ba, wi, bi, sp, onw)


def _lru_step_body(xr_ref, yr_ref, cs_ref, h0_ref, cw_ref, cb_ref, wa_ref, ba_ref, wi_ref, bi_ref, sp_ref, onw_ref,
                   rec_ref, hl_ref):
    t_len, bsz = xr_ref.shape[0], xr_ref.shape[1]
    xs = [cs_ref[j] for j in range(CONV_W - 1)] + [xr_ref[t] for t in range(t_len)]
    convs = []
    for t in range(t_len):
        conv = cb_ref[...] + cw_ref[0:1, :] * xs[t]
        for j in range(1, CONV_W):
            conv = conv + cw_ref[j:j + 1, :] * xs[t + j]
        convs.append(conv)
    a, b = _lru_coeffs(jnp.concatenate(convs, axis=0), wa_ref, ba_ref, wi_ref, bi_ref, sp_ref)
    h = h0_ref[...]
    for t in range(t_len):
        h = a[t * bsz:(t + 1) * bsz] * h + b[t * bsz:(t + 1) * bsz]
        rec = h * _gelu_tanh(yr_ref[t])
        rec_ref[t] = _rms(rec, onw_ref[...]).astype(BF16)
    hl_ref[...] = h


def _lru_step(xr_t, yr_t, cs_t, h0, cw, cb, wa, ba, wi, bi, sp, onw):
    t_len, bsz, d = xr_t.shape
    return pl.pallas_call(
        _lru_step_body,
        out_shape=[jax.ShapeDtypeStruct((t_len, bsz, d), BF16), jax.ShapeDtypeStruct((bsz, d), F32)],
        compiler_params=pltpu.CompilerParams(vmem_limit_bytes=VMEM_LIMIT),
        name="lru_step",
    )(xr_t, yr_t, cs_t, h0, cw, cb, wa, ba, wi, bi, sp, onw)


def _outproj_body(attn_ref, rec_ref, x_ref, cnt0_ref, anw_ref, woa_ref, wor_ref, fnw_ref, wr_ref, br_ref,
                  x1_ref, h2_ref, ti_ref, tg_ref, tp_ref, cnt_ref, carry):
    step = pl.program_id(0)
    tm = x_ref.shape[0]

    @pl.when(step == 0)
    def _():
        carry[...] = cnt0_ref[...]

    attn = jnp.concatenate(
        [jnp.concatenate([attn_ref[b, hd] for hd in range(N_HEADS)], axis=1) for b in range(attn_ref.shape[0])], axis=0)
    an = _rms(attn, anw_ref[...]).astype(BF16)
    x1 = x_ref[...] + (_dot(an, woa_ref[...]) + _dot(rec_ref[...], wor_ref[...]))
    x1_ref[...] = x1
    h2 = _rms(x1, fnw_ref[...])
    h2_ref[...] = h2
    lane = _iota((tm, LANES), 1)
    logits = jnp.dot(h2, wr_ref[...], precision=lax.Precision.HIGHEST, preferred_element_type=F32) + br_ref[...]
    lg = jnp.where(lane < N_EXPERTS, logits, -jnp.inf)
    vals, idxs = [], []
    for _ in range(TOP_K):
        m = jnp.max(lg, axis=-1, keepdims=True)
        ix = jnp.min(jnp.where(lg == m, lane, LANES), axis=-1, keepdims=True)
        vals.append(m)
        idxs.append(ix)
        lg = jnp.where(lane == ix, -jnp.inf, lg)
    es = [jnp.exp(v - vals[0]) for v in vals]
    den = es[0]
    for e in es[1:]:
        den = den + e
    onehot = jnp.zeros((tm, LANES), F32)
    for ix in idxs:
        onehot = jnp.where(lane == ix, 1.0, onehot)
    lower = jnp.where(_iota((tm, tm), 0) > _iota((tm, tm), 1), 1.0, 0.0).astype(BF16)
    rank = carry[...] + _dot(lower, onehot.astype(BF16))
    carry[...] = carry[...] + jnp.sum(onehot, axis=0, keepdims=True)
    ti = jnp.zeros((tm, LANES), I32)
    tg = jnp.zeros((tm, LANES), F32)
    tp = jnp.zeros((tm, LANES), I32)
    for k in range(TOP_K):
        pos = jnp.sum(jnp.where(lane == idxs[k], rank, 0.0), axis=-1, keepdims=True).astype(I32)
        ti = jnp.where(lane == k, idxs[k], ti)
        tg = jnp.where(lane == k, es[k] / den, tg)
        tp = jnp.where(lane == k, pos, tp)
    ti_ref[...] = ti
    tg_ref[...] = tg
    tp_ref[...] = tp
    cnt_ref[...] = carry[...]


def _outproj_router(attn_blk, recn, x2, cnt0, anw, woa, wor, fnw, wr, br, tm):
    n, d = x2.shape
    nqb = tm // QB
    row = lambda i: (i, 0)
    wspec = lambda a: pl.BlockSpec(a.shape, lambda i: (0,) * a.ndim, pipeline_mode=pl.Buffered(1))
    lanes_out = lambda dt: jax.ShapeDtypeStruct((n, LANES), dt)
    return pl.pallas_call(
        _outproj_body,
        grid=(n // tm,),
        in_specs=[pl.BlockSpec((nqb, N_HEADS, QB, DH), lambda i: (i, 0, 0, 0)),
                  pl.BlockSpec((tm, recn.shape[1]), row), pl.BlockSpec((tm, d), row),
                  wspec(cnt0), wspec(anw), wspec(woa), wspec(wor), wspec(fnw), wspec(wr), wspec(br)],
        out_specs=[pl.BlockSpec((tm, d), row), pl.BlockSpec((tm, d), row), pl.BlockSpec((tm, LANES), row),
                   pl.BlockSpec((tm, LANES), row), pl.BlockSpec((tm, LANES), row), pl.BlockSpec((1, LANES), lambda i: (0, 0))],
        out_shape=[jax.ShapeDtypeStruct((n, d), F32), jax.ShapeDtypeStruct((n, d), F32),
                   lanes_out(I32), lanes_out(F32), lanes_out(I32), jax.ShapeDtypeStruct((1, LANES), F32)],
        scratch_shapes=[pltpu.VMEM((1, LANES), F32)],
        compiler_params=_params(("arbitrary",)),
        name="outproj_router",
    )(attn_blk, recn, x2, cnt0, anw, woa, wor, fnw, wr, br)


def _gather_body(rowtok_ref, nused_ref, h2_hbm, out_ref, buf, sem):
    j = pl.program_id(0)
    tm = buf.shape[0]

    def row_copy(tok, r):
        return pltpu.make_async_copy(h2_hbm.at[pl.ds(tok, 1), :], buf.at[pl.ds(r, 1), :], sem)

    @pl.when(j < nused_ref[0])
    def _():
        def issue(r, c):
            row_copy(rowtok_ref[j * tm + r], r).start()
            return c

        lax.fori_loop(0, tm, issue, 0)

        def drain(r, c):
            row_copy(0, r).wait()
            return c

        lax.fori_loop(0, tm, drain, 0)
        out_ref[...] = buf[...].astype(BF16)

    @pl.when(j >= nused_ref[0])
    def _():
        out_ref[...] = jnp.zeros(out_ref.shape, BF16)


def _gather_rows(row_tok, n_used, h2, n_blocks):
    n, d = h2.shape
    tm = MOE_TM
    grid_spec = pltpu.PrefetchScalarGridSpec(
        num_scalar_prefetch=2,
        grid=(n_blocks,),
        in_specs=[pl.BlockSpec(memory_space=pl.ANY)],
        out_specs=pl.BlockSpec((tm, d), lambda j, rt, nu: (j, 0)),
        scratch_shapes=[pltpu.VMEM((tm, d), F32), pltpu.SemaphoreType.DMA],
    )
    return pl.pallas_call(
        _gather_body,
        grid_spec=grid_spec,
        out_shape=jax.ShapeDtypeStruct((n_blocks * tm, d), BF16),
        compiler_params=_params(("arbitrary",)),
        name="moe_gather",
    )(row_tok, n_used, h2)


def _expert_changed(be_ref, j):
    return (j == 0) | (be_ref[j] != be_ref[jnp.maximum(j - 1, 0)])


def _moe_up_body(be_ref, nu_ref, xs_ref, w_ref, b_ref, perm_ref, h_ref, wbf):
    j = pl.program_id(1)
    tn = wbf.shape[1]
    pw = perm_ref.shape[0]

    @pl.when(j < nu_ref[0])
    def _():
        @pl.when(_expert_changed(be_ref, j))
        def _():
            for c in range(tn // pw):
                w = w_ref[0, :, c * pw:(c + 1) * pw].astype(BF16)
                wbf[:, c * pw:(c + 1) * pw] = _dot(w, perm_ref[...]).astype(BF16)

        gu = _dot(xs_ref[...], wbf[...]) + b_ref[0]
        hw = pw // 2
        for c in range(tn // pw):
            glu = jnp.minimum(gu[:, c * pw:c * pw + hw], SWIGLU_LIMIT)
            lin = jnp.clip(gu[:, c * pw + hw:(c + 1) * pw], -SWIGLU_LIMIT, SWIGLU_LIMIT)
            h_ref[:, c * hw:(c + 1) * hw] = (glu * jax.nn.sigmoid(SWIGLU_ALPHA * glu) * (lin + 1.0)).astype(BF16)

    @pl.when(j >= nu_ref[0])
    def _():
        h_ref[...] = jnp.zeros(h_ref.shape, BF16)


def _moe_up(block_e, n_used, xs, w_gu, b_gu, perm):
    n_rows, d = xs.shape
    n_e, _, f2 = w_gu.shape
    tm, tn = MOE_TM, MOE_TN
    n_blocks = n_rows // tm
    blk = lambda j, nu: jnp.minimum(j, nu[0] - 1)
    grid_spec = pltpu.PrefetchScalarGridSpec(
        num_scalar_prefetch=2,
        grid=(f2 // tn, n_blocks),
        in_specs=[pl.BlockSpec((tm, d), lambda n, j, be, nu: (blk(j, nu), 0)),
                  pl.BlockSpec((1, d, tn), lambda n, j, be, nu: (be[blk(j, nu)], 0, n)),
                  pl.BlockSpec((1, 1, tn), lambda n, j, be, nu: (be[blk(j, nu)], 0, n)),
                  pl.BlockSpec(perm.shape, lambda n, j, be, nu: (0, 0))],
        out_specs=pl.BlockSpec((tm, tn // 2), lambda n, j, be, nu: (j, n)),
        scratch_shapes=[pltpu.VMEM((d, tn), BF16)],
    )
    return pl.pallas_call(
        _moe_up_body,
        grid_spec=grid_spec,
        out_shape=jax.ShapeDtypeStruct((n_rows, f2 // 2), BF16),
        compiler_params=_params(("arbitrary", "arbitrary")),
        name="moe_up",
    )(block_e, n_used, xs, w_gu, b_gu, perm)


def _moe_down_body(be_ref, nu_ref, h_ref, w_ref, b_ref, y_ref, wbf):
    j = pl.program_id(1)
    f = wbf.shape[0]
    rc = 512

    @pl.when(j < nu_ref[0])
    def _():
        @pl.when(_expert_changed(be_ref, j))
        def _():
            for c in range(f // rc):
                wbf[c * rc:(c + 1) * rc, :] = w_ref[0, c * rc:(c + 1) * rc, :].astype(BF16)

        y_ref[...] = _dot(h_ref[...], wbf[...]) + b_ref[0]

    @pl.when(j >= nu_ref[0])
    def _():
        y_ref[...] = jnp.zeros(y_ref.shape, F32)


def _moe_down(block_e, n_used, h, w_d, b_d):
    n_rows, f = h.shape
    d = w_d.shape[2]
    tm, tn = MOE_TM, MOE_TN
    n_blocks = n_rows // tm
    blk = lambda j, nu: jnp.minimum(j, nu[0] - 1)
    grid_spec = pltpu.PrefetchScalarGridSpec(
        num_scalar_prefetch=2,
        grid=(d // tn, n_blocks),
        in_specs=[pl.BlockSpec((tm, f), lambda n, j, be, nu: (blk(j, nu), 0)),
                  pl.BlockSpec((1, f, tn), lambda n, j, be, nu: (be[blk(j, nu)], 0, n)),
                  pl.BlockSpec((1, 1, tn), lambda n, j, be, nu: (be[blk(j, nu)], 0, n))],
        out_specs=pl.BlockSpec((tm, tn), lambda n, j, be, nu: (j, n)),
        scratch_shapes=[pltpu.VMEM((f, tn), BF16)],
    )
    return pl.pallas_call(
        _moe_down_body,
        grid_spec=grid_spec,
        out_shape=jax.ShapeDtypeStruct((n_rows, d), F32),
        compiler_params=_params(("arbitrary", "arbitrary")),
        name="moe_down",
    )(block_e, n_used, h, w_d, b_d)


def _combine_body(dest_hbm, gate_ref, x1_ref, y_hbm, out_ref, dsm, buf, sem_d, sem):
    j = pl.program_id(0)
    tk = x1_ref.shape[0]
    idx_copy = pltpu.make_async_copy(dest_hbm.at[j, 0], dsm, sem_d)
    idx_copy.start()
    idx_copy.wait()

    def row_copy(src, k, t):
        return pltpu.make_async_copy(y_hbm.at[pl.ds(src, 1), :], buf.at[k, pl.ds(t, 1), :], sem)

    def issue(t, c):
        for k in range(TOP_K):
            row_copy(dsm[t * TOP_K + k], k, t).start()
        return c

    lax.fori_loop(0, tk, issue, 0)

    def drain(t, c):
        for k in range(TOP_K):
            row_copy(0, k, t).wait()
        return c

    lax.fori_loop(0, tk, drain, 0)
    g = gate_ref[...]
    acc = g[:, 0:1] * buf[0]
    for k in range(1, TOP_K):
        acc = acc + g[:, k:k + 1] * buf[k]
    out_ref[...] = x1_ref[...] + acc


def _combine(dest3, gates, x1, y):
    n, d = x1.shape
    tk = dest3.shape[2] // TOP_K
    row = lambda j: (j, 0)
    return pl.pallas_call(
        _combine_body,
        grid=(n // tk,),
        in_specs=[pl.BlockSpec(memory_space=pl.ANY), pl.BlockSpec((tk, LANES), row), pl.BlockSpec((tk, d), row),
                  pl.BlockSpec(memory_space=pl.ANY)],
        out_specs=pl.BlockSpec((tk, d), row),
        out_shape=jax.ShapeDtypeStruct((n, d), F32),
        scratch_shapes=[pltpu.SMEM((tk * TOP_K,), I32), pltpu.VMEM((TOP_K, tk, d), F32),
                        pltpu.SemaphoreType.DMA, pltpu.SemaphoreType.DMA],
        compiler_params=_params(("arbitrary",)),
        name="moe_combine",
    )(dest3, gates, x1, y)


def _moe(h2, x1, topi, topg, topp, counts, w_gu, b_gu, w_d, b_d):
    n, d = h2.shape
    tm = MOE_TM
    n_asg = n * TOP_K
    n_blocks = n_asg // tm + N_EXPERTS
    padded = (counts + tm - 1) // tm * tm
    pad_end = jnp.cumsum(padded)
    pad_start = pad_end - padded
    dest = pad_start[topi[:, :TOP_K]] + topp[:, :TOP_K]
    tok = jnp.broadcast_to(jnp.arange(n, dtype=I32)[:, None], (n, TOP_K))
    row_tok = jnp.zeros((n_blocks * tm,), I32).at[dest.reshape(-1)].set(tok.reshape(-1))
    block_e = jnp.minimum(jnp.searchsorted(pad_end, jnp.arange(n_blocks, dtype=I32) * tm, side="right"),
                          N_EXPERTS - 1).astype(I32)
    n_used = (pad_end[-1:] // tm).astype(I32)
    f2 = w_gu.shape[2]
    pw = 2 * LANES
    src = jnp.arange(pw)
    perm = (jnp.arange(pw)[:, None] == jnp.where(src < LANES, 2 * src, 2 * (src - LANES) + 1)[None, :]).astype(BF16)
    b_gu_p = b_gu.reshape(N_EXPERTS, f2 // pw, LANES, 2).transpose(0, 1, 3, 2).reshape(N_EXPERTS, 1, f2)
    xs = _gather_rows(row_tok, n_used, h2, n_blocks)
    h = _moe_up(block_e, n_used, xs, w_gu, b_gu_p, perm)
    y = _moe_down(block_e, n_used, h, w_d, b_d.reshape(N_EXPERTS, 1, d))
    tk = 128
    dest3 = dest.astype(I32).reshape(n // tk, 1, tk * TOP_K)
    return _combine(dest3, topg, x1, y)


def _layer(xp, xs, cache_cmp, cache_sel, cache_win, state_conv, state_h, page_table, w):
    (norm_mix_w, w_in, q_norm_w, k_norm_w, w_cmp_k, w_cmp_v, conv_w, conv_b, w_gate_a, b_gate_a, w_gate_i, b_gate_i,
     lru_lambda, out_norm_attn, out_norm_rec, w_out, norm_ffn_w, w_router, b_router, w_gate_up, b_gate_up, w_down,
     b_down) = w
    bp, tp, d = xp.shape
    bs, ts, _ = xs.shape
    d_rec = d - D_ATTN
    past = page_table.shape[1] * cache_cmp.shape[1]
    assert bs * ts == QB and ts <= TP

    o1, o2 = D_ATTN, D_ATTN + 12 * DH
    o3 = o2 + 3 * N_HEADS
    wq = w_in[:, :o1].astype(BF16)
    wkv = w_in[:, o1:o2].astype(BF16)
    wg = jnp.pad(w_in[:, o2:o3], ((0, 0), (0, LANES - 3 * N_HEADS))).astype(BF16)
    wxy = w_in[:, o3:].astype(BF16)
    row2 = lambda v: v.reshape(1, -1)
    slopes = jnp.exp2(-8.0 * jnp.arange(1, N_HEADS + 1, dtype=F32) / N_HEADS)
    wl = jnp.repeat(jnp.concatenate([w_cmp_k, w_cmp_v], axis=0).T, DH, axis=1)
    zl = jnp.zeros_like(wl)
    wl = jnp.stack([jnp.concatenate([wl, zl], axis=0), jnp.concatenate([zl, wl], axis=0)])
    sp = row2(jax.nn.softplus(-lru_lambda.astype(F32)))
    wa, wi = w_gate_a.astype(BF16), w_gate_i.astype(BF16)
    woa, wor = w_out[:D_ATTN].astype(BF16), w_out[D_ATTN:].astype(BF16)
    wr = jnp.pad(w_router, ((0, 0), (0, LANES - N_EXPERTS)))
    br = row2(jnp.pad(b_router, (0, LANES - N_EXPERTS)))
    mix = (row2(norm_mix_w), wq, wkv, wg, wxy, row2(q_norm_w), k_norm_w)
    lru_w = (conv_w, row2(conv_b), wa, row2(b_gate_a), wi, row2(b_gate_i), sp, row2(out_norm_rec))

    np_tok = bp * tp
    q_p, cmp_p, sel_p, win_p, kvb_p, gate_p, xr_p, yr_p = _inproj(xp.reshape(np_tok, d), *mix, tm=256)
    kce_p, kco_p = _pool_prompt(cmp_p, wl)
    gates_g = gate_p[:, :3 * N_HEADS].reshape(bp, tp, N_KV, 3 * GQA).transpose(0, 2, 1, 3)
    attn_p = _prompt_attention(slopes, q_p, kvb_p, kce_p, kco_p, gates_g, bp, tp)
    recn_p, hl_p = _lru_seq(xr_p.reshape(bp, tp, d_rec), yr_p.reshape(bp, tp, d_rec),
                            jnp.zeros((bp, CONV_W - 1, d_rec), F32), jnp.zeros((bp, 1, d_rec), F32), *lru_w, tt=256)
    post = (row2(out_norm_attn), woa, wor, row2(norm_ffn_w), wr, br)
    x1_p, h2_p, ti_p, tg_p, tp_p, cnt_p = _outproj_router(
        attn_p, recn_p.reshape(np_tok, d_rec), xp.reshape(np_tok, d), jnp.zeros((1, LANES), F32), *post, tm=256)

    ns_tok = bs * ts
    q_s, cmp_s, sel_s, win_s, kvb_s, gate_s, xr_s, yr_s = _inproj(xs.reshape(ns_tok, d), *mix, tm=QB)
    kce_s, kco_s = _pool_pages(cache_cmp.reshape(cache_cmp.shape[0], cache_cmp.shape[1], 4 * DH), page_table, wl)
    q_t = q_s[0].reshape(N_KV, GQA, bs, ts, DH).transpose(2, 0, 1, 3, 4)
    q_t = jnp.pad(q_t, ((0, 0), (0, 0), (0, 0), (0, TP - ts), (0, 0))).reshape(bs, N_KV, GQA * TP, DH)
    oc_s, selm_s = _sample_select(slopes, q_t, kce_s, kco_s, past)
    new_kvb = jnp.pad(kvb_s.reshape(bs, ts, 12 * DH), ((0, 0), (0, TP - ts), (0, 0)))
    g_t = gate_s[:, :3 * N_HEADS].reshape(bs, ts, N_KV, GQA, 3).transpose(0, 2, 3, 1, 4)
    g_t = jnp.pad(g_t, ((0, 0), (0, 0), (0, 0), (0, TP - ts), (0, 0))).reshape(bs, N_KV, GQA * TP, 3)
    attn_s = _sample_attention(slopes, page_table, cache_sel.reshape(cache_sel.shape[0], cache_sel.shape[1], 4 * DH),
                               q_t, selm_s, new_kvb, cache_win.reshape(bs, cache_win.shape[1], 4 * DH), oc_s, g_t,
                               past, ts)
    attn_s = attn_s.reshape(bs, N_KV, GQA, TP, DH)[:, :, :, :ts].transpose(1, 2, 0, 3, 4).reshape(1, N_HEADS, QB, DH)
    tmaj = lambda a: a.reshape(bs, ts, d_rec).transpose(1, 0, 2)
    recn_s, hl_s = _lru_step(tmaj(xr_s), tmaj(yr_s), state_conv.transpose(1, 0, 2), state_h, *lru_w)
    x1_s, h2_s, ti_s, tg_s, tp_s, cnt = _outproj_router(
        attn_s, recn_s.transpose(1, 0, 2).reshape(ns_tok, d_rec), xs.reshape(ns_tok, d), cnt_p, *post, tm=QB)

    cat = lambda a, b: jnp.concatenate([a, b], axis=0)
    out = _moe(cat(h2_p, h2_s), cat(x1_p, x1_s), cat(ti_p, ti_s), cat(tg_p, tg_s), cat(tp_p, tp_s),
               cnt[0, :N_EXPERTS].astype(I32), w_gate_up, b_gate_up, w_down, b_down)
    y_p = out[:np_tok].reshape(bp, tp, d)
    y_s = out[np_tok:].reshape(bs, ts, d)

    kv5 = lambda a, b, t: a.reshape(b, t, 2, N_KV, DH)
    win_len_p = min(WINDOW, tp)
    st_p = (kv5(cmp_p, bp, tp), kv5(sel_p, bp, tp), kv5(win_p, bp, tp)[:, tp - win_len_p:],
            xr_p.reshape(bp, tp, d_rec)[:, tp - (CONV_W - 1):], hl_p.reshape(bp, d_rec))
    win_all = jnp.concatenate([cache_win, kv5(win_s, bs, ts)], axis=1)
    xcat = jnp.concatenate([state_conv, xr_s.reshape(bs, ts, d_rec)], axis=1)
    st_s = (kv5(cmp_s, bs, ts), kv5(sel_s, bs, ts), win_all[:, win_all.shape[1] - cache_win.shape[1]:],
            xcat[:, ts:], hl_s)
    return y_p, y_s, st_p, st_s


def kernel(x_prompt, x_sample, cache_cmp_kv, cache_sel_kv, cache_win_kv, state_conv, state_h, page_table, norm_mix_w, w_in, q_norm_w, k_norm_w, w_cmp_k, w_cmp_v, conv_w, conv_b, w_gate_a, b_gate_a, w_gate_i, b_gate_i, lru_lambda, out_norm_attn, out_norm_rec, w_out, norm_ffn_w, w_router, b_router, w_gate_up, b_gate_up, w_down, b_down):
    depth = w_in.shape[0]
    xp, xs = x_prompt, x_sample
    st_ps, st_ss = [], []
    for l in range(depth):
        w = (norm_mix_w[l], w_in[l], q_norm_w[l], k_norm_w[l], w_cmp_k[l], w_cmp_v[l], conv_w[l], conv_b[l],
             w_gate_a[l], b_gate_a[l], w_gate_i[l], b_gate_i[l], lru_lambda[l], out_norm_attn[l], out_norm_rec[l],
             w_out[l], norm_ffn_w[l], w_router[l], b_router[l], w_gate_up[l], b_gate_up[l], w_down[l], b_down[l])
        xp, xs, st_p, st_s = _layer(xp, xs, cache_cmp_kv[l], cache_sel_kv[l], cache_win_kv[l], state_conv[l],
                                    state_h[l], page_table, w)
        st_ps.append(st_p)
        st_ss.append(st_s)
    stack = lambda sts, i: jnp.stack([s[i] for s in sts])
    return (xp, xs) + tuple(stack(st_ps, i) for i in range(5)) + tuple(stack(st_ss, i) for i in range(5))
```

```python
import functools

import jax
import jax.numpy as jnp
from jax import lax
from jax.experimental import pallas as pl
from jax.experimental.pallas import tpu as pltpu

F32 = jnp.float32
BF16 = jnp.bfloat16
I32 = jnp.int32

N_HEADS = 8
N_KV = 2
GQA = N_HEADS // N_KV
DH = 128
D_ATTN = N_HEADS * DH
CONV_W = 4
LRU_C = 8.0
L_CMP = 32
L_SEL = 64
N_SEL = 16
WINDOW = 512
QB = 128
N_EXPERTS = 32
TOP_K = 4
SWIGLU_LIMIT = 7.0
SWIGLU_ALPHA = 1.702
EPS = 1e-6
NEG = -1e30
FORCE = 1e9
Q_SCALE = DH ** -0.5
LANES = 128
TP = 8
KV_ROWS = 2 * N_KV
MOE_TM = 256
MOE_TN = 1024
PAGES_PER_STEP = 8
VMEM_LIMIT = 56 * 1024 * 1024


def _dot(a, b):
    return jnp.dot(a, b, preferred_element_type=F32)


def _dot_nt(a, b):
    return lax.dot_general(a, b, (((1,), (1,)), ((), ())), preferred_element_type=F32)


def _iota(shape, dim):
    return lax.broadcasted_iota(I32, shape, dim)


def _rms(x, w):
    return x * lax.rsqrt(jnp.mean(x * x, axis=-1, keepdims=True) + EPS) * w


def _masked_softmax_rows(s, mask):
    s = jnp.where(mask, s, NEG)
    e = jnp.where(mask, jnp.exp(s - jnp.max(s, axis=-1, keepdims=True)), 0.0)
    return e / jnp.maximum(jnp.sum(e, axis=-1, keepdims=True), 1e-30)


def _params(sem, vmem=None):
    return pltpu.CompilerParams(dimension_semantics=sem, vmem_limit_bytes=vmem or VMEM_LIMIT)


def _inproj_body(x_ref, nw_ref, wq_ref, wkv_ref, wg_ref, wxy_ref, qnw_ref, knw_ref,
                 q_ref, cmp_ref, sel_ref, win_ref, kvb_ref, gate_ref, xr_ref, yr_ref):
    x = x_ref[...]
    h = _rms(x, nw_ref[...]).astype(BF16)
    q = _dot(h, wq_ref[...])
    qnw = qnw_ref[...]
    nqb = q_ref.shape[0]
    for hd in range(N_HEADS):
        qn = (_rms(q[:, hd * DH:(hd + 1) * DH], qnw) * Q_SCALE).astype(BF16)
        for b in range(nqb):
            q_ref[b, hd] = qn[b * QB:(b + 1) * QB]
    kv = _dot(h, wkv_ref[...])
    tm = x.shape[0]
    outs = (cmp_ref, sel_ref, win_ref)
    for br in range(3):
        knw = knw_ref[br:br + 1, :]
        for g in range(N_KV):
            c0 = br * 4 * DH + g * DH
            kn = _rms(kv[:, c0:c0 + DH], knw)
            v = kv[:, c0 + 2 * DH:c0 + 3 * DH]
            outs[br][pl.ds(g, tm, KV_ROWS), :] = kn
            outs[br][pl.ds(2 + g, tm, KV_ROWS), :] = v
            kvb_ref[:, c0:c0 + DH] = kn.astype(BF16)
            kvb_ref[:, c0 + 2 * DH:c0 + 3 * DH] = v.astype(BF16)
    gate_ref[...] = jax.nn.sigmoid(_dot(h, wg_ref[...]))
    xy = _dot(h, wxy_ref[...])
    d_rec = xr_ref.shape[1]
    xr_ref[...] = xy[:, :d_rec]
    yr_ref[...] = xy[:, d_rec:]


def _inproj(x2, nw, wq, wkv, wg, wxy, qnw, knw, tm):
    n, d = x2.shape
    d_rec = wxy.shape[1] // 2
    nqb = tm // QB
    row = lambda i: (i, 0)
    const = lambda i: (0, 0)
    wspec = lambda a: pl.BlockSpec(a.shape, const, pipeline_mode=pl.Buffered(1))
    return pl.pallas_call(
        _inproj_body,
        grid=(n // tm,),
        in_specs=[pl.BlockSpec((tm, d), row), wspec(nw), wspec(wq), wspec(wkv), wspec(wg), wspec(wxy),
                  wspec(qnw), wspec(knw)],
        out_specs=[pl.BlockSpec((nqb, N_HEADS, QB, DH), lambda i: (i, 0, 0, 0)),
                   pl.BlockSpec((tm * KV_ROWS, DH), row), pl.BlockSpec((tm * KV_ROWS, DH), row),
                   pl.BlockSpec((tm * KV_ROWS, DH), row),
                   pl.BlockSpec((tm, 12 * DH), row), pl.BlockSpec((tm, LANES), row),
                   pl.BlockSpec((tm, d_rec), row), pl.BlockSpec((tm, d_rec), row)],
        out_shape=[jax.ShapeDtypeStruct((n // QB, N_HEADS, QB, DH), BF16),
                   jax.ShapeDtypeStruct((n * KV_ROWS, DH), F32), jax.ShapeDtypeStruct((n * KV_ROWS, DH), F32),
                   jax.ShapeDtypeStruct((n * KV_ROWS, DH), F32), jax.ShapeDtypeStruct((n, 12 * DH), BF16),
                   jax.ShapeDtypeStruct((n, LANES), F32),
                   jax.ShapeDtypeStruct((n, d_rec), F32), jax.ShapeDtypeStruct((n, d_rec), F32)],
        compiler_params=_params(("parallel",)),
        name="inproj",
    )(x2, nw, wq, wkv, wg, wxy, qnw, knw)


def _pool_combo(rows, wl_ref, combo):
    r = rows.shape[0] // (2 * L_CMP)
    x3 = rows.reshape(r, 2 * L_CMP, rows.shape[1])
    even = jnp.sum(x3 * wl_ref[0, combo][None], axis=1)
    odd = jnp.sum(x3 * wl_ref[1, combo][None], axis=1)
    return even.astype(BF16), odd.astype(BF16)


def _pool_body(x_ref, wl_ref, e_ref, o_ref):
    n_tok = x_ref.shape[0] // KV_ROWS
    for combo in range(KV_ROWS):
        even, odd = _pool_combo(x_ref[pl.ds(combo, n_tok, KV_ROWS), :], wl_ref, combo)
        e_ref[:, combo * DH:(combo + 1) * DH] = even
        o_ref[:, combo * DH:(combo + 1) * DH] = odd


def _pool_prompt(cmp4, wl):
    n = cmp4.shape[0] // KV_ROWS
    toks = 1024
    ob = toks // (2 * L_CMP)
    c = KV_ROWS * DH
    return pl.pallas_call(
        _pool_body,
        grid=(n // toks,),
        in_specs=[pl.BlockSpec((toks * KV_ROWS, DH), lambda i: (i, 0)), pl.BlockSpec(wl.shape, lambda i: (0, 0, 0, 0))],
        out_specs=[pl.BlockSpec((ob, c), lambda i: (i, 0)), pl.BlockSpec((ob, c), lambda i: (i, 0))],
        out_shape=[jax.ShapeDtypeStruct((n // (2 * L_CMP), c), BF16)] * 2,
        compiler_params=_params(("parallel",)),
        name="pool_prompt",
    )(cmp4, wl)


def _page_rows(pages, combo):
    n_tok = pages[0].shape[1] // KV_ROWS
    return jnp.concatenate([p[0, pl.ds(combo, n_tok, KV_ROWS), :] for p in pages], axis=0)


def _pool_pages_body(pt_ref, *refs):
    pages = refs[:PAGES_PER_STEP]
    wl_ref, e_ref, o_ref = refs[PAGES_PER_STEP:]
    for combo in range(KV_ROWS):
        even, odd = _pool_combo(_page_rows(pages, combo), wl_ref, combo)
        e_ref[0, :, combo * DH:(combo + 1) * DH] = even
        o_ref[0, :, combo * DH:(combo + 1) * DH] = odd


def _pool_pages(cache3, page_ids, bsz, wl):
    rows = cache3.shape[1]
    n_pages = page_ids.shape[0] // bsz
    n_steps = n_pages // PAGES_PER_STEP
    ob = PAGES_PER_STEP * (rows // KV_ROWS) // (2 * L_CMP)
    c = KV_ROWS * DH
    page_spec = lambda i: pl.BlockSpec((1, rows, DH), lambda b, s, pt: (pt[b * n_pages + s * PAGES_PER_STEP + i], 0, 0))
    grid_spec = pltpu.PrefetchScalarGridSpec(
        num_scalar_prefetch=1,
        grid=(bsz, n_steps),
        in_specs=[page_spec(i) for i in range(PAGES_PER_STEP)] + [pl.BlockSpec(wl.shape, lambda b, s, pt: (0, 0, 0, 0))],
        out_specs=[pl.BlockSpec((1, ob, c), lambda b, s, pt: (b, s, 0))] * 2,
    )
    return pl.pallas_call(
        _pool_pages_body,
        grid_spec=grid_spec,
        out_shape=[jax.ShapeDtypeStruct((bsz, n_steps * ob, c), BF16)] * 2,
        compiler_params=_params(("parallel", "parallel")),
        name="pool_pages",
    )(page_ids, *([cache3] * PAGES_PER_STEP), wl)


def _select_blocks_cols(score, blk, n_pick, n_blk):
    sel = jnp.zeros(score.shape, F32)
    for _ in range(n_pick):
        m = jnp.max(score, axis=0, keepdims=True)
        idx = jnp.min(jnp.where(score == m, blk, n_blk), axis=0, keepdims=True)
        hit = blk == idx
        sel = jnp.where(hit, 1.0, sel)
        score = jnp.where(hit, -jnp.inf, score)
    return sel


def _pattn_body(slopes_ref, q_ref, ksel_ref, vsel_ref, kwin_ref, vwin_ref, kce_ref, kco_ref, vce_ref, vco_ref,
                gt_ref, o_ref, m_scr, l_scr, acc_scr):
    g = pl.program_id(1)
    i = pl.program_id(2)
    rows = GQA * QB
    q = q_ref[0].reshape(rows, DH)
    row = _iota((rows, 1), 0)
    qpos = i * QB + (row & (QB - 1))
    r_of_row = row >> 7
    slope = jnp.zeros((rows, 1), F32)
    for r in range(GQA):
        slope = jnp.where(r_of_row == r, slopes_ref[g * GQA + r], slope)

    kc = jnp.concatenate([kce_ref[...], kco_ref[...]], axis=0)
    vc = jnp.concatenate([vce_ref[...], vco_ref[...]], axis=0)
    nb = kc.shape[0]
    half = nb // 2
    lane = _iota((rows, nb), 1)
    blk_c = jnp.where(lane < half, 2 * lane, 2 * (lane - half) + 1)
    dist = qpos - (blk_c * L_CMP + (L_CMP - 1))
    mask = dist >= 0
    s = _dot_nt(q, kc) - slope * dist.astype(F32)
    p = _masked_softmax_rows(s, mask)
    o_c = _dot(p.astype(BF16), vc)
    imp = p[0:QB]
    for r in range(1, GQA):
        imp = imp + p[r * QB:(r + 1) * QB]
    imp_t = imp.T
    pair = imp_t[:half] + imp_t[half:]
    ns = half
    blk = _iota((ns, QB), 0)
    qp = i * QB + _iota((ns, QB), 1)
    cur = qp >> 6
    forced = (blk == 0) | (blk == cur) | (blk == cur - 1)
    score = jnp.where(blk * L_SEL <= qp, jnp.where(forced, FORCE, pair), NEG)
    sel = _select_blocks_cols(score, blk, min(N_SEL, ns), ns)
    sel = jnp.concatenate([sel, jnp.zeros((LANES - ns, QB), F32)], axis=0) if ns < LANES else sel
    sel_q = sel.T.astype(BF16)
    sel4 = jnp.concatenate([sel_q] * GQA, axis=0)

    m_scr[...] = jnp.full(m_scr.shape, NEG, F32)
    l_scr[...] = jnp.zeros(l_scr.shape, F32)
    acc_scr[...] = jnp.zeros(acc_scr.shape, F32)
    ck = 512

    def chunk(c, carry):
        k0 = pl.multiple_of(c * ck, ck)
        kch = ksel_ref[pl.ds(k0, ck), :]
        vch = vsel_ref[pl.ds(k0, ck), :]
        dist_s = qpos - (k0 + _iota((rows, ck), 1))
        expand = jnp.where((_iota((LANES, ck), 1) >> 6) + c * (ck // L_SEL) == _iota((LANES, ck), 0), 1.0, 0.0)
        picked = _dot(sel4, expand.astype(BF16))
        msk = jnp.where(dist_s >= 0, picked, 0.0) > 0.5
        sc = jnp.where(msk, _dot_nt(q, kch) - slope * dist_s.astype(F32), NEG)
        m_old = m_scr[...]
        m_new = jnp.maximum(m_old, jnp.max(sc, axis=-1, keepdims=True))
        alpha = jnp.exp(m_old - m_new)
        e = jnp.where(msk, jnp.exp(sc - m_new), 0.0)
        l_scr[...] = alpha * l_scr[...] + jnp.sum(e, axis=-1, keepdims=True)
        acc_scr[...] = alpha * acc_scr[...] + _dot(e.astype(BF16), vch)
        m_scr[...] = m_new
        return carry

    lax.fori_loop(0, (i >> 2) + 1, chunk, 0)
    o_s = acc_scr[...] / jnp.maximum(l_scr[...], 1e-30)

    span = WINDOW + QB
    start = pl.multiple_of(jnp.maximum(i * QB - WINDOW, 0), QB)
    kw = kwin_ref[pl.ds(start, span), :]
    vw = vwin_ref[pl.ds(start, span), :]
    dist_w = qpos - (start + _iota((rows, span), 1))
    mask_w = jnp.where(dist_w >= 0, dist_w, WINDOW) < WINDOW
    s_w = _dot_nt(q, kw) - slope * dist_w.astype(F32)
    o_w = _dot(_masked_softmax_rows(s_w, mask_w).astype(BF16), vw)

    gt = gt_ref[0, 0]
    for r in range(GQA):
        sl = slice(r * QB, (r + 1) * QB)
        o_ref[0, r] = (gt[:, 3 * r:3 * r + 1] * o_c[sl] + gt[:, 3 * r + 1:3 * r + 2] * o_s[sl]
                       + gt[:, 3 * r + 2:3 * r + 3] * o_w[sl])


def _prompt_attention(slopes, q_blk, kvb, kce, kco, gates_g, bsz, t_len):
    nq = t_len // QB
    nb = t_len // L_CMP
    assert nb % (2 * LANES) == 0 or nb == LANES, "compressed blocks must fill whole lane tiles"
    assert t_len >= WINDOW + QB
    half = nb // 2
    rows = GQA * QB
    kv_spec = lambda col: pl.BlockSpec((t_len, DH), lambda b, g, i, col=col: (b, col + g))
    kc_spec = lambda col: pl.BlockSpec((half, DH), lambda b, g, i, col=col: (b, col + g))
    return pl.pallas_call(
        _pattn_body,
        grid=(bsz, N_KV, nq),
        in_specs=[pl.BlockSpec(memory_space=pltpu.SMEM),
                  pl.BlockSpec((1, GQA, QB, DH), lambda b, g, i: (b * nq + i, g, 0, 0)),
                  kv_spec(4), kv_spec(6), kv_spec(8), kv_spec(10),
                  kc_spec(0), kc_spec(0), kc_spec(2), kc_spec(2),
                  pl.BlockSpec((1, 1, QB, 3 * GQA), lambda b, g, i: (b, g, i, 0))],
        out_specs=pl.BlockSpec((1, GQA, QB, DH), lambda b, g, i: (b * nq + i, g, 0, 0)),
        out_shape=jax.ShapeDtypeStruct((bsz * nq, N_HEADS, QB, DH), F32),
        scratch_shapes=[pltpu.VMEM((rows, 1), F32), pltpu.VMEM((rows, 1), F32), pltpu.VMEM((rows, DH), F32)],
        compiler_params=_params(("parallel", "parallel", "arbitrary")),
        name="prompt_attention",
    )(slopes, q_blk, kvb, kvb, kvb, kvb, kce, kco, kce, kco, gates_g)


def _sample_select_body(slopes_ref, q_ref, kce_ref, kco_ref, oc_ref, sel_ref, *, past):
    rows = GQA * TP
    row = _iota((rows, 1), 0)
    qpos = past + (row & (TP - 1))
    r_of_row = row >> 3
    half = kce_ref.shape[1]
    nb = 2 * half
    for g in range(N_KV):
        slope = jnp.zeros((rows, 1), F32)
        for r in range(GQA):
            slope = jnp.where(r_of_row == r, slopes_ref[g * GQA + r], slope)
        q = q_ref[0, g]
        kc = jnp.concatenate([kce_ref[0, :, g * DH:(g + 1) * DH], kco_ref[0, :, g * DH:(g + 1) * DH]], axis=0)
        vc = jnp.concatenate([kce_ref[0, :, (2 + g) * DH:(3 + g) * DH], kco_ref[0, :, (2 + g) * DH:(3 + g) * DH]], axis=0)
        lane = _iota((rows, nb), 1)
        blk_c = jnp.where(lane < half, 2 * lane, 2 * (lane - half) + 1)
        dist = qpos - (blk_c * L_CMP + (L_CMP - 1))
        s = _dot_nt(q, kc) - slope * dist.astype(F32)
        p = _masked_softmax_rows(s, dist >= 0)
        oc_ref[0, g] = _dot(p.astype(BF16), vc)
        imp = p[0:TP]
        for r in range(1, GQA):
            imp = imp + p[r * TP:(r + 1) * TP]
        pair = imp[:, :half] + imp[:, half:]
        blk = _iota((TP, half), 1)
        qp = past + _iota((TP, half), 0)
        cur = qp >> 6
        forced = (blk == 0) | (blk == cur) | (blk == cur - 1)
        score = jnp.where(blk * L_SEL <= qp, jnp.where(forced, FORCE, pair), NEG)
        sel = jnp.zeros((TP, half), F32)
        for _ in range(N_SEL - 1):
            m = jnp.max(score, axis=1, keepdims=True)
            idx = jnp.min(jnp.where(score == m, blk, half), axis=1, keepdims=True)
            hit = blk == idx
            sel = jnp.where(hit, 1.0, sel)
            score = jnp.where(hit, -jnp.inf, score)
        sel_ref[0, g] = jnp.concatenate([sel] * GQA, axis=0).astype(BF16)


def _sample_select(slopes, q_s, kce, kco, past):
    bsz = q_s.shape[0]
    half = kce.shape[1]
    assert half == LANES, "past selection blocks must fill one lane tile"
    rows = GQA * TP
    return pl.pallas_call(
        functools.partial(_sample_select_body, past=past),
        grid=(bsz,),
        in_specs=[pl.BlockSpec(memory_space=pltpu.SMEM),
                  pl.BlockSpec((1, N_KV, rows, DH), lambda b: (b, 0, 0, 0)),
                  pl.BlockSpec((1, half, 4 * DH), lambda b: (b, 0, 0)),
                  pl.BlockSpec((1, half, 4 * DH), lambda b: (b, 0, 0))],
        out_specs=[pl.BlockSpec((1, N_KV, rows, DH), lambda b: (b, 0, 0, 0)),
                   pl.BlockSpec((1, N_KV, rows, half), lambda b: (b, 0, 0, 0))],
        out_shape=[jax.ShapeDtypeStruct((bsz, N_KV, rows, DH), F32),
                   jax.ShapeDtypeStruct((bsz, N_KV, rows, half), BF16)],
        compiler_params=_params(("parallel",)),
        name="sample_select",
    )(slopes, q_s, kce, kco)


def _online_update(m_ref, l_ref, acc_ref, g, sc, msk, v):
    m_old = m_ref[g]
    m_new = jnp.maximum(m_old, jnp.max(sc, axis=-1, keepdims=True))
    alpha = jnp.exp(m_old - m_new)
    e = jnp.where(msk, jnp.exp(sc - m_new), 0.0)
    l_ref[g] = alpha * l_ref[g] + jnp.sum(e, axis=-1, keepdims=True)
    acc_ref[g] = alpha * acc_ref[g] + _dot(e.astype(BF16), v)
    m_ref[g] = m_new


def _sample_attn_body(pt_ref, slopes_ref, *refs, past, t_new):
    pages = refs[:PAGES_PER_STEP]
    q_ref, sel_ref, new_ref, win_ref, oc_ref, gt_ref, o_ref, m_scr, l_scr, acc_scr = refs[PAGES_PER_STEP:]
    c = pl.program_id(1)
    rows = GQA * TP
    row = _iota((rows, 1), 0)
    qpos = past + (row & (TP - 1))
    r_of_row = row >> 3
    ck = PAGES_PER_STEP * pages[0].shape[1] // KV_ROWS

    @pl.when(c == 0)
    def _():
        m_scr[...] = jnp.full(m_scr.shape, NEG, F32)
        l_scr[...] = jnp.zeros(l_scr.shape, F32)
        acc_scr[...] = jnp.zeros(acc_scr.shape, F32)

    dist_s = qpos - (c * ck + _iota((rows, ck), 1))
    expand = jnp.where((_iota((LANES, ck), 1) >> 6) + c * (ck // L_SEL) == _iota((LANES, ck), 0), 1.0, 0.0).astype(BF16)
    slopes = []
    for g in range(N_KV):
        slope = jnp.zeros((rows, 1), F32)
        for r in range(GQA):
            slope = jnp.where(r_of_row == r, slopes_ref[g * GQA + r], slope)
        slopes.append(slope)
        q = q_ref[0, g]
        kch = _page_rows(pages, g).astype(BF16)
        vch = _page_rows(pages, 2 + g).astype(BF16)
        picked = _dot(sel_ref[0, g], expand)
        msk = jnp.where(dist_s >= 0, picked, 0.0) > 0.5
        sc = jnp.where(msk, _dot_nt(q, kch) - slope * dist_s.astype(F32), NEG)
        _online_update(m_scr, l_scr, acc_scr, g, sc, msk, vch)

    @pl.when(c == pl.num_programs(1) - 1)
    def _():
        col = _iota((rows, TP), 1)
        dist_n = qpos - (past + col)
        mask_n = jnp.where(col < t_new, dist_n, -1) >= 0
        dist_c = qpos - (past - WINDOW + _iota((rows, WINDOW), 1))
        mask_c = jnp.where(dist_c >= 0, dist_c, WINDOW) < WINDOW
        mask_wn = jnp.where(mask_n, dist_n, WINDOW) < WINDOW
        for g in range(N_KV):
            slope = slopes[g]
            q = q_ref[0, g]
            kn = new_ref[0, :, (4 + g) * DH:(5 + g) * DH]
            vn = new_ref[0, :, (6 + g) * DH:(7 + g) * DH]
            sc = jnp.where(mask_n, _dot_nt(q, kn) - slope * dist_n.astype(F32), NEG)
            _online_update(m_scr, l_scr, acc_scr, g, sc, mask_n, vn)
            o_s = acc_scr[g] / jnp.maximum(l_scr[g], 1e-30)
            kwc = win_ref[0, pl.ds(g, WINDOW, KV_ROWS), :].astype(BF16)
            vwc = win_ref[0, pl.ds(2 + g, WINDOW, KV_ROWS), :].astype(BF16)
            kwn = new_ref[0, :, (8 + g) * DH:(9 + g) * DH]
            vwn = new_ref[0, :, (10 + g) * DH:(11 + g) * DH]
            s1 = jnp.where(mask_c, _dot_nt(q, kwc) - slope * dist_c.astype(F32), NEG)
            s2 = jnp.where(mask_wn, _dot_nt(q, kwn) - slope * dist_n.astype(F32), NEG)
            mx = jnp.maximum(jnp.max(s1, axis=-1, keepdims=True), jnp.max(s2, axis=-1, keepdims=True))
            e1 = jnp.where(mask_c, jnp.exp(s1 - mx), 0.0)
            e2 = jnp.where(mask_wn, jnp.exp(s2 - mx), 0.0)
            den = jnp.maximum(jnp.sum(e1, axis=-1, keepdims=True) + jnp.sum(e2, axis=-1, keepdims=True), 1e-30)
            o_w = _dot((e1 / den).astype(BF16), vwc) + _dot((e2 / den).astype(BF16), vwn)
            gt = gt_ref[0, g]
            o_ref[0, g] = gt[:, 0:1] * oc_ref[0, g] + gt[:, 1:2] * o_s + gt[:, 2:3] * o_w


def _sample_attention(slopes, page_ids, cache_sel3, q_s, sel_s, new_kvb, cache_win3, win_base, o_c, gates_s, past,
                      t_new):
    page_rows = cache_sel3.shape[1]
    bsz = q_s.shape[0]
    n_pages = page_ids.shape[0] // bsz
    n_steps = n_pages // PAGES_PER_STEP
    rows = GQA * TP
    assert cache_win3.shape[1] == WINDOW * KV_ROWS and past % L_SEL == 0 and t_new <= TP
    page_spec = lambda i: pl.BlockSpec((1, page_rows, DH),
                                       lambda b, s, pt: (pt[b * n_pages + s * PAGES_PER_STEP + i], 0, 0))
    per_b = lambda shape: pl.BlockSpec((1,) + shape, lambda b, s, pt: (b,) + (0,) * len(shape))
    grid_spec = pltpu.PrefetchScalarGridSpec(
        num_scalar_prefetch=1,
        grid=(bsz, n_steps),
        in_specs=[pl.BlockSpec(memory_space=pltpu.SMEM)] + [page_spec(i) for i in range(PAGES_PER_STEP)]
        + [per_b((N_KV, rows, DH)), per_b((N_KV, rows, LANES)), per_b((TP, 12 * DH)),
           pl.BlockSpec((1, WINDOW * KV_ROWS, DH), lambda b, s, pt: (win_base + b, 0, 0)),
           per_b((N_KV, rows, DH)), per_b((N_KV, rows, 3))],
        out_specs=per_b((N_KV, rows, DH)),
        scratch_shapes=[pltpu.VMEM((N_KV, rows, 1), F32), pltpu.VMEM((N_KV, rows, 1), F32),
                        pltpu.VMEM((N_KV, rows, DH), F32)],
    )
    return pl.pallas_call(
        functools.partial(_sample_attn_body, past=past, t_new=t_new),
        grid_spec=grid_spec,
        out_shape=jax.ShapeDtypeStruct((bsz, N_KV, rows, DH), F32),
        compiler_params=_params(("parallel", "arbitrary")),
        name="sample_attention",
    )(page_ids, slopes, *([cache_sel3] * PAGES_PER_STEP), q_s, sel_s, new_kvb, cache_win3, o_c, gates_s)


def _gelu_tanh(x):
    return 0.5 * x * (1.0 + jnp.tanh(0.7978845608028654 * (x + 0.044715 * (x * x * x))))


def _lru_coeffs(conv, wa_ref, ba_ref, wi_ref, bi_ref, sp_ref):
    cb = conv.astype(BF16)
    n_blk, blk = wa_ref.shape[0], wa_ref.shape[1]
    ra = jnp.concatenate([_dot(cb[:, n * blk:(n + 1) * blk], wa_ref[n]) for n in range(n_blk)], axis=1)
    ri = jnp.concatenate([_dot(cb[:, n * blk:(n + 1) * blk], wi_ref[n]) for n in range(n_blk)], axis=1)
    r = jax.nn.sigmoid(ra + ba_ref[...])
    gi = jax.nn.sigmoid(ri + bi_ref[...])
    log_a = -LRU_C * r * sp_ref[...]
    a = jnp.exp(log_a)
    b = jnp.sqrt(-jnp.tanh(log_a) * (a * a + 1.0)) * (gi * conv)
    return a, b


def _lru_seq_body(xr_ref, yr_ref, cs_ref, h0_ref, cw_ref, cb_ref, wa_ref, ba_ref, wi_ref, bi_ref, sp_ref, onw_ref,
                  rec_ref, hl_ref, xbuf, h_scr):
    k = pl.program_id(1)
    tt = xr_ref.shape[1]
    pad = 8

    @pl.when(k == 0)
    def _():
        xbuf[0:pad, :] = jnp.zeros((pad, xbuf.shape[1]), F32)
        xbuf[pad - (CONV_W - 1):pad, :] = cs_ref[0]
        h_scr[...] = h0_ref[0]

    x = xr_ref[0]
    xbuf[pad:pad + tt, :] = x
    conv = cb_ref[...] + cw_ref[CONV_W - 1:CONV_W, :] * x
    for j in range(CONV_W - 1):
        conv = conv + cw_ref[j:j + 1, :] * xbuf[pad - (CONV_W - 1) + j:pad - (CONV_W - 1) + j + tt, :]
    xbuf[0:pad, :] = x[tt - pad:tt]
    a, b = _lru_coeffs(conv, wa_ref, ba_ref, wi_ref, bi_ref, sp_ref)
    row = _iota((tt, 1), 0)
    s = 1
    while s < tt:
        keep = row >= s
        a_sh = jnp.where(keep, pltpu.roll(a, s, 0), 1.0)
        b_sh = jnp.where(keep, pltpu.roll(b, s, 0), 0.0)
        b = a * b_sh + b
        a = a * a_sh
        s *= 2
    hs = a * h_scr[...] + b
    h_scr[...] = hs[tt - 1:tt]
    hl_ref[0] = hs[tt - 1:tt]
    rec = hs * _gelu_tanh(yr_ref[0])
    rec_ref[0] = _rms(rec, onw_ref[...]).astype(BF16)


def _lru_seq(xr3, yr3, cs, h0, cw, cb, wa, ba, wi, bi, sp, onw, tt):
    bsz, t_len, d = xr3.shape
    seq = pl.BlockSpec((1, tt, d), lambda b, k: (b, k, 0))
    full = lambda a: pl.BlockSpec(a.shape, lambda b, k: (0,) * a.ndim)
    return pl.pallas_call(
        _lru_seq_body,
        grid=(bsz, t_len // tt),
        in_specs=[seq, seq, pl.BlockSpec((1, CONV_W - 1, d), lambda b, k: (b, 0, 0)),
                  pl.BlockSpec((1, 1, d), lambda b, k: (b, 0, 0)),
                  full(cw), full(cb), full(wa), full(ba), full(wi), full(bi), full(sp), full(onw)],
        out_specs=[seq, pl.BlockSpec((1, 1, d), lambda b, k: (b, 0, 0))],
        out_shape=[jax.ShapeDtypeStruct((bsz, t_len, d), BF16), jax.ShapeDtypeStruct((bsz, 1, d), F32)],
        scratch_shapes=[pltpu.VMEM((tt + 8, d), F32), pltpu.VMEM((1, d), F32)],
        compiler_params=_params(("parallel", "arbitrary")),
        name="lru_seq",
    )(xr3, yr3, cs, h0, cw, cb, wa, ba, wi, bi, sp, onw)


def _lru_step_body(xr_ref, yr_ref, cs_ref, h0_ref, cw_ref, cb_ref, wa_ref, ba_ref, wi_ref, bi_ref, sp_ref, onw_ref,
                   rec_ref, hl_ref):
    t_len, bsz = xr_ref.shape[0], xr_ref.shape[1]
    xs = [cs_ref[j] for j in range(CONV_W - 1)] + [xr_ref[t] for t in range(t_len)]
    convs = []
    for t in range(t_len):
        conv = cb_ref[...] + cw_ref[0:1, :] * xs[t]
        for j in range(1, CONV_W):
            conv = conv + cw_ref[j:j + 1, :] * xs[t + j]
        convs.append(conv)
    a, b = _lru_coeffs(jnp.concatenate(convs, axis=0), wa_ref, ba_ref, wi_ref, bi_ref, sp_ref)
    h = h0_ref[...]
    for t in range(t_len):
        h = a[t * bsz:(t + 1) * bsz] * h + b[t * bsz:(t + 1) * bsz]
        rec = h * _gelu_tanh(yr_ref[t])
        rec_ref[t] = _rms(rec, onw_ref[...]).astype(BF16)
    hl_ref[...] = h


def _lru_step(xr_t, yr_t, cs_t, h0, cw, cb, wa, ba, wi, bi, sp, onw):
    t_len, bsz, d = xr_t.shape
    return pl.pallas_call(
        _lru_step_body,
        out_shape=[jax.ShapeDtypeStruct((t_len, bsz, d), BF16), jax.ShapeDtypeStruct((bsz, d), F32)],
        compiler_params=pltpu.CompilerParams(vmem_limit_bytes=VMEM_LIMIT),
        name="lru_step",
    )(xr_t, yr_t, cs_t, h0, cw, cb, wa, ba, wi, bi, sp, onw)


def _outproj_body(attn_ref, rec_ref, x_ref, cnt0_ref, anw_ref, woa_ref, wor_ref, fnw_ref, wr_ref, br_ref,
                  x1_ref, h2_ref, ti_ref, tg_ref, tp_ref, cnt_ref, carry):
    step = pl.program_id(0)
    tm = x_ref.shape[0]

    @pl.when(step == 0)
    def _():
        carry[...] = cnt0_ref[...]

    attn = jnp.concatenate(
        [jnp.concatenate([attn_ref[b, hd] for hd in range(N_HEADS)], axis=1) for b in range(attn_ref.shape[0])], axis=0)
    an = _rms(attn, anw_ref[...]).astype(BF16)
    x1 = x_ref[...] + (_dot(an, woa_ref[...]) + _dot(rec_ref[...], wor_ref[...]))
    x1_ref[...] = x1
    h2 = _rms(x1, fnw_ref[...])
    h2_ref[...] = h2
    lane = _iota((tm, LANES), 1)
    logits = jnp.dot(h2, wr_ref[...], precision=lax.Precision.HIGHEST, preferred_element_type=F32) + br_ref[...]
    lg = jnp.where(lane < N_EXPERTS, logits, -jnp.inf)
    vals, idxs = [], []
    for _ in range(TOP_K):
        m = jnp.max(lg, axis=-1, keepdims=True)
        ix = jnp.min(jnp.where(lg == m, lane, LANES), axis=-1, keepdims=True)
        vals.append(m)
        idxs.append(ix)
        lg = jnp.where(lane == ix, -jnp.inf, lg)
    es = [jnp.exp(v - vals[0]) for v in vals]
    den = es[0]
    for e in es[1:]:
        den = den + e
    onehot = jnp.zeros((tm, LANES), F32)
    for ix in idxs:
        onehot = jnp.where(lane == ix, 1.0, onehot)
    lower = jnp.where(_iota((tm, tm), 0) > _iota((tm, tm), 1), 1.0, 0.0).astype(BF16)
    rank = carry[...] + _dot(lower, onehot.astype(BF16))
    carry[...] = carry[...] + jnp.sum(onehot, axis=0, keepdims=True)
    ti = jnp.zeros((tm, LANES), I32)
    tg = jnp.zeros((tm, LANES), F32)
    tp = jnp.zeros((tm, LANES), I32)
    for k in range(TOP_K):
        pos = jnp.sum(jnp.where(lane == idxs[k], rank, 0.0), axis=-1, keepdims=True).astype(I32)
        ti = jnp.where(lane == k, idxs[k], ti)
        tg = jnp.where(lane == k, es[k] / den, tg)
        tp = jnp.where(lane == k, pos, tp)
    ti_ref[...] = ti
    tg_ref[...] = tg
    tp_ref[...] = tp
    cnt_ref[...] = carry[...]


def _outproj_router(attn_blk, recn, x2, cnt0, anw, woa, wor, fnw, wr, br, tm):
    n, d = x2.shape
    nqb = tm // QB
    row = lambda i: (i, 0)
    wspec = lambda a: pl.BlockSpec(a.shape, lambda i: (0,) * a.ndim, pipeline_mode=pl.Buffered(1))
    lanes_out = lambda dt: jax.ShapeDtypeStruct((n, LANES), dt)
    return pl.pallas_call(
        _outproj_body,
        grid=(n // tm,),
        in_specs=[pl.BlockSpec((nqb, N_HEADS, QB, DH), lambda i: (i, 0, 0, 0)),
                  pl.BlockSpec((tm, recn.shape[1]), row), pl.BlockSpec((tm, d), row),
                  wspec(cnt0), wspec(anw), wspec(woa), wspec(wor), wspec(fnw), wspec(wr), wspec(br)],
        out_specs=[pl.BlockSpec((tm, d), row), pl.BlockSpec((tm, d), row), pl.BlockSpec((tm, LANES), row),
                   pl.BlockSpec((tm, LANES), row), pl.BlockSpec((tm, LANES), row), pl.BlockSpec((1, LANES), lambda i: (0, 0))],
        out_shape=[jax.ShapeDtypeStruct((n, d), F32), jax.ShapeDtypeStruct((n, d), F32),
                   lanes_out(I32), lanes_out(F32), lanes_out(I32), jax.ShapeDtypeStruct((1, LANES), F32)],
        scratch_shapes=[pltpu.VMEM((1, LANES), F32)],
        compiler_params=_params(("arbitrary",)),
        name="outproj_router",
    )(attn_blk, recn, x2, cnt0, anw, woa, wor, fnw, wr, br)


ISSUE_UNROLL = 8


def _gather_body(rowtok_ref, nused_ref, h2_hbm, out_ref, buf, sem):
    j = pl.program_id(0)
    tm = buf.shape[1]
    n_used = nused_ref[0]

    def issue_block(blk, slot):
        def body(it, c):
            for u in range(ISSUE_UNROLL):
                r = it * ISSUE_UNROLL + u
                tok = rowtok_ref[blk * tm + r]
                pltpu.make_async_copy(h2_hbm.at[pl.ds(tok, 1), :], buf.at[slot, pl.ds(r, 1), :], sem.at[slot]).start()
            return c

        lax.fori_loop(0, tm // ISSUE_UNROLL, body, 0)

    @pl.when(j == 0)
    def _():
        issue_block(0, 0)

    @pl.when(j + 1 < n_used)
    def _():
        issue_block(j + 1, (j + 1) % 2)

    @pl.when(j < n_used)
    def _():
        slot = j % 2
        pltpu.make_async_copy(h2_hbm.at[pl.ds(0, tm), :], buf.at[slot], sem.at[slot]).wait()
        out_ref[...] = buf[slot].astype(BF16)

    @pl.when(j >= n_used)
    def _():
        out_ref[...] = jnp.zeros(out_ref.shape, BF16)


def _gather_rows(row_tok, n_used, h2, n_blocks):
    n, d = h2.shape
    tm = MOE_TM
    grid_spec = pltpu.PrefetchScalarGridSpec(
        num_scalar_prefetch=2,
        grid=(n_blocks,),
        in_specs=[pl.BlockSpec(memory_space=pl.ANY)],
        out_specs=pl.BlockSpec((tm, d), lambda j, rt, nu: (j, 0)),
        scratch_shapes=[pltpu.VMEM((2, tm, d), F32), pltpu.SemaphoreType.DMA((2,))],
    )
    return pl.pallas_call(
        _gather_body,
        grid_spec=grid_spec,
        out_shape=jax.ShapeDtypeStruct((n_blocks * tm, d), BF16),
        compiler_params=_params(("arbitrary",)),
        name="moe_gather",
    )(row_tok, n_used, h2)


def _expert_changed(be_ref, j):
    return (j == 0) | (be_ref[j] != be_ref[jnp.maximum(j - 1, 0)])


def _moe_up_body(be_ref, nu_ref, xs_ref, w_ref, b_ref, perm_ref, h_ref, wbf):
    j = pl.program_id(1)
    tn = wbf.shape[1]
    pw = perm_ref.shape[0]

    @pl.when(j < nu_ref[0])
    def _():
        @pl.when(_expert_changed(be_ref, j))
        def _():
            for c in range(tn // pw):
                w = w_ref[0, :, c * pw:(c + 1) * pw].astype(BF16)
                wbf[:, c * pw:(c + 1) * pw] = _dot(w, perm_ref[...]).astype(BF16)

        gu = _dot(xs_ref[...], wbf[...]) + b_ref[0]
        hw = pw // 2
        for c in range(tn // pw):
            glu = jnp.minimum(gu[:, c * pw:c * pw + hw], SWIGLU_LIMIT)
            lin = jnp.clip(gu[:, c * pw + hw:(c + 1) * pw], -SWIGLU_LIMIT, SWIGLU_LIMIT)
            h_ref[:, c * hw:(c + 1) * hw] = (glu * jax.nn.sigmoid(SWIGLU_ALPHA * glu) * (lin + 1.0)).astype(BF16)

    @pl.when(j >= nu_ref[0])
    def _():
        h_ref[...] = jnp.zeros(h_ref.shape, BF16)


def _moe_up(block_e, n_used, xs, w_gu, b_gu, perm):
    n_rows, d = xs.shape
    n_e, _, f2 = w_gu.shape
    tm, tn = MOE_TM, MOE_TN
    n_blocks = n_rows // tm
    blk = lambda j, nu: jnp.minimum(j, nu[0] - 1)
    grid_spec = pltpu.PrefetchScalarGridSpec(
        num_scalar_prefetch=2,
        grid=(f2 // tn, n_blocks),
        in_specs=[pl.BlockSpec((tm, d), lambda n, j, be, nu: (blk(j, nu), 0)),
                  pl.BlockSpec((1, d, tn), lambda n, j, be, nu: (be[blk(j, nu)], 0, n)),
                  pl.BlockSpec((1, 1, tn), lambda n, j, be, nu: (be[blk(j, nu)], 0, n)),
                  pl.BlockSpec(perm.shape, lambda n, j, be, nu: (0, 0))],
        out_specs=pl.BlockSpec((tm, tn // 2), lambda n, j, be, nu: (j, n)),
        scratch_shapes=[pltpu.VMEM((d, tn), BF16)],
    )
    return pl.pallas_call(
        _moe_up_body,
        grid_spec=grid_spec,
        out_shape=jax.ShapeDtypeStruct((n_rows, f2 // 2), BF16),
        compiler_params=_params(("arbitrary", "arbitrary")),
        name="moe_up",
    )(block_e, n_used, xs, w_gu, b_gu, perm)


def _moe_down_body(be_ref, nu_ref, h_ref, w_ref, b_ref, y_ref, wbf):
    j = pl.program_id(1)
    f = wbf.shape[0]
    rc = 512

    @pl.when(j < nu_ref[0])
    def _():
        @pl.when(_expert_changed(be_ref, j))
        def _():
            for c in range(f // rc):
                wbf[c * rc:(c + 1) * rc, :] = w_ref[0, c * rc:(c + 1) * rc, :].astype(BF16)

        y_ref[...] = _dot(h_ref[...], wbf[...]) + b_ref[0]

    @pl.when(j >= nu_ref[0])
    def _():
        y_ref[...] = jnp.zeros(y_ref.shape, F32)


def _moe_down(block_e, n_used, h, w_d, b_d):
    n_rows, f = h.shape
    d = w_d.shape[2]
    tm, tn = MOE_TM, MOE_TN
    n_blocks = n_rows // tm
    blk = lambda j, nu: jnp.minimum(j, nu[0] - 1)
    grid_spec = pltpu.PrefetchScalarGridSpec(
        num_scalar_prefetch=2,
        grid=(d // tn, n_blocks),
        in_specs=[pl.BlockSpec((tm, f), lambda n, j, be, nu: (blk(j, nu), 0)),
                  pl.BlockSpec((1, f, tn), lambda n, j, be, nu: (be[blk(j, nu)], 0, n)),
                  pl.BlockSpec((1, 1, tn), lambda n, j, be, nu: (be[blk(j, nu)], 0, n))],
        out_specs=pl.BlockSpec((tm, tn), lambda n, j, be, nu: (j, n)),
        scratch_shapes=[pltpu.VMEM((f, tn), BF16)],
    )
    return pl.pallas_call(
        _moe_down_body,
        grid_spec=grid_spec,
        out_shape=jax.ShapeDtypeStruct((n_rows, d), F32),
        compiler_params=_params(("arbitrary", "arbitrary")),
        name="moe_down",
    )(block_e, n_used, h, w_d, b_d)


def _combine_body(dest_hbm, gate_ref, x1_ref, y_hbm, outp_ref, outs_ref, dsm, buf, sem_d, sem, *, np_blocks):
    j = pl.program_id(0)
    n = pl.num_programs(0)
    tk = x1_ref.shape[0]
    toks = ISSUE_UNROLL // TOP_K

    def idx_copy(step, slot):
        return pltpu.make_async_copy(dest_hbm.at[step, 0], dsm.at[slot], sem_d.at[slot])

    def issue_rows(slot):
        def body(it, c):
            for u in range(toks):
                t = it * toks + u
                for k in range(TOP_K):
                    src = dsm[slot, t * TOP_K + k]
                    pltpu.make_async_copy(y_hbm.at[pl.ds(src, 1), :], buf.at[slot, k, pl.ds(t, 1), :],
                                          sem.at[slot]).start()
            return c

        lax.fori_loop(0, tk // toks, body, 0)

    @pl.when(j == 0)
    def _():
        idx_copy(0, 0).start()
        idx_copy(0, 0).wait()
        issue_rows(0)

        @pl.when(1 < n)
        def _():
            idx_copy(1, 1).start()

    @pl.when(j + 1 < n)
    def _():
        slot = (j + 1) % 2
        idx_copy(j + 1, slot).wait()
        issue_rows(slot)

    @pl.when(j + 2 < n)
    def _():
        idx_copy(j + 2, j % 2).start()

    slot = j % 2
    for k in range(TOP_K):
        pltpu.make_async_copy(y_hbm.at[pl.ds(0, tk), :], buf.at[slot, k], sem.at[slot]).wait()
    g = gate_ref[...]
    acc = g[:, 0:1] * buf[slot, 0]
    for k in range(1, TOP_K):
        acc = acc + g[:, k:k + 1] * buf[slot, k]
    res = x1_ref[...] + acc

    @pl.when(j < np_blocks)
    def _():
        outp_ref[...] = res

    @pl.when(j >= np_blocks)
    def _():
        outs_ref[...] = res


def _combine(dest3, gates, x1, y, n_prompt):
    n, d = x1.shape
    tk = dest3.shape[2] // TOP_K
    np_blocks = n_prompt // tk
    row = lambda j: (j, 0)
    return pl.pallas_call(
        functools.partial(_combine_body, np_blocks=np_blocks),
        grid=(n // tk,),
        in_specs=[pl.BlockSpec(memory_space=pl.ANY), pl.BlockSpec((tk, LANES), row), pl.BlockSpec((tk, d), row),
                  pl.BlockSpec(memory_space=pl.ANY)],
        out_specs=[pl.BlockSpec((tk, d), lambda j: (jnp.minimum(j, np_blocks - 1), 0)),
                   pl.BlockSpec((tk, d), lambda j: (jnp.maximum(j - np_blocks, 0), 0))],
        out_shape=[jax.ShapeDtypeStruct((n_prompt, d), F32), jax.ShapeDtypeStruct((n - n_prompt, d), F32)],
        scratch_shapes=[pltpu.SMEM((2, tk * TOP_K), I32), pltpu.VMEM((2, TOP_K, tk, d), F32),
                        pltpu.SemaphoreType.DMA((2,)), pltpu.SemaphoreType.DMA((2,))],
        compiler_params=_params(("arbitrary",)),
        name="moe_combine",
    )(dest3, gates, x1, y)


def _moe(h2, x1, topi, topg, topp, counts, w_gu, b_gu, w_d, b_d, n_prompt):
    n, d = h2.shape
    tm = MOE_TM
    n_asg = n * TOP_K
    n_blocks = n_asg // tm + N_EXPERTS
    padded = (counts + tm - 1) // tm * tm
    pad_end = jnp.cumsum(padded)
    pad_start = pad_end - padded
    dest = pad_start[topi[:, :TOP_K]] + topp[:, :TOP_K]
    tok = jnp.broadcast_to(jnp.arange(n, dtype=I32)[:, None], (n, TOP_K))
    row_tok = jnp.zeros((n_blocks * tm,), I32).at[dest.reshape(-1)].set(tok.reshape(-1))
    blk_start = jnp.arange(n_blocks, dtype=I32) * tm
    block_e = jnp.minimum(jnp.sum((pad_end[None, :] <= blk_start[:, None]).astype(I32), axis=1), N_EXPERTS - 1)
    n_used = (pad_end[-1:] // tm).astype(I32)
    f2 = w_gu.shape[2]
    pw = 2 * LANES
    src = jnp.arange(pw)
    perm = (jnp.arange(pw)[:, None] == jnp.where(src < LANES, 2 * src, 2 * (src - LANES) + 1)[None, :]).astype(BF16)
    b_gu_p = b_gu.reshape(N_EXPERTS, f2 // pw, LANES, 2).transpose(0, 1, 3, 2).reshape(N_EXPERTS, 1, f2)
    xs = _gather_rows(row_tok, n_used, h2, n_blocks)
    h = _moe_up(block_e, n_used, xs, w_gu, b_gu_p, perm)
    y = _moe_down(block_e, n_used, h, w_d, b_d.reshape(N_EXPERTS, 1, d))
    tk = 128
    dest3 = dest.astype(I32).reshape(n // tk, 1, tk * TOP_K)
    return _combine(dest3, topg, x1, y, n_prompt)


def _layer(layer, xp, xs, cache_cmp_all, cache_sel_all, cache_win_all, state_conv, state_h, page_table, w):
    (norm_mix_w, w_in, q_norm_w, k_norm_w, w_cmp_k, w_cmp_v, conv_w, conv_b, w_gate_a, b_gate_a, w_gate_i, b_gate_i,
     lru_lambda, out_norm_attn, out_norm_rec, w_out, norm_ffn_w, w_router, b_router, w_gate_up, b_gate_up, w_down,
     b_down) = w
    bp, tp, d = xp.shape
    bs, ts, _ = xs.shape
    d_rec = d - D_ATTN
    past = page_table.shape[1] * cache_cmp_all.shape[2]
    assert bs * ts == QB and ts <= TP

    o1, o2 = D_ATTN, D_ATTN + 12 * DH
    o3 = o2 + 3 * N_HEADS
    wq = w_in[:, :o1].astype(BF16)
    wkv = w_in[:, o1:o2].astype(BF16)
    wg = jnp.pad(w_in[:, o2:o3], ((0, 0), (0, LANES - 3 * N_HEADS))).astype(BF16)
    wxy = w_in[:, o3:].astype(BF16)
    row2 = lambda v: v.reshape(1, -1)
    slopes = jnp.exp2(-8.0 * jnp.arange(1, N_HEADS + 1, dtype=F32) / N_HEADS)
    w4 = jnp.concatenate([w_cmp_k, w_cmp_v], axis=0)
    z4 = jnp.zeros_like(w4)
    wl = jnp.stack([jnp.concatenate([w4, z4], axis=1), jnp.concatenate([z4, w4], axis=1)])
    wl = jnp.broadcast_to(wl[..., None], wl.shape + (DH,))
    sp = row2(jax.nn.softplus(-lru_lambda.astype(F32)))
    wa, wi = w_gate_a.astype(BF16), w_gate_i.astype(BF16)
    woa, wor = w_out[:D_ATTN].astype(BF16), w_out[D_ATTN:].astype(BF16)
    wr = jnp.pad(w_router, ((0, 0), (0, LANES - N_EXPERTS)))
    br = row2(jnp.pad(b_router, (0, LANES - N_EXPERTS)))
    mix = (row2(norm_mix_w), wq, wkv, wg, wxy, row2(q_norm_w), k_norm_w)
    lru_w = (conv_w, row2(conv_b), wa, row2(b_gate_a), wi, row2(b_gate_i), sp, row2(out_norm_rec))

    np_tok = bp * tp
    q_p, cmp_p, sel_p, win_p, kvb_p, gate_p, xr_p, yr_p = _inproj(xp.reshape(np_tok, d), *mix, tm=256)
    kce_p, kco_p = _pool_prompt(cmp_p, wl)
    gates_g = gate_p[:, :3 * N_HEADS].reshape(bp, tp, N_KV, 3 * GQA).transpose(0, 2, 1, 3)
    attn_p = _prompt_attention(slopes, q_p, kvb_p, kce_p, kco_p, gates_g, bp, tp)
    recn_p, hl_p = _lru_seq(xr_p.reshape(bp, tp, d_rec), yr_p.reshape(bp, tp, d_rec),
                            jnp.zeros((bp, CONV_W - 1, d_rec), F32), jnp.zeros((bp, 1, d_rec), F32), *lru_w, tt=256)
    post = (row2(out_norm_attn), woa, wor, row2(norm_ffn_w), wr, br)
    x1_p, h2_p, ti_p, tg_p, tp_p, cnt_p = _outproj_router(
        attn_p, recn_p.reshape(np_tok, d_rec), xp.reshape(np_tok, d), jnp.zeros((1, LANES), F32), *post, tm=256)

    ns_tok = bs * ts
    q_s, cmp_s, sel_s, win_s, kvb_s, gate_s, xr_s, yr_s = _inproj(xs.reshape(ns_tok, d), *mix, tm=QB)
    n_pool, page = cache_cmp_all.shape[1], cache_cmp_all.shape[2]
    page_view = lambda c: c.reshape(c.shape[0] * n_pool, page * KV_ROWS, DH)
    page_ids = (page_table + layer * n_pool).reshape(-1)
    kce_s, kco_s = _pool_pages(page_view(cache_cmp_all), page_ids, bs, wl)
    q_t = q_s[0].reshape(N_KV, GQA, bs, ts, DH).transpose(2, 0, 1, 3, 4)
    q_t = jnp.pad(q_t, ((0, 0), (0, 0), (0, 0), (0, TP - ts), (0, 0))).reshape(bs, N_KV, GQA * TP, DH)
    oc_s, selm_s = _sample_select(slopes, q_t, kce_s, kco_s, past)
    new_kvb = jnp.pad(kvb_s.reshape(bs, ts, 12 * DH), ((0, 0), (0, TP - ts), (0, 0)))
    g_t = gate_s[:, :3 * N_HEADS].reshape(bs, ts, N_KV, GQA, 3).transpose(0, 2, 3, 1, 4)
    g_t = jnp.pad(g_t, ((0, 0), (0, 0), (0, 0), (0, TP - ts), (0, 0))).reshape(bs, N_KV, GQA * TP, 3)
    win_view = cache_win_all.reshape(cache_win_all.shape[0] * bs, cache_win_all.shape[2] * KV_ROWS, DH)
    attn_s = _sample_attention(slopes, page_ids, page_view(cache_sel_all), q_t, selm_s, new_kvb, win_view, layer * bs,
                               oc_s, g_t, past, ts)
    attn_s = attn_s.reshape(bs, N_KV, GQA, TP, DH)[:, :, :, :ts].transpose(1, 2, 0, 3, 4).reshape(1, N_HEADS, QB, DH)
    tmaj = lambda a: a.reshape(bs, ts, d_rec).transpose(1, 0, 2)
    recn_s, hl_s = _lru_step(tmaj(xr_s), tmaj(yr_s), state_conv.transpose(1, 0, 2), state_h, *lru_w)
    x1_s, h2_s, ti_s, tg_s, tp_s, cnt = _outproj_router(
        attn_s, recn_s.transpose(1, 0, 2).reshape(ns_tok, d_rec), xs.reshape(ns_tok, d), cnt_p, *post, tm=QB)

    cat = lambda a, b: jnp.concatenate([a, b], axis=0)
    out_p, out_s = _moe(cat(h2_p, h2_s), cat(x1_p, x1_s), cat(ti_p, ti_s), cat(tg_p, tg_s), cat(tp_p, tp_s),
                        cnt[0, :N_EXPERTS].astype(I32), w_gate_up, b_gate_up, w_down, b_down, np_tok)
    y_p = out_p.reshape(bp, tp, d)
    y_s = out_s.reshape(bs, ts, d)

    kv5 = lambda a, b, t: a.reshape(b, t, 2, N_KV, DH)
    win_len_p = min(WINDOW, tp)
    st_p = (kv5(cmp_p, bp, tp), kv5(sel_p, bp, tp), kv5(win_p, bp, tp)[:, tp - win_len_p:],
            xr_p.reshape(bp, tp, d_rec)[:, tp - (CONV_W - 1):], hl_p.reshape(bp, d_rec))
    cache_win = cache_win_all[layer]
    win_all = jnp.concatenate([cache_win, kv5(win_s, bs, ts)], axis=1)
    xcat = jnp.concatenate([state_conv, xr_s.reshape(bs, ts, d_rec)], axis=1)
    st_s = (kv5(cmp_s, bs, ts), kv5(sel_s, bs, ts), win_all[:, win_all.shape[1] - cache_win.shape[1]:],
            xcat[:, ts:], hl_s)
    return y_p, y_s, st_p, st_s


def kernel(x_prompt, x_sample, cache_cmp_kv, cache_sel_kv, cache_win_kv, state_conv, state_h, page_table, norm_mix_w, w_in, q_norm_w, k_norm_w, w_cmp_k, w_cmp_v, conv_w, conv_b, w_gate_a, b_gate_a, w_gate_i, b_gate_i, lru_lambda, out_norm_attn, out_norm_rec, w_out, norm_ffn_w, w_router, b_router, w_gate_up, b_gate_up, w_down, b_down):
    depth = w_in.shape[0]
    xp, xs = x_prompt, x_sample
    st_ps, st_ss = [], []
    for l in range(depth):
        w = (norm_mix_w[l], w_in[l], q_norm_w[l], k_norm_w[l], w_cmp_k[l], w_cmp_v[l], conv_w[l], conv_b[l],
             w_gate_a[l], b_gate_a[l], w_gate_i[l], b_gate_i[l], lru_lambda[l], out_norm_attn[l], out_norm_rec[l],
             w_out[l], norm_ffn_w[l], w_router[l], b_router[l], w_gate_up[l], b_gate_up[l], w_down[l], b_down[l])
        xp, xs, st_p, st_s = _layer(l, xp, xs, cache_cmp_kv, cache_sel_kv, cache_win_kv, state_conv[l], state_h[l],
                                    page_table, w)
        st_ps.append(st_p)
        st_ss.append(st_s)
    stack = lambda sts, i: jnp.stack([s[i] for s in sts])
    return (xp, xs) + tuple(stack(st_ps, i) for i in range(5)) + tuple(stack(st_ss, i) for i in range(5))
```

```python
import functools

import jax
import jax.numpy as jnp
from jax import lax
from jax.experimental import pallas as pl
from jax.experimental.pallas import tpu as pltpu

F32 = jnp.float32
BF16 = jnp.bfloat16
I32 = jnp.int32

N_HEADS = 8
N_KV = 2
GQA = N_HEADS // N_KV
DH = 128
D_ATTN = N_HEADS * DH
CONV_W = 4
LRU_C = 8.0
L_CMP = 32
L_SEL = 64
N_SEL = 16
WINDOW = 512
QB = 128
N_EXPERTS = 32
TOP_K = 4
SWIGLU_LIMIT = 7.0
SWIGLU_ALPHA = 1.702
EPS = 1e-6
NEG = -1e30
FORCE = 1e9
Q_SCALE = DH ** -0.5
LANES = 128
TP = 8
KV_ROWS = 2 * N_KV
MOE_TM = 256
MOE_TN = 1024
PAGES_PER_STEP = 8
VMEM_LIMIT = 56 * 1024 * 1024


def _dot(a, b):
    return jnp.dot(a, b, preferred_element_type=F32)


def _dot_nt(a, b):
    return lax.dot_general(a, b, (((1,), (1,)), ((), ())), preferred_element_type=F32)


def _iota(shape, dim):
    return lax.broadcasted_iota(I32, shape, dim)


def _rms(x, w):
    return x * lax.rsqrt(jnp.mean(x * x, axis=-1, keepdims=True) + EPS) * w


def _masked_softmax_rows(s, mask):
    s = jnp.where(mask, s, NEG)
    e = jnp.where(mask, jnp.exp(s - jnp.max(s, axis=-1, keepdims=True)), 0.0)
    return e / jnp.maximum(jnp.sum(e, axis=-1, keepdims=True), 1e-30)


def _params(sem, vmem=None):
    return pltpu.CompilerParams(dimension_semantics=sem, vmem_limit_bytes=vmem or VMEM_LIMIT)


def _inproj_body(x_ref, nw_ref, wq_ref, wkv_ref, wg_ref, wxy_ref, qnw_ref, knw_ref,
                 q_ref, cmp_ref, sel_ref, win_ref, kvb_ref, gate_ref, xr_ref, yr_ref):
    x = x_ref[...]
    h = _rms(x, nw_ref[...]).astype(BF16)
    q = _dot(h, wq_ref[...])
    qnw = qnw_ref[...]
    nqb = q_ref.shape[0]
    for hd in range(N_HEADS):
        qn = (_rms(q[:, hd * DH:(hd + 1) * DH], qnw) * Q_SCALE).astype(BF16)
        for b in range(nqb):
            q_ref[b, hd] = qn[b * QB:(b + 1) * QB]
    kv = _dot(h, wkv_ref[...])
    tm = x.shape[0]
    outs = (cmp_ref, sel_ref, win_ref)
    for br in range(3):
        knw = knw_ref[br:br + 1, :]
        for g in range(N_KV):
            c0 = br * 4 * DH + g * DH
            kn = _rms(kv[:, c0:c0 + DH], knw)
            v = kv[:, c0 + 2 * DH:c0 + 3 * DH]
            outs[br][pl.ds(g, tm, KV_ROWS), :] = kn
            outs[br][pl.ds(2 + g, tm, KV_ROWS), :] = v
            kvb_ref[:, c0:c0 + DH] = kn.astype(BF16)
            kvb_ref[:, c0 + 2 * DH:c0 + 3 * DH] = v.astype(BF16)
    gate_ref[...] = jax.nn.sigmoid(_dot(h, wg_ref[...]))
    xy = _dot(h, wxy_ref[...])
    d_rec = xr_ref.shape[1]
    xr_ref[...] = xy[:, :d_rec]
    yr_ref[...] = xy[:, d_rec:]


def _inproj(x2, nw, wq, wkv, wg, wxy, qnw, knw, tm):
    n, d = x2.shape
    d_rec = wxy.shape[1] // 2
    nqb = tm // QB
    row = lambda i: (i, 0)
    const = lambda i: (0, 0)
    wspec = lambda a: pl.BlockSpec(a.shape, const, pipeline_mode=pl.Buffered(1))
    return pl.pallas_call(
        _inproj_body,
        grid=(n // tm,),
        in_specs=[pl.BlockSpec((tm, d), row), wspec(nw), wspec(wq), wspec(wkv), wspec(wg), wspec(wxy),
                  wspec(qnw), wspec(knw)],
        out_specs=[pl.BlockSpec((nqb, N_HEADS, QB, DH), lambda i: (i, 0, 0, 0)),
                   pl.BlockSpec((tm * KV_ROWS, DH), row), pl.BlockSpec((tm * KV_ROWS, DH), row),
                   pl.BlockSpec((tm * KV_ROWS, DH), row),
                   pl.BlockSpec((tm, 12 * DH), row), pl.BlockSpec((tm, LANES), row),
                   pl.BlockSpec((tm, d_rec), row), pl.BlockSpec((tm, d_rec), row)],
        out_shape=[jax.ShapeDtypeStruct((n // QB, N_HEADS, QB, DH), BF16),
                   jax.ShapeDtypeStruct((n * KV_ROWS, DH), F32), jax.ShapeDtypeStruct((n * KV_ROWS, DH), F32),
                   jax.ShapeDtypeStruct((n * KV_ROWS, DH), F32), jax.ShapeDtypeStruct((n, 12 * DH), BF16),
                   jax.ShapeDtypeStruct((n, LANES), F32),
                   jax.ShapeDtypeStruct((n, d_rec), F32), jax.ShapeDtypeStruct((n, d_rec), F32)],
        compiler_params=_params(("parallel",)),
        name="inproj",
    )(x2, nw, wq, wkv, wg, wxy, qnw, knw)


def _pool_combo(rows, wl_ref, combo):
    r = rows.shape[0] // (2 * L_CMP)
    x3 = rows.reshape(r, 2 * L_CMP, rows.shape[1])
    even = jnp.sum(x3 * wl_ref[0, combo][None], axis=1)
    odd = jnp.sum(x3 * wl_ref[1, combo][None], axis=1)
    return even.astype(BF16), odd.astype(BF16)


def _pool_body(x_ref, wl_ref, e_ref, o_ref):
    n_tok = x_ref.shape[0] // KV_ROWS
    for combo in range(KV_ROWS):
        even, odd = _pool_combo(x_ref[pl.ds(combo, n_tok, KV_ROWS), :], wl_ref, combo)
        e_ref[:, combo * DH:(combo + 1) * DH] = even
        o_ref[:, combo * DH:(combo + 1) * DH] = odd


def _pool_prompt(cmp4, wl):
    n = cmp4.shape[0] // KV_ROWS
    toks = 1024
    ob = toks // (2 * L_CMP)
    c = KV_ROWS * DH
    return pl.pallas_call(
        _pool_body,
        grid=(n // toks,),
        in_specs=[pl.BlockSpec((toks * KV_ROWS, DH), lambda i: (i, 0)), pl.BlockSpec(wl.shape, lambda i: (0, 0, 0, 0))],
        out_specs=[pl.BlockSpec((ob, c), lambda i: (i, 0)), pl.BlockSpec((ob, c), lambda i: (i, 0))],
        out_shape=[jax.ShapeDtypeStruct((n // (2 * L_CMP), c), BF16)] * 2,
        compiler_params=_params(("parallel",)),
        name="pool_prompt",
    )(cmp4, wl)


def _page_rows(pages, combo):
    n_tok = pages[0].shape[1] // KV_ROWS
    return jnp.concatenate([p[0, pl.ds(combo, n_tok, KV_ROWS), :] for p in pages], axis=0)


def _pool_pages_body(pt_ref, *refs):
    pages = refs[:PAGES_PER_STEP]
    wl_ref, e_ref, o_ref = refs[PAGES_PER_STEP:]
    for combo in range(KV_ROWS):
        even, odd = _pool_combo(_page_rows(pages, combo), wl_ref, combo)
        e_ref[0, :, combo * DH:(combo + 1) * DH] = even
        o_ref[0, :, combo * DH:(combo + 1) * DH] = odd


def _pool_pages(cache3, page_ids, bsz, wl):
    rows = cache3.shape[1]
    n_pages = page_ids.shape[0] // bsz
    n_steps = n_pages // PAGES_PER_STEP
    ob = PAGES_PER_STEP * (rows // KV_ROWS) // (2 * L_CMP)
    c = KV_ROWS * DH
    page_spec = lambda i: pl.BlockSpec((1, rows, DH), lambda b, s, pt: (pt[b * n_pages + s * PAGES_PER_STEP + i], 0, 0))
    grid_spec = pltpu.PrefetchScalarGridSpec(
        num_scalar_prefetch=1,
        grid=(bsz, n_steps),
        in_specs=[page_spec(i) for i in range(PAGES_PER_STEP)] + [pl.BlockSpec(wl.shape, lambda b, s, pt: (0, 0, 0, 0))],
        out_specs=[pl.BlockSpec((1, ob, c), lambda b, s, pt: (b, s, 0))] * 2,
    )
    return pl.pallas_call(
        _pool_pages_body,
        grid_spec=grid_spec,
        out_shape=[jax.ShapeDtypeStruct((bsz, n_steps * ob, c), BF16)] * 2,
        compiler_params=_params(("parallel", "parallel")),
        name="pool_pages",
    )(page_ids, *([cache3] * PAGES_PER_STEP), wl)


def _select_blocks_cols(score, blk, n_pick, n_blk):
    sel = jnp.zeros(score.shape, F32)
    for _ in range(n_pick):
        m = jnp.max(score, axis=0, keepdims=True)
        idx = jnp.min(jnp.where(score == m, blk, n_blk), axis=0, keepdims=True)
        hit = blk == idx
        sel = jnp.where(hit, 1.0, sel)
        score = jnp.where(hit, -jnp.inf, score)
    return sel


def _pattn_body(slopes_ref, q_ref, ksel_ref, vsel_ref, kwin_ref, vwin_ref, kce_ref, kco_ref, vce_ref, vco_ref,
                gt_ref, o_ref, m_scr, l_scr, acc_scr):
    g = pl.program_id(1)
    i = pl.program_id(2)
    rows = GQA * QB
    q = q_ref[0].reshape(rows, DH)
    row = _iota((rows, 1), 0)
    qpos = i * QB + (row & (QB - 1))
    r_of_row = row >> 7
    slope = jnp.zeros((rows, 1), F32)
    for r in range(GQA):
        slope = jnp.where(r_of_row == r, slopes_ref[g * GQA + r], slope)

    kc = jnp.concatenate([kce_ref[...], kco_ref[...]], axis=0)
    vc = jnp.concatenate([vce_ref[...], vco_ref[...]], axis=0)
    nb = kc.shape[0]
    half = nb // 2
    lane = _iota((rows, nb), 1)
    blk_c = jnp.where(lane < half, 2 * lane, 2 * (lane - half) + 1)
    dist = qpos - (blk_c * L_CMP + (L_CMP - 1))
    mask = dist >= 0
    s = _dot_nt(q, kc) - slope * dist.astype(F32)
    p = _masked_softmax_rows(s, mask)
    o_c = _dot(p.astype(BF16), vc)
    imp = p[0:QB]
    for r in range(1, GQA):
        imp = imp + p[r * QB:(r + 1) * QB]
    imp_t = imp.T
    pair = imp_t[:half] + imp_t[half:]
    ns = half
    blk = _iota((ns, QB), 0)
    qp = i * QB + _iota((ns, QB), 1)
    cur = qp >> 6
    forced = (blk == 0) | (blk == cur) | (blk == cur - 1)
    score = jnp.where(blk * L_SEL <= qp, jnp.where(forced, FORCE, pair), NEG)
    sel = _select_blocks_cols(score, blk, min(N_SEL, ns), ns)
    sel = jnp.concatenate([sel, jnp.zeros((LANES - ns, QB), F32)], axis=0) if ns < LANES else sel
    sel_q = sel.T.astype(BF16)
    sel4 = jnp.concatenate([sel_q] * GQA, axis=0)

    m_scr[...] = jnp.full(m_scr.shape, NEG, F32)
    l_scr[...] = jnp.zeros(l_scr.shape, F32)
    acc_scr[...] = jnp.zeros(acc_scr.shape, F32)
    ck = 512

    def chunk(c, carry):
        k0 = pl.multiple_of(c * ck, ck)
        kch = ksel_ref[pl.ds(k0, ck), :]
        vch = vsel_ref[pl.ds(k0, ck), :]
        dist_s = qpos - (k0 + _iota((rows, ck), 1))
        expand = jnp.where((_iota((LANES, ck), 1) >> 6) + c * (ck // L_SEL) == _iota((LANES, ck), 0), 1.0, 0.0)
        picked = _dot(sel4, expand.astype(BF16))
        msk = jnp.where(dist_s >= 0, picked, 0.0) > 0.5
        sc = jnp.where(msk, _dot_nt(q, kch) - slope * dist_s.astype(F32), NEG)
        m_old = m_scr[...]
        m_new = jnp.maximum(m_old, jnp.max(sc, axis=-1, keepdims=True))
        alpha = jnp.exp(m_old - m_new)
        e = jnp.where(msk, jnp.exp(sc - m_new), 0.0)
        l_scr[...] = alpha * l_scr[...] + jnp.sum(e, axis=-1, keepdims=True)
        acc_scr[...] = alpha * acc_scr[...] + _dot(e.astype(BF16), vch)
        m_scr[...] = m_new
        return carry

    lax.fori_loop(0, (i >> 2) + 1, chunk, 0)
    o_s = acc_scr[...] / jnp.maximum(l_scr[...], 1e-30)

    span = WINDOW + QB
    start = pl.multiple_of(jnp.maximum(i * QB - WINDOW, 0), QB)
    kw = kwin_ref[pl.ds(start, span), :]
    vw = vwin_ref[pl.ds(start, span), :]
    dist_w = qpos - (start + _iota((rows, span), 1))
    mask_w = jnp.where(dist_w >= 0, dist_w, WINDOW) < WINDOW
    s_w = _dot_nt(q, kw) - slope * dist_w.astype(F32)
    o_w = _dot(_masked_softmax_rows(s_w, mask_w).astype(BF16), vw)

    gt = gt_ref[0, 0]
    for r in range(GQA):
        sl = slice(r * QB, (r + 1) * QB)
        o_ref[0, r] = (gt[:, 3 * r:3 * r + 1] * o_c[sl] + gt[:, 3 * r + 1:3 * r + 2] * o_s[sl]
                       + gt[:, 3 * r + 2:3 * r + 3] * o_w[sl])


def _prompt_attention(slopes, q_blk, kvb, kce, kco, gates_g, bsz, t_len):
    nq = t_len // QB
    nb = t_len // L_CMP
    assert nb % (2 * LANES) == 0 or nb == LANES, "compressed blocks must fill whole lane tiles"
    assert t_len >= WINDOW + QB
    half = nb // 2
    rows = GQA * QB
    kv_spec = lambda col: pl.BlockSpec((t_len, DH), lambda b, g, i, col=col: (b, col + g))
    kc_spec = lambda col: pl.BlockSpec((half, DH), lambda b, g, i, col=col: (b, col + g))
    return pl.pallas_call(
        _pattn_body,
        grid=(bsz, N_KV, nq),
        in_specs=[pl.BlockSpec(memory_space=pltpu.SMEM),
                  pl.BlockSpec((1, GQA, QB, DH), lambda b, g, i: (b * nq + i, g, 0, 0)),
                  kv_spec(4), kv_spec(6), kv_spec(8), kv_spec(10),
                  kc_spec(0), kc_spec(0), kc_spec(2), kc_spec(2),
                  pl.BlockSpec((1, 1, QB, 3 * GQA), lambda b, g, i: (b, g, i, 0))],
        out_specs=pl.BlockSpec((1, GQA, QB, DH), lambda b, g, i: (b * nq + i, g, 0, 0)),
        out_shape=jax.ShapeDtypeStruct((bsz * nq, N_HEADS, QB, DH), F32),
        scratch_shapes=[pltpu.VMEM((rows, 1), F32), pltpu.VMEM((rows, 1), F32), pltpu.VMEM((rows, DH), F32)],
        compiler_params=_params(("parallel", "parallel", "arbitrary")),
        name="prompt_attention",
    )(slopes, q_blk, kvb, kvb, kvb, kvb, kce, kco, kce, kco, gates_g)


def _sample_select_body(slopes_ref, q_ref, kce_ref, kco_ref, oc_ref, sel_ref, *, past):
    rows = GQA * TP
    row = _iota((rows, 1), 0)
    qpos = past + (row & (TP - 1))
    r_of_row = row >> 3
    half = kce_ref.shape[1]
    nb = 2 * half
    for g in range(N_KV):
        slope = jnp.zeros((rows, 1), F32)
        for r in range(GQA):
            slope = jnp.where(r_of_row == r, slopes_ref[g * GQA + r], slope)
        q = q_ref[0, g]
        kc = jnp.concatenate([kce_ref[0, :, g * DH:(g + 1) * DH], kco_ref[0, :, g * DH:(g + 1) * DH]], axis=0)
        vc = jnp.concatenate([kce_ref[0, :, (2 + g) * DH:(3 + g) * DH], kco_ref[0, :, (2 + g) * DH:(3 + g) * DH]], axis=0)
        lane = _iota((rows, nb), 1)
        blk_c = jnp.where(lane < half, 2 * lane, 2 * (lane - half) + 1)
        dist = qpos - (blk_c * L_CMP + (L_CMP - 1))
        s = _dot_nt(q, kc) - slope * dist.astype(F32)
        p = _masked_softmax_rows(s, dist >= 0)
        oc_ref[0, g] = _dot(p.astype(BF16), vc)
        imp = p[0:TP]
        for r in range(1, GQA):
            imp = imp + p[r * TP:(r + 1) * TP]
        pair = imp[:, :half] + imp[:, half:]
        blk = _iota((TP, half), 1)
        qp = past + _iota((TP, half), 0)
        cur = qp >> 6
        forced = (blk == 0) | (blk == cur) | (blk == cur - 1)
        score = jnp.where(blk * L_SEL <= qp, jnp.where(forced, FORCE, pair), NEG)
        sel = jnp.zeros((TP, half), F32)
        for _ in range(N_SEL - 1):
            m = jnp.max(score, axis=1, keepdims=True)
            idx = jnp.min(jnp.where(score == m, blk, half), axis=1, keepdims=True)
            hit = blk == idx
            sel = jnp.where(hit, 1.0, sel)
            score = jnp.where(hit, -jnp.inf, score)
        sel_ref[0, g] = jnp.concatenate([sel] * GQA, axis=0).astype(BF16)


def _sample_select(slopes, q_s, kce, kco, past):
    bsz = q_s.shape[0]
    half = kce.shape[1]
    assert half == LANES, "past selection blocks must fill one lane tile"
    rows = GQA * TP
    return pl.pallas_call(
        functools.partial(_sample_select_body, past=past),
        grid=(bsz,),
        in_specs=[pl.BlockSpec(memory_space=pltpu.SMEM),
                  pl.BlockSpec((1, N_KV, rows, DH), lambda b: (b, 0, 0, 0)),
                  pl.BlockSpec((1, half, 4 * DH), lambda b: (b, 0, 0)),
                  pl.BlockSpec((1, half, 4 * DH), lambda b: (b, 0, 0))],
        out_specs=[pl.BlockSpec((1, N_KV, rows, DH), lambda b: (b, 0, 0, 0)),
                   pl.BlockSpec((1, N_KV, rows, half), lambda b: (b, 0, 0, 0))],
        out_shape=[jax.ShapeDtypeStruct((bsz, N_KV, rows, DH), F32),
                   jax.ShapeDtypeStruct((bsz, N_KV, rows, half), BF16)],
        compiler_params=_params(("parallel",)),
        name="sample_select",
    )(slopes, q_s, kce, kco)


def _online_update(m_ref, l_ref, acc_ref, g, sc, msk, v):
    m_old = m_ref[g]
    m_new = jnp.maximum(m_old, jnp.max(sc, axis=-1, keepdims=True))
    alpha = jnp.exp(m_old - m_new)
    e = jnp.where(msk, jnp.exp(sc - m_new), 0.0)
    l_ref[g] = alpha * l_ref[g] + jnp.sum(e, axis=-1, keepdims=True)
    acc_ref[g] = alpha * acc_ref[g] + _dot(e.astype(BF16), v)
    m_ref[g] = m_new


def _sample_attn_body(pt_ref, slopes_ref, *refs, past, t_new):
    pages = refs[:PAGES_PER_STEP]
    q_ref, sel_ref, new_ref, win_ref, oc_ref, gt_ref, o_ref, m_scr, l_scr, acc_scr = refs[PAGES_PER_STEP:]
    c = pl.program_id(1)
    rows = GQA * TP
    row = _iota((rows, 1), 0)
    qpos = past + (row & (TP - 1))
    r_of_row = row >> 3
    ck = PAGES_PER_STEP * pages[0].shape[1] // KV_ROWS

    @pl.when(c == 0)
    def _():
        m_scr[...] = jnp.full(m_scr.shape, NEG, F32)
        l_scr[...] = jnp.zeros(l_scr.shape, F32)
        acc_scr[...] = jnp.zeros(acc_scr.shape, F32)

    dist_s = qpos - (c * ck + _iota((rows, ck), 1))
    expand = jnp.where((_iota((LANES, ck), 1) >> 6) + c * (ck // L_SEL) == _iota((LANES, ck), 0), 1.0, 0.0).astype(BF16)
    slopes = []
    for g in range(N_KV):
        slope = jnp.zeros((rows, 1), F32)
        for r in range(GQA):
            slope = jnp.where(r_of_row == r, slopes_ref[g * GQA + r], slope)
        slopes.append(slope)
        q = q_ref[0, g]
        kch = _page_rows(pages, g).astype(BF16)
        vch = _page_rows(pages, 2 + g).astype(BF16)
        picked = _dot(sel_ref[0, g], expand)
        msk = jnp.where(dist_s >= 0, picked, 0.0) > 0.5
        sc = jnp.where(msk, _dot_nt(q, kch) - slope * dist_s.astype(F32), NEG)
        _online_update(m_scr, l_scr, acc_scr, g, sc, msk, vch)

    @pl.when(c == pl.num_programs(1) - 1)
    def _():
        col = _iota((rows, TP), 1)
        dist_n = qpos - (past + col)
        mask_n = jnp.where(col < t_new, dist_n, -1) >= 0
        dist_c = qpos - (past - WINDOW + _iota((rows, WINDOW), 1))
        mask_c = jnp.where(dist_c >= 0, dist_c, WINDOW) < WINDOW
        mask_wn = jnp.where(mask_n, dist_n, WINDOW) < WINDOW
        for g in range(N_KV):
            slope = slopes[g]
            q = q_ref[0, g]
            kn = new_ref[0, :, (4 + g) * DH:(5 + g) * DH]
            vn = new_ref[0, :, (6 + g) * DH:(7 + g) * DH]
            sc = jnp.where(mask_n, _dot_nt(q, kn) - slope * dist_n.astype(F32), NEG)
            _online_update(m_scr, l_scr, acc_scr, g, sc, mask_n, vn)
            o_s = acc_scr[g] / jnp.maximum(l_scr[g], 1e-30)
            kwc = win_ref[0, pl.ds(g, WINDOW, KV_ROWS), :].astype(BF16)
            vwc = win_ref[0, pl.ds(2 + g, WINDOW, KV_ROWS), :].astype(BF16)
            kwn = new_ref[0, :, (8 + g) * DH:(9 + g) * DH]
            vwn = new_ref[0, :, (10 + g) * DH:(11 + g) * DH]
            s1 = jnp.where(mask_c, _dot_nt(q, kwc) - slope * dist_c.astype(F32), NEG)
            s2 = jnp.where(mask_wn, _dot_nt(q, kwn) - slope * dist_n.astype(F32), NEG)
            mx = jnp.maximum(jnp.max(s1, axis=-1, keepdims=True), jnp.max(s2, axis=-1, keepdims=True))
            e1 = jnp.where(mask_c, jnp.exp(s1 - mx), 0.0)
            e2 = jnp.where(mask_wn, jnp.exp(s2 - mx), 0.0)
            den = jnp.maximum(jnp.sum(e1, axis=-1, keepdims=True) + jnp.sum(e2, axis=-1, keepdims=True), 1e-30)
            o_w = _dot((e1 / den).astype(BF16), vwc) + _dot((e2 / den).astype(BF16), vwn)
            gt = gt_ref[0, g]
            o_ref[0, g] = gt[:, 0:1] * oc_ref[0, g] + gt[:, 1:2] * o_s + gt[:, 2:3] * o_w


def _sample_attention(slopes, page_ids, cache_sel3, q_s, sel_s, new_kvb, cache_win3, win_base, o_c, gates_s, past,
                      t_new):
    page_rows = cache_sel3.shape[1]
    bsz = q_s.shape[0]
    n_pages = page_ids.shape[0] // bsz
    n_steps = n_pages // PAGES_PER_STEP
    rows = GQA * TP
    assert cache_win3.shape[1] == WINDOW * KV_ROWS and past % L_SEL == 0 and t_new <= TP
    page_spec = lambda i: pl.BlockSpec((1, page_rows, DH),
                                       lambda b, s, pt: (pt[b * n_pages + s * PAGES_PER_STEP + i], 0, 0))
    per_b = lambda shape: pl.BlockSpec((1,) + shape, lambda b, s, pt: (b,) + (0,) * len(shape))
    grid_spec = pltpu.PrefetchScalarGridSpec(
        num_scalar_prefetch=1,
        grid=(bsz, n_steps),
        in_specs=[pl.BlockSpec(memory_space=pltpu.SMEM)] + [page_spec(i) for i in range(PAGES_PER_STEP)]
        + [per_b((N_KV, rows, DH)), per_b((N_KV, rows, LANES)), per_b((TP, 12 * DH)),
           pl.BlockSpec((1, WINDOW * KV_ROWS, DH), lambda b, s, pt: (win_base + b, 0, 0)),
           per_b((N_KV, rows, DH)), per_b((N_KV, rows, 3))],
        out_specs=per_b((N_KV, rows, DH)),
        scratch_shapes=[pltpu.VMEM((N_KV, rows, 1), F32), pltpu.VMEM((N_KV, rows, 1), F32),
                        pltpu.VMEM((N_KV, rows, DH), F32)],
    )
    return pl.pallas_call(
        functools.partial(_sample_attn_body, past=past, t_new=t_new),
        grid_spec=grid_spec,
        out_shape=jax.ShapeDtypeStruct((bsz, N_KV, rows, DH), F32),
        compiler_params=_params(("parallel", "arbitrary")),
        name="sample_attention",
    )(page_ids, slopes, *([cache_sel3] * PAGES_PER_STEP), q_s, sel_s, new_kvb, cache_win3, o_c, gates_s)


def _gelu_tanh(x):
    return 0.5 * x * (1.0 + jnp.tanh(0.7978845608028654 * (x + 0.044715 * (x * x * x))))


def _lru_coeffs(conv, wa_ref, ba_ref, wi_ref, bi_ref, sp_ref):
    cb = conv.astype(BF16)
    n_blk, blk = wa_ref.shape[0], wa_ref.shape[1]
    ra = jnp.concatenate([_dot(cb[:, n * blk:(n + 1) * blk], wa_ref[n]) for n in range(n_blk)], axis=1)
    ri = jnp.concatenate([_dot(cb[:, n * blk:(n + 1) * blk], wi_ref[n]) for n in range(n_blk)], axis=1)
    r = jax.nn.sigmoid(ra + ba_ref[...])
    gi = jax.nn.sigmoid(ri + bi_ref[...])
    log_a = -LRU_C * r * sp_ref[...]
    a = jnp.exp(log_a)
    b = jnp.sqrt(-jnp.tanh(log_a) * (a * a + 1.0)) * (gi * conv)
    return a, b


def _lru_seq_body(xr_ref, yr_ref, cs_ref, h0_ref, cw_ref, cb_ref, wa_ref, ba_ref, wi_ref, bi_ref, sp_ref, onw_ref,
                  rec_ref, hl_ref, xbuf, h_scr):
    k = pl.program_id(1)
    tt = xr_ref.shape[1]
    pad = 8

    @pl.when(k == 0)
    def _():
        xbuf[0:pad, :] = jnp.zeros((pad, xbuf.shape[1]), F32)
        xbuf[pad - (CONV_W - 1):pad, :] = cs_ref[0]
        h_scr[...] = h0_ref[0]

    x = xr_ref[0]
    xbuf[pad:pad + tt, :] = x
    conv = cb_ref[...] + cw_ref[CONV_W - 1:CONV_W, :] * x
    for j in range(CONV_W - 1):
        conv = conv + cw_ref[j:j + 1, :] * xbuf[pad - (CONV_W - 1) + j:pad - (CONV_W - 1) + j + tt, :]
    xbuf[0:pad, :] = x[tt - pad:tt]
    a, b = _lru_coeffs(conv, wa_ref, ba_ref, wi_ref, bi_ref, sp_ref)
    row = _iota((tt, 1), 0)
    s = 1
    while s < tt:
        keep = row >= s
        a_sh = jnp.where(keep, pltpu.roll(a, s, 0), 1.0)
        b_sh = jnp.where(keep, pltpu.roll(b, s, 0), 0.0)
        b = a * b_sh + b
        a = a * a_sh
        s *= 2
    hs = a * h_scr[...] + b
    h_scr[...] = hs[tt - 1:tt]
    hl_ref[0] = hs[tt - 1:tt]
    rec = hs * _gelu_tanh(yr_ref[0])
    rec_ref[0] = _rms(rec, onw_ref[...]).astype(BF16)


def _lru_seq(xr3, yr3, cs, h0, cw, cb, wa, ba, wi, bi, sp, onw, tt):
    bsz, t_len, d = xr3.shape
    seq = pl.BlockSpec((1, tt, d), lambda b, k: (b, k, 0))
    full = lambda a: pl.BlockSpec(a.shape, lambda b, k: (0,) * a.ndim)
    return pl.pallas_call(
        _lru_seq_body,
        grid=(bsz, t_len // tt),
        in_specs=[seq, seq, pl.BlockSpec((1, CONV_W - 1, d), lambda b, k: (b, 0, 0)),
                  pl.BlockSpec((1, 1, d), lambda b, k: (b, 0, 0)),
                  full(cw), full(cb), full(wa), full(ba), full(wi), full(bi), full(sp), full(onw)],
        out_specs=[seq, pl.BlockSpec((1, 1, d), lambda b, k: (b, 0, 0))],
        out_shape=[jax.ShapeDtypeStruct((bsz, t_len, d), BF16), jax.ShapeDtypeStruct((bsz, 1, d), F32)],
        scratch_shapes=[pltpu.VMEM((tt + 8, d), F32), pltpu.VMEM((1, d), F32)],
        compiler_params=_params(("parallel", "arbitrary")),
        name="lru_seq",
    )(xr3, yr3, cs, h0, cw, cb, wa, ba, wi, bi, sp, onw)


def _lru_step_body(xr_ref, yr_ref, cs_ref, h0_ref, cw_ref, cb_ref, wa_ref, ba_ref, wi_ref, bi_ref, sp_ref, onw_ref,
                   rec_ref, hl_ref):
    t_len, bsz = xr_ref.shape[0], xr_ref.shape[1]
    xs = [cs_ref[j] for j in range(CONV_W - 1)] + [xr_ref[t] for t in range(t_len)]
    convs = []
    for t in range(t_len):
        conv = cb_ref[...] + cw_ref[0:1, :] * xs[t]
        for j in range(1, CONV_W):
            conv = conv + cw_ref[j:j + 1, :] * xs[t + j]
        convs.append(conv)
    a, b = _lru_coeffs(jnp.concatenate(convs, axis=0), wa_ref, ba_ref, wi_ref, bi_ref, sp_ref)
    h = h0_ref[...]
    for t in range(t_len):
        h = a[t * bsz:(t + 1) * bsz] * h + b[t * bsz:(t + 1) * bsz]
        rec = h * _gelu_tanh(yr_ref[t])
        rec_ref[t] = _rms(rec, onw_ref[...]).astype(BF16)
    hl_ref[...] = h


def _lru_step(xr_t, yr_t, cs_t, h0, cw, cb, wa, ba, wi, bi, sp, onw):
    t_len, bsz, d = xr_t.shape
    return pl.pallas_call(
        _lru_step_body,
        out_shape=[jax.ShapeDtypeStruct((t_len, bsz, d), BF16), jax.ShapeDtypeStruct((bsz, d), F32)],
        compiler_params=pltpu.CompilerParams(vmem_limit_bytes=VMEM_LIMIT),
        name="lru_step",
    )(xr_t, yr_t, cs_t, h0, cw, cb, wa, ba, wi, bi, sp, onw)


def _outproj_body(attn_ref, rec_ref, x_ref, cnt0_ref, anw_ref, woa_ref, wor_ref, fnw_ref, wr_ref, br_ref,
                  x1_ref, h2_ref, ti_ref, tg_ref, tp_ref, cnt_ref, carry):
    step = pl.program_id(0)
    tm = x_ref.shape[0]

    @pl.when(step == 0)
    def _():
        carry[...] = cnt0_ref[...]

    attn = jnp.concatenate(
        [jnp.concatenate([attn_ref[b, hd] for hd in range(N_HEADS)], axis=1) for b in range(attn_ref.shape[0])], axis=0)
    an = _rms(attn, anw_ref[...]).astype(BF16)
    x1 = x_ref[...] + (_dot(an, woa_ref[...]) + _dot(rec_ref[...], wor_ref[...]))
    x1_ref[...] = x1
    h2 = _rms(x1, fnw_ref[...])
    h2_ref[...] = h2
    lane = _iota((tm, LANES), 1)
    logits = jnp.dot(h2, wr_ref[...], precision=lax.Precision.HIGHEST, preferred_element_type=F32) + br_ref[...]
    lg = jnp.where(lane < N_EXPERTS, logits, -jnp.inf)
    vals, idxs = [], []
    for _ in range(TOP_K):
        m = jnp.max(lg, axis=-1, keepdims=True)
        ix = jnp.min(jnp.where(lg == m, lane, LANES), axis=-1, keepdims=True)
        vals.append(m)
        idxs.append(ix)
        lg = jnp.where(lane == ix, -jnp.inf, lg)
    es = [jnp.exp(v - vals[0]) for v in vals]
    den = es[0]
    for e in es[1:]:
        den = den + e
    onehot = jnp.zeros((tm, LANES), F32)
    for ix in idxs:
        onehot = jnp.where(lane == ix, 1.0, onehot)
    lower = jnp.where(_iota((tm, tm), 0) > _iota((tm, tm), 1), 1.0, 0.0).astype(BF16)
    rank = carry[...] + _dot(lower, onehot.astype(BF16))
    carry[...] = carry[...] + jnp.sum(onehot, axis=0, keepdims=True)
    ti = jnp.zeros((tm, LANES), I32)
    tg = jnp.zeros((tm, LANES), F32)
    tp = jnp.zeros((tm, LANES), I32)
    for k in range(TOP_K):
        pos = jnp.sum(jnp.where(lane == idxs[k], rank, 0.0), axis=-1, keepdims=True).astype(I32)
        ti = jnp.where(lane == k, idxs[k], ti)
        tg = jnp.where(lane == k, es[k] / den, tg)
        tp = jnp.where(lane == k, pos, tp)
    ti_ref[...] = ti
    tg_ref[...] = tg
    tp_ref[...] = tp
    cnt_ref[...] = carry[...]


def _outproj_router(attn_blk, recn, x2, cnt0, anw, woa, wor, fnw, wr, br, tm):
    n, d = x2.shape
    nqb = tm // QB
    row = lambda i: (i, 0)
    wspec = lambda a: pl.BlockSpec(a.shape, lambda i: (0,) * a.ndim, pipeline_mode=pl.Buffered(1))
    lanes_out = lambda dt: jax.ShapeDtypeStruct((n, LANES), dt)
    return pl.pallas_call(
        _outproj_body,
        grid=(n // tm,),
        in_specs=[pl.BlockSpec((nqb, N_HEADS, QB, DH), lambda i: (i, 0, 0, 0)),
                  pl.BlockSpec((tm, recn.shape[1]), row), pl.BlockSpec((tm, d), row),
                  wspec(cnt0), wspec(anw), wspec(woa), wspec(wor), wspec(fnw), wspec(wr), wspec(br)],
        out_specs=[pl.BlockSpec((tm, d), row), pl.BlockSpec((tm, d), row), pl.BlockSpec((tm, LANES), row),
                   pl.BlockSpec((tm, LANES), row), pl.BlockSpec((tm, LANES), row), pl.BlockSpec((1, LANES), lambda i: (0, 0))],
        out_shape=[jax.ShapeDtypeStruct((n, d), F32), jax.ShapeDtypeStruct((n, d), F32),
                   lanes_out(I32), lanes_out(F32), lanes_out(I32), jax.ShapeDtypeStruct((1, LANES), F32)],
        scratch_shapes=[pltpu.VMEM((1, LANES), F32)],
        compiler_params=_params(("arbitrary",)),
        name="outproj_router",
    )(attn_blk, recn, x2, cnt0, anw, woa, wor, fnw, wr, br)


ISSUE_UNROLL = 8


def _gather_body(rowtok_ref, nused_ref, h2_hbm, out_ref, buf, sem):
    j = pl.program_id(0)
    tm = buf.shape[1]
    n_used = nused_ref[0]

    def issue_block(blk, slot):
        def body(it, c):
            for u in range(ISSUE_UNROLL):
                r = it * ISSUE_UNROLL + u
                tok = rowtok_ref[blk * tm + r]
                pltpu.make_async_copy(h2_hbm.at[pl.ds(tok, 1), :], buf.at[slot, pl.ds(r, 1), :], sem.at[slot]).start()
            return c

        lax.fori_loop(0, tm // ISSUE_UNROLL, body, 0)

    @pl.when((j == 0) & (n_used > 0))
    def _():
        issue_block(0, 0)

    @pl.when(j + 1 < n_used)
    def _():
        issue_block(j + 1, (j + 1) % 2)

    @pl.when(j < n_used)
    def _():
        slot = j % 2
        pltpu.make_async_copy(h2_hbm.at[pl.ds(0, tm), :], buf.at[slot], sem.at[slot]).wait()
        out_ref[...] = buf[slot].astype(BF16)

    @pl.when(j >= n_used)
    def _():
        out_ref[...] = jnp.zeros(out_ref.shape, BF16)


def _gather_rows(row_tok, n_used, h2, n_blocks):
    n, d = h2.shape
    tm = MOE_TM
    grid_spec = pltpu.PrefetchScalarGridSpec(
        num_scalar_prefetch=2,
        grid=(n_blocks,),
        in_specs=[pl.BlockSpec(memory_space=pl.ANY)],
        out_specs=pl.BlockSpec((tm, d), lambda j, rt, nu: (j, 0)),
        scratch_shapes=[pltpu.VMEM((2, tm, d), F32), pltpu.SemaphoreType.DMA((2,))],
    )
    return pl.pallas_call(
        _gather_body,
        grid_spec=grid_spec,
        out_shape=jax.ShapeDtypeStruct((n_blocks * tm, d), BF16),
        compiler_params=_params(("arbitrary",)),
        name="moe_gather",
    )(row_tok, n_used, h2)


def _expert_blocks(first_row, n_blk, n_used, n_blocks, x_hbm, out_hbm, col0, xbuf, obuf, xsem, osem, compute):
    tm = xbuf.shape[1]
    tc = obuf.shape[2]

    def x_copy(b, slot):
        r = pl.multiple_of(first_row + b * tm, tm)
        return pltpu.make_async_copy(x_hbm.at[pl.ds(r, tm), :], xbuf.at[slot], xsem.at[slot])

    def o_copy(r, slot):
        return pltpu.make_async_copy(obuf.at[slot], out_hbm.at[pl.ds(pl.multiple_of(r, tm), tm), pl.ds(col0, tc)],
                                     osem.at[slot])

    @pl.when(n_blk > 0)
    def _():
        x_copy(0, 0).start()

        def body(b, c):
            slot = b % 2

            @pl.when(b + 1 < n_blk)
            def _():
                x_copy(b + 1, 1 - slot).start()

            x_copy(b, slot).wait()

            @pl.when(b >= 2)
            def _():
                o_copy(first_row + (b - 2) * tm, slot).wait()

            obuf[slot] = compute(xbuf[slot])
            o_copy(first_row + b * tm, slot).start()
            return c

        lax.fori_loop(0, n_blk, body, 0)

        @pl.when(n_blk >= 2)
        def _():
            o_copy(first_row + (n_blk - 2) * tm, n_blk % 2).wait()

        o_copy(first_row + (n_blk - 1) * tm, (n_blk - 1) % 2).wait()

    @pl.when(pl.program_id(0) == pl.num_programs(0) - 1)
    def _():
        obuf[0] = jnp.zeros(obuf.shape[1:], obuf.dtype)

        def fill(t, c):
            o_copy(t * tm, 0).start()
            o_copy(t * tm, 0).wait()
            return c

        lax.fori_loop(n_used, n_blocks, fill, 0)


def _moe_up_body(row0_ref, nblk_ref, nu_ref, xs_hbm, w_ref, b_ref, perm_ref, h_hbm, wbf, xbuf, obuf, xsem, osem, *,
                 n_blocks):
    e = pl.program_id(0)
    tn = wbf.shape[1]
    pw = perm_ref.shape[0]
    hw = pw // 2

    @pl.when(nblk_ref[e] > 0)
    def _():
        for c in range(tn // pw):
            w = w_ref[0, :, c * pw:(c + 1) * pw].astype(BF16)
            wbf[:, c * pw:(c + 1) * pw] = _dot(w, perm_ref[...]).astype(BF16)

    def compute(x):
        gu = _dot(x, wbf[...]) + b_ref[0]
        acts = []
        for c in range(tn // pw):
            glu = jnp.minimum(gu[:, c * pw:c * pw + hw], SWIGLU_LIMIT)
            lin = jnp.clip(gu[:, c * pw + hw:(c + 1) * pw], -SWIGLU_LIMIT, SWIGLU_LIMIT)
            acts.append((glu * jax.nn.sigmoid(SWIGLU_ALPHA * glu) * (lin + 1.0)).astype(BF16))
        return jnp.concatenate(acts, axis=1)

    col0 = pl.multiple_of(pl.program_id(1) * (tn // 2), tn // 2)
    _expert_blocks(row0_ref[e], nblk_ref[e], nu_ref[0], n_blocks, xs_hbm, h_hbm, col0, xbuf, obuf, xsem, osem, compute)


def _moe_up(row0, nblk, n_used, xs, w_gu, b_gu, perm):
    n_rows, d = xs.shape
    n_e, _, f2 = w_gu.shape
    tm, tn = MOE_TM, MOE_TN
    grid_spec = pltpu.PrefetchScalarGridSpec(
        num_scalar_prefetch=3,
        grid=(n_e, f2 // tn),
        in_specs=[pl.BlockSpec(memory_space=pl.ANY),
                  pl.BlockSpec((1, d, tn), lambda e, n, *_: (e, 0, n)),
                  pl.BlockSpec((1, 1, tn), lambda e, n, *_: (e, 0, n)),
                  pl.BlockSpec(perm.shape, lambda e, n, *_: (0, 0))],
        out_specs=pl.BlockSpec(memory_space=pl.ANY),
        scratch_shapes=[pltpu.VMEM((d, tn), BF16), pltpu.VMEM((2, tm, d), BF16), pltpu.VMEM((2, tm, tn // 2), BF16),
                        pltpu.SemaphoreType.DMA((2,)), pltpu.SemaphoreType.DMA((2,))],
    )
    return pl.pallas_call(
        functools.partial(_moe_up_body, n_blocks=n_rows // tm),
        grid_spec=grid_spec,
        out_shape=jax.ShapeDtypeStruct((n_rows, f2 // 2), BF16),
        compiler_params=_params(("arbitrary", "arbitrary")),
        name="moe_up",
    )(row0, nblk, n_used, xs, w_gu, b_gu, perm)


def _moe_down_body(row0_ref, nblk_ref, nu_ref, h_hbm, w_ref, b_ref, y_hbm, wbf, xbuf, obuf, xsem, osem, *, n_blocks):
    e = pl.program_id(0)
    f, tn = wbf.shape
    rc = 512

    @pl.when(nblk_ref[e] > 0)
    def _():
        for c in range(f // rc):
            wbf[c * rc:(c + 1) * rc, :] = w_ref[0, c * rc:(c + 1) * rc, :].astype(BF16)

    def compute(h):
        return _dot(h, wbf[...]) + b_ref[0]

    col0 = pl.multiple_of(pl.program_id(1) * tn, tn)
    _expert_blocks(row0_ref[e], nblk_ref[e], nu_ref[0], n_blocks, h_hbm, y_hbm, col0, xbuf, obuf, xsem, osem, compute)


def _moe_down(row0, nblk, n_used, h, w_d, b_d):
    n_rows, f = h.shape
    n_e, _, d = w_d.shape
    tm, tn = MOE_TM, MOE_TN
    grid_spec = pltpu.PrefetchScalarGridSpec(
        num_scalar_prefetch=3,
        grid=(n_e, d // tn),
        in_specs=[pl.BlockSpec(memory_space=pl.ANY),
                  pl.BlockSpec((1, f, tn), lambda e, n, *_: (e, 0, n)),
                  pl.BlockSpec((1, 1, tn), lambda e, n, *_: (e, 0, n))],
        out_specs=pl.BlockSpec(memory_space=pl.ANY),
        scratch_shapes=[pltpu.VMEM((f, tn), BF16), pltpu.VMEM((2, tm, f), BF16), pltpu.VMEM((2, tm, tn), F32),
                        pltpu.SemaphoreType.DMA((2,)), pltpu.SemaphoreType.DMA((2,))],
    )
    return pl.pallas_call(
        functools.partial(_moe_down_body, n_blocks=n_rows // tm),
        grid_spec=grid_spec,
        out_shape=jax.ShapeDtypeStruct((n_rows, d), F32),
        compiler_params=_params(("arbitrary", "arbitrary")),
        name="moe_down",
    )(row0, nblk, n_used, h, w_d, b_d)


def _combine_body(dest_hbm, gate_ref, x1_ref, y_hbm, outp_ref, outs_ref, dsm, buf, sem_d, sem, *, np_blocks):
    j = pl.program_id(0)
    n = pl.num_programs(0)
    tk = x1_ref.shape[0]
    toks = ISSUE_UNROLL // TOP_K

    def idx_copy(step, slot):
        return pltpu.make_async_copy(dest_hbm.at[step, 0], dsm.at[slot], sem_d.at[slot])

    def issue_rows(slot):
        def body(it, c):
            for u in range(toks):
                t = it * toks + u
                for k in range(TOP_K):
                    src = dsm[slot, t * TOP_K + k]
                    pltpu.make_async_copy(y_hbm.at[pl.ds(src, 1), :], buf.at[slot, k, pl.ds(t, 1), :],
                                          sem.at[slot]).start()
            return c

        lax.fori_loop(0, tk // toks, body, 0)

    @pl.when(j == 0)
    def _():
        idx_copy(0, 0).start()
        idx_copy(0, 0).wait()
        issue_rows(0)

        @pl.when(1 < n)
        def _():
            idx_copy(1, 1).start()

    @pl.when(j + 1 < n)
    def _():
        slot = (j + 1) % 2
        idx_copy(j + 1, slot).wait()
        issue_rows(slot)

    @pl.when(j + 2 < n)
    def _():
        idx_copy(j + 2, j % 2).start()

    slot = j % 2
    for k in range(TOP_K):
        pltpu.make_async_copy(y_hbm.at[pl.ds(0, tk), :], buf.at[slot, k], sem.at[slot]).wait()
    g = gate_ref[...]
    acc = g[:, 0:1] * buf[slot, 0]
    for k in range(1, TOP_K):
        acc = acc + g[:, k:k + 1] * buf[slot, k]
    res = x1_ref[...] + acc

    @pl.when(j < np_blocks)
    def _():
        outp_ref[...] = res

    @pl.when(j >= np_blocks)
    def _():
        outs_ref[...] = res


def _combine(dest3, gates, x1, y, n_prompt):
    n, d = x1.shape
    tk = dest3.shape[2] // TOP_K
    np_blocks = n_prompt // tk
    row = lambda j: (j, 0)
    return pl.pallas_call(
        functools.partial(_combine_body, np_blocks=np_blocks),
        grid=(n // tk,),
        in_specs=[pl.BlockSpec(memory_space=pl.ANY), pl.BlockSpec((tk, LANES), row), pl.BlockSpec((tk, d), row),
                  pl.BlockSpec(memory_space=pl.ANY)],
        out_specs=[pl.BlockSpec((tk, d), lambda j: (jnp.minimum(j, np_blocks - 1), 0)),
                   pl.BlockSpec((tk, d), lambda j: (jnp.maximum(j - np_blocks, 0), 0))],
        out_shape=[jax.ShapeDtypeStruct((n_prompt, d), F32), jax.ShapeDtypeStruct((n - n_prompt, d), F32)],
        scratch_shapes=[pltpu.SMEM((2, tk * TOP_K), I32), pltpu.VMEM((2, TOP_K, tk, d), F32),
                        pltpu.SemaphoreType.DMA((2,)), pltpu.SemaphoreType.DMA((2,))],
        compiler_params=_params(("arbitrary",)),
        name="moe_combine",
    )(dest3, gates, x1, y)


def _moe(h2, x1, topi, topg, topp, counts, w_gu, b_gu, w_d, b_d, n_prompt):
    n, d = h2.shape
    tm = MOE_TM
    n_asg = n * TOP_K
    n_blocks = n_asg // tm + N_EXPERTS
    padded = (counts + tm - 1) // tm * tm
    pad_end = jnp.cumsum(padded)
    pad_start = pad_end - padded
    dest = pad_start[topi[:, :TOP_K]] + topp[:, :TOP_K]
    tok = jnp.broadcast_to(jnp.arange(n, dtype=I32)[:, None], (n, TOP_K))
    row_tok = jnp.zeros((n_blocks * tm,), I32).at[dest.reshape(-1)].set(tok.reshape(-1))
    n_used = (pad_end[-1:] // tm).astype(I32)
    row0 = pad_start.astype(I32)
    nblk = (padded // tm).astype(I32)
    f2 = w_gu.shape[2]
    pw = 2 * LANES
    src = jnp.arange(pw)
    perm = (jnp.arange(pw)[:, None] == jnp.where(src < LANES, 2 * src, 2 * (src - LANES) + 1)[None, :]).astype(BF16)
    b_gu_p = b_gu.reshape(N_EXPERTS, f2 // pw, LANES, 2).transpose(0, 1, 3, 2).reshape(N_EXPERTS, 1, f2)
    xs = _gather_rows(row_tok, n_used, h2, n_blocks)
    h = _moe_up(row0, nblk, n_used, xs, w_gu, b_gu_p, perm)
    y = _moe_down(row0, nblk, n_used, h, w_d, b_d.reshape(N_EXPERTS, 1, d))
    tk = 128
    dest3 = dest.astype(I32).reshape(n // tk, 1, tk * TOP_K)
    return _combine(dest3, topg, x1, y, n_prompt)


def _layer(layer, xp, xs, cache_cmp_all, cache_sel_all, cache_win_all, state_conv, state_h, page_table, w):
    (norm_mix_w, w_in, q_norm_w, k_norm_w, w_cmp_k, w_cmp_v, conv_w, conv_b, w_gate_a, b_gate_a, w_gate_i, b_gate_i,
     lru_lambda, out_norm_attn, out_norm_rec, w_out, norm_ffn_w, w_router, b_router, w_gate_up, b_gate_up, w_down,
     b_down) = w
    bp, tp, d = xp.shape
    bs, ts, _ = xs.shape
    d_rec = d - D_ATTN
    past = page_table.shape[1] * cache_cmp_all.shape[2]
    assert bs * ts == QB and ts <= TP

    o1, o2 = D_ATTN, D_ATTN + 12 * DH
    o3 = o2 + 3 * N_HEADS
    wq = w_in[:, :o1].astype(BF16)
    wkv = w_in[:, o1:o2].astype(BF16)
    wg = jnp.pad(w_in[:, o2:o3], ((0, 0), (0, LANES - 3 * N_HEADS))).astype(BF16)
    wxy = w_in[:, o3:].astype(BF16)
    row2 = lambda v: v.reshape(1, -1)
    slopes = jnp.exp2(-8.0 * jnp.arange(1, N_HEADS + 1, dtype=F32) / N_HEADS)
    w4 = jnp.concatenate([w_cmp_k, w_cmp_v], axis=0)
    z4 = jnp.zeros_like(w4)
    wl = jnp.stack([jnp.concatenate([w4, z4], axis=1), jnp.concatenate([z4, w4], axis=1)])
    wl = jnp.broadcast_to(wl[..., None], wl.shape + (DH,))
    sp = row2(jax.nn.softplus(-lru_lambda.astype(F32)))
    wa, wi = w_gate_a.astype(BF16), w_gate_i.astype(BF16)
    woa, wor = w_out[:D_ATTN].astype(BF16), w_out[D_ATTN:].astype(BF16)
    wr = jnp.pad(w_router, ((0, 0), (0, LANES - N_EXPERTS)))
    br = row2(jnp.pad(b_router, (0, LANES - N_EXPERTS)))
    mix = (row2(norm_mix_w), wq, wkv, wg, wxy, row2(q_norm_w), k_norm_w)
    lru_w = (conv_w, row2(conv_b), wa, row2(b_gate_a), wi, row2(b_gate_i), sp, row2(out_norm_rec))

    np_tok = bp * tp
    q_p, cmp_p, sel_p, win_p, kvb_p, gate_p, xr_p, yr_p = _inproj(xp.reshape(np_tok, d), *mix, tm=256)
    kce_p, kco_p = _pool_prompt(cmp_p, wl)
    gates_g = gate_p[:, :3 * N_HEADS].reshape(bp, tp, N_KV, 3 * GQA).transpose(0, 2, 1, 3)
    attn_p = _prompt_attention(slopes, q_p, kvb_p, kce_p, kco_p, gates_g, bp, tp)
    recn_p, hl_p = _lru_seq(xr_p.reshape(bp, tp, d_rec), yr_p.reshape(bp, tp, d_rec),
                            jnp.zeros((bp, CONV_W - 1, d_rec), F32), jnp.zeros((bp, 1, d_rec), F32), *lru_w, tt=256)
    post = (row2(out_norm_attn), woa, wor, row2(norm_ffn_w), wr, br)
    x1_p, h2_p, ti_p, tg_p, tp_p, cnt_p = _outproj_router(
        attn_p, recn_p.reshape(np_tok, d_rec), xp.reshape(np_tok, d), jnp.zeros((1, LANES), F32), *post, tm=256)

    ns_tok = bs * ts
    q_s, cmp_s, sel_s, win_s, kvb_s, gate_s, xr_s, yr_s = _inproj(xs.reshape(ns_tok, d), *mix, tm=QB)
    n_pool, page = cache_cmp_all.shape[1], cache_cmp_all.shape[2]
    page_view = lambda c: c.reshape(c.shape[0] * n_pool, page * KV_ROWS, DH)
    page_ids = (page_table + layer * n_pool).reshape(-1)
    kce_s, kco_s = _pool_pages(page_view(cache_cmp_all), page_ids, bs, wl)
    q_t = q_s[0].reshape(N_KV, GQA, bs, ts, DH).transpose(2, 0, 1, 3, 4)
    q_t = jnp.pad(q_t, ((0, 0), (0, 0), (0, 0), (0, TP - ts), (0, 0))).reshape(bs, N_KV, GQA * TP, DH)
    oc_s, selm_s = _sample_select(slopes, q_t, kce_s, kco_s, past)
    new_kvb = jnp.pad(kvb_s.reshape(bs, ts, 12 * DH), ((0, 0), (0, TP - ts), (0, 0)))
    g_t = gate_s[:, :3 * N_HEADS].reshape(bs, ts, N_KV, GQA, 3).transpose(0, 2, 3, 1, 4)
    g_t = jnp.pad(g_t, ((0, 0), (0, 0), (0, 0), (0, TP - ts), (0, 0))).reshape(bs, N_KV, GQA * TP, 3)
    win_view = cache_win_all.reshape(cache_win_all.shape[0] * bs, cache_win_all.shape[2] * KV_ROWS, DH)
    attn_s = _sample_attention(slopes, page_ids, page_view(cache_sel_all), q_t, selm_s, new_kvb, win_view, layer * bs,
                               oc_s, g_t, past, ts)
    attn_s = attn_s.reshape(bs, N_KV, GQA, TP, DH)[:, :, :, :ts].transpose(1, 2, 0, 3, 4).reshape(1, N_HEADS, QB, DH)
    tmaj = lambda a: a.reshape(bs, ts, d_rec).transpose(1, 0, 2)
    recn_s, hl_s = _lru_step(tmaj(xr_s), tmaj(yr_s), state_conv.transpose(1, 0, 2), state_h, *lru_w)
    x1_s, h2_s, ti_s, tg_s, tp_s, cnt = _outproj_router(
        attn_s, recn_s.transpose(1, 0, 2).reshape(ns_tok, d_rec), xs.reshape(ns_tok, d), cnt_p, *post, tm=QB)

    cat = lambda a, b: jnp.concatenate([a, b], axis=0)
    out_p, out_s = _moe(cat(h2_p, h2_s), cat(x1_p, x1_s), cat(ti_p, ti_s), cat(tg_p, tg_s), cat(tp_p, tp_s),
                        cnt[0, :N_EXPERTS].astype(I32), w_gate_up, b_gate_up, w_down, b_down, np_tok)
    y_p = out_p.reshape(bp, tp, d)
    y_s = out_s.reshape(bs, ts, d)

    kv5 = lambda a, b, t: a.reshape(b, t, 2, N_KV, DH)
    win_len_p = min(WINDOW, tp)
    st_p = (kv5(cmp_p, bp, tp), kv5(sel_p, bp, tp), kv5(win_p, bp, tp)[:, tp - win_len_p:],
            xr_p.reshape(bp, tp, d_rec)[:, tp - (CONV_W - 1):], hl_p.reshape(bp, d_rec))
    cache_win = cache_win_all[layer]
    win_all = jnp.concatenate([cache_win, kv5(win_s, bs, ts)], axis=1)
    xcat = jnp.concatenate([state_conv, xr_s.reshape(bs, ts, d_rec)], axis=1)
    st_s = (kv5(cmp_s, bs, ts), kv5(sel_s, bs, ts), win_all[:, win_all.shape[1] - cache_win.shape[1]:],
            xcat[:, ts:], hl_s)
    return y_p, y_s, st_p, st_s


def kernel(x_prompt, x_sample, cache_cmp_kv, cache_sel_kv, cache_win_kv, state_conv, state_h, page_table, norm_mix_w, w_in, q_norm_w, k_norm_w, w_cmp_k, w_cmp_v, conv_w, conv_b, w_gate_a, b_gate_a, w_gate_i, b_gate_i, lru_lambda, out_norm_attn, out_norm_rec, w_out, norm_ffn_w, w_router, b_router, w_gate_up, b_gate_up, w_down, b_down):
    depth = w_in.shape[0]
    xp, xs = x_prompt, x_sample
    st_ps, st_ss = [], []
    for l in range(depth):
        w = (norm_mix_w[l], w_in[l], q_norm_w[l], k_norm_w[l], w_cmp_k[l], w_cmp_v[l], conv_w[l], conv_b[l],
             w_gate_a[l], b_gate_a[l], w_gate_i[l], b_gate_i[l], lru_lambda[l], out_norm_attn[l], out_norm_rec[l],
             w_out[l], norm_ffn_w[l], w_router[l], b_router[l], w_gate_up[l], b_gate_up[l], w_down[l], b_down[l])
        xp, xs, st_p, st_s = _layer(l, xp, xs, cache_cmp_kv, cache_sel_kv, cache_win_kv, state_conv[l], state_h[l],
                                    page_table, w)
        st_ps.append(st_p)
        st_ss.append(st_s)
    stack = lambda sts, i: jnp.stack([s[i] for s in sts])
    return (xp, xs) + tuple(stack(st_ps, i) for i in range(5)) + tuple(stack(st_ss, i) for i in range(5))
```

```python
import functools

import jax
import jax.numpy as jnp
from jax import lax
from jax.experimental import pallas as pl
from jax.experimental.pallas import tpu as pltpu

F32 = jnp.float32
BF16 = jnp.bfloat16
I32 = jnp.int32

N_HEADS = 8
N_KV = 2
GQA = N_HEADS // N_KV
DH = 128
D_ATTN = N_HEADS * DH
CONV_W = 4
LRU_C = 8.0
L_CMP = 32
L_SEL = 64
N_SEL = 16
WINDOW = 512
QB = 128
N_EXPERTS = 32
TOP_K = 4
SWIGLU_LIMIT = 7.0
SWIGLU_ALPHA = 1.702
EPS = 1e-6
NEG = -1e30
FORCE = 1e9
Q_SCALE = DH ** -0.5
LANES = 128
TP = 8
KV_ROWS = 2 * N_KV
MOE_TM = 256
MOE_TN = 1024
PAGES_PER_STEP = 16
VMEM_LIMIT = 56 * 1024 * 1024


def _dot(a, b):
    return jnp.dot(a, b, preferred_element_type=F32)


def _dot_nt(a, b):
    return lax.dot_general(a, b, (((1,), (1,)), ((), ())), preferred_element_type=F32)


def _iota(shape, dim):
    return lax.broadcasted_iota(I32, shape, dim)


def _rms(x, w):
    return x * lax.rsqrt(jnp.mean(x * x, axis=-1, keepdims=True) + EPS) * w


def _masked_softmax_rows(s, mask):
    s = jnp.where(mask, s, NEG)
    e = jnp.where(mask, jnp.exp(s - jnp.max(s, axis=-1, keepdims=True)), 0.0)
    return e / jnp.maximum(jnp.sum(e, axis=-1, keepdims=True), 1e-30)


def _params(sem, vmem=None):
    return pltpu.CompilerParams(dimension_semantics=sem, vmem_limit_bytes=vmem or VMEM_LIMIT)


def _inproj_body(x_ref, nw_ref, wq_ref, wkv_ref, wg_ref, wxy_ref, qnw_ref, knw_ref,
                 q_ref, cmp_ref, sel_ref, win_ref, kvb_ref, gate_ref, xr_ref, yr_ref):
    x = x_ref[...]
    h = _rms(x, nw_ref[...]).astype(BF16)
    q = _dot(h, wq_ref[...])
    qnw = qnw_ref[...]
    nqb = q_ref.shape[0]
    for hd in range(N_HEADS):
        qn = (_rms(q[:, hd * DH:(hd + 1) * DH], qnw) * Q_SCALE).astype(BF16)
        for b in range(nqb):
            q_ref[b, hd] = qn[b * QB:(b + 1) * QB]
    kv = _dot(h, wkv_ref[...])
    tm = x.shape[0]
    outs = (cmp_ref, sel_ref, win_ref)
    for br in range(3):
        knw = knw_ref[br:br + 1, :]
        for g in range(N_KV):
            c0 = br * 4 * DH + g * DH
            kn = _rms(kv[:, c0:c0 + DH], knw)
            v = kv[:, c0 + 2 * DH:c0 + 3 * DH]
            outs[br][pl.ds(g, tm, KV_ROWS), :] = kn
            outs[br][pl.ds(2 + g, tm, KV_ROWS), :] = v
            kvb_ref[:, c0:c0 + DH] = kn.astype(BF16)
            kvb_ref[:, c0 + 2 * DH:c0 + 3 * DH] = v.astype(BF16)
    gate_ref[...] = jax.nn.sigmoid(_dot(h, wg_ref[...]))
    xy = _dot(h, wxy_ref[...])
    d_rec = xr_ref.shape[1]
    xr_ref[...] = xy[:, :d_rec]
    yr_ref[...] = xy[:, d_rec:]


def _inproj(x2, nw, wq, wkv, wg, wxy, qnw, knw, tm):
    n, d = x2.shape
    d_rec = wxy.shape[1] // 2
    nqb = tm // QB
    row = lambda i: (i, 0)
    const = lambda i: (0, 0)
    wspec = lambda a: pl.BlockSpec(a.shape, const, pipeline_mode=pl.Buffered(1))
    return pl.pallas_call(
        _inproj_body,
        grid=(n // tm,),
        in_specs=[pl.BlockSpec((tm, d), row), wspec(nw), wspec(wq), wspec(wkv), wspec(wg), wspec(wxy),
                  wspec(qnw), wspec(knw)],
        out_specs=[pl.BlockSpec((nqb, N_HEADS, QB, DH), lambda i: (i, 0, 0, 0)),
                   pl.BlockSpec((tm * KV_ROWS, DH), row), pl.BlockSpec((tm * KV_ROWS, DH), row),
                   pl.BlockSpec((tm * KV_ROWS, DH), row),
                   pl.BlockSpec((tm, 12 * DH), row), pl.BlockSpec((tm, LANES), row),
                   pl.BlockSpec((tm, d_rec), row), pl.BlockSpec((tm, d_rec), row)],
        out_shape=[jax.ShapeDtypeStruct((n // QB, N_HEADS, QB, DH), BF16),
                   jax.ShapeDtypeStruct((n * KV_ROWS, DH), F32), jax.ShapeDtypeStruct((n * KV_ROWS, DH), F32),
                   jax.ShapeDtypeStruct((n * KV_ROWS, DH), F32), jax.ShapeDtypeStruct((n, 12 * DH), BF16),
                   jax.ShapeDtypeStruct((n, LANES), F32),
                   jax.ShapeDtypeStruct((n, d_rec), F32), jax.ShapeDtypeStruct((n, d_rec), F32)],
        compiler_params=_params(("parallel",)),
        name="inproj",
    )(x2, nw, wq, wkv, wg, wxy, qnw, knw)


def _pool_combo(rows, wl_ref, combo):
    r = rows.shape[0] // (2 * L_CMP)
    x3 = rows.reshape(r, 2 * L_CMP, rows.shape[1])
    even = jnp.sum(x3 * wl_ref[0, combo][None], axis=1)
    odd = jnp.sum(x3 * wl_ref[1, combo][None], axis=1)
    return even.astype(BF16), odd.astype(BF16)


def _pool_body(x_ref, wl_ref, e_ref, o_ref):
    n_tok = x_ref.shape[0] // KV_ROWS
    for combo in range(KV_ROWS):
        even, odd = _pool_combo(x_ref[pl.ds(combo, n_tok, KV_ROWS), :], wl_ref, combo)
        e_ref[:, combo * DH:(combo + 1) * DH] = even
        o_ref[:, combo * DH:(combo + 1) * DH] = odd


def _pool_prompt(cmp4, wl):
    n = cmp4.shape[0] // KV_ROWS
    toks = 1024
    ob = toks // (2 * L_CMP)
    c = KV_ROWS * DH
    return pl.pallas_call(
        _pool_body,
        grid=(n // toks,),
        in_specs=[pl.BlockSpec((toks * KV_ROWS, DH), lambda i: (i, 0)), pl.BlockSpec(wl.shape, lambda i: (0, 0, 0, 0))],
        out_specs=[pl.BlockSpec((ob, c), lambda i: (i, 0)), pl.BlockSpec((ob, c), lambda i: (i, 0))],
        out_shape=[jax.ShapeDtypeStruct((n // (2 * L_CMP), c), BF16)] * 2,
        compiler_params=_params(("parallel",)),
        name="pool_prompt",
    )(cmp4, wl)


def _page_rows(pages, combo):
    n_tok = pages[0].shape[1] // KV_ROWS
    return jnp.concatenate([p[0, pl.ds(combo, n_tok, KV_ROWS), :] for p in pages], axis=0)


def _pool_pages_body(pt_ref, *refs):
    pages = refs[:PAGES_PER_STEP]
    wl_ref, e_ref, o_ref = refs[PAGES_PER_STEP:]
    for combo in range(KV_ROWS):
        even, odd = _pool_combo(_page_rows(pages, combo), wl_ref, combo)
        e_ref[0, :, combo * DH:(combo + 1) * DH] = even
        o_ref[0, :, combo * DH:(combo + 1) * DH] = odd


def _pool_pages(cache3, page_ids, bsz, wl):
    rows = cache3.shape[1]
    n_pages = page_ids.shape[0] // bsz
    n_steps = n_pages // PAGES_PER_STEP
    ob = PAGES_PER_STEP * (rows // KV_ROWS) // (2 * L_CMP)
    c = KV_ROWS * DH
    page_spec = lambda i: pl.BlockSpec((1, rows, DH), lambda b, s, pt: (pt[b * n_pages + s * PAGES_PER_STEP + i], 0, 0))
    grid_spec = pltpu.PrefetchScalarGridSpec(
        num_scalar_prefetch=1,
        grid=(bsz, n_steps),
        in_specs=[page_spec(i) for i in range(PAGES_PER_STEP)] + [pl.BlockSpec(wl.shape, lambda b, s, pt: (0, 0, 0, 0))],
        out_specs=[pl.BlockSpec((1, ob, c), lambda b, s, pt: (b, s, 0))] * 2,
    )
    return pl.pallas_call(
        _pool_pages_body,
        grid_spec=grid_spec,
        out_shape=[jax.ShapeDtypeStruct((bsz, n_steps * ob, c), BF16)] * 2,
        compiler_params=_params(("parallel", "parallel")),
        name="pool_pages",
    )(page_ids, *([cache3] * PAGES_PER_STEP), wl)


def _select_blocks_cols(score, blk, n_pick, n_blk):
    sel = jnp.zeros(score.shape, F32)
    for _ in range(n_pick):
        m = jnp.max(score, axis=0, keepdims=True)
        idx = jnp.min(jnp.where(score == m, blk, n_blk), axis=0, keepdims=True)
        hit = blk == idx
        sel = jnp.where(hit, 1.0, sel)
        score = jnp.where(hit, -jnp.inf, score)
    return sel


def _pattn_body(slopes_ref, q_ref, ksel_ref, vsel_ref, kwin_ref, vwin_ref, kce_ref, kco_ref, vce_ref, vco_ref,
                gt_ref, o_ref, m_scr, l_scr, acc_scr):
    g = pl.program_id(1)
    i = pl.program_id(2)
    rows = GQA * QB
    q = q_ref[0].reshape(rows, DH)
    row = _iota((rows, 1), 0)
    qpos = i * QB + (row & (QB - 1))
    r_of_row = row >> 7
    slope = jnp.zeros((rows, 1), F32)
    for r in range(GQA):
        slope = jnp.where(r_of_row == r, slopes_ref[g * GQA + r], slope)

    kc = jnp.concatenate([kce_ref[...], kco_ref[...]], axis=0)
    vc = jnp.concatenate([vce_ref[...], vco_ref[...]], axis=0)
    nb = kc.shape[0]
    half = nb // 2
    lane = _iota((rows, nb), 1)
    blk_c = jnp.where(lane < half, 2 * lane, 2 * (lane - half) + 1)
    dist = qpos - (blk_c * L_CMP + (L_CMP - 1))
    mask = dist >= 0
    s = _dot_nt(q, kc) - slope * dist.astype(F32)
    p = _masked_softmax_rows(s, mask)
    o_c = _dot(p.astype(BF16), vc)
    imp = p[0:QB]
    for r in range(1, GQA):
        imp = imp + p[r * QB:(r + 1) * QB]
    imp_t = imp.T
    pair = imp_t[:half] + imp_t[half:]
    ns = half
    blk = _iota((ns, QB), 0)
    qp = i * QB + _iota((ns, QB), 1)
    cur = qp >> 6
    forced = (blk == 0) | (blk == cur) | (blk == cur - 1)
    score = jnp.where(blk * L_SEL <= qp, jnp.where(forced, FORCE, pair), NEG)
    sel = _select_blocks_cols(score, blk, min(N_SEL, ns), ns)
    sel = jnp.concatenate([sel, jnp.zeros((LANES - ns, QB), F32)], axis=0) if ns < LANES else sel
    sel_q = sel.T.astype(BF16)
    sel4 = jnp.concatenate([sel_q] * GQA, axis=0)

    m_scr[...] = jnp.full(m_scr.shape, NEG, F32)
    l_scr[...] = jnp.zeros(l_scr.shape, F32)
    acc_scr[...] = jnp.zeros(acc_scr.shape, F32)
    ck = 512

    def chunk(c, carry):
        k0 = pl.multiple_of(c * ck, ck)
        kch = ksel_ref[pl.ds(k0, ck), :]
        vch = vsel_ref[pl.ds(k0, ck), :]
        dist_s = qpos - (k0 + _iota((rows, ck), 1))
        expand = jnp.where((_iota((LANES, ck), 1) >> 6) + c * (ck // L_SEL) == _iota((LANES, ck), 0), 1.0, 0.0)
        picked = _dot(sel4, expand.astype(BF16))
        msk = jnp.where(dist_s >= 0, picked, 0.0) > 0.5
        sc = jnp.where(msk, _dot_nt(q, kch) - slope * dist_s.astype(F32), NEG)
        m_old = m_scr[...]
        m_new = jnp.maximum(m_old, jnp.max(sc, axis=-1, keepdims=True))
        alpha = jnp.exp(m_old - m_new)
        e = jnp.where(msk, jnp.exp(sc - m_new), 0.0)
        l_scr[...] = alpha * l_scr[...] + jnp.sum(e, axis=-1, keepdims=True)
        acc_scr[...] = alpha * acc_scr[...] + _dot(e.astype(BF16), vch)
        m_scr[...] = m_new
        return carry

    lax.fori_loop(0, (i >> 2) + 1, chunk, 0)
    o_s = acc_scr[...] / jnp.maximum(l_scr[...], 1e-30)

    span = WINDOW + QB
    start = pl.multiple_of(jnp.maximum(i * QB - WINDOW, 0), QB)
    kw = kwin_ref[pl.ds(start, span), :]
    vw = vwin_ref[pl.ds(start, span), :]
    dist_w = qpos - (start + _iota((rows, span), 1))
    mask_w = jnp.where(dist_w >= 0, dist_w, WINDOW) < WINDOW
    s_w = _dot_nt(q, kw) - slope * dist_w.astype(F32)
    o_w = _dot(_masked_softmax_rows(s_w, mask_w).astype(BF16), vw)

    gt = gt_ref[0, 0]
    for r in range(GQA):
        sl = slice(r * QB, (r + 1) * QB)
        o_ref[0, r] = (gt[:, 3 * r:3 * r + 1] * o_c[sl] + gt[:, 3 * r + 1:3 * r + 2] * o_s[sl]
                       + gt[:, 3 * r + 2:3 * r + 3] * o_w[sl])


def _prompt_attention(slopes, q_blk, kvb, kce, kco, gates_g, bsz, t_len):
    nq = t_len // QB
    nb = t_len // L_CMP
    assert nb % (2 * LANES) == 0 or nb == LANES, "compressed blocks must fill whole lane tiles"
    assert t_len >= WINDOW + QB
    half = nb // 2
    rows = GQA * QB
    kv_spec = lambda col: pl.BlockSpec((t_len, DH), lambda b, g, i, col=col: (b, col + g))
    kc_spec = lambda col: pl.BlockSpec((half, DH), lambda b, g, i, col=col: (b, col + g))
    return pl.pallas_call(
        _pattn_body,
        grid=(bsz, N_KV, nq),
        in_specs=[pl.BlockSpec(memory_space=pltpu.SMEM),
                  pl.BlockSpec((1, GQA, QB, DH), lambda b, g, i: (b * nq + i, g, 0, 0)),
                  kv_spec(4), kv_spec(6), kv_spec(8), kv_spec(10),
                  kc_spec(0), kc_spec(0), kc_spec(2), kc_spec(2),
                  pl.BlockSpec((1, 1, QB, 3 * GQA), lambda b, g, i: (b, g, i, 0))],
        out_specs=pl.BlockSpec((1, GQA, QB, DH), lambda b, g, i: (b * nq + i, g, 0, 0)),
        out_shape=jax.ShapeDtypeStruct((bsz * nq, N_HEADS, QB, DH), F32),
        scratch_shapes=[pltpu.VMEM((rows, 1), F32), pltpu.VMEM((rows, 1), F32), pltpu.VMEM((rows, DH), F32)],
        compiler_params=_params(("parallel", "parallel", "arbitrary")),
        name="prompt_attention",
    )(slopes, q_blk, kvb, kvb, kvb, kvb, kce, kco, kce, kco, gates_g)


def _sample_select_body(slopes_ref, q_ref, kce_ref, kco_ref, oc_ref, sel_ref, *, past):
    rows = GQA * TP
    row = _iota((rows, 1), 0)
    qpos = past + (row & (TP - 1))
    r_of_row = row >> 3
    half = kce_ref.shape[1]
    nb = 2 * half
    for g in range(N_KV):
        slope = jnp.zeros((rows, 1), F32)
        for r in range(GQA):
            slope = jnp.where(r_of_row == r, slopes_ref[g * GQA + r], slope)
        q = q_ref[0, g]
        kc = jnp.concatenate([kce_ref[0, :, g * DH:(g + 1) * DH], kco_ref[0, :, g * DH:(g + 1) * DH]], axis=0)
        vc = jnp.concatenate([kce_ref[0, :, (2 + g) * DH:(3 + g) * DH], kco_ref[0, :, (2 + g) * DH:(3 + g) * DH]], axis=0)
        lane = _iota((rows, nb), 1)
        blk_c = jnp.where(lane < half, 2 * lane, 2 * (lane - half) + 1)
        dist = qpos - (blk_c * L_CMP + (L_CMP - 1))
        s = _dot_nt(q, kc) - slope * dist.astype(F32)
        p = _masked_softmax_rows(s, dist >= 0)
        oc_ref[0, g] = _dot(p.astype(BF16), vc)
        imp = p[0:TP]
        for r in range(1, GQA):
            imp = imp + p[r * TP:(r + 1) * TP]
        pair = imp[:, :half] + imp[:, half:]
        blk = _iota((TP, half), 1)
        qp = past + _iota((TP, half), 0)
        cur = qp >> 6
        forced = (blk == 0) | (blk == cur) | (blk == cur - 1)
        score = jnp.where(blk * L_SEL <= qp, jnp.where(forced, FORCE, pair), NEG)
        sel = jnp.zeros((TP, half), F32)
        for _ in range(N_SEL - 1):
            m = jnp.max(score, axis=1, keepdims=True)
            idx = jnp.min(jnp.where(score == m, blk, half), axis=1, keepdims=True)
            hit = blk == idx
            sel = jnp.where(hit, 1.0, sel)
            score = jnp.where(hit, -jnp.inf, score)
        sel_ref[0, g] = jnp.concatenate([sel] * GQA, axis=0).astype(BF16)


def _sample_select(slopes, q_s, kce, kco, past):
    bsz = q_s.shape[0]
    half = kce.shape[1]
    assert half == LANES, "past selection blocks must fill one lane tile"
    rows = GQA * TP
    return pl.pallas_call(
        functools.partial(_sample_select_body, past=past),
        grid=(bsz,),
        in_specs=[pl.BlockSpec(memory_space=pltpu.SMEM),
                  pl.BlockSpec((1, N_KV, rows, DH), lambda b: (b, 0, 0, 0)),
                  pl.BlockSpec((1, half, 4 * DH), lambda b: (b, 0, 0)),
                  pl.BlockSpec((1, half, 4 * DH), lambda b: (b, 0, 0))],
        out_specs=[pl.BlockSpec((1, N_KV, rows, DH), lambda b: (b, 0, 0, 0)),
                   pl.BlockSpec((1, N_KV, rows, half), lambda b: (b, 0, 0, 0))],
        out_shape=[jax.ShapeDtypeStruct((bsz, N_KV, rows, DH), F32),
                   jax.ShapeDtypeStruct((bsz, N_KV, rows, half), BF16)],
        compiler_params=_params(("parallel",)),
        name="sample_select",
    )(slopes, q_s, kce, kco)


def _online_update(m_ref, l_ref, acc_ref, g, sc, msk, v):
    m_old = m_ref[g]
    m_new = jnp.maximum(m_old, jnp.max(sc, axis=-1, keepdims=True))
    alpha = jnp.exp(m_old - m_new)
    e = jnp.where(msk, jnp.exp(sc - m_new), 0.0)
    l_ref[g] = alpha * l_ref[g] + jnp.sum(e, axis=-1, keepdims=True)
    acc_ref[g] = alpha * acc_ref[g] + _dot(e.astype(BF16), v)
    m_ref[g] = m_new


def _sample_attn_body(pt_ref, slopes_ref, *refs, past, t_new):
    pages = refs[:PAGES_PER_STEP]
    q_ref, sel_ref, new_ref, win_ref, oc_ref, gt_ref, o_ref, m_scr, l_scr, acc_scr = refs[PAGES_PER_STEP:]
    c = pl.program_id(1)
    rows = GQA * TP
    row = _iota((rows, 1), 0)
    qpos = past + (row & (TP - 1))
    r_of_row = row >> 3
    ck = PAGES_PER_STEP * pages[0].shape[1] // KV_ROWS

    @pl.when(c == 0)
    def _():
        m_scr[...] = jnp.full(m_scr.shape, NEG, F32)
        l_scr[...] = jnp.zeros(l_scr.shape, F32)
        acc_scr[...] = jnp.zeros(acc_scr.shape, F32)

    dist_s = qpos - (c * ck + _iota((rows, ck), 1))
    expand = jnp.where((_iota((LANES, ck), 1) >> 6) + c * (ck // L_SEL) == _iota((LANES, ck), 0), 1.0, 0.0).astype(BF16)
    slopes = []
    for g in range(N_KV):
        slope = jnp.zeros((rows, 1), F32)
        for r in range(GQA):
            slope = jnp.where(r_of_row == r, slopes_ref[g * GQA + r], slope)
        slopes.append(slope)
        q = q_ref[0, g]
        kch = _page_rows(pages, g).astype(BF16)
        vch = _page_rows(pages, 2 + g).astype(BF16)
        picked = _dot(sel_ref[0, g], expand)
        msk = jnp.where(dist_s >= 0, picked, 0.0) > 0.5
        sc = jnp.where(msk, _dot_nt(q, kch) - slope * dist_s.astype(F32), NEG)
        _online_update(m_scr, l_scr, acc_scr, g, sc, msk, vch)

    @pl.when(c == pl.num_programs(1) - 1)
    def _():
        col = _iota((rows, TP), 1)
        dist_n = qpos - (past + col)
        mask_n = jnp.where(col < t_new, dist_n, -1) >= 0
        dist_c = qpos - (past - WINDOW + _iota((rows, WINDOW), 1))
        mask_c = jnp.where(dist_c >= 0, dist_c, WINDOW) < WINDOW
        mask_wn = jnp.where(mask_n, dist_n, WINDOW) < WINDOW
        for g in range(N_KV):
            slope = slopes[g]
            q = q_ref[0, g]
            kn = new_ref[0, :, (4 + g) * DH:(5 + g) * DH]
            vn = new_ref[0, :, (6 + g) * DH:(7 + g) * DH]
            sc = jnp.where(mask_n, _dot_nt(q, kn) - slope * dist_n.astype(F32), NEG)
            _online_update(m_scr, l_scr, acc_scr, g, sc, mask_n, vn)
            o_s = acc_scr[g] / jnp.maximum(l_scr[g], 1e-30)
            kwc = win_ref[0, pl.ds(g, WINDOW, KV_ROWS), :].astype(BF16)
            vwc = win_ref[0, pl.ds(2 + g, WINDOW, KV_ROWS), :].astype(BF16)
            kwn = new_ref[0, :, (8 + g) * DH:(9 + g) * DH]
            vwn = new_ref[0, :, (10 + g) * DH:(11 + g) * DH]
            s1 = jnp.where(mask_c, _dot_nt(q, kwc) - slope * dist_c.astype(F32), NEG)
            s2 = jnp.where(mask_wn, _dot_nt(q, kwn) - slope * dist_n.astype(F32), NEG)
            mx = jnp.maximum(jnp.max(s1, axis=-1, keepdims=True), jnp.max(s2, axis=-1, keepdims=True))
            e1 = jnp.where(mask_c, jnp.exp(s1 - mx), 0.0)
            e2 = jnp.where(mask_wn, jnp.exp(s2 - mx), 0.0)
            den = jnp.maximum(jnp.sum(e1, axis=-1, keepdims=True) + jnp.sum(e2, axis=-1, keepdims=True), 1e-30)
            o_w = _dot((e1 / den).astype(BF16), vwc) + _dot((e2 / den).astype(BF16), vwn)
            gt = gt_ref[0, g]
            o_ref[0, g] = gt[:, 0:1] * oc_ref[0, g] + gt[:, 1:2] * o_s + gt[:, 2:3] * o_w


def _sample_attention(slopes, page_ids, cache_sel3, q_s, sel_s, new_kvb, cache_win3, win_base, o_c, gates_s, past,
                      t_new):
    page_rows = cache_sel3.shape[1]
    bsz = q_s.shape[0]
    n_pages = page_ids.shape[0] // bsz
    n_steps = n_pages // PAGES_PER_STEP
    rows = GQA * TP
    assert cache_win3.shape[1] == WINDOW * KV_ROWS and past % L_SEL == 0 and t_new <= TP
    page_spec = lambda i: pl.BlockSpec((1, page_rows, DH),
                                       lambda b, s, pt: (pt[b * n_pages + s * PAGES_PER_STEP + i], 0, 0))
    per_b = lambda shape: pl.BlockSpec((1,) + shape, lambda b, s, pt: (b,) + (0,) * len(shape))
    grid_spec = pltpu.PrefetchScalarGridSpec(
        num_scalar_prefetch=1,
        grid=(bsz, n_steps),
        in_specs=[pl.BlockSpec(memory_space=pltpu.SMEM)] + [page_spec(i) for i in range(PAGES_PER_STEP)]
        + [per_b((N_KV, rows, DH)), per_b((N_KV, rows, LANES)), per_b((TP, 12 * DH)),
           pl.BlockSpec((1, WINDOW * KV_ROWS, DH), lambda b, s, pt: (win_base + b, 0, 0)),
           per_b((N_KV, rows, DH)), per_b((N_KV, rows, 3))],
        out_specs=per_b((N_KV, rows, DH)),
        scratch_shapes=[pltpu.VMEM((N_KV, rows, 1), F32), pltpu.VMEM((N_KV, rows, 1), F32),
                        pltpu.VMEM((N_KV, rows, DH), F32)],
    )
    return pl.pallas_call(
        functools.partial(_sample_attn_body, past=past, t_new=t_new),
        grid_spec=grid_spec,
        out_shape=jax.ShapeDtypeStruct((bsz, N_KV, rows, DH), F32),
        compiler_params=_params(("parallel", "arbitrary")),
        name="sample_attention",
    )(page_ids, slopes, *([cache_sel3] * PAGES_PER_STEP), q_s, sel_s, new_kvb, cache_win3, o_c, gates_s)


def _gelu_tanh(x):
    return 0.5 * x * (1.0 + jnp.tanh(0.7978845608028654 * (x + 0.044715 * (x * x * x))))


def _lru_coeffs(conv, wa_ref, ba_ref, wi_ref, bi_ref, sp_ref):
    cb = conv.astype(BF16)
    n_blk, blk = wa_ref.shape[0], wa_ref.shape[1]
    ra = jnp.concatenate([_dot(cb[:, n * blk:(n + 1) * blk], wa_ref[n]) for n in range(n_blk)], axis=1)
    ri = jnp.concatenate([_dot(cb[:, n * blk:(n + 1) * blk], wi_ref[n]) for n in range(n_blk)], axis=1)
    r = jax.nn.sigmoid(ra + ba_ref[...])
    gi = jax.nn.sigmoid(ri + bi_ref[...])
    log_a = -LRU_C * r * sp_ref[...]
    a = jnp.exp(log_a)
    b = jnp.sqrt(-jnp.tanh(log_a) * (a * a + 1.0)) * (gi * conv)
    return a, b


def _lru_seq_body(xr_ref, yr_ref, cs_ref, h0_ref, cw_ref, cb_ref, wa_ref, ba_ref, wi_ref, bi_ref, sp_ref, onw_ref,
                  rec_ref, hl_ref, xbuf, h_scr):
    k = pl.program_id(1)
    tt = xr_ref.shape[1]
    pad = 8

    @pl.when(k == 0)
    def _():
        xbuf[0:pad, :] = jnp.zeros((pad, xbuf.shape[1]), F32)
        xbuf[pad - (CONV_W - 1):pad, :] = cs_ref[0]
        h_scr[...] = h0_ref[0]

    x = xr_ref[0]
    xbuf[pad:pad + tt, :] = x
    conv = cb_ref[...] + cw_ref[CONV_W - 1:CONV_W, :] * x
    for j in range(CONV_W - 1):
        conv = conv + cw_ref[j:j + 1, :] * xbuf[pad - (CONV_W - 1) + j:pad - (CONV_W - 1) + j + tt, :]
    xbuf[0:pad, :] = x[tt - pad:tt]
    a, b = _lru_coeffs(conv, wa_ref, ba_ref, wi_ref, bi_ref, sp_ref)
    row = _iota((tt, 1), 0)
    s = 1
    while s < tt:
        keep = row >= s
        a_sh = jnp.where(keep, pltpu.roll(a, s, 0), 1.0)
        b_sh = jnp.where(keep, pltpu.roll(b, s, 0), 0.0)
        b = a * b_sh + b
        a = a * a_sh
        s *= 2
    hs = a * h_scr[...] + b
    h_scr[...] = hs[tt - 1:tt]
    hl_ref[0] = hs[tt - 1:tt]
    rec = hs * _gelu_tanh(yr_ref[0])
    rec_ref[0] = _rms(rec, onw_ref[...]).astype(BF16)


def _lru_seq(xr3, yr3, cs, h0, cw, cb, wa, ba, wi, bi, sp, onw, tt):
    bsz, t_len, d = xr3.shape
    seq = pl.BlockSpec((1, tt, d), lambda b, k: (b, k, 0))
    full = lambda a: pl.BlockSpec(a.shape, lambda b, k: (0,) * a.ndim)
    return pl.pallas_call(
        _lru_seq_body,
        grid=(bsz, t_len // tt),
        in_specs=[seq, seq, pl.BlockSpec((1, CONV_W - 1, d), lambda b, k: (b, 0, 0)),
                  pl.BlockSpec((1, 1, d), lambda b, k: (b, 0, 0)),
                  full(cw), full(cb), full(wa), full(ba), full(wi), full(bi), full(sp), full(onw)],
        out_specs=[seq, pl.BlockSpec((1, 1, d), lambda b, k: (b, 0, 0))],
        out_shape=[jax.ShapeDtypeStruct((bsz, t_len, d), BF16), jax.ShapeDtypeStruct((bsz, 1, d), F32)],
        scratch_shapes=[pltpu.VMEM((tt + 8, d), F32), pltpu.VMEM((1, d), F32)],
        compiler_params=_params(("parallel", "arbitrary")),
        name="lru_seq",
    )(xr3, yr3, cs, h0, cw, cb, wa, ba, wi, bi, sp, onw)


def _lru_step_body(xr_ref, yr_ref, cs_ref, h0_ref, cw_ref, cb_ref, wa_ref, ba_ref, wi_ref, bi_ref, sp_ref, onw_ref,
                   rec_ref, hl_ref):
    t_len, bsz = xr_ref.shape[0], xr_ref.shape[1]
    xs = [cs_ref[j] for j in range(CONV_W - 1)] + [xr_ref[t] for t in range(t_len)]
    convs = []
    for t in range(t_len):
        conv = cb_ref[...] + cw_ref[0:1, :] * xs[t]
        for j in range(1, CONV_W):
            conv = conv + cw_ref[j:j + 1, :] * xs[t + j]
        convs.append(conv)
    a, b = _lru_coeffs(jnp.concatenate(convs, axis=0), wa_ref, ba_ref, wi_ref, bi_ref, sp_ref)
    h = h0_ref[...]
    for t in range(t_len):
        h = a[t * bsz:(t + 1) * bsz] * h + b[t * bsz:(t + 1) * bsz]
        rec = h * _gelu_tanh(yr_ref[t])
        rec_ref[t] = _rms(rec, onw_ref[...]).astype(BF16)
    hl_ref[...] = h


def _lru_step(xr_t, yr_t, cs_t, h0, cw, cb, wa, ba, wi, bi, sp, onw):
    t_len, bsz, d = xr_t.shape
    return pl.pallas_call(
        _lru_step_body,
        out_shape=[jax.ShapeDtypeStruct((t_len, bsz, d), BF16), jax.ShapeDtypeStruct((bsz, d), F32)],
        compiler_params=pltpu.CompilerParams(vmem_limit_bytes=VMEM_LIMIT),
        name="lru_step",
    )(xr_t, yr_t, cs_t, h0, cw, cb, wa, ba, wi, bi, sp, onw)


def _outproj_body(attn_ref, rec_ref, x_ref, cnt0_ref, anw_ref, woa_ref, wor_ref, fnw_ref, wr_ref, br_ref,
                  x1_ref, h2_ref, ti_ref, tg_ref, tp_ref, cnt_ref, carry):
    step = pl.program_id(0)
    tm = x_ref.shape[0]

    @pl.when(step == 0)
    def _():
        carry[...] = cnt0_ref[...]

    attn = jnp.concatenate(
        [jnp.concatenate([attn_ref[b, hd] for hd in range(N_HEADS)], axis=1) for b in range(attn_ref.shape[0])], axis=0)
    an = _rms(attn, anw_ref[...]).astype(BF16)
    x1 = x_ref[...] + (_dot(an, woa_ref[...]) + _dot(rec_ref[...], wor_ref[...]))
    x1_ref[...] = x1
    h2 = _rms(x1, fnw_ref[...])
    h2_ref[...] = h2
    lane = _iota((tm, LANES), 1)
    logits = jnp.dot(h2, wr_ref[...], precision=lax.Precision.HIGHEST, preferred_element_type=F32) + br_ref[...]
    lg = jnp.where(lane < N_EXPERTS, logits, -jnp.inf)
    vals, idxs = [], []
    for _ in range(TOP_K):
        m = jnp.max(lg, axis=-1, keepdims=True)
        ix = jnp.min(jnp.where(lg == m, lane, LANES), axis=-1, keepdims=True)
        vals.append(m)
        idxs.append(ix)
        lg = jnp.where(lane == ix, -jnp.inf, lg)
    es = [jnp.exp(v - vals[0]) for v in vals]
    den = es[0]
    for e in es[1:]:
        den = den + e
    onehot = jnp.zeros((tm, LANES), F32)
    for ix in idxs:
        onehot = jnp.where(lane == ix, 1.0, onehot)
    lower = jnp.where(_iota((tm, tm), 0) > _iota((tm, tm), 1), 1.0, 0.0).astype(BF16)
    rank = carry[...] + _dot(lower, onehot.astype(BF16))
    carry[...] = carry[...] + jnp.sum(onehot, axis=0, keepdims=True)
    ti = jnp.zeros((tm, LANES), I32)
    tg = jnp.zeros((tm, LANES), F32)
    tp = jnp.zeros((tm, LANES), I32)
    for k in range(TOP_K):
        pos = jnp.sum(jnp.where(lane == idxs[k], rank, 0.0), axis=-1, keepdims=True).astype(I32)
        ti = jnp.where(lane == k, idxs[k], ti)
        tg = jnp.where(lane == k, es[k] / den, tg)
        tp = jnp.where(lane == k, pos, tp)
    ti_ref[...] = ti
    tg_ref[...] = tg
    tp_ref[...] = tp
    cnt_ref[...] = carry[...]


def _outproj_router(attn_blk, recn, x2, cnt0, anw, woa, wor, fnw, wr, br, tm):
    n, d = x2.shape
    nqb = tm // QB
    row = lambda i: (i, 0)
    wspec = lambda a: pl.BlockSpec(a.shape, lambda i: (0,) * a.ndim, pipeline_mode=pl.Buffered(1))
    lanes_out = lambda dt: jax.ShapeDtypeStruct((n, LANES), dt)
    return pl.pallas_call(
        _outproj_body,
        grid=(n // tm,),
        in_specs=[pl.BlockSpec((nqb, N_HEADS, QB, DH), lambda i: (i, 0, 0, 0)),
                  pl.BlockSpec((tm, recn.shape[1]), row), pl.BlockSpec((tm, d), row),
                  wspec(cnt0), wspec(anw), wspec(woa), wspec(wor), wspec(fnw), wspec(wr), wspec(br)],
        out_specs=[pl.BlockSpec((tm, d), row), pl.BlockSpec((tm, d), row), pl.BlockSpec((tm, LANES), row),
                   pl.BlockSpec((tm, LANES), row), pl.BlockSpec((tm, LANES), row), pl.BlockSpec((1, LANES), lambda i: (0, 0))],
        out_shape=[jax.ShapeDtypeStruct((n, d), F32), jax.ShapeDtypeStruct((n, d), F32),
                   lanes_out(I32), lanes_out(F32), lanes_out(I32), jax.ShapeDtypeStruct((1, LANES), F32)],
        scratch_shapes=[pltpu.VMEM((1, LANES), F32)],
        compiler_params=_params(("arbitrary",)),
        name="outproj_router",
    )(attn_blk, recn, x2, cnt0, anw, woa, wor, fnw, wr, br)


ISSUE_UNROLL = 8


def _gather_body(rowtok_ref, nused_ref, h2_hbm, out_ref, buf, sem):
    j = pl.program_id(0)
    tm = buf.shape[1]
    n_used = nused_ref[0]

    def issue_block(blk, slot):
        def body(it, c):
            for u in range(ISSUE_UNROLL):
                r = it * ISSUE_UNROLL + u
                tok = rowtok_ref[blk * tm + r]
                pltpu.make_async_copy(h2_hbm.at[pl.ds(tok, 1), :], buf.at[slot, pl.ds(r, 1), :],
                                      sem.at[slot]).start(priority=u % 2)
            return c

        lax.fori_loop(0, tm // ISSUE_UNROLL, body, 0)

    @pl.when((j == 0) & (n_used > 0))
    def _():
        issue_block(0, 0)

    @pl.when(j + 1 < n_used)
    def _():
        issue_block(j + 1, (j + 1) % 2)

    @pl.when(j < n_used)
    def _():
        slot = j % 2
        pltpu.make_async_copy(h2_hbm.at[pl.ds(0, tm), :], buf.at[slot], sem.at[slot]).wait()
        out_ref[...] = buf[slot].astype(BF16)

    @pl.when(j >= n_used)
    def _():
        out_ref[...] = jnp.zeros(out_ref.shape, BF16)


def _gather_rows(row_tok, n_used, h2, n_blocks):
    n, d = h2.shape
    tm = MOE_TM
    grid_spec = pltpu.PrefetchScalarGridSpec(
        num_scalar_prefetch=2,
        grid=(n_blocks,),
        in_specs=[pl.BlockSpec(memory_space=pl.ANY)],
        out_specs=pl.BlockSpec((tm, d), lambda j, rt, nu: (j, 0)),
        scratch_shapes=[pltpu.VMEM((2, tm, d), F32), pltpu.SemaphoreType.DMA((2,))],
    )
    return pl.pallas_call(
        _gather_body,
        grid_spec=grid_spec,
        out_shape=jax.ShapeDtypeStruct((n_blocks * tm, d), BF16),
        compiler_params=_params(("arbitrary",)),
        name="moe_gather",
    )(row_tok, n_used, h2)


COPY_SPLIT = 4


def _expert_blocks(first_row, n_blk, n_used, n_blocks, x_hbm, out_hbm, col0, xbuf, obuf, xsem, osem, prepare,
                   compute):
    tm = xbuf.shape[1]
    tc = obuf.shape[2]
    rc = tm // COPY_SPLIT

    def x_copies(b, slot):
        r = pl.multiple_of(first_row + b * tm, tm)
        return [pltpu.make_async_copy(x_hbm.at[pl.ds(r + s * rc, rc), :], xbuf.at[slot, pl.ds(s * rc, rc), :],
                                      xsem.at[slot]) for s in range(COPY_SPLIT)]

    def o_copies(r, slot):
        r = pl.multiple_of(r, tm)
        return [pltpu.make_async_copy(obuf.at[slot, pl.ds(s * rc, rc), :],
                                      out_hbm.at[pl.ds(r + s * rc, rc), pl.ds(col0, tc)], osem.at[slot])
                for s in range(COPY_SPLIT)]

    def start(copies):
        for cp in copies:
            cp.start()

    def wait(copies):
        for cp in copies:
            cp.wait()

    @pl.when(n_blk > 0)
    def _():
        start(x_copies(0, 0))
        prepare()

        def body(b, c):
            slot = b % 2

            @pl.when(b + 1 < n_blk)
            def _():
                start(x_copies(b + 1, 1 - slot))

            wait(x_copies(b, slot))

            @pl.when(b >= 2)
            def _():
                wait(o_copies(first_row + (b - 2) * tm, slot))

            obuf[slot] = compute(xbuf[slot])
            start(o_copies(first_row + b * tm, slot))
            return c

        lax.fori_loop(0, n_blk, body, 0)

        @pl.when(n_blk >= 2)
        def _():
            wait(o_copies(first_row + (n_blk - 2) * tm, n_blk % 2))

        wait(o_copies(first_row + (n_blk - 1) * tm, (n_blk - 1) % 2))

    @pl.when(pl.program_id(0) == pl.num_programs(0) - 1)
    def _():
        obuf[0] = jnp.zeros(obuf.shape[1:], obuf.dtype)

        def fill(t, c):
            start(o_copies(t * tm, 0))
            wait(o_copies(t * tm, 0))
            return c

        lax.fori_loop(n_used, n_blocks, fill, 0)


W_SPLIT = 4


def _moe_up_body(row0_ref, nblk_ref, nu_ref, xs_hbm, *refs, n_blocks):
    w_refs = refs[:W_SPLIT]
    b_ref, perm_ref, h_hbm, wbf, xbuf, obuf, xsem, osem = refs[W_SPLIT:]
    e = pl.program_id(0)
    tn = wbf.shape[1]
    pw = perm_ref.shape[0]
    hw = pw // 2
    kc = w_refs[0].shape[1]

    def prepare():
        for q, w_ref in enumerate(w_refs):
            for c in range(tn // pw):
                w = w_ref[0, :, c * pw:(c + 1) * pw].astype(BF16)
                wbf[q * kc:(q + 1) * kc, c * pw:(c + 1) * pw] = _dot(w, perm_ref[...]).astype(BF16)

    def compute(x):
        gu = _dot(x, wbf[...]) + b_ref[0]
        acts = []
        for c in range(tn // pw):
            glu = jnp.minimum(gu[:, c * pw:c * pw + hw], SWIGLU_LIMIT)
            lin = jnp.clip(gu[:, c * pw + hw:(c + 1) * pw], -SWIGLU_LIMIT, SWIGLU_LIMIT)
            acts.append((glu * jax.nn.sigmoid(SWIGLU_ALPHA * glu) * (lin + 1.0)).astype(BF16))
        return jnp.concatenate(acts, axis=1)

    col0 = pl.multiple_of(pl.program_id(1) * (tn // 2), tn // 2)
    _expert_blocks(row0_ref[e], nblk_ref[e], nu_ref[0], n_blocks, xs_hbm, h_hbm, col0, xbuf, obuf, xsem, osem,
                   prepare, compute)


def _w_specs(k, tn):
    return [pl.BlockSpec((1, k // W_SPLIT, tn), lambda e, n, *_, q=q: (e, q, n)) for q in range(W_SPLIT)]


def _moe_up(row0, nblk, n_used, xs, w_gu, b_gu, perm):
    n_rows, d = xs.shape
    n_e, _, f2 = w_gu.shape
    tm, tn = MOE_TM, MOE_TN
    grid_spec = pltpu.PrefetchScalarGridSpec(
        num_scalar_prefetch=3,
        grid=(n_e, f2 // tn),
        in_specs=[pl.BlockSpec(memory_space=pl.ANY)] + _w_specs(d, tn)
        + [pl.BlockSpec((1, 1, tn), lambda e, n, *_: (e, 0, n)),
           pl.BlockSpec(perm.shape, lambda e, n, *_: (0, 0))],
        out_specs=pl.BlockSpec(memory_space=pl.ANY),
        scratch_shapes=[pltpu.VMEM((d, tn), BF16), pltpu.VMEM((2, tm, d), BF16), pltpu.VMEM((2, tm, tn // 2), BF16),
                        pltpu.SemaphoreType.DMA((2,)), pltpu.SemaphoreType.DMA((2,))],
    )
    return pl.pallas_call(
        functools.partial(_moe_up_body, n_blocks=n_rows // tm),
        grid_spec=grid_spec,
        out_shape=jax.ShapeDtypeStruct((n_rows, f2 // 2), BF16),
        compiler_params=_params(("arbitrary", "arbitrary")),
        name="moe_up",
    )(row0, nblk, n_used, xs, *([w_gu] * W_SPLIT), b_gu, perm)


def _moe_down_body(row0_ref, nblk_ref, nu_ref, h_hbm, *refs, n_blocks):
    w_refs = refs[:W_SPLIT]
    b_ref, y_hbm, wbf, xbuf, obuf, xsem, osem = refs[W_SPLIT:]
    e = pl.program_id(0)
    tn = wbf.shape[1]
    kc = w_refs[0].shape[1]

    def prepare():
        for q, w_ref in enumerate(w_refs):
            wbf[q * kc:(q + 1) * kc, :] = w_ref[0].astype(BF16)

    def compute(h):
        return _dot(h, wbf[...]) + b_ref[0]

    col0 = pl.multiple_of(pl.program_id(1) * tn, tn)
    _expert_blocks(row0_ref[e], nblk_ref[e], nu_ref[0], n_blocks, h_hbm, y_hbm, col0, xbuf, obuf, xsem, osem,
                   prepare, compute)


def _moe_down(row0, nblk, n_used, h, w_d, b_d):
    n_rows, f = h.shape
    n_e, _, d = w_d.shape
    tm, tn = MOE_TM, MOE_TN
    grid_spec = pltpu.PrefetchScalarGridSpec(
        num_scalar_prefetch=3,
        grid=(n_e, d // tn),
        in_specs=[pl.BlockSpec(memory_space=pl.ANY)] + _w_specs(f, tn)
        + [pl.BlockSpec((1, 1, tn), lambda e, n, *_: (e, 0, n))],
        out_specs=pl.BlockSpec(memory_space=pl.ANY),
        scratch_shapes=[pltpu.VMEM((f, tn), BF16), pltpu.VMEM((2, tm, f), BF16), pltpu.VMEM((2, tm, tn), F32),
                        pltpu.SemaphoreType.DMA((2,)), pltpu.SemaphoreType.DMA((2,))],
    )
    return pl.pallas_call(
        functools.partial(_moe_down_body, n_blocks=n_rows // tm),
        grid_spec=grid_spec,
        out_shape=jax.ShapeDtypeStruct((n_rows, d), F32),
        compiler_params=_params(("arbitrary", "arbitrary")),
        name="moe_down",
    )(row0, nblk, n_used, h, *([w_d] * W_SPLIT), b_d)


def _combine_body(dest_hbm, gate_ref, x1_ref, y_hbm, outp_ref, outs_ref, dsm, buf, sem_d, sem, *, np_blocks):
    j = pl.program_id(0)
    n = pl.num_programs(0)
    tk = x1_ref.shape[0]
    toks = ISSUE_UNROLL // TOP_K

    def idx_copy(step, slot):
        return pltpu.make_async_copy(dest_hbm.at[step, 0], dsm.at[slot], sem_d.at[slot])

    def issue_rows(slot):
        def body(it, c):
            for u in range(toks):
                t = it * toks + u
                for k in range(TOP_K):
                    src = dsm[slot, t * TOP_K + k]
                    pltpu.make_async_copy(y_hbm.at[pl.ds(src, 1), :], buf.at[slot, k, pl.ds(t, 1), :],
                                          sem.at[slot]).start(priority=k % 2)
            return c

        lax.fori_loop(0, tk // toks, body, 0)

    @pl.when(j == 0)
    def _():
        idx_copy(0, 0).start()
        idx_copy(0, 0).wait()
        issue_rows(0)

        @pl.when(1 < n)
        def _():
            idx_copy(1, 1).start()

    @pl.when(j + 1 < n)
    def _():
        slot = (j + 1) % 2
        idx_copy(j + 1, slot).wait()
        issue_rows(slot)

    @pl.when(j + 2 < n)
    def _():
        idx_copy(j + 2, j % 2).start()

    slot = j % 2
    for k in range(TOP_K):
        pltpu.make_async_copy(y_hbm.at[pl.ds(0, tk), :], buf.at[slot, k], sem.at[slot]).wait()
    g = gate_ref[...]
    acc = g[:, 0:1] * buf[slot, 0]
    for k in range(1, TOP_K):
        acc = acc + g[:, k:k + 1] * buf[slot, k]
    res = x1_ref[...] + acc

    @pl.when(j < np_blocks)
    def _():
        outp_ref[...] = res

    @pl.when(j >= np_blocks)
    def _():
        outs_ref[...] = res


def _combine(dest3, gates, x1, y, n_prompt):
    n, d = x1.shape
    tk = dest3.shape[2] // TOP_K
    np_blocks = n_prompt // tk
    row = lambda j: (j, 0)
    return pl.pallas_call(
        functools.partial(_combine_body, np_blocks=np_blocks),
        grid=(n // tk,),
        in_specs=[pl.BlockSpec(memory_space=pl.ANY), pl.BlockSpec((tk, LANES), row), pl.BlockSpec((tk, d), row),
                  pl.BlockSpec(memory_space=pl.ANY)],
        out_specs=[pl.BlockSpec((tk, d), lambda j: (jnp.minimum(j, np_blocks - 1), 0)),
                   pl.BlockSpec((tk, d), lambda j: (jnp.maximum(j - np_blocks, 0), 0))],
        out_shape=[jax.ShapeDtypeStruct((n_prompt, d), F32), jax.ShapeDtypeStruct((n - n_prompt, d), F32)],
        scratch_shapes=[pltpu.SMEM((2, tk * TOP_K), I32), pltpu.VMEM((2, TOP_K, tk, d), F32),
                        pltpu.SemaphoreType.DMA((2,)), pltpu.SemaphoreType.DMA((2,))],
        compiler_params=_params(("arbitrary",)),
        name="moe_combine",
    )(dest3, gates, x1, y)


def _moe(h2, x1, topi, topg, topp, counts, w_gu, b_gu, w_d, b_d, n_prompt):
    n, d = h2.shape
    tm = MOE_TM
    n_asg = n * TOP_K
    n_blocks = n_asg // tm + N_EXPERTS
    padded = (counts + tm - 1) // tm * tm
    pad_end = jnp.cumsum(padded)
    pad_start = pad_end - padded
    dest = pad_start[topi[:, :TOP_K]] + topp[:, :TOP_K]
    tok = jnp.broadcast_to(jnp.arange(n, dtype=I32)[:, None], (n, TOP_K))
    row_tok = jnp.zeros((n_blocks * tm,), I32).at[dest.reshape(-1)].set(tok.reshape(-1))
    n_used = (pad_end[-1:] // tm).astype(I32)
    row0 = pad_start.astype(I32)
    nblk = (padded // tm).astype(I32)
    f2 = w_gu.shape[2]
    pw = 2 * LANES
    src = jnp.arange(pw)
    perm = (jnp.arange(pw)[:, None] == jnp.where(src < LANES, 2 * src, 2 * (src - LANES) + 1)[None, :]).astype(BF16)
    b_gu_p = b_gu.reshape(N_EXPERTS, f2 // pw, LANES, 2).transpose(0, 1, 3, 2).reshape(N_EXPERTS, 1, f2)
    xs = _gather_rows(row_tok, n_used, h2, n_blocks)
    h = _moe_up(row0, nblk, n_used, xs, w_gu, b_gu_p, perm)
    y = _moe_down(row0, nblk, n_used, h, w_d, b_d.reshape(N_EXPERTS, 1, d))
    tk = 128
    dest3 = dest.astype(I32).reshape(n // tk, 1, tk * TOP_K)
    return _combine(dest3, topg, x1, y, n_prompt)


def _layer(layer, xp, xs, cache_cmp_all, cache_sel_all, cache_win_all, state_conv, state_h, page_table, w):
    (norm_mix_w, w_in, q_norm_w, k_norm_w, w_cmp_k, w_cmp_v, conv_w, conv_b, w_gate_a, b_gate_a, w_gate_i, b_gate_i,
     lru_lambda, out_norm_attn, out_norm_rec, w_out, norm_ffn_w, w_router, b_router, w_gate_up, b_gate_up, w_down,
     b_down) = w
    bp, tp, d = xp.shape
    bs, ts, _ = xs.shape
    d_rec = d - D_ATTN
    past = page_table.shape[1] * cache_cmp_all.shape[2]
    assert bs * ts == QB and ts <= TP

    o1, o2 = D_ATTN, D_ATTN + 12 * DH
    o3 = o2 + 3 * N_HEADS
    wq = w_in[:, :o1].astype(BF16)
    wkv = w_in[:, o1:o2].astype(BF16)
    wg = jnp.pad(w_in[:, o2:o3], ((0, 0), (0, LANES - 3 * N_HEADS))).astype(BF16)
    wxy = w_in[:, o3:].astype(BF16)
    row2 = lambda v: v.reshape(1, -1)
    slopes = jnp.exp2(-8.0 * jnp.arange(1, N_HEADS + 1, dtype=F32) / N_HEADS)
    w4 = jnp.concatenate([w_cmp_k, w_cmp_v], axis=0)
    z4 = jnp.zeros_like(w4)
    wl = jnp.stack([jnp.concatenate([w4, z4], axis=1), jnp.concatenate([z4, w4], axis=1)])
    wl = jnp.broadcast_to(wl[..., None], wl.shape + (DH,))
    sp = row2(jax.nn.softplus(-lru_lambda.astype(F32)))
    wa, wi = w_gate_a.astype(BF16), w_gate_i.astype(BF16)
    woa, wor = w_out[:D_ATTN].astype(BF16), w_out[D_ATTN:].astype(BF16)
    wr = jnp.pad(w_router, ((0, 0), (0, LANES - N_EXPERTS)))
    br = row2(jnp.pad(b_router, (0, LANES - N_EXPERTS)))
    mix = (row2(norm_mix_w), wq, wkv, wg, wxy, row2(q_norm_w), k_norm_w)
    lru_w = (conv_w, row2(conv_b), wa, row2(b_gate_a), wi, row2(b_gate_i), sp, row2(out_norm_rec))

    np_tok = bp * tp
    q_p, cmp_p, sel_p, win_p, kvb_p, gate_p, xr_p, yr_p = _inproj(xp.reshape(np_tok, d), *mix, tm=256)
    kce_p, kco_p = _pool_prompt(cmp_p, wl)
    gates_g = gate_p[:, :3 * N_HEADS].reshape(bp, tp, N_KV, 3 * GQA).transpose(0, 2, 1, 3)
    attn_p = _prompt_attention(slopes, q_p, kvb_p, kce_p, kco_p, gates_g, bp, tp)
    recn_p, hl_p = _lru_seq(xr_p.reshape(bp, tp, d_rec), yr_p.reshape(bp, tp, d_rec),
                            jnp.zeros((bp, CONV_W - 1, d_rec), F32), jnp.zeros((bp, 1, d_rec), F32), *lru_w, tt=256)
    post = (row2(out_norm_attn), woa, wor, row2(norm_ffn_w), wr, br)
    x1_p, h2_p, ti_p, tg_p, tp_p, cnt_p = _outproj_router(
        attn_p, recn_p.reshape(np_tok, d_rec), xp.reshape(np_tok, d), jnp.zeros((1, LANES), F32), *post, tm=256)

    ns_tok = bs * ts
    q_s, cmp_s, sel_s, win_s, kvb_s, gate_s, xr_s, yr_s = _inproj(xs.reshape(ns_tok, d), *mix, tm=QB)
    n_pool, page = cache_cmp_all.shape[1], cache_cmp_all.shape[2]
    page_view = lambda c: c.reshape(c.shape[0] * n_pool, page * KV_ROWS, DH)
    page_ids = (page_table + layer * n_pool).reshape(-1)
    kce_s, kco_s = _pool_pages(page_view(cache_cmp_all), page_ids, bs, wl)
    q_t = q_s[0].reshape(N_KV, GQA, bs, ts, DH).transpose(2, 0, 1, 3, 4)
    q_t = jnp.pad(q_t, ((0, 0), (0, 0), (0, 0), (0, TP - ts), (0, 0))).reshape(bs, N_KV, GQA * TP, DH)
    oc_s, selm_s = _sample_select(slopes, q_t, kce_s, kco_s, past)
    new_kvb = jnp.pad(kvb_s.reshape(bs, ts, 12 * DH), ((0, 0), (0, TP - ts), (0, 0)))
    g_t = gate_s[:, :3 * N_HEADS].reshape(bs, ts, N_KV, GQA, 3).transpose(0, 2, 3, 1, 4)
    g_t = jnp.pad(g_t, ((0, 0), (0, 0), (0, 0), (0, TP - ts), (0, 0))).reshape(bs, N_KV, GQA * TP, 3)
    win_view = cache_win_all.reshape(cache_win_all.shape[0] * bs, cache_win_all.shape[2] * KV_ROWS, DH)
    attn_s = _sample_attention(slopes, page_ids, page_view(cache_sel_all), q_t, selm_s, new_kvb, win_view, layer * bs,
                               oc_s, g_t, past, ts)
    attn_s = attn_s.reshape(bs, N_KV, GQA, TP, DH)[:, :, :, :ts].transpose(1, 2, 0, 3, 4).reshape(1, N_HEADS, QB, DH)
    tmaj = lambda a: a.reshape(bs, ts, d_rec).transpose(1, 0, 2)
    recn_s, hl_s = _lru_step(tmaj(xr_s), tmaj(yr_s), state_conv.transpose(1, 0, 2), state_h, *lru_w)
    x1_s, h2_s, ti_s, tg_s, tp_s, cnt = _outproj_router(
        attn_s, recn_s.transpose(1, 0, 2).reshape(ns_tok, d_rec), xs.reshape(ns_tok, d), cnt_p, *post, tm=QB)

    cat = lambda a, b: jnp.concatenate([a, b], axis=0)
    out_p, out_s = _moe(cat(h2_p, h2_s), cat(x1_p, x1_s), cat(ti_p, ti_s), cat(tg_p, tg_s), cat(tp_p, tp_s),
                        cnt[0, :N_EXPERTS].astype(I32), w_gate_up, b_gate_up, w_down, b_down, np_tok)
    y_p = out_p.reshape(bp, tp, d)
    y_s = out_s.reshape(bs, ts, d)

    kv5 = lambda a, b, t: a.reshape(b, t, 2, N_KV, DH)
    win_len_p = min(WINDOW, tp)
    st_p = (kv5(cmp_p, bp, tp), kv5(sel_p, bp, tp), kv5(win_p, bp, tp)[:, tp - win_len_p:],
            xr_p.reshape(bp, tp, d_rec)[:, tp - (CONV_W - 1):], hl_p.reshape(bp, d_rec))
    cache_win = cache_win_all[layer]
    win_all = jnp.concatenate([cache_win, kv5(win_s, bs, ts)], axis=1)
    xcat = jnp.concatenate([state_conv, xr_s.reshape(bs, ts, d_rec)], axis=1)
    st_s = (kv5(cmp_s, bs, ts), kv5(sel_s, bs, ts), win_all[:, win_all.shape[1] - cache_win.shape[1]:],
            xcat[:, ts:], hl_s)
    return y_p, y_s, st_p, st_s


def kernel(x_prompt, x_sample, cache_cmp_kv, cache_sel_kv, cache_win_kv, state_conv, state_h, page_table, norm_mix_w, w_in, q_norm_w, k_norm_w, w_cmp_k, w_cmp_v, conv_w, conv_b, w_gate_a, b_gate_a, w_gate_i, b_gate_i, lru_lambda, out_norm_attn, out_norm_rec, w_out, norm_ffn_w, w_router, b_router, w_gate_up, b_gate_up, w_down, b_down):
    depth = w_in.shape[0]
    xp, xs = x_prompt, x_sample
    st_ps, st_ss = [], []
    for l in range(depth):
        w = (norm_mix_w[l], w_in[l], q_norm_w[l], k_norm_w[l], w_cmp_k[l], w_cmp_v[l], conv_w[l], conv_b[l],
             w_gate_a[l], b_gate_a[l], w_gate_i[l], b_gate_i[l], lru_lambda[l], out_norm_attn[l], out_norm_rec[l],
             w_out[l], norm_ffn_w[l], w_router[l], b_router[l], w_gate_up[l], b_gate_up[l], w_down[l], b_down[l])
        xp, xs, st_p, st_s = _layer(l, xp, xs, cache_cmp_kv, cache_sel_kv, cache_win_kv, state_conv[l], state_h[l],
                                    page_table, w)
        st_ps.append(st_p)
        st_ss.append(st_s)
    stack = lambda sts, i: jnp.stack([s[i] for s in sts])
    return (xp, xs) + tuple(stack(st_ps, i) for i in range(5)) + tuple(stack(st_ss, i) for i in range(5))
```

```python
import functools

import jax
import jax.numpy as jnp
from jax import lax
from jax.experimental import pallas as pl
from jax.experimental.pallas import tpu as pltpu

F32 = jnp.float32
BF16 = jnp.bfloat16
I32 = jnp.int32

N_HEADS = 8
N_KV = 2
GQA = N_HEADS // N_KV
DH = 128
D_ATTN = N_HEADS * DH
CONV_W = 4
LRU_C = 8.0
L_CMP = 32
L_SEL = 64
N_SEL = 16
WINDOW = 512
QB = 128
N_EXPERTS = 32
TOP_K = 4
SWIGLU_LIMIT = 7.0
SWIGLU_ALPHA = 1.702
EPS = 1e-6
NEG = -1e30
FORCE = 1e9
Q_SCALE = DH ** -0.5
LANES = 128
TP = 8
KV_ROWS = 2 * N_KV
MOE_TM = 256
MOE_TN = 2048
PAGES_PER_STEP = 16
VMEM_LIMIT = 56 * 1024 * 1024


def _dot(a, b):
    return jnp.dot(a, b, preferred_element_type=F32)


def _dot_nt(a, b):
    return lax.dot_general(a, b, (((1,), (1,)), ((), ())), preferred_element_type=F32)


def _iota(shape, dim):
    return lax.broadcasted_iota(I32, shape, dim)


def _rms(x, w):
    return x * lax.rsqrt(jnp.mean(x * x, axis=-1, keepdims=True) + EPS) * w


def _masked_softmax_rows(s, mask):
    s = jnp.where(mask, s, NEG)
    e = jnp.where(mask, jnp.exp(s - jnp.max(s, axis=-1, keepdims=True)), 0.0)
    return e / jnp.maximum(jnp.sum(e, axis=-1, keepdims=True), 1e-30)


def _params(sem, vmem=None):
    return pltpu.CompilerParams(dimension_semantics=sem, vmem_limit_bytes=vmem or VMEM_LIMIT)


def _inproj_body(x_ref, nw_ref, wq_ref, wkv_ref, wg_ref, wxy_ref, qnw_ref, knw_ref,
                 q_ref, cmp_ref, sel_ref, win_ref, kvb_ref, gate_ref, xr_ref, yr_ref):
    x = x_ref[...]
    h = _rms(x, nw_ref[...]).astype(BF16)
    q = _dot(h, wq_ref[...])
    qnw = qnw_ref[...]
    nqb = q_ref.shape[0]
    for hd in range(N_HEADS):
        qn = (_rms(q[:, hd * DH:(hd + 1) * DH], qnw) * Q_SCALE).astype(BF16)
        for b in range(nqb):
            q_ref[b, hd] = qn[b * QB:(b + 1) * QB]
    kv = _dot(h, wkv_ref[...])
    tm = x.shape[0]
    outs = (cmp_ref, sel_ref, win_ref)
    for br in range(3):
        knw = knw_ref[br:br + 1, :]
        for g in range(N_KV):
            c0 = br * 4 * DH + g * DH
            kn = _rms(kv[:, c0:c0 + DH], knw)
            v = kv[:, c0 + 2 * DH:c0 + 3 * DH]
            outs[br][pl.ds(g, tm, KV_ROWS), :] = kn
            outs[br][pl.ds(2 + g, tm, KV_ROWS), :] = v
            kvb_ref[:, c0:c0 + DH] = kn.astype(BF16)
            kvb_ref[:, c0 + 2 * DH:c0 + 3 * DH] = v.astype(BF16)
    gate_ref[...] = jax.nn.sigmoid(_dot(h, wg_ref[...]))
    xy = _dot(h, wxy_ref[...])
    d_rec = xr_ref.shape[1]
    xr_ref[...] = xy[:, :d_rec]
    yr_ref[...] = xy[:, d_rec:]


def _inproj(x2, nw, wq, wkv, wg, wxy, qnw, knw, tm):
    n, d = x2.shape
    d_rec = wxy.shape[1] // 2
    nqb = tm // QB
    row = lambda i: (i, 0)
    const = lambda i: (0, 0)
    wspec = lambda a: pl.BlockSpec(a.shape, const, pipeline_mode=pl.Buffered(1))
    return pl.pallas_call(
        _inproj_body,
        grid=(n // tm,),
        in_specs=[pl.BlockSpec((tm, d), row), wspec(nw), wspec(wq), wspec(wkv), wspec(wg), wspec(wxy),
                  wspec(qnw), wspec(knw)],
        out_specs=[pl.BlockSpec((nqb, N_HEADS, QB, DH), lambda i: (i, 0, 0, 0)),
                   pl.BlockSpec((tm * KV_ROWS, DH), row), pl.BlockSpec((tm * KV_ROWS, DH), row),
                   pl.BlockSpec((tm * KV_ROWS, DH), row),
                   pl.BlockSpec((tm, 12 * DH), row), pl.BlockSpec((tm, LANES), row),
                   pl.BlockSpec((tm, d_rec), row), pl.BlockSpec((tm, d_rec), row)],
        out_shape=[jax.ShapeDtypeStruct((n // QB, N_HEADS, QB, DH), BF16),
                   jax.ShapeDtypeStruct((n * KV_ROWS, DH), F32), jax.ShapeDtypeStruct((n * KV_ROWS, DH), F32),
                   jax.ShapeDtypeStruct((n * KV_ROWS, DH), F32), jax.ShapeDtypeStruct((n, 12 * DH), BF16),
                   jax.ShapeDtypeStruct((n, LANES), F32),
                   jax.ShapeDtypeStruct((n, d_rec), F32), jax.ShapeDtypeStruct((n, d_rec), F32)],
        compiler_params=_params(("parallel",)),
        name="inproj",
    )(x2, nw, wq, wkv, wg, wxy, qnw, knw)


def _pool_combo(rows, wl_ref, combo):
    r = rows.shape[0] // (2 * L_CMP)
    x3 = rows.reshape(r, 2 * L_CMP, rows.shape[1])
    even = jnp.sum(x3 * wl_ref[0, combo][None], axis=1)
    odd = jnp.sum(x3 * wl_ref[1, combo][None], axis=1)
    return even.astype(BF16), odd.astype(BF16)


def _pool_body(x_ref, wl_ref, e_ref, o_ref):
    n_tok = x_ref.shape[0] // KV_ROWS
    for combo in range(KV_ROWS):
        even, odd = _pool_combo(x_ref[pl.ds(combo, n_tok, KV_ROWS), :], wl_ref, combo)
        e_ref[:, combo * DH:(combo + 1) * DH] = even
        o_ref[:, combo * DH:(combo + 1) * DH] = odd


def _pool_prompt(cmp4, wl):
    n = cmp4.shape[0] // KV_ROWS
    toks = 1024
    ob = toks // (2 * L_CMP)
    c = KV_ROWS * DH
    return pl.pallas_call(
        _pool_body,
        grid=(n // toks,),
        in_specs=[pl.BlockSpec((toks * KV_ROWS, DH), lambda i: (i, 0)), pl.BlockSpec(wl.shape, lambda i: (0, 0, 0, 0))],
        out_specs=[pl.BlockSpec((ob, c), lambda i: (i, 0)), pl.BlockSpec((ob, c), lambda i: (i, 0))],
        out_shape=[jax.ShapeDtypeStruct((n // (2 * L_CMP), c), BF16)] * 2,
        compiler_params=_params(("parallel",)),
        name="pool_prompt",
    )(cmp4, wl)


def _page_rows(pages, combo):
    n_tok = pages[0].shape[1] // KV_ROWS
    return jnp.concatenate([p[0, pl.ds(combo, n_tok, KV_ROWS), :] for p in pages], axis=0)


def _pool_pages_body(pt_ref, *refs):
    pages = refs[:PAGES_PER_STEP]
    wl_ref, e_ref, o_ref = refs[PAGES_PER_STEP:]
    for combo in range(KV_ROWS):
        even, odd = _pool_combo(_page_rows(pages, combo), wl_ref, combo)
        e_ref[0, :, combo * DH:(combo + 1) * DH] = even
        o_ref[0, :, combo * DH:(combo + 1) * DH] = odd


def _pool_pages(cache3, page_ids, bsz, wl):
    rows = cache3.shape[1]
    n_pages = page_ids.shape[0] // bsz
    n_steps = n_pages // PAGES_PER_STEP
    ob = PAGES_PER_STEP * (rows // KV_ROWS) // (2 * L_CMP)
    c = KV_ROWS * DH
    page_spec = lambda i: pl.BlockSpec((1, rows, DH), lambda b, s, pt: (pt[b * n_pages + s * PAGES_PER_STEP + i], 0, 0))
    grid_spec = pltpu.PrefetchScalarGridSpec(
        num_scalar_prefetch=1,
        grid=(bsz, n_steps),
        in_specs=[page_spec(i) for i in range(PAGES_PER_STEP)] + [pl.BlockSpec(wl.shape, lambda b, s, pt: (0, 0, 0, 0))],
        out_specs=[pl.BlockSpec((1, ob, c), lambda b, s, pt: (b, s, 0))] * 2,
    )
    return pl.pallas_call(
        _pool_pages_body,
        grid_spec=grid_spec,
        out_shape=[jax.ShapeDtypeStruct((bsz, n_steps * ob, c), BF16)] * 2,
        compiler_params=_params(("parallel", "parallel")),
        name="pool_pages",
    )(page_ids, *([cache3] * PAGES_PER_STEP), wl)


def _select_blocks_cols(score, blk, n_pick, n_blk):
    sel = jnp.zeros(score.shape, F32)
    for _ in range(n_pick):
        m = jnp.max(score, axis=0, keepdims=True)
        idx = jnp.min(jnp.where(score == m, blk, n_blk), axis=0, keepdims=True)
        hit = blk == idx
        sel = jnp.where(hit, 1.0, sel)
        score = jnp.where(hit, -jnp.inf, score)
    return sel


def _pattn_body(slopes_ref, q_ref, ksel_ref, vsel_ref, kwin_ref, vwin_ref, kce_ref, kco_ref, vce_ref, vco_ref,
                gt_ref, o_ref, m_scr, l_scr, acc_scr):
    g = pl.program_id(1)
    i = pl.program_id(2)
    rows = GQA * QB
    q = q_ref[0].reshape(rows, DH)
    row = _iota((rows, 1), 0)
    qpos = i * QB + (row & (QB - 1))
    r_of_row = row >> 7
    slope = jnp.zeros((rows, 1), F32)
    for r in range(GQA):
        slope = jnp.where(r_of_row == r, slopes_ref[g * GQA + r], slope)

    kc = jnp.concatenate([kce_ref[...], kco_ref[...]], axis=0)
    vc = jnp.concatenate([vce_ref[...], vco_ref[...]], axis=0)
    nb = kc.shape[0]
    half = nb // 2
    lane = _iota((rows, nb), 1)
    blk_c = jnp.where(lane < half, 2 * lane, 2 * (lane - half) + 1)
    dist = qpos - (blk_c * L_CMP + (L_CMP - 1))
    mask = dist >= 0
    s = _dot_nt(q, kc) - slope * dist.astype(F32)
    p = _masked_softmax_rows(s, mask)
    o_c = _dot(p.astype(BF16), vc)
    imp = p[0:QB]
    for r in range(1, GQA):
        imp = imp + p[r * QB:(r + 1) * QB]
    imp_t = imp.T
    pair = imp_t[:half] + imp_t[half:]
    ns = half
    blk = _iota((ns, QB), 0)
    qp = i * QB + _iota((ns, QB), 1)
    cur = qp >> 6
    forced = (blk == 0) | (blk == cur) | (blk == cur - 1)
    score = jnp.where(blk * L_SEL <= qp, jnp.where(forced, FORCE, pair), NEG)
    sel = _select_blocks_cols(score, blk, min(N_SEL, ns), ns)
    sel = jnp.concatenate([sel, jnp.zeros((LANES - ns, QB), F32)], axis=0) if ns < LANES else sel
    sel_q = sel.T.astype(BF16)
    sel4 = jnp.concatenate([sel_q] * GQA, axis=0)

    m_scr[...] = jnp.full(m_scr.shape, NEG, F32)
    l_scr[...] = jnp.zeros(l_scr.shape, F32)
    acc_scr[...] = jnp.zeros(acc_scr.shape, F32)
    ck = 512

    def chunk(c, carry):
        k0 = pl.multiple_of(c * ck, ck)
        kch = ksel_ref[pl.ds(k0, ck), :]
        vch = vsel_ref[pl.ds(k0, ck), :]
        dist_s = qpos - (k0 + _iota((rows, ck), 1))
        expand = jnp.where((_iota((LANES, ck), 1) >> 6) + c * (ck // L_SEL) == _iota((LANES, ck), 0), 1.0, 0.0)
        picked = _dot(sel4, expand.astype(BF16))
        msk = jnp.where(dist_s >= 0, picked, 0.0) > 0.5
        sc = jnp.where(msk, _dot_nt(q, kch) - slope * dist_s.astype(F32), NEG)
        m_old = m_scr[...]
        m_new = jnp.maximum(m_old, jnp.max(sc, axis=-1, keepdims=True))
        alpha = jnp.exp(m_old - m_new)
        e = jnp.where(msk, jnp.exp(sc - m_new), 0.0)
        l_scr[...] = alpha * l_scr[...] + jnp.sum(e, axis=-1, keepdims=True)
        acc_scr[...] = alpha * acc_scr[...] + _dot(e.astype(BF16), vch)
        m_scr[...] = m_new
        return carry

    lax.fori_loop(0, (i >> 2) + 1, chunk, 0)
    o_s = acc_scr[...] / jnp.maximum(l_scr[...], 1e-30)

    span = WINDOW + QB
    start = pl.multiple_of(jnp.maximum(i * QB - WINDOW, 0), QB)
    kw = kwin_ref[pl.ds(start, span), :]
    vw = vwin_ref[pl.ds(start, span), :]
    dist_w = qpos - (start + _iota((rows, span), 1))
    mask_w = jnp.where(dist_w >= 0, dist_w, WINDOW) < WINDOW
    s_w = _dot_nt(q, kw) - slope * dist_w.astype(F32)
    o_w = _dot(_masked_softmax_rows(s_w, mask_w).astype(BF16), vw)

    gt = gt_ref[0, 0]
    for r in range(GQA):
        sl = slice(r * QB, (r + 1) * QB)
        o_ref[0, r] = (gt[:, 3 * r:3 * r + 1] * o_c[sl] + gt[:, 3 * r + 1:3 * r + 2] * o_s[sl]
                       + gt[:, 3 * r + 2:3 * r + 3] * o_w[sl])


def _prompt_attention(slopes, q_blk, kvb, kce, kco, gates_g, bsz, t_len):
    nq = t_len // QB
    nb = t_len // L_CMP
    assert nb % (2 * LANES) == 0 or nb == LANES, "compressed blocks must fill whole lane tiles"
    assert t_len >= WINDOW + QB
    half = nb // 2
    rows = GQA * QB
    kv_spec = lambda col: pl.BlockSpec((t_len, DH), lambda b, g, i, col=col: (b, col + g))
    kc_spec = lambda col: pl.BlockSpec((half, DH), lambda b, g, i, col=col: (b, col + g))
    return pl.pallas_call(
        _pattn_body,
        grid=(bsz, N_KV, nq),
        in_specs=[pl.BlockSpec(memory_space=pltpu.SMEM),
                  pl.BlockSpec((1, GQA, QB, DH), lambda b, g, i: (b * nq + i, g, 0, 0)),
                  kv_spec(4), kv_spec(6), kv_spec(8), kv_spec(10),
                  kc_spec(0), kc_spec(0), kc_spec(2), kc_spec(2),
                  pl.BlockSpec((1, 1, QB, 3 * GQA), lambda b, g, i: (b, g, i, 0))],
        out_specs=pl.BlockSpec((1, GQA, QB, DH), lambda b, g, i: (b * nq + i, g, 0, 0)),
        out_shape=jax.ShapeDtypeStruct((bsz * nq, N_HEADS, QB, DH), F32),
        scratch_shapes=[pltpu.VMEM((rows, 1), F32), pltpu.VMEM((rows, 1), F32), pltpu.VMEM((rows, DH), F32)],
        compiler_params=_params(("parallel", "parallel", "arbitrary")),
        name="prompt_attention",
    )(slopes, q_blk, kvb, kvb, kvb, kvb, kce, kco, kce, kco, gates_g)


def _sample_select_body(slopes_ref, q_ref, kce_ref, kco_ref, oc_ref, sel_ref, *, past):
    rows = GQA * TP
    row = _iota((rows, 1), 0)
    qpos = past + (row & (TP - 1))
    r_of_row = row >> 3
    half = kce_ref.shape[1]
    nb = 2 * half
    n_b = q_ref.shape[0]
    lane = _iota((rows, nb), 1)
    blk_c = jnp.where(lane < half, 2 * lane, 2 * (lane - half) + 1)
    dist = qpos - (blk_c * L_CMP + (L_CMP - 1))
    pairs = []
    for b in range(n_b):
        for g in range(N_KV):
            slope = jnp.zeros((rows, 1), F32)
            for r in range(GQA):
                slope = jnp.where(r_of_row == r, slopes_ref[g * GQA + r], slope)
            q = q_ref[b, g]
            kc = jnp.concatenate([kce_ref[b, :, g * DH:(g + 1) * DH], kco_ref[b, :, g * DH:(g + 1) * DH]], axis=0)
            vc = jnp.concatenate([kce_ref[b, :, (2 + g) * DH:(3 + g) * DH], kco_ref[b, :, (2 + g) * DH:(3 + g) * DH]],
                                 axis=0)
            s = _dot_nt(q, kc) - slope * dist.astype(F32)
            p = _masked_softmax_rows(s, dist >= 0)
            oc_ref[b, g] = _dot(p.astype(BF16), vc)
            imp = p[0:TP]
            for r in range(1, GQA):
                imp = imp + p[r * TP:(r + 1) * TP]
            pairs.append(imp[:, :half] + imp[:, half:])
    pair = jnp.concatenate(pairs, axis=0)
    n_rows = pair.shape[0]
    blk = _iota((n_rows, half), 1)
    qp = past + (_iota((n_rows, half), 0) & (TP - 1))
    cur = qp >> 6
    forced = (blk == 0) | (blk == cur) | (blk == cur - 1)
    score = jnp.where(blk * L_SEL <= qp, jnp.where(forced, FORCE, pair), NEG)
    sel = jnp.zeros((n_rows, half), F32)
    for _ in range(N_SEL - 1):
        m = jnp.max(score, axis=1, keepdims=True)
        idx = jnp.min(jnp.where(score == m, blk, half), axis=1, keepdims=True)
        hit = blk == idx
        sel = jnp.where(hit, 1.0, sel)
        score = jnp.where(hit, -jnp.inf, score)
    sel = sel.astype(BF16)
    for b in range(n_b):
        for g in range(N_KV):
            i0 = (b * N_KV + g) * TP
            sel_ref[b, g] = jnp.concatenate([sel[i0:i0 + TP]] * GQA, axis=0)


SELECT_BATCH = 8


def _sample_select(slopes, q_s, kce, kco, past):
    bsz = q_s.shape[0]
    half = kce.shape[1]
    assert half == LANES, "past selection blocks must fill one lane tile"
    rows = GQA * TP
    nbt = SELECT_BATCH
    return pl.pallas_call(
        functools.partial(_sample_select_body, past=past),
        grid=(bsz // nbt,),
        in_specs=[pl.BlockSpec(memory_space=pltpu.SMEM),
                  pl.BlockSpec((nbt, N_KV, rows, DH), lambda b: (b, 0, 0, 0)),
                  pl.BlockSpec((nbt, half, 4 * DH), lambda b: (b, 0, 0)),
                  pl.BlockSpec((nbt, half, 4 * DH), lambda b: (b, 0, 0))],
        out_specs=[pl.BlockSpec((nbt, N_KV, rows, DH), lambda b: (b, 0, 0, 0)),
                   pl.BlockSpec((nbt, N_KV, rows, half), lambda b: (b, 0, 0, 0))],
        out_shape=[jax.ShapeDtypeStruct((bsz, N_KV, rows, DH), F32),
                   jax.ShapeDtypeStruct((bsz, N_KV, rows, half), BF16)],
        compiler_params=_params(("parallel",)),
        name="sample_select",
    )(slopes, q_s, kce, kco)


def _online_update(m_ref, l_ref, acc_ref, g, sc, msk, v):
    m_old = m_ref[g]
    m_new = jnp.maximum(m_old, jnp.max(sc, axis=-1, keepdims=True))
    alpha = jnp.exp(m_old - m_new)
    e = jnp.where(msk, jnp.exp(sc - m_new), 0.0)
    l_ref[g] = alpha * l_ref[g] + jnp.sum(e, axis=-1, keepdims=True)
    acc_ref[g] = alpha * acc_ref[g] + _dot(e.astype(BF16), v)
    m_ref[g] = m_new


def _sample_attn_body(pt_ref, slopes_ref, *refs, past, t_new):
    pages = refs[:PAGES_PER_STEP]
    q_ref, sel_ref, new_ref, win_ref, oc_ref, gt_ref, o_ref, m_scr, l_scr, acc_scr = refs[PAGES_PER_STEP:]
    c = pl.program_id(1)
    rows = GQA * TP
    row = _iota((rows, 1), 0)
    qpos = past + (row & (TP - 1))
    r_of_row = row >> 3
    ck = PAGES_PER_STEP * pages[0].shape[1] // KV_ROWS

    @pl.when(c == 0)
    def _():
        m_scr[...] = jnp.full(m_scr.shape, NEG, F32)
        l_scr[...] = jnp.zeros(l_scr.shape, F32)
        acc_scr[...] = jnp.zeros(acc_scr.shape, F32)

    dist_s = qpos - (c * ck + _iota((rows, ck), 1))
    expand = jnp.where((_iota((LANES, ck), 1) >> 6) + c * (ck // L_SEL) == _iota((LANES, ck), 0), 1.0, 0.0).astype(BF16)
    slopes = []
    for g in range(N_KV):
        slope = jnp.zeros((rows, 1), F32)
        for r in range(GQA):
            slope = jnp.where(r_of_row == r, slopes_ref[g * GQA + r], slope)
        slopes.append(slope)
        q = q_ref[0, g]
        kch = _page_rows(pages, g).astype(BF16)
        vch = _page_rows(pages, 2 + g).astype(BF16)
        picked = _dot(sel_ref[0, g], expand)
        msk = jnp.where(dist_s >= 0, picked, 0.0) > 0.5
        sc = jnp.where(msk, _dot_nt(q, kch) - slope * dist_s.astype(F32), NEG)
        _online_update(m_scr, l_scr, acc_scr, g, sc, msk, vch)

    @pl.when(c == pl.num_programs(1) - 1)
    def _():
        col = _iota((rows, TP), 1)
        dist_n = qpos - (past + col)
        mask_n = jnp.where(col < t_new, dist_n, -1) >= 0
        dist_c = qpos - (past - WINDOW + _iota((rows, WINDOW), 1))
        mask_c = jnp.where(dist_c >= 0, dist_c, WINDOW) < WINDOW
        mask_wn = jnp.where(mask_n, dist_n, WINDOW) < WINDOW
        for g in range(N_KV):
            slope = slopes[g]
            q = q_ref[0, g]
            kn = new_ref[0, :, (4 + g) * DH:(5 + g) * DH]
            vn = new_ref[0, :, (6 + g) * DH:(7 + g) * DH]
            sc = jnp.where(mask_n, _dot_nt(q, kn) - slope * dist_n.astype(F32), NEG)
            _online_update(m_scr, l_scr, acc_scr, g, sc, mask_n, vn)
            o_s = acc_scr[g] / jnp.maximum(l_scr[g], 1e-30)
            kwc = win_ref[0, pl.ds(g, WINDOW, KV_ROWS), :].astype(BF16)
            vwc = win_ref[0, pl.ds(2 + g, WINDOW, KV_ROWS), :].astype(BF16)
            kwn = new_ref[0, :, (8 + g) * DH:(9 + g) * DH]
            vwn = new_ref[0, :, (10 + g) * DH:(11 + g) * DH]
            s1 = jnp.where(mask_c, _dot_nt(q, kwc) - slope * dist_c.astype(F32), NEG)
            s2 = jnp.where(mask_wn, _dot_nt(q, kwn) - slope * dist_n.astype(F32), NEG)
            mx = jnp.maximum(jnp.max(s1, axis=-1, keepdims=True), jnp.max(s2, axis=-1, keepdims=True))
            e1 = jnp.where(mask_c, jnp.exp(s1 - mx), 0.0)
            e2 = jnp.where(mask_wn, jnp.exp(s2 - mx), 0.0)
            den = jnp.maximum(jnp.sum(e1, axis=-1, keepdims=True) + jnp.sum(e2, axis=-1, keepdims=True), 1e-30)
            o_w = _dot((e1 / den).astype(BF16), vwc) + _dot((e2 / den).astype(BF16), vwn)
            gt = gt_ref[0, g]
            o_ref[0, g] = gt[:, 0:1] * oc_ref[0, g] + gt[:, 1:2] * o_s + gt[:, 2:3] * o_w


def _sample_attention(slopes, page_ids, cache_sel3, q_s, sel_s, new_kvb, cache_win3, win_base, o_c, gates_s, past,
                      t_new):
    page_rows = cache_sel3.shape[1]
    bsz = q_s.shape[0]
    n_pages = page_ids.shape[0] // bsz
    n_steps = n_pages // PAGES_PER_STEP
    rows = GQA * TP
    assert cache_win3.shape[1] == WINDOW * KV_ROWS and past % L_SEL == 0 and t_new <= TP
    page_spec = lambda i: pl.BlockSpec((1, page_rows, DH),
                                       lambda b, s, pt: (pt[b * n_pages + s * PAGES_PER_STEP + i], 0, 0))
    per_b = lambda shape: pl.BlockSpec((1,) + shape, lambda b, s, pt: (b,) + (0,) * len(shape))
    grid_spec = pltpu.PrefetchScalarGridSpec(
        num_scalar_prefetch=1,
        grid=(bsz, n_steps),
        in_specs=[pl.BlockSpec(memory_space=pltpu.SMEM)] + [page_spec(i) for i in range(PAGES_PER_STEP)]
        + [per_b((N_KV, rows, DH)), per_b((N_KV, rows, LANES)), per_b((TP, 12 * DH)),
           pl.BlockSpec((1, WINDOW * KV_ROWS, DH), lambda b, s, pt: (win_base + b, 0, 0)),
           per_b((N_KV, rows, DH)), per_b((N_KV, rows, 3))],
        out_specs=per_b((N_KV, rows, DH)),
        scratch_shapes=[pltpu.VMEM((N_KV, rows, 1), F32), pltpu.VMEM((N_KV, rows, 1), F32),
                        pltpu.VMEM((N_KV, rows, DH), F32)],
    )
    return pl.pallas_call(
        functools.partial(_sample_attn_body, past=past, t_new=t_new),
        grid_spec=grid_spec,
        out_shape=jax.ShapeDtypeStruct((bsz, N_KV, rows, DH), F32),
        compiler_params=_params(("parallel", "arbitrary")),
        name="sample_attention",
    )(page_ids, slopes, *([cache_sel3] * PAGES_PER_STEP), q_s, sel_s, new_kvb, cache_win3, o_c, gates_s)


def _gelu_tanh(x):
    return 0.5 * x * (1.0 + jnp.tanh(0.7978845608028654 * (x + 0.044715 * (x * x * x))))


def _lru_coeffs(conv, wa_ref, ba_ref, wi_ref, bi_ref, sp_ref):
    cb = conv.astype(BF16)
    n_blk, blk = wa_ref.shape[0], wa_ref.shape[1]
    ra = jnp.concatenate([_dot(cb[:, n * blk:(n + 1) * blk], wa_ref[n]) for n in range(n_blk)], axis=1)
    ri = jnp.concatenate([_dot(cb[:, n * blk:(n + 1) * blk], wi_ref[n]) for n in range(n_blk)], axis=1)
    r = jax.nn.sigmoid(ra + ba_ref[...])
    gi = jax.nn.sigmoid(ri + bi_ref[...])
    log_a = -LRU_C * r * sp_ref[...]
    a = jnp.exp(log_a)
    b = jnp.sqrt(-jnp.tanh(log_a) * (a * a + 1.0)) * (gi * conv)
    return a, b


def _lru_seq_body(xr_ref, yr_ref, cs_ref, h0_ref, cw_ref, cb_ref, wa_ref, ba_ref, wi_ref, bi_ref, sp_ref, onw_ref,
                  rec_ref, hl_ref, xbuf, h_scr):
    k = pl.program_id(1)
    tt = xr_ref.shape[1]
    pad = 8

    @pl.when(k == 0)
    def _():
        xbuf[0:pad, :] = jnp.zeros((pad, xbuf.shape[1]), F32)
        xbuf[pad - (CONV_W - 1):pad, :] = cs_ref[0]
        h_scr[...] = h0_ref[0]

    x = xr_ref[0]
    xbuf[pad:pad + tt, :] = x
    conv = cb_ref[...] + cw_ref[CONV_W - 1:CONV_W, :] * x
    for j in range(CONV_W - 1):
        conv = conv + cw_ref[j:j + 1, :] * xbuf[pad - (CONV_W - 1) + j:pad - (CONV_W - 1) + j + tt, :]
    xbuf[0:pad, :] = x[tt - pad:tt]
    a, b = _lru_coeffs(conv, wa_ref, ba_ref, wi_ref, bi_ref, sp_ref)
    row = _iota((tt, 1), 0)
    s = 1
    while s < tt:
        keep = row >= s
        a_sh = jnp.where(keep, pltpu.roll(a, s, 0), 1.0)
        b_sh = jnp.where(keep, pltpu.roll(b, s, 0), 0.0)
        b = a * b_sh + b
        a = a * a_sh
        s *= 2
    hs = a * h_scr[...] + b
    h_scr[...] = hs[tt - 1:tt]
    hl_ref[0] = hs[tt - 1:tt]
    rec = hs * _gelu_tanh(yr_ref[0])
    rec_ref[0] = _rms(rec, onw_ref[...]).astype(BF16)


def _lru_seq(xr3, yr3, cs, h0, cw, cb, wa, ba, wi, bi, sp, onw, tt):
    bsz, t_len, d = xr3.shape
    seq = pl.BlockSpec((1, tt, d), lambda b, k: (b, k, 0))
    full = lambda a: pl.BlockSpec(a.shape, lambda b, k: (0,) * a.ndim)
    return pl.pallas_call(
        _lru_seq_body,
        grid=(bsz, t_len // tt),
        in_specs=[seq, seq, pl.BlockSpec((1, CONV_W - 1, d), lambda b, k: (b, 0, 0)),
                  pl.BlockSpec((1, 1, d), lambda b, k: (b, 0, 0)),
                  full(cw), full(cb), full(wa), full(ba), full(wi), full(bi), full(sp), full(onw)],
        out_specs=[seq, pl.BlockSpec((1, 1, d), lambda b, k: (b, 0, 0))],
        out_shape=[jax.ShapeDtypeStruct((bsz, t_len, d), BF16), jax.ShapeDtypeStruct((bsz, 1, d), F32)],
        scratch_shapes=[pltpu.VMEM((tt + 8, d), F32), pltpu.VMEM((1, d), F32)],
        compiler_params=_params(("parallel", "arbitrary")),
        name="lru_seq",
    )(xr3, yr3, cs, h0, cw, cb, wa, ba, wi, bi, sp, onw)


def _lru_step_body(xr_ref, yr_ref, cs_ref, h0_ref, cw_ref, cb_ref, wa_ref, ba_ref, wi_ref, bi_ref, sp_ref, onw_ref,
                   rec_ref, hl_ref):
    t_len, bsz = xr_ref.shape[0], xr_ref.shape[1]
    xs = [cs_ref[j] for j in range(CONV_W - 1)] + [xr_ref[t] for t in range(t_len)]
    convs = []
    for t in range(t_len):
        conv = cb_ref[...] + cw_ref[0:1, :] * xs[t]
        for j in range(1, CONV_W):
            conv = conv + cw_ref[j:j + 1, :] * xs[t + j]
        convs.append(conv)
    a, b = _lru_coeffs(jnp.concatenate(convs, axis=0), wa_ref, ba_ref, wi_ref, bi_ref, sp_ref)
    h = h0_ref[...]
    for t in range(t_len):
        h = a[t * bsz:(t + 1) * bsz] * h + b[t * bsz:(t + 1) * bsz]
        rec = h * _gelu_tanh(yr_ref[t])
        rec_ref[t] = _rms(rec, onw_ref[...]).astype(BF16)
    hl_ref[...] = h


def _lru_step(xr_t, yr_t, cs_t, h0, cw, cb, wa, ba, wi, bi, sp, onw):
    t_len, bsz, d = xr_t.shape
    return pl.pallas_call(
        _lru_step_body,
        out_shape=[jax.ShapeDtypeStruct((t_len, bsz, d), BF16), jax.ShapeDtypeStruct((bsz, d), F32)],
        compiler_params=pltpu.CompilerParams(vmem_limit_bytes=VMEM_LIMIT),
        name="lru_step",
    )(xr_t, yr_t, cs_t, h0, cw, cb, wa, ba, wi, bi, sp, onw)


def _outproj_body(attn_ref, rec_ref, x_ref, cnt0_ref, anw_ref, woa_ref, wor_ref, fnw_ref, wr_ref, br_ref,
                  x1_ref, h2_ref, ti_ref, tg_ref, tp_ref, cnt_ref, carry):
    step = pl.program_id(0)
    tm = x_ref.shape[0]

    @pl.when(step == 0)
    def _():
        carry[...] = cnt0_ref[...]

    attn = jnp.concatenate(
        [jnp.concatenate([attn_ref[b, hd] for hd in range(N_HEADS)], axis=1) for b in range(attn_ref.shape[0])], axis=0)
    an = _rms(attn, anw_ref[...]).astype(BF16)
    x1 = x_ref[...] + (_dot(an, woa_ref[...]) + _dot(rec_ref[...], wor_ref[...]))
    x1_ref[...] = x1
    h2 = _rms(x1, fnw_ref[...])
    for c in range(h2_ref.shape[1]):
        h2_ref[:, c, :] = h2[:, c * LANES:(c + 1) * LANES]
    lane = _iota((tm, LANES), 1)
    logits = jnp.dot(h2, wr_ref[...], precision=lax.Precision.HIGHEST, preferred_element_type=F32) + br_ref[...]
    lg = jnp.where(lane < N_EXPERTS, logits, -jnp.inf)
    vals, idxs = [], []
    for _ in range(TOP_K):
        m = jnp.max(lg, axis=-1, keepdims=True)
        ix = jnp.min(jnp.where(lg == m, lane, LANES), axis=-1, keepdims=True)
        vals.append(m)
        idxs.append(ix)
        lg = jnp.where(lane == ix, -jnp.inf, lg)
    es = [jnp.exp(v - vals[0]) for v in vals]
    den = es[0]
    for e in es[1:]:
        den = den + e
    onehot = jnp.zeros((tm, LANES), F32)
    for ix in idxs:
        onehot = jnp.where(lane == ix, 1.0, onehot)
    lower = jnp.where(_iota((tm, tm), 0) > _iota((tm, tm), 1), 1.0, 0.0).astype(BF16)
    rank = carry[...] + _dot(lower, onehot.astype(BF16))
    carry[...] = carry[...] + jnp.sum(onehot, axis=0, keepdims=True)
    ti = jnp.zeros((tm, LANES), I32)
    tg = jnp.zeros((tm, LANES), F32)
    tp = jnp.zeros((tm, LANES), I32)
    for k in range(TOP_K):
        pos = jnp.sum(jnp.where(lane == idxs[k], rank, 0.0), axis=-1, keepdims=True).astype(I32)
        ti = jnp.where(lane == k, idxs[k], ti)
        tg = jnp.where(lane == k, es[k] / den, tg)
        tp = jnp.where(lane == k, pos, tp)
    ti_ref[...] = ti
    tg_ref[...] = tg
    tp_ref[...] = tp
    cnt_ref[...] = carry[...]


def _outproj_router(attn_blk, recn, x2, cnt0, anw, woa, wor, fnw, wr, br, tm):
    n, d = x2.shape
    nqb = tm // QB
    row = lambda i: (i, 0)
    wspec = lambda a: pl.BlockSpec(a.shape, lambda i: (0,) * a.ndim, pipeline_mode=pl.Buffered(1))
    lanes_out = lambda dt: jax.ShapeDtypeStruct((n, LANES), dt)
    return pl.pallas_call(
        _outproj_body,
        grid=(n // tm,),
        in_specs=[pl.BlockSpec((nqb, N_HEADS, QB, DH), lambda i: (i, 0, 0, 0)),
                  pl.BlockSpec((tm, recn.shape[1]), row), pl.BlockSpec((tm, d), row),
                  wspec(cnt0), wspec(anw), wspec(woa), wspec(wor), wspec(fnw), wspec(wr), wspec(br)],
        out_specs=[pl.BlockSpec((tm, d), row), pl.BlockSpec((tm, d // LANES, LANES), lambda i: (i, 0, 0)),
                   pl.BlockSpec((tm, LANES), row),
                   pl.BlockSpec((tm, LANES), row), pl.BlockSpec((tm, LANES), row), pl.BlockSpec((1, LANES), lambda i: (0, 0))],
        out_shape=[jax.ShapeDtypeStruct((n, d), F32), jax.ShapeDtypeStruct((n, d // LANES, LANES), F32),
                   lanes_out(I32), lanes_out(F32), lanes_out(I32), jax.ShapeDtypeStruct((1, LANES), F32)],
        scratch_shapes=[pltpu.VMEM((1, LANES), F32)],
        compiler_params=_params(("arbitrary",)),
        name="outproj_router",
    )(attn_blk, recn, x2, cnt0, anw, woa, wor, fnw, wr, br)


ISSUE_UNROLL = 8


def _gather_body(rowtok_ref, nused_ref, h2_hbm, out_ref, buf, sem):
    j = pl.program_id(0)
    tm = buf.shape[1]
    n_used = nused_ref[0]

    def issue_block(blk, slot):
        def body(it, c):
            for u in range(ISSUE_UNROLL):
                r = it * ISSUE_UNROLL + u
                tok = rowtok_ref[blk * tm + r]
                pltpu.make_async_copy(h2_hbm.at[pl.ds(tok, 1)], buf.at[slot, pl.ds(r, 1)],
                                      sem.at[slot]).start(priority=u % 2)
            return c

        lax.fori_loop(0, tm // ISSUE_UNROLL, body, 0)

    @pl.when((j == 0) & (n_used > 0))
    def _():
        issue_block(0, 0)

    @pl.when(j + 1 < n_used)
    def _():
        issue_block(j + 1, (j + 1) % 2)

    @pl.when(j < n_used)
    def _():
        slot = j % 2
        pltpu.make_async_copy(h2_hbm.at[pl.ds(0, tm)], buf.at[slot], sem.at[slot]).wait()
        for c in range(buf.shape[2]):
            out_ref[:, c * LANES:(c + 1) * LANES] = buf[slot, :, c, :].astype(BF16)

    @pl.when(j >= n_used)
    def _():
        out_ref[...] = jnp.zeros(out_ref.shape, BF16)


def _gather_rows(row_tok, n_used, h2, n_blocks):
    n, dc, _ = h2.shape
    d = dc * LANES
    tm = MOE_TM
    grid_spec = pltpu.PrefetchScalarGridSpec(
        num_scalar_prefetch=2,
        grid=(n_blocks,),
        in_specs=[pl.BlockSpec(memory_space=pl.ANY)],
        out_specs=pl.BlockSpec((tm, d), lambda j, rt, nu: (j, 0)),
        scratch_shapes=[pltpu.VMEM((2, tm, dc, LANES), F32), pltpu.SemaphoreType.DMA((2,))],
    )
    return pl.pallas_call(
        _gather_body,
        grid_spec=grid_spec,
        out_shape=jax.ShapeDtypeStruct((n_blocks * tm, d), BF16),
        compiler_params=_params(("arbitrary",)),
        name="moe_gather",
    )(row_tok, n_used, h2)


COPY_SPLIT = 4


def _expert_blocks(first_row, n_blk, n_used, n_blocks, x_hbm, out_rows, xbuf, obuf, xsem, osem, prepare, compute):
    tm = xbuf.shape[1]
    rc = tm // COPY_SPLIT

    def x_copies(b, slot):
        r = pl.multiple_of(first_row + b * tm, tm)
        return [pltpu.make_async_copy(x_hbm.at[pl.ds(r + s * rc, rc), :], xbuf.at[slot, pl.ds(s * rc, rc), :],
                                      xsem.at[slot]) for s in range(COPY_SPLIT)]

    def o_copies(r, slot):
        r = pl.multiple_of(r, tm)
        return [pltpu.make_async_copy(obuf.at[slot, pl.ds(s * rc, rc)], out_rows(r + s * rc, rc), osem.at[slot])
                for s in range(COPY_SPLIT)]

    def start(copies):
        for cp in copies:
            cp.start()

    def wait(copies):
        for cp in copies:
            cp.wait()

    @pl.when(n_blk > 0)
    def _():
        start(x_copies(0, 0))
        prepare()

        def body(b, c):
            slot = b % 2

            @pl.when(b + 1 < n_blk)
            def _():
                start(x_copies(b + 1, 1 - slot))

            wait(x_copies(b, slot))

            @pl.when(b >= 2)
            def _():
                wait(o_copies(first_row + (b - 2) * tm, slot))

            compute(xbuf[slot], slot)
            start(o_copies(first_row + b * tm, slot))
            return c

        lax.fori_loop(0, n_blk, body, 0)

        @pl.when(n_blk >= 2)
        def _():
            wait(o_copies(first_row + (n_blk - 2) * tm, n_blk % 2))

        wait(o_copies(first_row + (n_blk - 1) * tm, (n_blk - 1) % 2))

    @pl.when(pl.program_id(0) == pl.num_programs(0) - 1)
    def _():
        obuf[0] = jnp.zeros(obuf.shape[1:], obuf.dtype)

        def fill(t, c):
            start(o_copies(t * tm, 0))
            wait(o_copies(t * tm, 0))
            return c

        lax.fori_loop(n_used, n_blocks, fill, 0)


W_SPLIT = 4


def _moe_up_body(row0_ref, nblk_ref, nu_ref, xs_hbm, *refs, n_blocks):
    w_refs = refs[:W_SPLIT]
    b_ref, perm_ref, h_hbm, wbf, xbuf, obuf, xsem, osem = refs[W_SPLIT:]
    e = pl.program_id(0)
    tn = wbf.shape[1]
    pw = perm_ref.shape[0]
    hw = pw // 2
    kc = w_refs[0].shape[1]

    def prepare():
        for q, w_ref in enumerate(w_refs):
            for c in range(tn // pw):
                w = w_ref[0, :, c * pw:(c + 1) * pw].astype(BF16)
                wbf[q * kc:(q + 1) * kc, c * pw:(c + 1) * pw] = _dot(w, perm_ref[...]).astype(BF16)

    def compute(x, slot):
        gu = _dot(x, wbf[...]) + b_ref[0]
        for c in range(tn // pw):
            glu = jnp.minimum(gu[:, c * pw:c * pw + hw], SWIGLU_LIMIT)
            lin = jnp.clip(gu[:, c * pw + hw:(c + 1) * pw], -SWIGLU_LIMIT, SWIGLU_LIMIT)
            obuf[slot, :, c * hw:(c + 1) * hw] = (glu * jax.nn.sigmoid(SWIGLU_ALPHA * glu) * (lin + 1.0)).astype(BF16)

    col0 = pl.multiple_of(pl.program_id(1) * (tn // 2), tn // 2)
    out_rows = lambda r, n: h_hbm.at[pl.ds(r, n), pl.ds(col0, tn // 2)]
    _expert_blocks(row0_ref[e], nblk_ref[e], nu_ref[0], n_blocks, xs_hbm, out_rows, xbuf, obuf, xsem, osem,
                   prepare, compute)


def _w_specs(k, tn):
    return [pl.BlockSpec((1, k // W_SPLIT, tn), lambda e, n, *_, q=q: (e, q, n)) for q in range(W_SPLIT)]


def _moe_up(row0, nblk, n_used, xs, w_gu, b_gu, perm):
    n_rows, d = xs.shape
    n_e, _, f2 = w_gu.shape
    tm, tn = MOE_TM, MOE_TN
    grid_spec = pltpu.PrefetchScalarGridSpec(
        num_scalar_prefetch=3,
        grid=(n_e, f2 // tn),
        in_specs=[pl.BlockSpec(memory_space=pl.ANY)] + _w_specs(d, tn)
        + [pl.BlockSpec((1, 1, tn), lambda e, n, *_: (e, 0, n)),
           pl.BlockSpec(perm.shape, lambda e, n, *_: (0, 0))],
        out_specs=pl.BlockSpec(memory_space=pl.ANY),
        scratch_shapes=[pltpu.VMEM((d, tn), BF16), pltpu.VMEM((2, tm, d), BF16), pltpu.VMEM((2, tm, tn // 2), BF16),
                        pltpu.SemaphoreType.DMA((2,)), pltpu.SemaphoreType.DMA((2,))],
    )
    return pl.pallas_call(
        functools.partial(_moe_up_body, n_blocks=n_rows // tm),
        grid_spec=grid_spec,
        out_shape=jax.ShapeDtypeStruct((n_rows, f2 // 2), BF16),
        compiler_params=_params(("arbitrary", "arbitrary")),
        name="moe_up",
    )(row0, nblk, n_used, xs, *([w_gu] * W_SPLIT), b_gu, perm)


def _moe_down_body(row0_ref, nblk_ref, nu_ref, h_hbm, *refs, n_blocks):
    w_refs = refs[:W_SPLIT]
    b_ref, y_hbm, wbf, xbuf, obuf, xsem, osem = refs[W_SPLIT:]
    e = pl.program_id(0)
    tn = wbf.shape[1]
    kc = w_refs[0].shape[1]

    def prepare():
        for q, w_ref in enumerate(w_refs):
            wbf[q * kc:(q + 1) * kc, :] = w_ref[0].astype(BF16)

    def compute(h, slot):
        y = _dot(h, wbf[...]) + b_ref[0]
        for c in range(tn // LANES):
            obuf[slot, :, c, :] = y[:, c * LANES:(c + 1) * LANES]

    out_rows = lambda r, n: y_hbm.at[pl.ds(r, n)]
    _expert_blocks(row0_ref[e], nblk_ref[e], nu_ref[0], n_blocks, h_hbm, out_rows, xbuf, obuf, xsem, osem,
                   prepare, compute)


def _moe_down(row0, nblk, n_used, h, w_d, b_d):
    n_rows, f = h.shape
    n_e, _, d = w_d.shape
    tm, tn = MOE_TM, MOE_TN
    assert tn == d, "a grid step writes whole output rows"
    grid_spec = pltpu.PrefetchScalarGridSpec(
        num_scalar_prefetch=3,
        grid=(n_e, d // tn),
        in_specs=[pl.BlockSpec(memory_space=pl.ANY)] + _w_specs(f, tn)
        + [pl.BlockSpec((1, 1, tn), lambda e, n, *_: (e, 0, n))],
        out_specs=pl.BlockSpec(memory_space=pl.ANY),
        scratch_shapes=[pltpu.VMEM((f, tn), BF16), pltpu.VMEM((2, tm, f), BF16),
                        pltpu.VMEM((2, tm, d // LANES, LANES), F32),
                        pltpu.SemaphoreType.DMA((2,)), pltpu.SemaphoreType.DMA((2,))],
    )
    return pl.pallas_call(
        functools.partial(_moe_down_body, n_blocks=n_rows // tm),
        grid_spec=grid_spec,
        out_shape=jax.ShapeDtypeStruct((n_rows, d // LANES, LANES), F32),
        compiler_params=_params(("arbitrary", "arbitrary")),
        name="moe_down",
    )(row0, nblk, n_used, h, *([w_d] * W_SPLIT), b_d)


def _combine_body(dest_hbm, gate_ref, x1_ref, y_hbm, outp_ref, outs_ref, dsm, buf, sem_d, sem, *, np_blocks):
    j = pl.program_id(0)
    n = pl.num_programs(0)
    tk = x1_ref.shape[0]
    toks = ISSUE_UNROLL // TOP_K

    def idx_copy(step, slot):
        return pltpu.make_async_copy(dest_hbm.at[step, 0], dsm.at[slot], sem_d.at[slot])

    def issue_rows(slot):
        def body(it, c):
            for u in range(toks):
                t = it * toks + u
                for k in range(TOP_K):
                    src = dsm[slot, t * TOP_K + k]
                    pltpu.make_async_copy(y_hbm.at[pl.ds(src, 1)], buf.at[slot, k, pl.ds(t, 1)],
                                          sem.at[slot]).start(priority=k % 2)
            return c

        lax.fori_loop(0, tk // toks, body, 0)

    @pl.when(j == 0)
    def _():
        idx_copy(0, 0).start()
        idx_copy(0, 0).wait()
        issue_rows(0)

        @pl.when(1 < n)
        def _():
            idx_copy(1, 1).start()

    @pl.when(j + 1 < n)
    def _():
        slot = (j + 1) % 2
        idx_copy(j + 1, slot).wait()
        issue_rows(slot)

    @pl.when(j + 2 < n)
    def _():
        idx_copy(j + 2, j % 2).start()

    slot = j % 2
    for k in range(TOP_K):
        pltpu.make_async_copy(y_hbm.at[pl.ds(0, tk)], buf.at[slot, k], sem.at[slot]).wait()
    g = gate_ref[...]
    gk = [jnp.broadcast_to(g[:, k:k + 1], (tk, LANES)) for k in range(TOP_K)]

    def result(c):
        acc = gk[0] * buf[slot, 0, :, c, :]
        for k in range(1, TOP_K):
            acc = acc + gk[k] * buf[slot, k, :, c, :]
        return x1_ref[:, c * LANES:(c + 1) * LANES] + acc

    @pl.when(j < np_blocks)
    def _():
        for c in range(buf.shape[3]):
            outp_ref[:, c * LANES:(c + 1) * LANES] = result(c)

    @pl.when(j >= np_blocks)
    def _():
        for c in range(buf.shape[3]):
            outs_ref[:, c * LANES:(c + 1) * LANES] = result(c)


def _combine(dest3, gates, x1, y, n_prompt):
    n, d = x1.shape
    tk = dest3.shape[2] // TOP_K
    np_blocks = n_prompt // tk
    row = lambda j: (j, 0)
    return pl.pallas_call(
        functools.partial(_combine_body, np_blocks=np_blocks),
        grid=(n // tk,),
        in_specs=[pl.BlockSpec(memory_space=pl.ANY), pl.BlockSpec((tk, LANES), row), pl.BlockSpec((tk, d), row),
                  pl.BlockSpec(memory_space=pl.ANY)],
        out_specs=[pl.BlockSpec((tk, d), lambda j: (jnp.minimum(j, np_blocks - 1), 0)),
                   pl.BlockSpec((tk, d), lambda j: (jnp.maximum(j - np_blocks, 0), 0))],
        out_shape=[jax.ShapeDtypeStruct((n_prompt, d), F32), jax.ShapeDtypeStruct((n - n_prompt, d), F32)],
        scratch_shapes=[pltpu.SMEM((2, tk * TOP_K), I32), pltpu.VMEM((2, TOP_K, tk, d // LANES, LANES), F32),
                        pltpu.SemaphoreType.DMA((2,)), pltpu.SemaphoreType.DMA((2,))],
        compiler_params=_params(("arbitrary",)),
        name="moe_combine",
    )(dest3, gates, x1, y)


def _moe(h2, x1, topi, topg, topp, counts, w_gu, b_gu, w_d, b_d, n_prompt):
    n, d = x1.shape
    tm = MOE_TM
    n_asg = n * TOP_K
    n_blocks = n_asg // tm + N_EXPERTS
    padded = (counts + tm - 1) // tm * tm
    pad_end = jnp.cumsum(padded)
    pad_start = pad_end - padded
    dest = pad_start[topi[:, :TOP_K]] + topp[:, :TOP_K]
    tok = jnp.broadcast_to(jnp.arange(n, dtype=I32)[:, None], (n, TOP_K))
    row_tok = jnp.zeros((n_blocks * tm,), I32).at[dest.reshape(-1)].set(tok.reshape(-1))
    n_used = (pad_end[-1:] // tm).astype(I32)
    row0 = pad_start.astype(I32)
    nblk = (padded // tm).astype(I32)
    f2 = w_gu.shape[2]
    pw = 2 * LANES
    src = jnp.arange(pw)
    perm = (jnp.arange(pw)[:, None] == jnp.where(src < LANES, 2 * src, 2 * (src - LANES) + 1)[None, :]).astype(BF16)
    b_gu_p = b_gu.reshape(N_EXPERTS, f2 // pw, LANES, 2).transpose(0, 1, 3, 2).reshape(N_EXPERTS, 1, f2)
    xs = _gather_rows(row_tok, n_used, h2, n_blocks)
    h = _moe_up(row0, nblk, n_used, xs, w_gu, b_gu_p, perm)
    y = _moe_down(row0, nblk, n_used, h, w_d, b_d.reshape(N_EXPERTS, 1, d))
    tk = 128
    dest3 = dest.astype(I32).reshape(n // tk, 1, tk * TOP_K)
    return _combine(dest3, topg, x1, y, n_prompt)


def _layer(layer, xp, xs, cache_cmp_all, cache_sel_all, cache_win_all, state_conv, state_h, page_table, w):
    (norm_mix_w, w_in, q_norm_w, k_norm_w, w_cmp_k, w_cmp_v, conv_w, conv_b, w_gate_a, b_gate_a, w_gate_i, b_gate_i,
     lru_lambda, out_norm_attn, out_norm_rec, w_out, norm_ffn_w, w_router, b_router, w_gate_up, b_gate_up, w_down,
     b_down) = w
    bp, tp, d = xp.shape
    bs, ts, _ = xs.shape
    d_rec = d - D_ATTN
    past = page_table.shape[1] * cache_cmp_all.shape[2]
    assert bs * ts == QB and ts <= TP

    o1, o2 = D_ATTN, D_ATTN + 12 * DH
    o3 = o2 + 3 * N_HEADS
    wq = w_in[:, :o1].astype(BF16)
    wkv = w_in[:, o1:o2].astype(BF16)
    wg = jnp.pad(w_in[:, o2:o3], ((0, 0), (0, LANES - 3 * N_HEADS))).astype(BF16)
    wxy = w_in[:, o3:].astype(BF16)
    row2 = lambda v: v.reshape(1, -1)
    slopes = jnp.exp2(-8.0 * jnp.arange(1, N_HEADS + 1, dtype=F32) / N_HEADS)
    w4 = jnp.concatenate([w_cmp_k, w_cmp_v], axis=0)
    z4 = jnp.zeros_like(w4)
    wl = jnp.stack([jnp.concatenate([w4, z4], axis=1), jnp.concatenate([z4, w4], axis=1)])
    wl = jnp.broadcast_to(wl[..., None], wl.shape + (DH,))
    sp = row2(jax.nn.softplus(-lru_lambda.astype(F32)))
    wa, wi = w_gate_a.astype(BF16), w_gate_i.astype(BF16)
    woa, wor = w_out[:D_ATTN].astype(BF16), w_out[D_ATTN:].astype(BF16)
    wr = jnp.pad(w_router, ((0, 0), (0, LANES - N_EXPERTS)))
    br = row2(jnp.pad(b_router, (0, LANES - N_EXPERTS)))
    mix = (row2(norm_mix_w), wq, wkv, wg, wxy, row2(q_norm_w), k_norm_w)
    lru_w = (conv_w, row2(conv_b), wa, row2(b_gate_a), wi, row2(b_gate_i), sp, row2(out_norm_rec))

    np_tok = bp * tp
    q_p, cmp_p, sel_p, win_p, kvb_p, gate_p, xr_p, yr_p = _inproj(xp.reshape(np_tok, d), *mix, tm=256)
    kce_p, kco_p = _pool_prompt(cmp_p, wl)
    gates_g = gate_p[:, :3 * N_HEADS].reshape(bp, tp, N_KV, 3 * GQA).transpose(0, 2, 1, 3)
    attn_p = _prompt_attention(slopes, q_p, kvb_p, kce_p, kco_p, gates_g, bp, tp)
    recn_p, hl_p = _lru_seq(xr_p.reshape(bp, tp, d_rec), yr_p.reshape(bp, tp, d_rec),
                            jnp.zeros((bp, CONV_W - 1, d_rec), F32), jnp.zeros((bp, 1, d_rec), F32), *lru_w, tt=256)
    post = (row2(out_norm_attn), woa, wor, row2(norm_ffn_w), wr, br)
    x1_p, h2_p, ti_p, tg_p, tp_p, cnt_p = _outproj_router(
        attn_p, recn_p.reshape(np_tok, d_rec), xp.reshape(np_tok, d), jnp.zeros((1, LANES), F32), *post, tm=256)

    ns_tok = bs * ts
    q_s, cmp_s, sel_s, win_s, kvb_s, gate_s, xr_s, yr_s = _inproj(xs.reshape(ns_tok, d), *mix, tm=QB)
    n_pool, page = cache_cmp_all.shape[1], cache_cmp_all.shape[2]
    page_view = lambda c: c.reshape(c.shape[0] * n_pool, page * KV_ROWS, DH)
    page_ids = (page_table + layer * n_pool).reshape(-1)
    kce_s, kco_s = _pool_pages(page_view(cache_cmp_all), page_ids, bs, wl)
    q_t = q_s[0].reshape(N_KV, GQA, bs, ts, DH).transpose(2, 0, 1, 3, 4)
    q_t = jnp.pad(q_t, ((0, 0), (0, 0), (0, 0), (0, TP - ts), (0, 0))).reshape(bs, N_KV, GQA * TP, DH)
    oc_s, selm_s = _sample_select(slopes, q_t, kce_s, kco_s, past)
    new_kvb = jnp.pad(kvb_s.reshape(bs, ts, 12 * DH), ((0, 0), (0, TP - ts), (0, 0)))
    g_t = gate_s[:, :3 * N_HEADS].reshape(bs, ts, N_KV, GQA, 3).transpose(0, 2, 3, 1, 4)
    g_t = jnp.pad(g_t, ((0, 0), (0, 0), (0, 0), (0, TP - ts), (0, 0))).reshape(bs, N_KV, GQA * TP, 3)
    win_view = cache_win_all.reshape(cache_win_all.shape[0] * bs, cache_win_all.shape[2] * KV_ROWS, DH)
    attn_s = _sample_attention(slopes, page_ids, page_view(cache_sel_all), q_t, selm_s, new_kvb, win_view, layer * bs,
                               oc_s, g_t, past, ts)
    attn_s = attn_s.reshape(bs, N_KV, GQA, TP, DH)[:, :, :, :ts].transpose(1, 2, 0, 3, 4).reshape(1, N_HEADS, QB, DH)
    tmaj = lambda a: a.reshape(bs, ts, d_rec).transpose(1, 0, 2)
    recn_s, hl_s = _lru_step(tmaj(xr_s), tmaj(yr_s), state_conv.transpose(1, 0, 2), state_h, *lru_w)
    x1_s, h2_s, ti_s, tg_s, tp_s, cnt = _outproj_router(
        attn_s, recn_s.transpose(1, 0, 2).reshape(ns_tok, d_rec), xs.reshape(ns_tok, d), cnt_p, *post, tm=QB)

    cat = lambda a, b: jnp.concatenate([a, b], axis=0)
    out_p, out_s = _moe(cat(h2_p, h2_s), cat(x1_p, x1_s), cat(ti_p, ti_s), cat(tg_p, tg_s), cat(tp_p, tp_s),
                        cnt[0, :N_EXPERTS].astype(I32), w_gate_up, b_gate_up, w_down, b_down, np_tok)
    y_p = out_p.reshape(bp, tp, d)
    y_s = out_s.reshape(bs, ts, d)

    kv5 = lambda a, b, t: a.reshape(b, t, 2, N_KV, DH)
    win_len_p = min(WINDOW, tp)
    st_p = (kv5(cmp_p, bp, tp), kv5(sel_p, bp, tp), kv5(win_p, bp, tp)[:, tp - win_len_p:],
            xr_p.reshape(bp, tp, d_rec)[:, tp - (CONV_W - 1):], hl_p.reshape(bp, d_rec))
    cache_win = cache_win_all[layer]
    win_all = jnp.concatenate([cache_win, kv5(win_s, bs, ts)], axis=1)
    xcat = jnp.concatenate([state_conv, xr_s.reshape(bs, ts, d_rec)], axis=1)
    st_s = (kv5(cmp_s, bs, ts), kv5(sel_s, bs, ts), win_all[:, win_all.shape[1] - cache_win.shape[1]:],
            xcat[:, ts:], hl_s)
    return y_p, y_s, st_p, st_s


def kernel(x_prompt, x_sample, cache_cmp_kv, cache_sel_kv, cache_win_kv, state_conv, state_h, page_table, norm_mix_w, w_in, q_norm_w, k_norm_w, w_cmp_k, w_cmp_v, conv_w, conv_b, w_gate_a, b_gate_a, w_gate_i, b_gate_i, lru_lambda, out_norm_attn, out_norm_rec, w_out, norm_ffn_w, w_router, b_router, w_gate_up, b_gate_up, w_down, b_down):
    depth = w_in.shape[0]
    xp, xs = x_prompt, x_sample
    st_ps, st_ss = [], []
    for l in range(depth):
        w = (norm_mix_w[l], w_in[l], q_norm_w[l], k_norm_w[l], w_cmp_k[l], w_cmp_v[l], conv_w[l], conv_b[l],
             w_gate_a[l], b_gate_a[l], w_gate_i[l], b_gate_i[l], lru_lambda[l], out_norm_attn[l], out_norm_rec[l],
             w_out[l], norm_ffn_w[l], w_router[l], b_router[l], w_gate_up[l], b_gate_up[l], w_down[l], b_down[l])
        xp, xs, st_p, st_s = _layer(l, xp, xs, cache_cmp_kv, cache_sel_kv, cache_win_kv, state_conv[l], state_h[l],
                                    page_table, w)
        st_ps.append(st_p)
        st_ss.append(st_s)
    stack = lambda sts, i: jnp.stack([s[i] for s in sts])
    return (xp, xs) + tuple(stack(st_ps, i) for i in range(5)) + tuple(stack(st_ss, i) for i in range(5))
```

```python
import functools

import jax
import jax.numpy as jnp
from jax import lax
from jax.experimental import pallas as pl
from jax.experimental.pallas import tpu as pltpu

F32 = jnp.float32
BF16 = jnp.bfloat16
I32 = jnp.int32

N_HEADS = 8
N_KV = 2
GQA = N_HEADS // N_KV
DH = 128
D_ATTN = N_HEADS * DH
CONV_W = 4
LRU_C = 8.0
L_CMP = 32
L_SEL = 64
N_SEL = 16
WINDOW = 512
QB = 128
N_EXPERTS = 32
TOP_K = 4
SWIGLU_LIMIT = 7.0
SWIGLU_ALPHA = 1.702
EPS = 1e-6
NEG = -1e30
FORCE = 1e9
Q_SCALE = DH ** -0.5
LANES = 128
TP = 8
KV_ROWS = 2 * N_KV
MOE_TM = 256
MOE_TN = 2048
PAGES_PER_STEP = 16
VMEM_LIMIT = 56 * 1024 * 1024


def _dot(a, b):
    return jnp.dot(a, b, preferred_element_type=F32)


def _dot_nt(a, b):
    return lax.dot_general(a, b, (((1,), (1,)), ((), ())), preferred_element_type=F32)


def _iota(shape, dim):
    return lax.broadcasted_iota(I32, shape, dim)


def _rms(x, w):
    return x * lax.rsqrt(jnp.mean(x * x, axis=-1, keepdims=True) + EPS) * w


def _masked_softmax_rows(s, mask):
    s = jnp.where(mask, s, NEG)
    e = jnp.where(mask, jnp.exp(s - jnp.max(s, axis=-1, keepdims=True)), 0.0)
    return e / jnp.maximum(jnp.sum(e, axis=-1, keepdims=True), 1e-30)


def _params(sem, vmem=None):
    return pltpu.CompilerParams(dimension_semantics=sem, vmem_limit_bytes=vmem or VMEM_LIMIT)


def _inproj_body(x_ref, nw_ref, wq_ref, wkv_ref, wg_ref, wxy_ref, qnw_ref, knw_ref,
                 q_ref, cmp_ref, sel_ref, win_ref, kvb_ref, gate_ref, xr_ref, yr_ref):
    x = x_ref[...]
    h = _rms(x, nw_ref[...]).astype(BF16)
    q = _dot(h, wq_ref[...])
    qnw = qnw_ref[...]
    nqb = q_ref.shape[0]
    for hd in range(N_HEADS):
        qn = (_rms(q[:, hd * DH:(hd + 1) * DH], qnw) * Q_SCALE).astype(BF16)
        for b in range(nqb):
            q_ref[b, hd] = qn[b * QB:(b + 1) * QB]
    kv = _dot(h, wkv_ref[...])
    tm = x.shape[0]
    outs = (cmp_ref, sel_ref, win_ref)
    for br in range(3):
        knw = knw_ref[br:br + 1, :]
        for g in range(N_KV):
            c0 = br * 4 * DH + g * DH
            kn = _rms(kv[:, c0:c0 + DH], knw)
            v = kv[:, c0 + 2 * DH:c0 + 3 * DH]
            outs[br][pl.ds(g, tm, KV_ROWS), :] = kn
            outs[br][pl.ds(2 + g, tm, KV_ROWS), :] = v
            kvb_ref[:, c0:c0 + DH] = kn.astype(BF16)
            kvb_ref[:, c0 + 2 * DH:c0 + 3 * DH] = v.astype(BF16)
    gate_ref[...] = jax.nn.sigmoid(_dot(h, wg_ref[...]))
    xy = _dot(h, wxy_ref[...])
    d_rec = xr_ref.shape[1]
    xr_ref[...] = xy[:, :d_rec]
    yr_ref[...] = xy[:, d_rec:]


def _inproj(x2, nw, wq, wkv, wg, wxy, qnw, knw, tm):
    n, d = x2.shape
    d_rec = wxy.shape[1] // 2
    nqb = tm // QB
    row = lambda i: (i, 0)
    const = lambda i: (0, 0)
    wspec = lambda a: pl.BlockSpec(a.shape, const, pipeline_mode=pl.Buffered(1))
    return pl.pallas_call(
        _inproj_body,
        grid=(n // tm,),
        in_specs=[pl.BlockSpec((tm, d), row), wspec(nw), wspec(wq), wspec(wkv), wspec(wg), wspec(wxy),
                  wspec(qnw), wspec(knw)],
        out_specs=[pl.BlockSpec((nqb, N_HEADS, QB, DH), lambda i: (i, 0, 0, 0)),
                   pl.BlockSpec((tm * KV_ROWS, DH), row), pl.BlockSpec((tm * KV_ROWS, DH), row),
                   pl.BlockSpec((tm * KV_ROWS, DH), row),
                   pl.BlockSpec((tm, 12 * DH), row), pl.BlockSpec((tm, LANES), row),
                   pl.BlockSpec((tm, d_rec), row), pl.BlockSpec((tm, d_rec), row)],
        out_shape=[jax.ShapeDtypeStruct((n // QB, N_HEADS, QB, DH), BF16),
                   jax.ShapeDtypeStruct((n * KV_ROWS, DH), F32), jax.ShapeDtypeStruct((n * KV_ROWS, DH), F32),
                   jax.ShapeDtypeStruct((n * KV_ROWS, DH), F32), jax.ShapeDtypeStruct((n, 12 * DH), BF16),
                   jax.ShapeDtypeStruct((n, LANES), F32),
                   jax.ShapeDtypeStruct((n, d_rec), F32), jax.ShapeDtypeStruct((n, d_rec), F32)],
        compiler_params=_params(("parallel",)),
        name="inproj",
    )(x2, nw, wq, wkv, wg, wxy, qnw, knw)


def _pool_combo(rows, wl_ref, combo):
    r = rows.shape[0] // (2 * L_CMP)
    x3 = rows.reshape(r, 2 * L_CMP, rows.shape[1])
    even = jnp.sum(x3 * wl_ref[0, combo][None], axis=1)
    odd = jnp.sum(x3 * wl_ref[1, combo][None], axis=1)
    return even.astype(BF16), odd.astype(BF16)


def _pool_body(x_ref, wl_ref, e_ref, o_ref):
    n_tok = x_ref.shape[0] // KV_ROWS
    for combo in range(KV_ROWS):
        even, odd = _pool_combo(x_ref[pl.ds(combo, n_tok, KV_ROWS), :], wl_ref, combo)
        e_ref[:, combo * DH:(combo + 1) * DH] = even
        o_ref[:, combo * DH:(combo + 1) * DH] = odd


def _pool_prompt(cmp4, wl):
    n = cmp4.shape[0] // KV_ROWS
    toks = 1024
    ob = toks // (2 * L_CMP)
    c = KV_ROWS * DH
    return pl.pallas_call(
        _pool_body,
        grid=(n // toks,),
        in_specs=[pl.BlockSpec((toks * KV_ROWS, DH), lambda i: (i, 0)), pl.BlockSpec(wl.shape, lambda i: (0, 0, 0, 0))],
        out_specs=[pl.BlockSpec((ob, c), lambda i: (i, 0)), pl.BlockSpec((ob, c), lambda i: (i, 0))],
        out_shape=[jax.ShapeDtypeStruct((n // (2 * L_CMP), c), BF16)] * 2,
        compiler_params=_params(("parallel",)),
        name="pool_prompt",
    )(cmp4, wl)


def _page_rows(pages, combo):
    n_tok = pages[0].shape[1] // KV_ROWS
    return jnp.concatenate([p[0, pl.ds(combo, n_tok, KV_ROWS), :] for p in pages], axis=0)


def _pool_pages_body(pt_ref, *refs):
    pages = refs[:PAGES_PER_STEP]
    wl_ref, e_ref, o_ref = refs[PAGES_PER_STEP:]
    for combo in range(KV_ROWS):
        even, odd = _pool_combo(_page_rows(pages, combo), wl_ref, combo)
        e_ref[0, :, combo * DH:(combo + 1) * DH] = even
        o_ref[0, :, combo * DH:(combo + 1) * DH] = odd


def _pool_pages(cache3, page_ids, bsz, wl):
    rows = cache3.shape[1]
    n_pages = page_ids.shape[0] // bsz
    n_steps = n_pages // PAGES_PER_STEP
    ob = PAGES_PER_STEP * (rows // KV_ROWS) // (2 * L_CMP)
    c = KV_ROWS * DH
    page_spec = lambda i: pl.BlockSpec((1, rows, DH), lambda b, s, pt: (pt[b * n_pages + s * PAGES_PER_STEP + i], 0, 0))
    grid_spec = pltpu.PrefetchScalarGridSpec(
        num_scalar_prefetch=1,
        grid=(bsz, n_steps),
        in_specs=[page_spec(i) for i in range(PAGES_PER_STEP)] + [pl.BlockSpec(wl.shape, lambda b, s, pt: (0, 0, 0, 0))],
        out_specs=[pl.BlockSpec((1, ob, c), lambda b, s, pt: (b, s, 0))] * 2,
    )
    return pl.pallas_call(
        _pool_pages_body,
        grid_spec=grid_spec,
        out_shape=[jax.ShapeDtypeStruct((bsz, n_steps * ob, c), BF16)] * 2,
        compiler_params=_params(("parallel", "parallel")),
        name="pool_pages",
    )(page_ids, *([cache3] * PAGES_PER_STEP), wl)


def _select_blocks_cols(score, blk, n_pick, n_blk):
    sel = jnp.zeros(score.shape, F32)
    for _ in range(n_pick):
        m = jnp.max(score, axis=0, keepdims=True)
        idx = jnp.min(jnp.where(score == m, blk, n_blk), axis=0, keepdims=True)
        hit = blk == idx
        sel = jnp.where(hit, 1.0, sel)
        score = jnp.where(hit, -jnp.inf, score)
    return sel


def _pattn_body(slopes_ref, q_ref, ksel_ref, vsel_ref, kwin_ref, vwin_ref, kce_ref, kco_ref, vce_ref, vco_ref,
                gt_ref, o_ref, m_scr, l_scr, acc_scr):
    g = pl.program_id(1)
    i = pl.program_id(2)
    rows = GQA * QB
    q = q_ref[0].reshape(rows, DH)
    row = _iota((rows, 1), 0)
    qpos = i * QB + (row & (QB - 1))
    r_of_row = row >> 7
    slope = jnp.zeros((rows, 1), F32)
    for r in range(GQA):
        slope = jnp.where(r_of_row == r, slopes_ref[g * GQA + r], slope)

    kc = jnp.concatenate([kce_ref[...], kco_ref[...]], axis=0)
    vc = jnp.concatenate([vce_ref[...], vco_ref[...]], axis=0)
    nb = kc.shape[0]
    half = nb // 2
    lane = _iota((rows, nb), 1)
    blk_c = jnp.where(lane < half, 2 * lane, 2 * (lane - half) + 1)
    dist = qpos - (blk_c * L_CMP + (L_CMP - 1))
    mask = dist >= 0
    s = _dot_nt(q, kc) - slope * dist.astype(F32)
    p = _masked_softmax_rows(s, mask)
    o_c = _dot(p.astype(BF16), vc)
    imp = p[0:QB]
    for r in range(1, GQA):
        imp = imp + p[r * QB:(r + 1) * QB]
    imp_t = imp.T
    pair = imp_t[:half] + imp_t[half:]
    ns = half
    blk = _iota((ns, QB), 0)
    qp = i * QB + _iota((ns, QB), 1)
    cur = qp >> 6
    forced = (blk == 0) | (blk == cur) | (blk == cur - 1)
    score = jnp.where(blk * L_SEL <= qp, jnp.where(forced, FORCE, pair), NEG)
    sel = _select_blocks_cols(score, blk, min(N_SEL, ns), ns)
    sel = jnp.concatenate([sel, jnp.zeros((LANES - ns, QB), F32)], axis=0) if ns < LANES else sel
    sel_q = sel.T.astype(BF16)
    sel4 = jnp.concatenate([sel_q] * GQA, axis=0)

    m_scr[...] = jnp.full(m_scr.shape, NEG, F32)
    l_scr[...] = jnp.zeros(l_scr.shape, F32)
    acc_scr[...] = jnp.zeros(acc_scr.shape, F32)
    ck = 512

    def chunk(c, carry):
        k0 = pl.multiple_of(c * ck, ck)
        kch = ksel_ref[pl.ds(k0, ck), :]
        vch = vsel_ref[pl.ds(k0, ck), :]
        dist_s = qpos - (k0 + _iota((rows, ck), 1))
        expand = jnp.where((_iota((LANES, ck), 1) >> 6) + c * (ck // L_SEL) == _iota((LANES, ck), 0), 1.0, 0.0)
        picked = _dot(sel4, expand.astype(BF16))
        msk = jnp.where(dist_s >= 0, picked, 0.0) > 0.5
        sc = jnp.where(msk, _dot_nt(q, kch) - slope * dist_s.astype(F32), NEG)
        m_old = m_scr[...]
        m_new = jnp.maximum(m_old, jnp.max(sc, axis=-1, keepdims=True))
        alpha = jnp.exp(m_old - m_new)
        e = jnp.where(msk, jnp.exp(sc - m_new), 0.0)
        l_scr[...] = alpha * l_scr[...] + jnp.sum(e, axis=-1, keepdims=True)
        acc_scr[...] = alpha * acc_scr[...] + _dot(e.astype(BF16), vch)
        m_scr[...] = m_new
        return carry

    lax.fori_loop(0, (i >> 2) + 1, chunk, 0)
    o_s = acc_scr[...] / jnp.maximum(l_scr[...], 1e-30)

    span = WINDOW + QB
    start = pl.multiple_of(jnp.maximum(i * QB - WINDOW, 0), QB)
    kw = kwin_ref[pl.ds(start, span), :]
    vw = vwin_ref[pl.ds(start, span), :]
    dist_w = qpos - (start + _iota((rows, span), 1))
    mask_w = jnp.where(dist_w >= 0, dist_w, WINDOW) < WINDOW
    s_w = _dot_nt(q, kw) - slope * dist_w.astype(F32)
    o_w = _dot(_masked_softmax_rows(s_w, mask_w).astype(BF16), vw)

    gt = gt_ref[0, 0]
    for r in range(GQA):
        sl = slice(r * QB, (r + 1) * QB)
        o_ref[0, r] = (gt[:, 3 * r:3 * r + 1] * o_c[sl] + gt[:, 3 * r + 1:3 * r + 2] * o_s[sl]
                       + gt[:, 3 * r + 2:3 * r + 3] * o_w[sl])


def _prompt_attention(slopes, q_blk, kvb, kce, kco, gates_g, bsz, t_len):
    nq = t_len // QB
    nb = t_len // L_CMP
    assert nb % (2 * LANES) == 0 or nb == LANES, "compressed blocks must fill whole lane tiles"
    assert t_len >= WINDOW + QB
    half = nb // 2
    rows = GQA * QB
    kv_spec = lambda col: pl.BlockSpec((t_len, DH), lambda b, g, i, col=col: (b, col + g))
    kc_spec = lambda col: pl.BlockSpec((half, DH), lambda b, g, i, col=col: (b, col + g))
    return pl.pallas_call(
        _pattn_body,
        grid=(bsz, N_KV, nq),
        in_specs=[pl.BlockSpec(memory_space=pltpu.SMEM),
                  pl.BlockSpec((1, GQA, QB, DH), lambda b, g, i: (b * nq + i, g, 0, 0)),
                  kv_spec(4), kv_spec(6), kv_spec(8), kv_spec(10),
                  kc_spec(0), kc_spec(0), kc_spec(2), kc_spec(2),
                  pl.BlockSpec((1, 1, QB, 3 * GQA), lambda b, g, i: (b, g, i, 0))],
        out_specs=pl.BlockSpec((1, GQA, QB, DH), lambda b, g, i: (b * nq + i, g, 0, 0)),
        out_shape=jax.ShapeDtypeStruct((bsz * nq, N_HEADS, QB, DH), F32),
        scratch_shapes=[pltpu.VMEM((rows, 1), F32), pltpu.VMEM((rows, 1), F32), pltpu.VMEM((rows, DH), F32)],
        compiler_params=_params(("parallel", "parallel", "arbitrary")),
        name="prompt_attention",
    )(slopes, q_blk, kvb, kvb, kvb, kvb, kce, kco, kce, kco, gates_g)


def _sample_select_body(slopes_ref, q_ref, kce_ref, kco_ref, oc_ref, sel_ref, *, past):
    rows = GQA * TP
    row = _iota((rows, 1), 0)
    qpos = past + (row & (TP - 1))
    r_of_row = row >> 3
    half = kce_ref.shape[1]
    nb = 2 * half
    n_b = q_ref.shape[0]
    lane = _iota((rows, nb), 1)
    blk_c = jnp.where(lane < half, 2 * lane, 2 * (lane - half) + 1)
    dist = qpos - (blk_c * L_CMP + (L_CMP - 1))
    pairs = []
    for b in range(n_b):
        for g in range(N_KV):
            slope = jnp.zeros((rows, 1), F32)
            for r in range(GQA):
                slope = jnp.where(r_of_row == r, slopes_ref[g * GQA + r], slope)
            q = q_ref[b, g]
            kc = jnp.concatenate([kce_ref[b, :, g * DH:(g + 1) * DH], kco_ref[b, :, g * DH:(g + 1) * DH]], axis=0)
            vc = jnp.concatenate([kce_ref[b, :, (2 + g) * DH:(3 + g) * DH], kco_ref[b, :, (2 + g) * DH:(3 + g) * DH]],
                                 axis=0)
            s = _dot_nt(q, kc) - slope * dist.astype(F32)
            p = _masked_softmax_rows(s, dist >= 0)
            oc_ref[b, g] = _dot(p.astype(BF16), vc)
            imp = p[0:TP]
            for r in range(1, GQA):
                imp = imp + p[r * TP:(r + 1) * TP]
            pairs.append(imp[:, :half] + imp[:, half:])
    pair = jnp.concatenate(pairs, axis=0)
    n_rows = pair.shape[0]
    blk = _iota((n_rows, half), 1)
    qp = past + (_iota((n_rows, half), 0) & (TP - 1))
    cur = qp >> 6
    forced = (blk == 0) | (blk == cur) | (blk == cur - 1)
    score = jnp.where(blk * L_SEL <= qp, jnp.where(forced, FORCE, pair), NEG)
    sel = jnp.zeros((n_rows, half), F32)
    for _ in range(N_SEL - 1):
        m = jnp.max(score, axis=1, keepdims=True)
        idx = jnp.min(jnp.where(score == m, blk, half), axis=1, keepdims=True)
        hit = blk == idx
        sel = jnp.where(hit, 1.0, sel)
        score = jnp.where(hit, -jnp.inf, score)
    sel = sel.astype(BF16)
    for b in range(n_b):
        for g in range(N_KV):
            i0 = (b * N_KV + g) * TP
            sel_ref[b, g] = jnp.concatenate([sel[i0:i0 + TP]] * GQA, axis=0)


SELECT_BATCH = 8


def _sample_select(slopes, q_s, kce, kco, past):
    bsz = q_s.shape[0]
    half = kce.shape[1]
    assert half == LANES, "past selection blocks must fill one lane tile"
    rows = GQA * TP
    nbt = SELECT_BATCH
    return pl.pallas_call(
        functools.partial(_sample_select_body, past=past),
        grid=(bsz // nbt,),
        in_specs=[pl.BlockSpec(memory_space=pltpu.SMEM),
                  pl.BlockSpec((nbt, N_KV, rows, DH), lambda b: (b, 0, 0, 0)),
                  pl.BlockSpec((nbt, half, 4 * DH), lambda b: (b, 0, 0)),
                  pl.BlockSpec((nbt, half, 4 * DH), lambda b: (b, 0, 0))],
        out_specs=[pl.BlockSpec((nbt, N_KV, rows, DH), lambda b: (b, 0, 0, 0)),
                   pl.BlockSpec((nbt, N_KV, rows, half), lambda b: (b, 0, 0, 0))],
        out_shape=[jax.ShapeDtypeStruct((bsz, N_KV, rows, DH), F32),
                   jax.ShapeDtypeStruct((bsz, N_KV, rows, half), BF16)],
        compiler_params=_params(("parallel",)),
        name="sample_select",
    )(slopes, q_s, kce, kco)


def _online_update(m_ref, l_ref, acc_ref, g, sc, msk, v):
    m_old = m_ref[g]
    m_new = jnp.maximum(m_old, jnp.max(sc, axis=-1, keepdims=True))
    alpha = jnp.exp(m_old - m_new)
    e = jnp.where(msk, jnp.exp(sc - m_new), 0.0)
    l_ref[g] = alpha * l_ref[g] + jnp.sum(e, axis=-1, keepdims=True)
    acc_ref[g] = alpha * acc_ref[g] + _dot(e.astype(BF16), v)
    m_ref[g] = m_new


def _sample_attn_body(pt_ref, slopes_ref, *refs, past, t_new):
    pages = refs[:PAGES_PER_STEP]
    q_ref, sel_ref, new_ref, win_ref, oc_ref, gt_ref, o_ref, m_scr, l_scr, acc_scr = refs[PAGES_PER_STEP:]
    c = pl.program_id(1)
    rows = GQA * TP
    row = _iota((rows, 1), 0)
    qpos = past + (row & (TP - 1))
    r_of_row = row >> 3
    ck = PAGES_PER_STEP * pages[0].shape[1] // KV_ROWS

    @pl.when(c == 0)
    def _():
        m_scr[...] = jnp.full(m_scr.shape, NEG, F32)
        l_scr[...] = jnp.zeros(l_scr.shape, F32)
        acc_scr[...] = jnp.zeros(acc_scr.shape, F32)

    dist_s = qpos - (c * ck + _iota((rows, ck), 1))
    expand = jnp.where((_iota((LANES, ck), 1) >> 6) + c * (ck // L_SEL) == _iota((LANES, ck), 0), 1.0, 0.0).astype(BF16)
    slopes = []
    for g in range(N_KV):
        slope = jnp.zeros((rows, 1), F32)
        for r in range(GQA):
            slope = jnp.where(r_of_row == r, slopes_ref[g * GQA + r], slope)
        slopes.append(slope)
        q = q_ref[0, g]
        kch = _page_rows(pages, g).astype(BF16)
        vch = _page_rows(pages, 2 + g).astype(BF16)
        picked = _dot(sel_ref[0, g], expand)
        msk = jnp.where(dist_s >= 0, picked, 0.0) > 0.5
        sc = jnp.where(msk, _dot_nt(q, kch) - slope * dist_s.astype(F32), NEG)
        _online_update(m_scr, l_scr, acc_scr, g, sc, msk, vch)

    @pl.when(c == pl.num_programs(1) - 1)
    def _():
        col = _iota((rows, TP), 1)
        dist_n = qpos - (past + col)
        mask_n = jnp.where(col < t_new, dist_n, -1) >= 0
        dist_c = qpos - (past - WINDOW + _iota((rows, WINDOW), 1))
        mask_c = jnp.where(dist_c >= 0, dist_c, WINDOW) < WINDOW
        mask_wn = jnp.where(mask_n, dist_n, WINDOW) < WINDOW
        for g in range(N_KV):
            slope = slopes[g]
            q = q_ref[0, g]
            kn = new_ref[0, :, (4 + g) * DH:(5 + g) * DH]
            vn = new_ref[0, :, (6 + g) * DH:(7 + g) * DH]
            sc = jnp.where(mask_n, _dot_nt(q, kn) - slope * dist_n.astype(F32), NEG)
            _online_update(m_scr, l_scr, acc_scr, g, sc, mask_n, vn)
            o_s = acc_scr[g] / jnp.maximum(l_scr[g], 1e-30)
            kwc = win_ref[0, pl.ds(g, WINDOW, KV_ROWS), :].astype(BF16)
            vwc = win_ref[0, pl.ds(2 + g, WINDOW, KV_ROWS), :].astype(BF16)
            kwn = new_ref[0, :, (8 + g) * DH:(9 + g) * DH]
            vwn = new_ref[0, :, (10 + g) * DH:(11 + g) * DH]
            s1 = jnp.where(mask_c, _dot_nt(q, kwc) - slope * dist_c.astype(F32), NEG)
            s2 = jnp.where(mask_wn, _dot_nt(q, kwn) - slope * dist_n.astype(F32), NEG)
            mx = jnp.maximum(jnp.max(s1, axis=-1, keepdims=True), jnp.max(s2, axis=-1, keepdims=True))
            e1 = jnp.where(mask_c, jnp.exp(s1 - mx), 0.0)
            e2 = jnp.where(mask_wn, jnp.exp(s2 - mx), 0.0)
            den = jnp.maximum(jnp.sum(e1, axis=-1, keepdims=True) + jnp.sum(e2, axis=-1, keepdims=True), 1e-30)
            o_w = _dot((e1 / den).astype(BF16), vwc) + _dot((e2 / den).astype(BF16), vwn)
            gt = gt_ref[0, g]
            o_ref[0, g] = gt[:, 0:1] * oc_ref[0, g] + gt[:, 1:2] * o_s + gt[:, 2:3] * o_w


def _sample_attention(slopes, page_ids, cache_sel3, q_s, sel_s, new_kvb, cache_win3, win_base, o_c, gates_s, past,
                      t_new):
    page_rows = cache_sel3.shape[1]
    bsz = q_s.shape[0]
    n_pages = page_ids.shape[0] // bsz
    n_steps = n_pages // PAGES_PER_STEP
    rows = GQA * TP
    assert cache_win3.shape[1] == WINDOW * KV_ROWS and past % L_SEL == 0 and t_new <= TP
    page_spec = lambda i: pl.BlockSpec((1, page_rows, DH),
                                       lambda b, s, pt: (pt[b * n_pages + s * PAGES_PER_STEP + i], 0, 0))
    per_b = lambda shape: pl.BlockSpec((1,) + shape, lambda b, s, pt: (b,) + (0,) * len(shape))
    grid_spec = pltpu.PrefetchScalarGridSpec(
        num_scalar_prefetch=1,
        grid=(bsz, n_steps),
        in_specs=[pl.BlockSpec(memory_space=pltpu.SMEM)] + [page_spec(i) for i in range(PAGES_PER_STEP)]
        + [per_b((N_KV, rows, DH)), per_b((N_KV, rows, LANES)), per_b((TP, 12 * DH)),
           pl.BlockSpec((1, WINDOW * KV_ROWS, DH), lambda b, s, pt: (win_base + b, 0, 0)),
           per_b((N_KV, rows, DH)), per_b((N_KV, rows, 3))],
        out_specs=per_b((N_KV, rows, DH)),
        scratch_shapes=[pltpu.VMEM((N_KV, rows, 1), F32), pltpu.VMEM((N_KV, rows, 1), F32),
                        pltpu.VMEM((N_KV, rows, DH), F32)],
    )
    return pl.pallas_call(
        functools.partial(_sample_attn_body, past=past, t_new=t_new),
        grid_spec=grid_spec,
        out_shape=jax.ShapeDtypeStruct((bsz, N_KV, rows, DH), F32),
        compiler_params=_params(("parallel", "arbitrary")),
        name="sample_attention",
    )(page_ids, slopes, *([cache_sel3] * PAGES_PER_STEP), q_s, sel_s, new_kvb, cache_win3, o_c, gates_s)


def _gelu_tanh(x):
    return 0.5 * x * (1.0 + jnp.tanh(0.7978845608028654 * (x + 0.044715 * (x * x * x))))


def _lru_coeffs(conv, wa_ref, ba_ref, wi_ref, bi_ref, sp_ref):
    cb = conv.astype(BF16)
    n_blk, blk = wa_ref.shape[0], wa_ref.shape[1]
    ra = jnp.concatenate([_dot(cb[:, n * blk:(n + 1) * blk], wa_ref[n]) for n in range(n_blk)], axis=1)
    ri = jnp.concatenate([_dot(cb[:, n * blk:(n + 1) * blk], wi_ref[n]) for n in range(n_blk)], axis=1)
    r = jax.nn.sigmoid(ra + ba_ref[...])
    gi = jax.nn.sigmoid(ri + bi_ref[...])
    log_a = -LRU_C * r * sp_ref[...]
    a = jnp.exp(log_a)
    b = jnp.sqrt(-jnp.tanh(log_a) * (a * a + 1.0)) * (gi * conv)
    return a, b


def _lru_seq_body(xr_ref, yr_ref, cs_ref, h0_ref, cw_ref, cb_ref, wa_ref, ba_ref, wi_ref, bi_ref, sp_ref, onw_ref,
                  rec_ref, hl_ref, xbuf, h_scr):
    k = pl.program_id(1)
    tt = xr_ref.shape[1]
    pad = 8

    @pl.when(k == 0)
    def _():
        xbuf[0:pad, :] = jnp.zeros((pad, xbuf.shape[1]), F32)
        xbuf[pad - (CONV_W - 1):pad, :] = cs_ref[0]
        h_scr[...] = h0_ref[0]

    x = xr_ref[0]
    xbuf[pad:pad + tt, :] = x
    conv = cb_ref[...] + cw_ref[CONV_W - 1:CONV_W, :] * x
    for j in range(CONV_W - 1):
        conv = conv + cw_ref[j:j + 1, :] * xbuf[pad - (CONV_W - 1) + j:pad - (CONV_W - 1) + j + tt, :]
    xbuf[0:pad, :] = x[tt - pad:tt]
    a, b = _lru_coeffs(conv, wa_ref, ba_ref, wi_ref, bi_ref, sp_ref)
    row = _iota((tt, 1), 0)
    s = 1
    while s < tt:
        keep = row >= s
        a_sh = jnp.where(keep, pltpu.roll(a, s, 0), 1.0)
        b_sh = jnp.where(keep, pltpu.roll(b, s, 0), 0.0)
        b = a * b_sh + b
        a = a * a_sh
        s *= 2
    hs = a * h_scr[...] + b
    h_scr[...] = hs[tt - 1:tt]
    hl_ref[0] = hs[tt - 1:tt]
    rec = hs * _gelu_tanh(yr_ref[0])
    rec_ref[0] = _rms(rec, onw_ref[...]).astype(BF16)


def _lru_seq(xr3, yr3, cs, h0, cw, cb, wa, ba, wi, bi, sp, onw, tt):
    bsz, t_len, d = xr3.shape
    seq = pl.BlockSpec((1, tt, d), lambda b, k: (b, k, 0))
    full = lambda a: pl.BlockSpec(a.shape, lambda b, k: (0,) * a.ndim)
    return pl.pallas_call(
        _lru_seq_body,
        grid=(bsz, t_len // tt),
        in_specs=[seq, seq, pl.BlockSpec((1, CONV_W - 1, d), lambda b, k: (b, 0, 0)),
                  pl.BlockSpec((1, 1, d), lambda b, k: (b, 0, 0)),
                  full(cw), full(cb), full(wa), full(ba), full(wi), full(bi), full(sp), full(onw)],
        out_specs=[seq, pl.BlockSpec((1, 1, d), lambda b, k: (b, 0, 0))],
        out_shape=[jax.ShapeDtypeStruct((bsz, t_len, d), BF16), jax.ShapeDtypeStruct((bsz, 1, d), F32)],
        scratch_shapes=[pltpu.VMEM((tt + 8, d), F32), pltpu.VMEM((1, d), F32)],
        compiler_params=_params(("parallel", "arbitrary")),
        name="lru_seq",
    )(xr3, yr3, cs, h0, cw, cb, wa, ba, wi, bi, sp, onw)


def _lru_step_body(xr_ref, yr_ref, cs_ref, h0_ref, cw_ref, cb_ref, wa_ref, ba_ref, wi_ref, bi_ref, sp_ref, onw_ref,
                   rec_ref, hl_ref):
    t_len, bsz = xr_ref.shape[0], xr_ref.shape[1]
    xs = [cs_ref[j] for j in range(CONV_W - 1)] + [xr_ref[t] for t in range(t_len)]
    convs = []
    for t in range(t_len):
        conv = cb_ref[...] + cw_ref[0:1, :] * xs[t]
        for j in range(1, CONV_W):
            conv = conv + cw_ref[j:j + 1, :] * xs[t + j]
        convs.append(conv)
    a, b = _lru_coeffs(jnp.concatenate(convs, axis=0), wa_ref, ba_ref, wi_ref, bi_ref, sp_ref)
    h = h0_ref[...]
    for t in range(t_len):
        h = a[t * bsz:(t + 1) * bsz] * h + b[t * bsz:(t + 1) * bsz]
        rec = h * _gelu_tanh(yr_ref[t])
        rec_ref[t] = _rms(rec, onw_ref[...]).astype(BF16)
    hl_ref[...] = h


def _lru_step(xr_t, yr_t, cs_t, h0, cw, cb, wa, ba, wi, bi, sp, onw):
    t_len, bsz, d = xr_t.shape
    return pl.pallas_call(
        _lru_step_body,
        out_shape=[jax.ShapeDtypeStruct((t_len, bsz, d), BF16), jax.ShapeDtypeStruct((bsz, d), F32)],
        compiler_params=pltpu.CompilerParams(vmem_limit_bytes=VMEM_LIMIT),
        name="lru_step",
    )(xr_t, yr_t, cs_t, h0, cw, cb, wa, ba, wi, bi, sp, onw)


def _outproj_body(attn_ref, rec_ref, x_ref, cnt0_ref, anw_ref, woa_ref, wor_ref, fnw_ref, wr_ref, br_ref,
                  x1_ref, h2_ref, ti_ref, tg_ref, tp_ref, cnt_ref, carry):
    step = pl.program_id(0)
    tm = x_ref.shape[0]

    @pl.when(step == 0)
    def _():
        carry[...] = cnt0_ref[...]

    attn = jnp.concatenate(
        [jnp.concatenate([attn_ref[b, hd] for hd in range(N_HEADS)], axis=1) for b in range(attn_ref.shape[0])], axis=0)
    an = _rms(attn, anw_ref[...]).astype(BF16)
    x1 = x_ref[...] + (_dot(an, woa_ref[...]) + _dot(rec_ref[...], wor_ref[...]))
    x1_ref[...] = x1
    h2 = _rms(x1, fnw_ref[...])
    dc = h2.shape[1] // LANES
    for c in range(dc):
        h2_ref[pl.ds(c, tm, dc), :] = h2[:, c * LANES:(c + 1) * LANES]
    lane = _iota((tm, LANES), 1)
    logits = jnp.dot(h2, wr_ref[...], precision=lax.Precision.HIGHEST, preferred_element_type=F32) + br_ref[...]
    lg = jnp.where(lane < N_EXPERTS, logits, -jnp.inf)
    vals, idxs = [], []
    for _ in range(TOP_K):
        m = jnp.max(lg, axis=-1, keepdims=True)
        ix = jnp.min(jnp.where(lg == m, lane, LANES), axis=-1, keepdims=True)
        vals.append(m)
        idxs.append(ix)
        lg = jnp.where(lane == ix, -jnp.inf, lg)
    es = [jnp.exp(v - vals[0]) for v in vals]
    den = es[0]
    for e in es[1:]:
        den = den + e
    onehot = jnp.zeros((tm, LANES), F32)
    for ix in idxs:
        onehot = jnp.where(lane == ix, 1.0, onehot)
    lower = jnp.where(_iota((tm, tm), 0) > _iota((tm, tm), 1), 1.0, 0.0).astype(BF16)
    rank = carry[...] + _dot(lower, onehot.astype(BF16))
    carry[...] = carry[...] + jnp.sum(onehot, axis=0, keepdims=True)
    ti = jnp.zeros((tm, LANES), I32)
    tg = jnp.zeros((tm, LANES), F32)
    tp = jnp.zeros((tm, LANES), I32)
    for k in range(TOP_K):
        pos = jnp.sum(jnp.where(lane == idxs[k], rank, 0.0), axis=-1, keepdims=True).astype(I32)
        ti = jnp.where(lane == k, idxs[k], ti)
        tg = jnp.where(lane == k, es[k] / den, tg)
        tp = jnp.where(lane == k, pos, tp)
    ti_ref[...] = ti
    tg_ref[...] = tg
    tp_ref[...] = tp
    cnt_ref[...] = carry[...]


def _outproj_router(attn_blk, recn, x2, cnt0, anw, woa, wor, fnw, wr, br, tm):
    n, d = x2.shape
    nqb = tm // QB
    row = lambda i: (i, 0)
    wspec = lambda a: pl.BlockSpec(a.shape, lambda i: (0,) * a.ndim, pipeline_mode=pl.Buffered(1))
    lanes_out = lambda dt: jax.ShapeDtypeStruct((n, LANES), dt)
    return pl.pallas_call(
        _outproj_body,
        grid=(n // tm,),
        in_specs=[pl.BlockSpec((nqb, N_HEADS, QB, DH), lambda i: (i, 0, 0, 0)),
                  pl.BlockSpec((tm, recn.shape[1]), row), pl.BlockSpec((tm, d), row),
                  wspec(cnt0), wspec(anw), wspec(woa), wspec(wor), wspec(fnw), wspec(wr), wspec(br)],
        out_specs=[pl.BlockSpec((tm, d), row), pl.BlockSpec((tm * (d // LANES), LANES), row),
                   pl.BlockSpec((tm, LANES), row),
                   pl.BlockSpec((tm, LANES), row), pl.BlockSpec((tm, LANES), row), pl.BlockSpec((1, LANES), lambda i: (0, 0))],
        out_shape=[jax.ShapeDtypeStruct((n, d), F32), jax.ShapeDtypeStruct((n * (d // LANES), LANES), F32),
                   lanes_out(I32), lanes_out(F32), lanes_out(I32), jax.ShapeDtypeStruct((1, LANES), F32)],
        scratch_shapes=[pltpu.VMEM((1, LANES), F32)],
        compiler_params=_params(("arbitrary",)),
        name="outproj_router",
    )(attn_blk, recn, x2, cnt0, anw, woa, wor, fnw, wr, br)


ISSUE_UNROLL = 8


def _gather_body(rowtok_ref, nused_ref, h2_hbm, out_ref, buf, sem):
    j = pl.program_id(0)
    tm = out_ref.shape[0]
    dc = buf.shape[1] // tm
    n_used = nused_ref[0]

    def issue_block(blk, slot):
        def body(it, c):
            for u in range(ISSUE_UNROLL):
                r = it * ISSUE_UNROLL + u
                src = pl.multiple_of(rowtok_ref[blk * tm + r] * dc, dc)
                pltpu.make_async_copy(h2_hbm.at[pl.ds(src, dc), :], buf.at[slot, pl.ds(r * dc, dc), :],
                                      sem.at[slot]).start(priority=u % 2)
            return c

        lax.fori_loop(0, tm // ISSUE_UNROLL, body, 0)

    @pl.when((j == 0) & (n_used > 0))
    def _():
        issue_block(0, 0)

    @pl.when(j + 1 < n_used)
    def _():
        issue_block(j + 1, (j + 1) % 2)

    @pl.when(j < n_used)
    def _():
        slot = j % 2
        pltpu.make_async_copy(h2_hbm.at[pl.ds(0, tm * dc), :], buf.at[slot], sem.at[slot]).wait()
        for c in range(dc):
            out_ref[:, c * LANES:(c + 1) * LANES] = buf[slot, pl.ds(c, tm, dc), :].astype(BF16)

    @pl.when(j >= n_used)
    def _():
        out_ref[...] = jnp.zeros(out_ref.shape, BF16)


def _gather_rows(row_tok, n_used, h2, n_blocks, d):
    tm = MOE_TM
    dc = d // LANES
    grid_spec = pltpu.PrefetchScalarGridSpec(
        num_scalar_prefetch=2,
        grid=(n_blocks,),
        in_specs=[pl.BlockSpec(memory_space=pl.ANY)],
        out_specs=pl.BlockSpec((tm, d), lambda j, rt, nu: (j, 0)),
        scratch_shapes=[pltpu.VMEM((2, tm * dc, LANES), F32), pltpu.SemaphoreType.DMA((2,))],
    )
    return pl.pallas_call(
        _gather_body,
        grid_spec=grid_spec,
        out_shape=jax.ShapeDtypeStruct((n_blocks * tm, d), BF16),
        compiler_params=_params(("arbitrary",)),
        name="moe_gather",
    )(row_tok, n_used, h2)


COPY_SPLIT = 4


def _expert_blocks(first_row, n_blk, n_used, n_blocks, x_hbm, out_rows, xbuf, obuf, xsem, osem, prepare, compute):
    tm = xbuf.shape[1]
    rc = tm // COPY_SPLIT
    orc = obuf.shape[1] // COPY_SPLIT

    def x_copies(b, slot):
        r = pl.multiple_of(first_row + b * tm, tm)
        return [pltpu.make_async_copy(x_hbm.at[pl.ds(r + s * rc, rc), :], xbuf.at[slot, pl.ds(s * rc, rc), :],
                                      xsem.at[slot]) for s in range(COPY_SPLIT)]

    def o_copies(r, slot):
        r = pl.multiple_of(r, tm)
        return [pltpu.make_async_copy(obuf.at[slot, pl.ds(s * orc, orc), :], out_rows(r + s * rc, rc), osem.at[slot])
                for s in range(COPY_SPLIT)]

    def start(copies):
        for cp in copies:
            cp.start(priority=1)

    def wait(copies):
        for cp in copies:
            cp.wait()

    @pl.when(n_blk > 0)
    def _():
        start(x_copies(0, 0))
        prepare()

        def body(b, c):
            slot = b % 2

            @pl.when(b + 1 < n_blk)
            def _():
                start(x_copies(b + 1, 1 - slot))

            wait(x_copies(b, slot))

            @pl.when(b >= 2)
            def _():
                wait(o_copies(first_row + (b - 2) * tm, slot))

            compute(xbuf[slot], slot)
            start(o_copies(first_row + b * tm, slot))
            return c

        lax.fori_loop(0, n_blk, body, 0)

        @pl.when(n_blk >= 2)
        def _():
            wait(o_copies(first_row + (n_blk - 2) * tm, n_blk % 2))

        wait(o_copies(first_row + (n_blk - 1) * tm, (n_blk - 1) % 2))

    @pl.when(pl.program_id(0) == pl.num_programs(0) - 1)
    def _():
        obuf[0] = jnp.zeros(obuf.shape[1:], obuf.dtype)

        def fill(t, c):
            start(o_copies(t * tm, 0))
            wait(o_copies(t * tm, 0))
            return c

        lax.fori_loop(n_used, n_blocks, fill, 0)


W_SPLIT = 4


def _moe_up_body(row0_ref, nblk_ref, nu_ref, xs_hbm, *refs, n_blocks):
    w_refs = refs[:W_SPLIT]
    b_ref, perm_ref, h_hbm, wbf, xbuf, obuf, xsem, osem = refs[W_SPLIT:]
    e = pl.program_id(0)
    tn = wbf.shape[1]
    pw = perm_ref.shape[0]
    hw = pw // 2
    kc = w_refs[0].shape[1]

    def prepare():
        for q, w_ref in enumerate(w_refs):
            for c in range(tn // pw):
                w = w_ref[0, :, c * pw:(c + 1) * pw].astype(BF16)
                wbf[q * kc:(q + 1) * kc, c * pw:(c + 1) * pw] = _dot(w, perm_ref[...]).astype(BF16)

    def compute(x, slot):
        gu = _dot(x, wbf[...]) + b_ref[0]
        for c in range(tn // pw):
            glu = jnp.minimum(gu[:, c * pw:c * pw + hw], SWIGLU_LIMIT)
            lin = jnp.clip(gu[:, c * pw + hw:(c + 1) * pw], -SWIGLU_LIMIT, SWIGLU_LIMIT)
            obuf[slot, :, c * hw:(c + 1) * hw] = (glu * jax.nn.sigmoid(SWIGLU_ALPHA * glu) * (lin + 1.0)).astype(BF16)

    col0 = pl.multiple_of(pl.program_id(1) * (tn // 2), tn // 2)
    out_rows = lambda r, n: h_hbm.at[pl.ds(r, n), pl.ds(col0, tn // 2)]
    _expert_blocks(row0_ref[e], nblk_ref[e], nu_ref[0], n_blocks, xs_hbm, out_rows, xbuf, obuf, xsem, osem,
                   prepare, compute)


def _w_specs(k, tn):
    return [pl.BlockSpec((1, k // W_SPLIT, tn), lambda e, n, *_, q=q: (e, q, n)) for q in range(W_SPLIT)]


def _moe_up(row0, nblk, n_used, xs, w_gu, b_gu, perm):
    n_rows, d = xs.shape
    n_e, _, f2 = w_gu.shape
    tm, tn = MOE_TM, MOE_TN
    grid_spec = pltpu.PrefetchScalarGridSpec(
        num_scalar_prefetch=3,
        grid=(n_e, f2 // tn),
        in_specs=[pl.BlockSpec(memory_space=pl.ANY)] + _w_specs(d, tn)
        + [pl.BlockSpec((1, 1, tn), lambda e, n, *_: (e, 0, n)),
           pl.BlockSpec(perm.shape, lambda e, n, *_: (0, 0))],
        out_specs=pl.BlockSpec(memory_space=pl.ANY),
        scratch_shapes=[pltpu.VMEM((d, tn), BF16), pltpu.VMEM((2, tm, d), BF16), pltpu.VMEM((2, tm, tn // 2), BF16),
                        pltpu.SemaphoreType.DMA((2,)), pltpu.SemaphoreType.DMA((2,))],
    )
    return pl.pallas_call(
        functools.partial(_moe_up_body, n_blocks=n_rows // tm),
        grid_spec=grid_spec,
        out_shape=jax.ShapeDtypeStruct((n_rows, f2 // 2), BF16),
        compiler_params=_params(("arbitrary", "arbitrary")),
        name="moe_up",
    )(row0, nblk, n_used, xs, *([w_gu] * W_SPLIT), b_gu, perm)


def _moe_down_body(row0_ref, nblk_ref, nu_ref, h_hbm, *refs, n_blocks):
    w_refs = refs[:W_SPLIT]
    b_ref, y_hbm, wbf, xbuf, obuf, xsem, osem = refs[W_SPLIT:]
    e = pl.program_id(0)
    tn = wbf.shape[1]
    kc = w_refs[0].shape[1]

    def prepare():
        for q, w_ref in enumerate(w_refs):
            wbf[q * kc:(q + 1) * kc, :] = w_ref[0].astype(BF16)

    dc = tn // LANES
    tm = xbuf.shape[1]

    def compute(h, slot):
        y = _dot(h, wbf[...]) + b_ref[0]
        for c in range(dc):
            obuf[slot, pl.ds(c, tm, dc), :] = y[:, c * LANES:(c + 1) * LANES]

    out_rows = lambda r, n: y_hbm.at[pl.ds(pl.multiple_of(r * dc, dc), n * dc), :]
    _expert_blocks(row0_ref[e], nblk_ref[e], nu_ref[0], n_blocks, h_hbm, out_rows, xbuf, obuf, xsem, osem,
                   prepare, compute)


def _moe_down(row0, nblk, n_used, h, w_d, b_d):
    n_rows, f = h.shape
    n_e, _, d = w_d.shape
    tm, tn = MOE_TM, MOE_TN
    assert tn == d, "a grid step writes whole output rows"
    grid_spec = pltpu.PrefetchScalarGridSpec(
        num_scalar_prefetch=3,
        grid=(n_e, d // tn),
        in_specs=[pl.BlockSpec(memory_space=pl.ANY)] + _w_specs(f, tn)
        + [pl.BlockSpec((1, 1, tn), lambda e, n, *_: (e, 0, n))],
        out_specs=pl.BlockSpec(memory_space=pl.ANY),
        scratch_shapes=[pltpu.VMEM((f, tn), BF16), pltpu.VMEM((2, tm, f), BF16),
                        pltpu.VMEM((2, tm * (d // LANES), LANES), F32),
                        pltpu.SemaphoreType.DMA((2,)), pltpu.SemaphoreType.DMA((2,))],
    )
    return pl.pallas_call(
        functools.partial(_moe_down_body, n_blocks=n_rows // tm),
        grid_spec=grid_spec,
        out_shape=jax.ShapeDtypeStruct((n_rows * (d // LANES), LANES), F32),
        compiler_params=_params(("arbitrary", "arbitrary")),
        name="moe_down",
    )(row0, nblk, n_used, h, *([w_d] * W_SPLIT), b_d)


def _combine_body(dest_hbm, gate_ref, x1_ref, y_hbm, outp_ref, outs_ref, dsm, buf, sem_d, sem, *, np_blocks):
    j = pl.program_id(0)
    n = pl.num_programs(0)
    tk = x1_ref.shape[0]
    dc = x1_ref.shape[1] // LANES
    toks = ISSUE_UNROLL // TOP_K

    def idx_copy(step, slot):
        return pltpu.make_async_copy(dest_hbm.at[step, 0], dsm.at[slot], sem_d.at[slot])

    def issue_rows(slot):
        def body(it, c):
            for u in range(toks):
                t = it * toks + u
                for k in range(TOP_K):
                    src = pl.multiple_of(dsm[slot, t * TOP_K + k] * dc, dc)
                    pltpu.make_async_copy(y_hbm.at[pl.ds(src, dc), :], buf.at[slot, k, pl.ds(t * dc, dc), :],
                                          sem.at[slot]).start(priority=k % 2)
            return c

        lax.fori_loop(0, tk // toks, body, 0)

    @pl.when(j == 0)
    def _():
        idx_copy(0, 0).start()
        idx_copy(0, 0).wait()
        issue_rows(0)

        @pl.when(1 < n)
        def _():
            idx_copy(1, 1).start()

    @pl.when(j + 1 < n)
    def _():
        slot = (j + 1) % 2
        idx_copy(j + 1, slot).wait()
        issue_rows(slot)

    @pl.when(j + 2 < n)
    def _():
        idx_copy(j + 2, j % 2).start()

    slot = j % 2
    for k in range(TOP_K):
        pltpu.make_async_copy(y_hbm.at[pl.ds(0, tk * dc), :], buf.at[slot, k], sem.at[slot]).wait()
    g = gate_ref[...]
    gk = [jnp.broadcast_to(g[:, k:k + 1], (tk, LANES)) for k in range(TOP_K)]

    def result(c):
        acc = gk[0] * buf[slot, 0, pl.ds(c, tk, dc), :]
        for k in range(1, TOP_K):
            acc = acc + gk[k] * buf[slot, k, pl.ds(c, tk, dc), :]
        return x1_ref[:, c * LANES:(c + 1) * LANES] + acc

    @pl.when(j < np_blocks)
    def _():
        for c in range(dc):
            outp_ref[:, c * LANES:(c + 1) * LANES] = result(c)

    @pl.when(j >= np_blocks)
    def _():
        for c in range(dc):
            outs_ref[:, c * LANES:(c + 1) * LANES] = result(c)


def _combine(dest3, gates, x1, y, n_prompt):
    n, d = x1.shape
    tk = dest3.shape[2] // TOP_K
    np_blocks = n_prompt // tk
    row = lambda j: (j, 0)
    return pl.pallas_call(
        functools.partial(_combine_body, np_blocks=np_blocks),
        grid=(n // tk,),
        in_specs=[pl.BlockSpec(memory_space=pl.ANY), pl.BlockSpec((tk, LANES), row), pl.BlockSpec((tk, d), row),
                  pl.BlockSpec(memory_space=pl.ANY)],
        out_specs=[pl.BlockSpec((tk, d), lambda j: (jnp.minimum(j, np_blocks - 1), 0)),
                   pl.BlockSpec((tk, d), lambda j: (jnp.maximum(j - np_blocks, 0), 0))],
        out_shape=[jax.ShapeDtypeStruct((n_prompt, d), F32), jax.ShapeDtypeStruct((n - n_prompt, d), F32)],
        scratch_shapes=[pltpu.SMEM((2, tk * TOP_K), I32), pltpu.VMEM((2, TOP_K, tk * (d // LANES), LANES), F32),
                        pltpu.SemaphoreType.DMA((2,)), pltpu.SemaphoreType.DMA((2,))],
        compiler_params=_params(("arbitrary",)),
        name="moe_combine",
    )(dest3, gates, x1, y)


def _moe(h2, x1, topi, topg, topp, counts, w_gu, b_gu, w_d, b_d, n_prompt):
    n, d = x1.shape
    tm = MOE_TM
    n_asg = n * TOP_K
    n_blocks = n_asg // tm + N_EXPERTS
    padded = (counts + tm - 1) // tm * tm
    pad_end = jnp.cumsum(padded)
    pad_start = pad_end - padded
    dest = pad_start[topi[:, :TOP_K]] + topp[:, :TOP_K]
    tok = jnp.broadcast_to(jnp.arange(n, dtype=I32)[:, None], (n, TOP_K))
    row_tok = jnp.zeros((n_blocks * tm,), I32).at[dest.reshape(-1)].set(tok.reshape(-1))
    n_used = (pad_end[-1:] // tm).astype(I32)
    row0 = pad_start.astype(I32)
    nblk = (padded // tm).astype(I32)
    f2 = w_gu.shape[2]
    pw = 2 * LANES
    src = jnp.arange(pw)
    perm = (jnp.arange(pw)[:, None] == jnp.where(src < LANES, 2 * src, 2 * (src - LANES) + 1)[None, :]).astype(BF16)
    b_gu_p = b_gu.reshape(N_EXPERTS, f2 // pw, LANES, 2).transpose(0, 1, 3, 2).reshape(N_EXPERTS, 1, f2)
    xs = _gather_rows(row_tok, n_used, h2, n_blocks, d)
    h = _moe_up(row0, nblk, n_used, xs, w_gu, b_gu_p, perm)
    y = _moe_down(row0, nblk, n_used, h, w_d, b_d.reshape(N_EXPERTS, 1, d))
    tk = 128
    dest3 = dest.astype(I32).reshape(n // tk, 1, tk * TOP_K)
    return _combine(dest3, topg, x1, y, n_prompt)


def _layer(layer, xp, xs, cache_cmp_all, cache_sel_all, cache_win_all, state_conv, state_h, page_table, w):
    (norm_mix_w, w_in, q_norm_w, k_norm_w, w_cmp_k, w_cmp_v, conv_w, conv_b, w_gate_a, b_gate_a, w_gate_i, b_gate_i,
     lru_lambda, out_norm_attn, out_norm_rec, w_out, norm_ffn_w, w_router, b_router, w_gate_up, b_gate_up, w_down,
     b_down) = w
    bp, tp, d = xp.shape
    bs, ts, _ = xs.shape
    d_rec = d - D_ATTN
    past = page_table.shape[1] * cache_cmp_all.shape[2]
    assert bs * ts == QB and ts <= TP

    o1, o2 = D_ATTN, D_ATTN + 12 * DH
    o3 = o2 + 3 * N_HEADS
    wq = w_in[:, :o1].astype(BF16)
    wkv = w_in[:, o1:o2].astype(BF16)
    wg = jnp.pad(w_in[:, o2:o3], ((0, 0), (0, LANES - 3 * N_HEADS))).astype(BF16)
    wxy = w_in[:, o3:].astype(BF16)
    row2 = lambda v: v.reshape(1, -1)
    slopes = jnp.exp2(-8.0 * jnp.arange(1, N_HEADS + 1, dtype=F32) / N_HEADS)
    w4 = jnp.concatenate([w_cmp_k, w_cmp_v], axis=0)
    z4 = jnp.zeros_like(w4)
    wl = jnp.stack([jnp.concatenate([w4, z4], axis=1), jnp.concatenate([z4, w4], axis=1)])
    wl = jnp.broadcast_to(wl[..., None], wl.shape + (DH,))
    sp = row2(jax.nn.softplus(-lru_lambda.astype(F32)))
    wa, wi = w_gate_a.astype(BF16), w_gate_i.astype(BF16)
    woa, wor = w_out[:D_ATTN].astype(BF16), w_out[D_ATTN:].astype(BF16)
    wr = jnp.pad(w_router, ((0, 0), (0, LANES - N_EXPERTS)))
    br = row2(jnp.pad(b_router, (0, LANES - N_EXPERTS)))
    mix = (row2(norm_mix_w), wq, wkv, wg, wxy, row2(q_norm_w), k_norm_w)
    lru_w = (conv_w, row2(conv_b), wa, row2(b_gate_a), wi, row2(b_gate_i), sp, row2(out_norm_rec))

    np_tok = bp * tp
    q_p, cmp_p, sel_p, win_p, kvb_p, gate_p, xr_p, yr_p = _inproj(xp.reshape(np_tok, d), *mix, tm=256)
    kce_p, kco_p = _pool_prompt(cmp_p, wl)
    gates_g = gate_p[:, :3 * N_HEADS].reshape(bp, tp, N_KV, 3 * GQA).transpose(0, 2, 1, 3)
    attn_p = _prompt_attention(slopes, q_p, kvb_p, kce_p, kco_p, gates_g, bp, tp)
    recn_p, hl_p = _lru_seq(xr_p.reshape(bp, tp, d_rec), yr_p.reshape(bp, tp, d_rec),
                            jnp.zeros((bp, CONV_W - 1, d_rec), F32), jnp.zeros((bp, 1, d_rec), F32), *lru_w, tt=256)
    post = (row2(out_norm_attn), woa, wor, row2(norm_ffn_w), wr, br)
    x1_p, h2_p, ti_p, tg_p, tp_p, cnt_p = _outproj_router(
        attn_p, recn_p.reshape(np_tok, d_rec), xp.reshape(np_tok, d), jnp.zeros((1, LANES), F32), *post, tm=256)

    ns_tok = bs * ts
    q_s, cmp_s, sel_s, win_s, kvb_s, gate_s, xr_s, yr_s = _inproj(xs.reshape(ns_tok, d), *mix, tm=QB)
    n_pool, page = cache_cmp_all.shape[1], cache_cmp_all.shape[2]
    page_view = lambda c: c.reshape(c.shape[0] * n_pool, page * KV_ROWS, DH)
    page_ids = (page_table + layer * n_pool).reshape(-1)
    kce_s, kco_s = _pool_pages(page_view(cache_cmp_all), page_ids, bs, wl)
    q_t = q_s[0].reshape(N_KV, GQA, bs, ts, DH).transpose(2, 0, 1, 3, 4)
    q_t = jnp.pad(q_t, ((0, 0), (0, 0), (0, 0), (0, TP - ts), (0, 0))).reshape(bs, N_KV, GQA * TP, DH)
    oc_s, selm_s = _sample_select(slopes, q_t, kce_s, kco_s, past)
    new_kvb = jnp.pad(kvb_s.reshape(bs, ts, 12 * DH), ((0, 0), (0, TP - ts), (0, 0)))
    g_t = gate_s[:, :3 * N_HEADS].reshape(bs, ts, N_KV, GQA, 3).transpose(0, 2, 3, 1, 4)
    g_t = jnp.pad(g_t, ((0, 0), (0, 0), (0, 0), (0, TP - ts), (0, 0))).reshape(bs, N_KV, GQA * TP, 3)
    win_view = cache_win_all.reshape(cache_win_all.shape[0] * bs, cache_win_all.shape[2] * KV_ROWS, DH)
    attn_s = _sample_attention(slopes, page_ids, page_view(cache_sel_all), q_t, selm_s, new_kvb, win_view, layer * bs,
                               oc_s, g_t, past, ts)
    attn_s = attn_s.reshape(bs, N_KV, GQA, TP, DH)[:, :, :, :ts].transpose(1, 2, 0, 3, 4).reshape(1, N_HEADS, QB, DH)
    tmaj = lambda a: a.reshape(bs, ts, d_rec).transpose(1, 0, 2)
    recn_s, hl_s = _lru_step(tmaj(xr_s), tmaj(yr_s), state_conv.transpose(1, 0, 2), state_h, *lru_w)
    x1_s, h2_s, ti_s, tg_s, tp_s, cnt = _outproj_router(
        attn_s, recn_s.transpose(1, 0, 2).reshape(ns_tok, d_rec), xs.reshape(ns_tok, d), cnt_p, *post, tm=QB)

    cat = lambda a, b: jnp.concatenate([a, b], axis=0)
    out_p, out_s = _moe(cat(h2_p, h2_s), cat(x1_p, x1_s), cat(ti_p, ti_s), cat(tg_p, tg_s), cat(tp_p, tp_s),
                        cnt[0, :N_EXPERTS].astype(I32), w_gate_up, b_gate_up, w_down, b_down, np_tok)
    y_p = out_p.reshape(bp, tp, d)
    y_s = out_s.reshape(bs, ts, d)

    kv5 = lambda a, b, t: a.reshape(b, t, 2, N_KV, DH)
    win_len_p = min(WINDOW, tp)
    st_p = (kv5(cmp_p, bp, tp), kv5(sel_p, bp, tp), kv5(win_p, bp, tp)[:, tp - win_len_p:],
            xr_p.reshape(bp, tp, d_rec)[:, tp - (CONV_W - 1):], hl_p.reshape(bp, d_rec))
    cache_win = cache_win_all[layer]
    win_all = jnp.concatenate([cache_win, kv5(win_s, bs, ts)], axis=1)
    xcat = jnp.concatenate([state_conv, xr_s.reshape(bs, ts, d_rec)], axis=1)
    st_s = (kv5(cmp_s, bs, ts), kv5(sel_s, bs, ts), win_all[:, win_all.shape[1] - cache_win.shape[1]:],
            xcat[:, ts:], hl_s)
    return y_p, y_s, st_p, st_s


def kernel(x_prompt, x_sample, cache_cmp_kv, cache_sel_kv, cache_win_kv, state_conv, state_h, page_table, norm_mix_w, w_in, q_norm_w, k_norm_w, w_cmp_k, w_cmp_v, conv_w, conv_b, w_gate_a, b_gate_a, w_gate_i, b_gate_i, lru_lambda, out_norm_attn, out_norm_rec, w_out, norm_ffn_w, w_router, b_router, w_gate_up, b_gate_up, w_down, b_down):
    depth = w_in.shape[0]
    xp, xs = x_prompt, x_sample
    st_ps, st_ss = [], []
    for l in range(depth):
        w = (norm_mix_w[l], w_in[l], q_norm_w[l], k_norm_w[l], w_cmp_k[l], w_cmp_v[l], conv_w[l], conv_b[l],
             w_gate_a[l], b_gate_a[l], w_gate_i[l], b_gate_i[l], lru_lambda[l], out_norm_attn[l], out_norm_rec[l],
             w_out[l], norm_ffn_w[l], w_router[l], b_router[l], w_gate_up[l], b_gate_up[l], w_down[l], b_down[l])
        xp, xs, st_p, st_s = _layer(l, xp, xs, cache_cmp_kv, cache_sel_kv, cache_win_kv, state_conv[l], state_h[l],
                                    page_table, w)
        st_ps.append(st_p)
        st_ss.append(st_s)
    stack = lambda sts, i: jnp.stack([s[i] for s in sts])
    return (xp, xs) + tuple(stack(st_ps, i) for i in range(5)) + tuple(stack(st_ss, i) for i in range(5))
```

```python
import functools

import jax
import jax.numpy as jnp
from jax import lax
from jax.experimental import pallas as pl
from jax.experimental.pallas import tpu as pltpu

F32 = jnp.float32
BF16 = jnp.bfloat16
I32 = jnp.int32

N_HEADS = 8
N_KV = 2
GQA = N_HEADS // N_KV
DH = 128
D_ATTN = N_HEADS * DH
CONV_W = 4
LRU_C = 8.0
L_CMP = 32
L_SEL = 64
N_SEL = 16
WINDOW = 512
QB = 128
N_EXPERTS = 32
TOP_K = 4
SWIGLU_LIMIT = 7.0
SWIGLU_ALPHA = 1.702
EPS = 1e-6
NEG = -1e30
FORCE = 1e9
Q_SCALE = DH ** -0.5
LANES = 128
TP = 8
KV_ROWS = 2 * N_KV
MOE_TM = 256
MOE_TN = 2048
PAGES_PER_STEP = 16
VMEM_LIMIT = 56 * 1024 * 1024


def _dot(a, b):
    return jnp.dot(a, b, preferred_element_type=F32)


def _dot_nt(a, b):
    return lax.dot_general(a, b, (((1,), (1,)), ((), ())), preferred_element_type=F32)


def _dot_tn(a, b):
    return lax.dot_general(a, b, (((0,), (0,)), ((), ())), preferred_element_type=F32)


def _iota(shape, dim):
    return lax.broadcasted_iota(I32, shape, dim)


def _rms(x, w):
    return x * lax.rsqrt(jnp.mean(x * x, axis=-1, keepdims=True) + EPS) * w


def _masked_softmax_rows(s, mask):
    s = jnp.where(mask, s, NEG)
    e = jnp.where(mask, jnp.exp(s - jnp.max(s, axis=-1, keepdims=True)), 0.0)
    return e / jnp.maximum(jnp.sum(e, axis=-1, keepdims=True), 1e-30)


def _masked_softmax_cols(s, mask):
    s = jnp.where(mask, s, NEG)
    e = jnp.where(mask, jnp.exp(s - jnp.max(s, axis=0, keepdims=True)), 0.0)
    return e / jnp.maximum(jnp.sum(e, axis=0, keepdims=True), 1e-30)


def _params(sem, vmem=None):
    return pltpu.CompilerParams(dimension_semantics=sem, vmem_limit_bytes=vmem or VMEM_LIMIT)


def _inproj_body(x_ref, nw_ref, wq_ref, wkv_ref, wg_ref, wxy_ref, qnw_ref, knw_ref,
                 q_ref, cmp_ref, sel_ref, win_ref, kvb_ref, gate_ref, xr_ref, yr_ref):
    x = x_ref[...]
    h = _rms(x, nw_ref[...]).astype(BF16)
    q = _dot(h, wq_ref[...])
    qnw = qnw_ref[...]
    nqb = q_ref.shape[0]
    for hd in range(N_HEADS):
        qn = (_rms(q[:, hd * DH:(hd + 1) * DH], qnw) * Q_SCALE).astype(BF16)
        for b in range(nqb):
            q_ref[b, hd] = qn[b * QB:(b + 1) * QB]
    kv = _dot(h, wkv_ref[...])
    tm = x.shape[0]
    outs = (cmp_ref, sel_ref, win_ref)
    for br in range(3):
        knw = knw_ref[br:br + 1, :]
        for g in range(N_KV):
            c0 = br * 4 * DH + g * DH
            kn = _rms(kv[:, c0:c0 + DH], knw)
            v = kv[:, c0 + 2 * DH:c0 + 3 * DH]
            outs[br][pl.ds(g, tm, KV_ROWS), :] = kn
            outs[br][pl.ds(2 + g, tm, KV_ROWS), :] = v
            kvb_ref[:, c0:c0 + DH] = kn.astype(BF16)
            kvb_ref[:, c0 + 2 * DH:c0 + 3 * DH] = v.astype(BF16)
    gate_ref[...] = jax.nn.sigmoid(_dot(h, wg_ref[...]))
    xy = _dot(h, wxy_ref[...])
    d_rec = xr_ref.shape[1]
    xr_ref[...] = xy[:, :d_rec]
    yr_ref[...] = xy[:, d_rec:]


def _inproj(x2, nw, wq, wkv, wg, wxy, qnw, knw, tm):
    n, d = x2.shape
    d_rec = wxy.shape[1] // 2
    nqb = tm // QB
    row = lambda i: (i, 0)
    const = lambda i: (0, 0)
    wspec = lambda a: pl.BlockSpec(a.shape, const, pipeline_mode=pl.Buffered(1))
    return pl.pallas_call(
        _inproj_body,
        grid=(n // tm,),
        in_specs=[pl.BlockSpec((tm, d), row), wspec(nw), wspec(wq), wspec(wkv), wspec(wg), wspec(wxy),
                  wspec(qnw), wspec(knw)],
        out_specs=[pl.BlockSpec((nqb, N_HEADS, QB, DH), lambda i: (i, 0, 0, 0)),
                   pl.BlockSpec((tm * KV_ROWS, DH), row), pl.BlockSpec((tm * KV_ROWS, DH), row),
                   pl.BlockSpec((tm * KV_ROWS, DH), row),
                   pl.BlockSpec((tm, 12 * DH), row), pl.BlockSpec((tm, LANES), row),
                   pl.BlockSpec((tm, d_rec), row), pl.BlockSpec((tm, d_rec), row)],
        out_shape=[jax.ShapeDtypeStruct((n // QB, N_HEADS, QB, DH), BF16),
                   jax.ShapeDtypeStruct((n * KV_ROWS, DH), F32), jax.ShapeDtypeStruct((n * KV_ROWS, DH), F32),
                   jax.ShapeDtypeStruct((n * KV_ROWS, DH), F32), jax.ShapeDtypeStruct((n, 12 * DH), BF16),
                   jax.ShapeDtypeStruct((n, LANES), F32),
                   jax.ShapeDtypeStruct((n, d_rec), F32), jax.ShapeDtypeStruct((n, d_rec), F32)],
        compiler_params=_params(("parallel",)),
        name="inproj",
    )(x2, nw, wq, wkv, wg, wxy, qnw, knw)


def _pool_combo(rows, wl_ref, combo):
    r = rows.shape[0] // (2 * L_CMP)
    x3 = rows.reshape(r, 2 * L_CMP, rows.shape[1])
    even = jnp.sum(x3 * wl_ref[0, combo][None], axis=1)
    odd = jnp.sum(x3 * wl_ref[1, combo][None], axis=1)
    return even.astype(BF16), odd.astype(BF16)


def _pool_body(x_ref, wl_ref, e_ref, o_ref):
    n_tok = x_ref.shape[0] // KV_ROWS
    for combo in range(KV_ROWS):
        even, odd = _pool_combo(x_ref[pl.ds(combo, n_tok, KV_ROWS), :], wl_ref, combo)
        e_ref[:, combo * DH:(combo + 1) * DH] = even
        o_ref[:, combo * DH:(combo + 1) * DH] = odd


def _pool_prompt(cmp4, wl):
    n = cmp4.shape[0] // KV_ROWS
    toks = 1024
    ob = toks // (2 * L_CMP)
    c = KV_ROWS * DH
    return pl.pallas_call(
        _pool_body,
        grid=(n // toks,),
        in_specs=[pl.BlockSpec((toks * KV_ROWS, DH), lambda i: (i, 0)), pl.BlockSpec(wl.shape, lambda i: (0, 0, 0, 0))],
        out_specs=[pl.BlockSpec((ob, c), lambda i: (i, 0)), pl.BlockSpec((ob, c), lambda i: (i, 0))],
        out_shape=[jax.ShapeDtypeStruct((n // (2 * L_CMP), c), BF16)] * 2,
        compiler_params=_params(("parallel",)),
        name="pool_prompt",
    )(cmp4, wl)


def _page_rows(pages, combo):
    n_tok = pages[0].shape[1] // KV_ROWS
    return jnp.concatenate([p[0, pl.ds(combo, n_tok, KV_ROWS), :] for p in pages], axis=0)


def _pool_pages_body(pt_ref, *refs):
    pages = refs[:PAGES_PER_STEP]
    wl_ref, e_ref, o_ref = refs[PAGES_PER_STEP:]
    for combo in range(KV_ROWS):
        even, odd = _pool_combo(_page_rows(pages, combo), wl_ref, combo)
        e_ref[0, :, combo * DH:(combo + 1) * DH] = even
        o_ref[0, :, combo * DH:(combo + 1) * DH] = odd


def _pool_pages(cache3, page_ids, bsz, wl):
    rows = cache3.shape[1]
    n_pages = page_ids.shape[0] // bsz
    n_steps = n_pages // PAGES_PER_STEP
    ob = PAGES_PER_STEP * (rows // KV_ROWS) // (2 * L_CMP)
    c = KV_ROWS * DH
    page_spec = lambda i: pl.BlockSpec((1, rows, DH), lambda b, s, pt: (pt[b * n_pages + s * PAGES_PER_STEP + i], 0, 0))
    grid_spec = pltpu.PrefetchScalarGridSpec(
        num_scalar_prefetch=1,
        grid=(bsz, n_steps),
        in_specs=[page_spec(i) for i in range(PAGES_PER_STEP)] + [pl.BlockSpec(wl.shape, lambda b, s, pt: (0, 0, 0, 0))],
        out_specs=[pl.BlockSpec((1, ob, c), lambda b, s, pt: (b, s, 0))] * 2,
    )
    return pl.pallas_call(
        _pool_pages_body,
        grid_spec=grid_spec,
        out_shape=[jax.ShapeDtypeStruct((bsz, n_steps * ob, c), BF16)] * 2,
        compiler_params=_params(("parallel", "parallel")),
        name="pool_pages",
    )(page_ids, *([cache3] * PAGES_PER_STEP), wl)


def _select_blocks_cols(score, blk, n_pick, n_blk):
    sel = jnp.zeros(score.shape, F32)
    for _ in range(n_pick):
        m = jnp.max(score, axis=0, keepdims=True)
        idx = jnp.min(jnp.where(score == m, blk, n_blk), axis=0, keepdims=True)
        hit = blk == idx
        sel = jnp.where(hit, 1.0, sel)
        score = jnp.where(hit, -jnp.inf, score)
    return sel


def _pattn_body(slopes_ref, q_ref, ksel_ref, vsel_ref, kwin_ref, vwin_ref, kce_ref, kco_ref, vce_ref, vco_ref,
                gt_ref, o_ref, m_scr, l_scr, acc_scr):
    g = pl.program_id(1)
    i = pl.program_id(2)
    rows = GQA * QB
    q = q_ref[0].reshape(rows, DH)
    col = _iota((1, rows), 1)
    qpos = i * QB + (col & (QB - 1))
    slope = jnp.zeros((1, rows), F32)
    slope_r = jnp.zeros((rows, 1), F32)
    for r in range(GQA):
        slope = jnp.where((col >> 7) == r, slopes_ref[g * GQA + r], slope)
        slope_r = jnp.where((_iota((rows, 1), 0) >> 7) == r, slopes_ref[g * GQA + r], slope_r)

    kc = jnp.concatenate([kce_ref[...], kco_ref[...]], axis=0)
    vc = jnp.concatenate([vce_ref[...], vco_ref[...]], axis=0)
    nb = kc.shape[0]
    half = nb // 2
    brow = _iota((nb, rows), 0)
    blk_c = jnp.where(brow < half, 2 * brow, 2 * (brow - half) + 1)
    dist = qpos - (blk_c * L_CMP + (L_CMP - 1))
    s = _dot_nt(kc, q) - slope * dist.astype(F32)
    p = _masked_softmax_cols(s, dist >= 0)
    o_c = _dot_tn(vc, p.astype(BF16))
    imp_t = p[:, 0:QB]
    for r in range(1, GQA):
        imp_t = imp_t + p[:, r * QB:(r + 1) * QB]
    pair = imp_t[:half] + imp_t[half:]
    ns = half
    blk = _iota((ns, QB), 0)
    qp = i * QB + _iota((ns, QB), 1)
    cur = qp >> 6
    forced = (blk == 0) | (blk == cur) | (blk == cur - 1)
    visible = blk * L_SEL <= qp
    score = jnp.where(visible, jnp.where(forced, FORCE, pair), NEG)
    sel = _select_blocks_cols(score, blk, min(N_SEL, ns), ns)

    unpicked = jnp.where(visible, 1.0 - sel, 1.0)
    unpicked = jnp.concatenate([unpicked, jnp.zeros((LANES - ns, QB), F32)], axis=0)
    tail = jnp.concatenate([unpicked.T] * GQA, axis=0)
    lane_t = _iota((rows, LANES), 1)
    q_first = (i * QB).astype(F32)
    tail = tail + jnp.where(lane_t == ns, slope_r * L_SEL,
                            jnp.where(lane_t == ns + 1, slope_r, jnp.where(lane_t == ns + 2, -slope_r * q_first, 0.0)))
    q_aug = jnp.concatenate([q, tail.astype(BF16)], axis=1)

    m_scr[...] = jnp.full(m_scr.shape, NEG, F32)
    l_scr[...] = jnp.zeros(l_scr.shape, F32)
    acc_scr[...] = jnp.zeros(acc_scr.shape, F32)
    ck = 512

    def attend(k0, causal):
        sc = _dot_nt(ksel_ref[pl.ds(k0, ck), :], q_aug)
        if causal:
            sc = jnp.where(qpos >= k0 + _iota((ck, rows), 0), sc, NEG)
        m_old = m_scr[...]
        m_new = jnp.maximum(m_old, jnp.max(sc, axis=0, keepdims=True))
        alpha = jnp.exp(m_old - m_new)
        e = jnp.exp(sc - m_new)
        l_scr[...] = alpha * l_scr[...] + jnp.sum(e, axis=0, keepdims=True)
        acc_scr[...] = alpha * acc_scr[...] + _dot_tn(vsel_ref[pl.ds(k0, ck), :], e.astype(BF16))
        m_scr[...] = m_new

    def chunk(c, carry):
        attend(pl.multiple_of(c * ck, ck), causal=False)
        return carry

    lax.fori_loop(0, i >> 2, chunk, 0)
    attend(pl.multiple_of((i >> 2) * ck, ck), causal=True)
    o_s = acc_scr[...] / jnp.maximum(l_scr[...], 1e-30)

    span = WINDOW + QB
    start = pl.multiple_of(jnp.maximum(i * QB - WINDOW, 0), QB)
    vw = vwin_ref[pl.ds(start, span), :]
    dist_w = qpos - (start + _iota((span, rows), 0))
    mask_w = jnp.where(dist_w >= 0, dist_w, WINDOW) < WINDOW
    s_w = _dot_nt(kwin_ref[pl.ds(start, span), :], q_aug)
    o_w = _dot_tn(vw, _masked_softmax_cols(s_w, mask_w).astype(BF16))

    gt = gt_ref[0, 0]
    for r in range(GQA):
        sl = slice(r * QB, (r + 1) * QB)
        o_t = (gt[3 * r:3 * r + 1, :] * o_c[:, sl] + gt[3 * r + 1:3 * r + 2, :] * o_s[:, sl]
               + gt[3 * r + 2:3 * r + 3, :] * o_w[:, sl])
        o_ref[0, r] = o_t.T


MASK_BIAS = -(2.0 ** 100)


def _augmented_keys(kvb, bsz, t_len):
    ns = t_len // L_SEL
    pos = jnp.arange(t_len, dtype=I32)[:, None]
    lane = jnp.arange(LANES, dtype=I32)[None, :]
    feat = jnp.where(lane == ns, pos >> 6, jnp.where(lane == ns + 1, pos & (L_SEL - 1), jnp.where(lane == ns + 2, 1, 0)))
    feat = feat.astype(F32)
    tails = (jnp.where(lane == (pos >> 6), MASK_BIAS, 0.0) + feat, feat)
    kv4 = kvb.reshape(bsz, t_len, 12, DH)

    def aug(col0, tail):
        t = jnp.broadcast_to(tail.astype(BF16)[None, :, None, :], (bsz, t_len, N_KV, LANES))
        return jnp.concatenate([kv4[:, :, col0:col0 + N_KV], t], axis=-1).reshape(bsz * t_len, N_KV * 2 * DH)

    return aug(4, tails[0]), aug(8, tails[1])


def _prompt_attention(slopes, q_blk, kvb, kce, kco, gates_g, bsz, t_len):
    nq = t_len // QB
    nb = t_len // L_CMP
    assert nb % (2 * LANES) == 0 or nb == LANES, "compressed blocks must fill whole lane tiles"
    assert t_len >= WINDOW + QB and t_len // L_SEL + 3 <= LANES
    half = nb // 2
    rows = GQA * QB
    ksel_aug, kwin_aug = _augmented_keys(kvb, bsz, t_len)
    kv_spec = lambda col: pl.BlockSpec((t_len, DH), lambda b, g, i, col=col: (b, col + g))
    ka_spec = pl.BlockSpec((t_len, 2 * DH), lambda b, g, i: (b, g))
    kc_spec = lambda col: pl.BlockSpec((half, DH), lambda b, g, i, col=col: (b, col + g))
    return pl.pallas_call(
        _pattn_body,
        grid=(bsz, N_KV, nq),
        in_specs=[pl.BlockSpec(memory_space=pltpu.SMEM),
                  pl.BlockSpec((1, GQA, QB, DH), lambda b, g, i: (b * nq + i, g, 0, 0)),
                  ka_spec, kv_spec(6), ka_spec, kv_spec(10),
                  kc_spec(0), kc_spec(0), kc_spec(2), kc_spec(2),
                  pl.BlockSpec((1, 1, 3 * GQA, QB), lambda b, g, i: (b, g, 0, i))],
        out_specs=pl.BlockSpec((1, GQA, QB, DH), lambda b, g, i: (b * nq + i, g, 0, 0)),
        out_shape=jax.ShapeDtypeStruct((bsz * nq, N_HEADS, QB, DH), F32),
        scratch_shapes=[pltpu.VMEM((1, rows), F32), pltpu.VMEM((1, rows), F32), pltpu.VMEM((DH, rows), F32)],
        compiler_params=_params(("parallel", "parallel", "arbitrary")),
        name="prompt_attention",
    )(slopes, q_blk, ksel_aug, kvb, kwin_aug, kvb, kce, kco, kce, kco, gates_g)


def _sample_select_body(slopes_ref, q_ref, kce_ref, kco_ref, oc_ref, sel_ref, *, past):
    rows = GQA * TP
    row = _iota((rows, 1), 0)
    qpos = past + (row & (TP - 1))
    r_of_row = row >> 3
    half = kce_ref.shape[1]
    nb = 2 * half
    n_b = q_ref.shape[0]
    lane = _iota((rows, nb), 1)
    blk_c = jnp.where(lane < half, 2 * lane, 2 * (lane - half) + 1)
    dist = qpos - (blk_c * L_CMP + (L_CMP - 1))
    pairs = []
    for b in range(n_b):
        for g in range(N_KV):
            slope = jnp.zeros((rows, 1), F32)
            for r in range(GQA):
                slope = jnp.where(r_of_row == r, slopes_ref[g * GQA + r], slope)
            q = q_ref[b, g]
            kc = jnp.concatenate([kce_ref[b, :, g * DH:(g + 1) * DH], kco_ref[b, :, g * DH:(g + 1) * DH]], axis=0)
            vc = jnp.concatenate([kce_ref[b, :, (2 + g) * DH:(3 + g) * DH], kco_ref[b, :, (2 + g) * DH:(3 + g) * DH]],
                                 axis=0)
            s = _dot_nt(q, kc) - slope * dist.astype(F32)
            p = _masked_softmax_rows(s, dist >= 0)
            oc_ref[b, g] = _dot(p.astype(BF16), vc)
            imp = p[0:TP]
            for r in range(1, GQA):
                imp = imp + p[r * TP:(r + 1) * TP]
            pairs.append(imp[:, :half] + imp[:, half:])
    pair = jnp.concatenate(pairs, axis=0)
    n_rows = pair.shape[0]
    blk = _iota((n_rows, half), 1)
    qp = past + (_iota((n_rows, half), 0) & (TP - 1))
    cur = qp >> 6
    forced = (blk == 0) | (blk == cur) | (blk == cur - 1)
    score = jnp.where(blk * L_SEL <= qp, jnp.where(forced, FORCE, pair), NEG)
    sel = jnp.zeros((n_rows, half), F32)
    for _ in range(N_SEL - 1):
        m = jnp.max(score, axis=1, keepdims=True)
        idx = jnp.min(jnp.where(score == m, blk, half), axis=1, keepdims=True)
        hit = blk == idx
        sel = jnp.where(hit, 1.0, sel)
        score = jnp.where(hit, -jnp.inf, score)
    sel = sel.astype(BF16)
    for b in range(n_b):
        for g in range(N_KV):
            i0 = (b * N_KV + g) * TP
            sel_ref[b, g] = jnp.concatenate([sel[i0:i0 + TP]] * GQA, axis=0)


SELECT_BATCH = 8


def _sample_select(slopes, q_s, kce, kco, past):
    bsz = q_s.shape[0]
    half = kce.shape[1]
    assert half == LANES, "past selection blocks must fill one lane tile"
    rows = GQA * TP
    nbt = SELECT_BATCH
    return pl.pallas_call(
        functools.partial(_sample_select_body, past=past),
        grid=(bsz // nbt,),
        in_specs=[pl.BlockSpec(memory_space=pltpu.SMEM),
                  pl.BlockSpec((nbt, N_KV, rows, DH), lambda b: (b, 0, 0, 0)),
                  pl.BlockSpec((nbt, half, 4 * DH), lambda b: (b, 0, 0)),
                  pl.BlockSpec((nbt, half, 4 * DH), lambda b: (b, 0, 0))],
        out_specs=[pl.BlockSpec((nbt, N_KV, rows, DH), lambda b: (b, 0, 0, 0)),
                   pl.BlockSpec((nbt, N_KV, rows, half), lambda b: (b, 0, 0, 0))],
        out_shape=[jax.ShapeDtypeStruct((bsz, N_KV, rows, DH), F32),
                   jax.ShapeDtypeStruct((bsz, N_KV, rows, half), BF16)],
        compiler_params=_params(("parallel",)),
        name="sample_select",
    )(slopes, q_s, kce, kco)


def _online_update(m_ref, l_ref, acc_ref, g, sc, msk, v):
    m_old = m_ref[g]
    m_new = jnp.maximum(m_old, jnp.max(sc, axis=-1, keepdims=True))
    alpha = jnp.exp(m_old - m_new)
    e = jnp.where(msk, jnp.exp(sc - m_new), 0.0)
    l_ref[g] = alpha * l_ref[g] + jnp.sum(e, axis=-1, keepdims=True)
    acc_ref[g] = alpha * acc_ref[g] + _dot(e.astype(BF16), v)
    m_ref[g] = m_new


def _sample_attn_body(pt_ref, slopes_ref, *refs, past, t_new):
    pages = refs[:PAGES_PER_STEP]
    q_ref, sel_ref, new_ref, win_ref, oc_ref, gt_ref, o_ref, m_scr, l_scr, acc_scr = refs[PAGES_PER_STEP:]
    c = pl.program_id(1)
    rows = GQA * TP
    row = _iota((rows, 1), 0)
    qpos = past + (row & (TP - 1))
    r_of_row = row >> 3
    ck = PAGES_PER_STEP * pages[0].shape[1] // KV_ROWS

    @pl.when(c == 0)
    def _():
        m_scr[...] = jnp.full(m_scr.shape, NEG, F32)
        l_scr[...] = jnp.zeros(l_scr.shape, F32)
        acc_scr[...] = jnp.zeros(acc_scr.shape, F32)

    dist_s = qpos - (c * ck + _iota((rows, ck), 1))
    expand = jnp.where((_iota((LANES, ck), 1) >> 6) + c * (ck // L_SEL) == _iota((LANES, ck), 0), 1.0, 0.0).astype(BF16)
    slopes = []
    for g in range(N_KV):
        slope = jnp.zeros((rows, 1), F32)
        for r in range(GQA):
            slope = jnp.where(r_of_row == r, slopes_ref[g * GQA + r], slope)
        slopes.append(slope)
        q = q_ref[0, g]
        kch = _page_rows(pages, g).astype(BF16)
        vch = _page_rows(pages, 2 + g).astype(BF16)
        picked = _dot(sel_ref[0, g], expand)
        msk = jnp.where(dist_s >= 0, picked, 0.0) > 0.5
        sc = jnp.where(msk, _dot_nt(q, kch) - slope * dist_s.astype(F32), NEG)
        _online_update(m_scr, l_scr, acc_scr, g, sc, msk, vch)

    @pl.when(c == pl.num_programs(1) - 1)
    def _():
        col = _iota((rows, TP), 1)
        dist_n = qpos - (past + col)
        mask_n = jnp.where(col < t_new, dist_n, -1) >= 0
        dist_c = qpos - (past - WINDOW + _iota((rows, WINDOW), 1))
        mask_c = jnp.where(dist_c >= 0, dist_c, WINDOW) < WINDOW
        mask_wn = jnp.where(mask_n, dist_n, WINDOW) < WINDOW
        for g in range(N_KV):
            slope = slopes[g]
            q = q_ref[0, g]
            kn = new_ref[0, :, (4 + g) * DH:(5 + g) * DH]
            vn = new_ref[0, :, (6 + g) * DH:(7 + g) * DH]
            sc = jnp.where(mask_n, _dot_nt(q, kn) - slope * dist_n.astype(F32), NEG)
            _online_update(m_scr, l_scr, acc_scr, g, sc, mask_n, vn)
            o_s = acc_scr[g] / jnp.maximum(l_scr[g], 1e-30)
            kwc = win_ref[0, pl.ds(g, WINDOW, KV_ROWS), :].astype(BF16)
            vwc = win_ref[0, pl.ds(2 + g, WINDOW, KV_ROWS), :].astype(BF16)
            kwn = new_ref[0, :, (8 + g) * DH:(9 + g) * DH]
            vwn = new_ref[0, :, (10 + g) * DH:(11 + g) * DH]
            s1 = jnp.where(mask_c, _dot_nt(q, kwc) - slope * dist_c.astype(F32), NEG)
            s2 = jnp.where(mask_wn, _dot_nt(q, kwn) - slope * dist_n.astype(F32), NEG)
            mx = jnp.maximum(jnp.max(s1, axis=-1, keepdims=True), jnp.max(s2, axis=-1, keepdims=True))
            e1 = jnp.where(mask_c, jnp.exp(s1 - mx), 0.0)
            e2 = jnp.where(mask_wn, jnp.exp(s2 - mx), 0.0)
            den = jnp.maximum(jnp.sum(e1, axis=-1, keepdims=True) + jnp.sum(e2, axis=-1, keepdims=True), 1e-30)
            o_w = _dot((e1 / den).astype(BF16), vwc) + _dot((e2 / den).astype(BF16), vwn)
            gt = gt_ref[0, g]
            o_ref[0, g] = gt[:, 0:1] * oc_ref[0, g] + gt[:, 1:2] * o_s + gt[:, 2:3] * o_w


def _sample_attention(slopes, page_ids, cache_sel3, q_s, sel_s, new_kvb, cache_win3, win_base, o_c, gates_s, past,
                      t_new):
    page_rows = cache_sel3.shape[1]
    bsz = q_s.shape[0]
    n_pages = page_ids.shape[0] // bsz
    n_steps = n_pages // PAGES_PER_STEP
    rows = GQA * TP
    assert cache_win3.shape[1] == WINDOW * KV_ROWS and past % L_SEL == 0 and t_new <= TP
    page_spec = lambda i: pl.BlockSpec((1, page_rows, DH),
                                       lambda b, s, pt: (pt[b * n_pages + s * PAGES_PER_STEP + i], 0, 0))
    per_b = lambda shape: pl.BlockSpec((1,) + shape, lambda b, s, pt: (b,) + (0,) * len(shape))
    grid_spec = pltpu.PrefetchScalarGridSpec(
        num_scalar_prefetch=1,
        grid=(bsz, n_steps),
        in_specs=[pl.BlockSpec(memory_space=pltpu.SMEM)] + [page_spec(i) for i in range(PAGES_PER_STEP)]
        + [per_b((N_KV, rows, DH)), per_b((N_KV, rows, LANES)), per_b((TP, 12 * DH)),
           pl.BlockSpec((1, WINDOW * KV_ROWS, DH), lambda b, s, pt: (win_base + b, 0, 0)),
           per_b((N_KV, rows, DH)), per_b((N_KV, rows, 3))],
        out_specs=per_b((N_KV, rows, DH)),
        scratch_shapes=[pltpu.VMEM((N_KV, rows, 1), F32), pltpu.VMEM((N_KV, rows, 1), F32),
                        pltpu.VMEM((N_KV, rows, DH), F32)],
    )
    return pl.pallas_call(
        functools.partial(_sample_attn_body, past=past, t_new=t_new),
        grid_spec=grid_spec,
        out_shape=jax.ShapeDtypeStruct((bsz, N_KV, rows, DH), F32),
        compiler_params=_params(("parallel", "arbitrary")),
        name="sample_attention",
    )(page_ids, slopes, *([cache_sel3] * PAGES_PER_STEP), q_s, sel_s, new_kvb, cache_win3, o_c, gates_s)


def _gelu_tanh(x):
    return 0.5 * x * (1.0 + jnp.tanh(0.7978845608028654 * (x + 0.044715 * (x * x * x))))


def _lru_coeffs(conv, wa_ref, ba_ref, wi_ref, bi_ref, sp_ref):
    cb = conv.astype(BF16)
    n_blk, blk = wa_ref.shape[0], wa_ref.shape[1]
    ra = jnp.concatenate([_dot(cb[:, n * blk:(n + 1) * blk], wa_ref[n]) for n in range(n_blk)], axis=1)
    ri = jnp.concatenate([_dot(cb[:, n * blk:(n + 1) * blk], wi_ref[n]) for n in range(n_blk)], axis=1)
    r = jax.nn.sigmoid(ra + ba_ref[...])
    gi = jax.nn.sigmoid(ri + bi_ref[...])
    log_a = -LRU_C * r * sp_ref[...]
    a = jnp.exp(log_a)
    b = jnp.sqrt(-jnp.tanh(log_a) * (a * a + 1.0)) * (gi * conv)
    return a, b


def _lru_seq_body(xr_ref, yr_ref, cs_ref, h0_ref, cw_ref, cb_ref, wa_ref, ba_ref, wi_ref, bi_ref, sp_ref, onw_ref,
                  rec_ref, hl_ref, xbuf, h_scr):
    k = pl.program_id(1)
    tt = xr_ref.shape[1]
    pad = 8

    @pl.when(k == 0)
    def _():
        xbuf[0:pad, :] = jnp.zeros((pad, xbuf.shape[1]), F32)
        xbuf[pad - (CONV_W - 1):pad, :] = cs_ref[0]
        h_scr[...] = h0_ref[0]

    x = xr_ref[0]
    xbuf[pad:pad + tt, :] = x
    conv = cb_ref[...] + cw_ref[CONV_W - 1:CONV_W, :] * x
    for j in range(CONV_W - 1):
        conv = conv + cw_ref[j:j + 1, :] * xbuf[pad - (CONV_W - 1) + j:pad - (CONV_W - 1) + j + tt, :]
    xbuf[0:pad, :] = x[tt - pad:tt]
    a, b = _lru_coeffs(conv, wa_ref, ba_ref, wi_ref, bi_ref, sp_ref)
    row = _iota((tt, 1), 0)
    s = 1
    while s < tt:
        keep = row >= s
        a_sh = jnp.where(keep, pltpu.roll(a, s, 0), 1.0)
        b_sh = jnp.where(keep, pltpu.roll(b, s, 0), 0.0)
        b = a * b_sh + b
        a = a * a_sh
        s *= 2
    hs = a * h_scr[...] + b
    h_scr[...] = hs[tt - 1:tt]
    hl_ref[0] = hs[tt - 1:tt]
    rec = hs * _gelu_tanh(yr_ref[0])
    rec_ref[0] = _rms(rec, onw_ref[...]).astype(BF16)


def _lru_seq(xr3, yr3, cs, h0, cw, cb, wa, ba, wi, bi, sp, onw, tt):
    bsz, t_len, d = xr3.shape
    seq = pl.BlockSpec((1, tt, d), lambda b, k: (b, k, 0))
    full = lambda a: pl.BlockSpec(a.shape, lambda b, k: (0,) * a.ndim)
    return pl.pallas_call(
        _lru_seq_body,
        grid=(bsz, t_len // tt),
        in_specs=[seq, seq, pl.BlockSpec((1, CONV_W - 1, d), lambda b, k: (b, 0, 0)),
                  pl.BlockSpec((1, 1, d), lambda b, k: (b, 0, 0)),
                  full(cw), full(cb), full(wa), full(ba), full(wi), full(bi), full(sp), full(onw)],
        out_specs=[seq, pl.BlockSpec((1, 1, d), lambda b, k: (b, 0, 0))],
        out_shape=[jax.ShapeDtypeStruct((bsz, t_len, d), BF16), jax.ShapeDtypeStruct((bsz, 1, d), F32)],
        scratch_shapes=[pltpu.VMEM((tt + 8, d), F32), pltpu.VMEM((1, d), F32)],
        compiler_params=_params(("parallel", "arbitrary")),
        name="lru_seq",
    )(xr3, yr3, cs, h0, cw, cb, wa, ba, wi, bi, sp, onw)


def _lru_step_body(xr_ref, yr_ref, cs_ref, h0_ref, cw_ref, cb_ref, wa_ref, ba_ref, wi_ref, bi_ref, sp_ref, onw_ref,
                   rec_ref, hl_ref):
    t_len, bsz = xr_ref.shape[0], xr_ref.shape[1]
    xs = [cs_ref[j] for j in range(CONV_W - 1)] + [xr_ref[t] for t in range(t_len)]
    convs = []
    for t in range(t_len):
        conv = cb_ref[...] + cw_ref[0:1, :] * xs[t]
        for j in range(1, CONV_W):
            conv = conv + cw_ref[j:j + 1, :] * xs[t + j]
        convs.append(conv)
    a, b = _lru_coeffs(jnp.concatenate(convs, axis=0), wa_ref, ba_ref, wi_ref, bi_ref, sp_ref)
    h = h0_ref[...]
    for t in range(t_len):
        h = a[t * bsz:(t + 1) * bsz] * h + b[t * bsz:(t + 1) * bsz]
        rec = h * _gelu_tanh(yr_ref[t])
        rec_ref[t] = _rms(rec, onw_ref[...]).astype(BF16)
    hl_ref[...] = h


def _lru_step(xr_t, yr_t, cs_t, h0, cw, cb, wa, ba, wi, bi, sp, onw):
    t_len, bsz, d = xr_t.shape
    return pl.pallas_call(
        _lru_step_body,
        out_shape=[jax.ShapeDtypeStruct((t_len, bsz, d), BF16), jax.ShapeDtypeStruct((bsz, d), F32)],
        compiler_params=pltpu.CompilerParams(vmem_limit_bytes=VMEM_LIMIT),
        name="lru_step",
    )(xr_t, yr_t, cs_t, h0, cw, cb, wa, ba, wi, bi, sp, onw)


def _outproj_body(attn_ref, rec_ref, x_ref, cnt0_ref, anw_ref, woa_ref, wor_ref, fnw_ref, wr_ref, br_ref,
                  x1_ref, h2_ref, ti_ref, tg_ref, tp_ref, cnt_ref, carry):
    step = pl.program_id(0)
    tm = x_ref.shape[0]

    @pl.when(step == 0)
    def _():
        carry[...] = cnt0_ref[...]

    attn = jnp.concatenate(
        [jnp.concatenate([attn_ref[b, hd] for hd in range(N_HEADS)], axis=1) for b in range(attn_ref.shape[0])], axis=0)
    an = _rms(attn, anw_ref[...]).astype(BF16)
    x1 = x_ref[...] + (_dot(an, woa_ref[...]) + _dot(rec_ref[...], wor_ref[...]))
    x1_ref[...] = x1
    h2 = _rms(x1, fnw_ref[...])
    dc = h2.shape[1] // LANES
    for c in range(dc):
        h2_ref[pl.ds(c, tm, dc), :] = h2[:, c * LANES:(c + 1) * LANES]
    lane = _iota((tm, LANES), 1)
    logits = jnp.dot(h2, wr_ref[...], precision=lax.Precision.HIGHEST, preferred_element_type=F32) + br_ref[...]
    lg = jnp.where(lane < N_EXPERTS, logits, -jnp.inf)
    vals, idxs = [], []
    for _ in range(TOP_K):
        m = jnp.max(lg, axis=-1, keepdims=True)
        ix = jnp.min(jnp.where(lg == m, lane, LANES), axis=-1, keepdims=True)
        vals.append(m)
        idxs.append(ix)
        lg = jnp.where(lane == ix, -jnp.inf, lg)
    es = [jnp.exp(v - vals[0]) for v in vals]
    den = es[0]
    for e in es[1:]:
        den = den + e
    onehot = jnp.zeros((tm, LANES), F32)
    for ix in idxs:
        onehot = jnp.where(lane == ix, 1.0, onehot)
    lower = jnp.where(_iota((tm, tm), 0) > _iota((tm, tm), 1), 1.0, 0.0).astype(BF16)
    rank = carry[...] + _dot(lower, onehot.astype(BF16))
    carry[...] = carry[...] + jnp.sum(onehot, axis=0, keepdims=True)
    ti = jnp.zeros((tm, LANES), I32)
    tg = jnp.zeros((tm, LANES), F32)
    tp = jnp.zeros((tm, LANES), I32)
    for k in range(TOP_K):
        pos = jnp.sum(jnp.where(lane == idxs[k], rank, 0.0), axis=-1, keepdims=True).astype(I32)
        ti = jnp.where(lane == k, idxs[k], ti)
        tg = jnp.where(lane == k, es[k] / den, tg)
        tp = jnp.where(lane == k, pos, tp)
    ti_ref[...] = ti
    tg_ref[...] = tg
    tp_ref[...] = tp
    cnt_ref[...] = carry[...]


def _outproj_router(attn_blk, recn, x2, cnt0, anw, woa, wor, fnw, wr, br, tm):
    n, d = x2.shape
    nqb = tm // QB
    row = lambda i: (i, 0)
    wspec = lambda a: pl.BlockSpec(a.shape, lambda i: (0,) * a.ndim, pipeline_mode=pl.Buffered(1))
    lanes_out = lambda dt: jax.ShapeDtypeStruct((n, LANES), dt)
    return pl.pallas_call(
        _outproj_body,
        grid=(n // tm,),
        in_specs=[pl.BlockSpec((nqb, N_HEADS, QB, DH), lambda i: (i, 0, 0, 0)),
                  pl.BlockSpec((tm, recn.shape[1]), row), pl.BlockSpec((tm, d), row),
                  wspec(cnt0), wspec(anw), wspec(woa), wspec(wor), wspec(fnw), wspec(wr), wspec(br)],
        out_specs=[pl.BlockSpec((tm, d), row), pl.BlockSpec((tm * (d // LANES), LANES), row),
                   pl.BlockSpec((tm, LANES), row),
                   pl.BlockSpec((tm, LANES), row), pl.BlockSpec((tm, LANES), row), pl.BlockSpec((1, LANES), lambda i: (0, 0))],
        out_shape=[jax.ShapeDtypeStruct((n, d), F32), jax.ShapeDtypeStruct((n * (d // LANES), LANES), F32),
                   lanes_out(I32), lanes_out(F32), lanes_out(I32), jax.ShapeDtypeStruct((1, LANES), F32)],
        scratch_shapes=[pltpu.VMEM((1, LANES), F32)],
        compiler_params=_params(("arbitrary",)),
        name="outproj_router",
    )(attn_blk, recn, x2, cnt0, anw, woa, wor, fnw, wr, br)


ISSUE_UNROLL = 8


def _gather_body(rowtok_ref, nused_ref, h2_hbm, out_ref, buf, sem):
    j = pl.program_id(0)
    tm = out_ref.shape[0]
    dc = buf.shape[1] // tm
    n_used = nused_ref[0]

    def issue_block(blk, slot):
        def body(it, c):
            for u in range(ISSUE_UNROLL):
                r = it * ISSUE_UNROLL + u
                src = pl.multiple_of(rowtok_ref[blk * tm + r] * dc, dc)
                pltpu.make_async_copy(h2_hbm.at[pl.ds(src, dc), :], buf.at[slot, pl.ds(r * dc, dc), :],
                                      sem.at[slot]).start(priority=u % 2)
            return c

        lax.fori_loop(0, tm // ISSUE_UNROLL, body, 0)

    @pl.when((j == 0) & (n_used > 0))
    def _():
        issue_block(0, 0)

    @pl.when(j + 1 < n_used)
    def _():
        issue_block(j + 1, (j + 1) % 2)

    @pl.when(j < n_used)
    def _():
        slot = j % 2
        pltpu.make_async_copy(h2_hbm.at[pl.ds(0, tm * dc), :], buf.at[slot], sem.at[slot]).wait()
        for c in range(dc):
            out_ref[:, c * LANES:(c + 1) * LANES] = buf[slot, pl.ds(c, tm, dc), :].astype(BF16)

    @pl.when(j >= n_used)
    def _():
        out_ref[...] = jnp.zeros(out_ref.shape, BF16)


def _gather_rows(row_tok, n_used, h2, n_blocks, d):
    tm = MOE_TM
    dc = d // LANES
    grid_spec = pltpu.PrefetchScalarGridSpec(
        num_scalar_prefetch=2,
        grid=(n_blocks,),
        in_specs=[pl.BlockSpec(memory_space=pl.ANY)],
        out_specs=pl.BlockSpec((tm, d), lambda j, rt, nu: (j, 0)),
        scratch_shapes=[pltpu.VMEM((2, tm * dc, LANES), F32), pltpu.SemaphoreType.DMA((2,))],
    )
    return pl.pallas_call(
        _gather_body,
        grid_spec=grid_spec,
        out_shape=jax.ShapeDtypeStruct((n_blocks * tm, d), BF16),
        compiler_params=_params(("arbitrary",)),
        name="moe_gather",
    )(row_tok, n_used, h2)


COPY_SPLIT = 4


def _expert_blocks(first_row, n_blk, n_used, n_blocks, x_hbm, out_rows, xbuf, obuf, xsem, osem, prepare, compute):
    tm = xbuf.shape[1]
    rc = tm // COPY_SPLIT
    orc = obuf.shape[1] // COPY_SPLIT

    def x_copies(b, slot):
        r = pl.multiple_of(first_row + b * tm, tm)
        return [pltpu.make_async_copy(x_hbm.at[pl.ds(r + s * rc, rc), :], xbuf.at[slot, pl.ds(s * rc, rc), :],
                                      xsem.at[slot]) for s in range(COPY_SPLIT)]

    def o_copies(r, slot):
        r = pl.multiple_of(r, tm)
        return [pltpu.make_async_copy(obuf.at[slot, pl.ds(s * orc, orc), :], out_rows(r + s * rc, rc), osem.at[slot])
                for s in range(COPY_SPLIT)]

    def start(copies):
        for cp in copies:
            cp.start(priority=1)

    def wait(copies):
        for cp in copies:
            cp.wait()

    @pl.when(n_blk > 0)
    def _():
        start(x_copies(0, 0))
        prepare()

        def body(b, c):
            slot = b % 2

            @pl.when(b + 1 < n_blk)
            def _():
                start(x_copies(b + 1, 1 - slot))

            wait(x_copies(b, slot))

            @pl.when(b >= 2)
            def _():
                wait(o_copies(first_row + (b - 2) * tm, slot))

            compute(xbuf[slot], slot)
            start(o_copies(first_row + b * tm, slot))
            return c

        lax.fori_loop(0, n_blk, body, 0)

        @pl.when(n_blk >= 2)
        def _():
            wait(o_copies(first_row + (n_blk - 2) * tm, n_blk % 2))

        wait(o_copies(first_row + (n_blk - 1) * tm, (n_blk - 1) % 2))

    @pl.when(pl.program_id(0) == pl.num_programs(0) - 1)
    def _():
        obuf[0] = jnp.zeros(obuf.shape[1:], obuf.dtype)

        def fill(t, c):
            start(o_copies(t * tm, 0))
            wait(o_copies(t * tm, 0))
            return c

        lax.fori_loop(n_used, n_blocks, fill, 0)


W_SPLIT = 4


def _moe_up_body(row0_ref, nblk_ref, nu_ref, xs_hbm, *refs, n_blocks):
    w_refs = refs[:W_SPLIT]
    b_ref, perm_ref, h_hbm, wbf, xbuf, obuf, xsem, osem = refs[W_SPLIT:]
    e = pl.program_id(0)
    tn = wbf.shape[1]
    pw = perm_ref.shape[0]
    hw = pw // 2
    kc = w_refs[0].shape[1]

    def prepare():
        for q, w_ref in enumerate(w_refs):
            for c in range(tn // pw):
                w = w_ref[0, :, c * pw:(c + 1) * pw].astype(BF16)
                wbf[q * kc:(q + 1) * kc, c * pw:(c + 1) * pw] = _dot(w, perm_ref[...]).astype(BF16)

    def compute(x, slot):
        gu = _dot(x, wbf[...]) + b_ref[0]
        for c in range(tn // pw):
            glu = jnp.minimum(gu[:, c * pw:c * pw + hw], SWIGLU_LIMIT)
            lin = jnp.clip(gu[:, c * pw + hw:(c + 1) * pw], -SWIGLU_LIMIT, SWIGLU_LIMIT)
            obuf[slot, :, c * hw:(c + 1) * hw] = (glu * jax.nn.sigmoid(SWIGLU_ALPHA * glu) * (lin + 1.0)).astype(BF16)

    col0 = pl.multiple_of(pl.program_id(1) * (tn // 2), tn // 2)
    out_rows = lambda r, n: h_hbm.at[pl.ds(r, n), pl.ds(col0, tn // 2)]
    _expert_blocks(row0_ref[e], nblk_ref[e], nu_ref[0], n_blocks, xs_hbm, out_rows, xbuf, obuf, xsem, osem,
                   prepare, compute)


def _w_specs(k, tn):
    return [pl.BlockSpec((1, k // W_SPLIT, tn), lambda e, n, *_, q=q: (e, q, n)) for q in range(W_SPLIT)]


def _moe_up(row0, nblk, n_used, xs, w_gu, b_gu, perm):
    n_rows, d = xs.shape
    n_e, _, f2 = w_gu.shape
    tm, tn = MOE_TM, MOE_TN
    grid_spec = pltpu.PrefetchScalarGridSpec(
        num_scalar_prefetch=3,
        grid=(n_e, f2 // tn),
        in_specs=[pl.BlockSpec(memory_space=pl.ANY)] + _w_specs(d, tn)
        + [pl.BlockSpec((1, 1, tn), lambda e, n, *_: (e, 0, n)),
           pl.BlockSpec(perm.shape, lambda e, n, *_: (0, 0))],
        out_specs=pl.BlockSpec(memory_space=pl.ANY),
        scratch_shapes=[pltpu.VMEM((d, tn), BF16), pltpu.VMEM((2, tm, d), BF16), pltpu.VMEM((2, tm, tn // 2), BF16),
                        pltpu.SemaphoreType.DMA((2,)), pltpu.SemaphoreType.DMA((2,))],
    )
    return pl.pallas_call(
        functools.partial(_moe_up_body, n_blocks=n_rows // tm),
        grid_spec=grid_spec,
        out_shape=jax.ShapeDtypeStruct((n_rows, f2 // 2), BF16),
        compiler_params=_params(("arbitrary", "arbitrary")),
        name="moe_up",
    )(row0, nblk, n_used, xs, *([w_gu] * W_SPLIT), b_gu, perm)


def _moe_down_body(row0_ref, nblk_ref, nu_ref, h_hbm, *refs, n_blocks):
    w_refs = refs[:W_SPLIT]
    b_ref, y_hbm, wbf, xbuf, obuf, xsem, osem = refs[W_SPLIT:]
    e = pl.program_id(0)
    tn = wbf.shape[1]
    kc = w_refs[0].shape[1]

    def prepare():
        for q, w_ref in enumerate(w_refs):
            wbf[q * kc:(q + 1) * kc, :] = w_ref[0].astype(BF16)

    dc = tn // LANES
    tm = xbuf.shape[1]

    def compute(h, slot):
        y = _dot(h, wbf[...]) + b_ref[0]
        for c in range(dc):
            obuf[slot, pl.ds(c, tm, dc), :] = y[:, c * LANES:(c + 1) * LANES]

    out_rows = lambda r, n: y_hbm.at[pl.ds(pl.multiple_of(r * dc, dc), n * dc), :]
    _expert_blocks(row0_ref[e], nblk_ref[e], nu_ref[0], n_blocks, h_hbm, out_rows, xbuf, obuf, xsem, osem,
                   prepare, compute)


def _moe_down(row0, nblk, n_used, h, w_d, b_d):
    n_rows, f = h.shape
    n_e, _, d = w_d.shape
    tm, tn = MOE_TM, MOE_TN
    assert tn == d, "a grid step writes whole output rows"
    grid_spec = pltpu.PrefetchScalarGridSpec(
        num_scalar_prefetch=3,
        grid=(n_e, d // tn),
        in_specs=[pl.BlockSpec(memory_space=pl.ANY)] + _w_specs(f, tn)
        + [pl.BlockSpec((1, 1, tn), lambda e, n, *_: (e, 0, n))],
        out_specs=pl.BlockSpec(memory_space=pl.ANY),
        scratch_shapes=[pltpu.VMEM((f, tn), BF16), pltpu.VMEM((2, tm, f), BF16),
                        pltpu.VMEM((2, tm * (d // LANES), LANES), F32),
                        pltpu.SemaphoreType.DMA((2,)), pltpu.SemaphoreType.DMA((2,))],
    )
    return pl.pallas_call(
        functools.partial(_moe_down_body, n_blocks=n_rows // tm),
        grid_spec=grid_spec,
        out_shape=jax.ShapeDtypeStruct((n_rows * (d // LANES), LANES), F32),
        compiler_params=_params(("arbitrary", "arbitrary")),
        name="moe_down",
    )(row0, nblk, n_used, h, *([w_d] * W_SPLIT), b_d)


def _combine_body(dest_hbm, gate_ref, x1_ref, y_hbm, outp_ref, outs_ref, dsm, buf, sem_d, sem, *, np_blocks):
    j = pl.program_id(0)
    n = pl.num_programs(0)
    tk = x1_ref.shape[0]
    dc = x1_ref.shape[1] // LANES
    toks = ISSUE_UNROLL // TOP_K

    def idx_copy(step, slot):
        return pltpu.make_async_copy(dest_hbm.at[step, 0], dsm.at[slot], sem_d.at[slot])

    def issue_rows(slot):
        def body(it, c):
            for u in range(toks):
                t = it * toks + u
                for k in range(TOP_K):
                    src = pl.multiple_of(dsm[slot, t * TOP_K + k] * dc, dc)
                    pltpu.make_async_copy(y_hbm.at[pl.ds(src, dc), :], buf.at[slot, k, pl.ds(t * dc, dc), :],
                                          sem.at[slot]).start(priority=k % 2)
            return c

        lax.fori_loop(0, tk // toks, body, 0)

    @pl.when(j == 0)
    def _():
        idx_copy(0, 0).start()
        idx_copy(0, 0).wait()
        issue_rows(0)

        @pl.when(1 < n)
        def _():
            idx_copy(1, 1).start()

    @pl.when(j + 1 < n)
    def _():
        slot = (j + 1) % 2
        idx_copy(j + 1, slot).wait()
        issue_rows(slot)

    @pl.when(j + 2 < n)
    def _():
        idx_copy(j + 2, j % 2).start()

    slot = j % 2
    for k in range(TOP_K):
        pltpu.make_async_copy(y_hbm.at[pl.ds(0, tk * dc), :], buf.at[slot, k], sem.at[slot]).wait()
    g = gate_ref[...]
    gk = [jnp.broadcast_to(g[:, k:k + 1], (tk, LANES)) for k in range(TOP_K)]

    def result(c):
        acc = gk[0] * buf[slot, 0, pl.ds(c, tk, dc), :]
        for k in range(1, TOP_K):
            acc = acc + gk[k] * buf[slot, k, pl.ds(c, tk, dc), :]
        return x1_ref[:, c * LANES:(c + 1) * LANES] + acc

    @pl.when(j < np_blocks)
    def _():
        for c in range(dc):
            outp_ref[:, c * LANES:(c + 1) * LANES] = result(c)

    @pl.when(j >= np_blocks)
    def _():
        for c in range(dc):
            outs_ref[:, c * LANES:(c + 1) * LANES] = result(c)


def _combine(dest3, gates, x1, y, n_prompt):
    n, d = x1.shape
    tk = dest3.shape[2] // TOP_K
    np_blocks = n_prompt // tk
    row = lambda j: (j, 0)
    return pl.pallas_call(
        functools.partial(_combine_body, np_blocks=np_blocks),
        grid=(n // tk,),
        in_specs=[pl.BlockSpec(memory_space=pl.ANY), pl.BlockSpec((tk, LANES), row), pl.BlockSpec((tk, d), row),
                  pl.BlockSpec(memory_space=pl.ANY)],
        out_specs=[pl.BlockSpec((tk, d), lambda j: (jnp.minimum(j, np_blocks - 1), 0)),
                   pl.BlockSpec((tk, d), lambda j: (jnp.maximum(j - np_blocks, 0), 0))],
        out_shape=[jax.ShapeDtypeStruct((n_prompt, d), F32), jax.ShapeDtypeStruct((n - n_prompt, d), F32)],
        scratch_shapes=[pltpu.SMEM((2, tk * TOP_K), I32), pltpu.VMEM((2, TOP_K, tk * (d // LANES), LANES), F32),
                        pltpu.SemaphoreType.DMA((2,)), pltpu.SemaphoreType.DMA((2,))],
        compiler_params=_params(("arbitrary",)),
        name="moe_combine",
    )(dest3, gates, x1, y)


def _moe(h2, x1, topi, topg, topp, counts, w_gu, b_gu, w_d, b_d, n_prompt):
    n, d = x1.shape
    tm = MOE_TM
    n_asg = n * TOP_K
    n_blocks = n_asg // tm + N_EXPERTS
    padded = (counts + tm - 1) // tm * tm
    pad_end = jnp.cumsum(padded)
    pad_start = pad_end - padded
    dest = pad_start[topi[:, :TOP_K]] + topp[:, :TOP_K]
    tok = jnp.broadcast_to(jnp.arange(n, dtype=I32)[:, None], (n, TOP_K))
    row_tok = jnp.zeros((n_blocks * tm,), I32).at[dest.reshape(-1)].set(tok.reshape(-1))
    n_used = (pad_end[-1:] // tm).astype(I32)
    row0 = pad_start.astype(I32)
    nblk = (padded // tm).astype(I32)
    f2 = w_gu.shape[2]
    pw = 2 * LANES
    src = jnp.arange(pw)
    perm = (jnp.arange(pw)[:, None] == jnp.where(src < LANES, 2 * src, 2 * (src - LANES) + 1)[None, :]).astype(BF16)
    b_gu_p = b_gu.reshape(N_EXPERTS, f2 // pw, LANES, 2).transpose(0, 1, 3, 2).reshape(N_EXPERTS, 1, f2)
    xs = _gather_rows(row_tok, n_used, h2, n_blocks, d)
    h = _moe_up(row0, nblk, n_used, xs, w_gu, b_gu_p, perm)
    y = _moe_down(row0, nblk, n_used, h, w_d, b_d.reshape(N_EXPERTS, 1, d))
    tk = 128
    dest3 = dest.astype(I32).reshape(n // tk, 1, tk * TOP_K)
    return _combine(dest3, topg, x1, y, n_prompt)


def _layer(layer, xp, xs, cache_cmp_all, cache_sel_all, cache_win_all, state_conv, state_h, page_table, w):
    (norm_mix_w, w_in, q_norm_w, k_norm_w, w_cmp_k, w_cmp_v, conv_w, conv_b, w_gate_a, b_gate_a, w_gate_i, b_gate_i,
     lru_lambda, out_norm_attn, out_norm_rec, w_out, norm_ffn_w, w_router, b_router, w_gate_up, b_gate_up, w_down,
     b_down) = w
    bp, tp, d = xp.shape
    bs, ts, _ = xs.shape
    d_rec = d - D_ATTN
    past = page_table.shape[1] * cache_cmp_all.shape[2]
    assert bs * ts == QB and ts <= TP

    o1, o2 = D_ATTN, D_ATTN + 12 * DH
    o3 = o2 + 3 * N_HEADS
    wq = w_in[:, :o1].astype(BF16)
    wkv = w_in[:, o1:o2].astype(BF16)
    wg = jnp.pad(w_in[:, o2:o3], ((0, 0), (0, LANES - 3 * N_HEADS))).astype(BF16)
    wxy = w_in[:, o3:].astype(BF16)
    row2 = lambda v: v.reshape(1, -1)
    slopes = jnp.exp2(-8.0 * jnp.arange(1, N_HEADS + 1, dtype=F32) / N_HEADS)
    w4 = jnp.concatenate([w_cmp_k, w_cmp_v], axis=0)
    z4 = jnp.zeros_like(w4)
    wl = jnp.stack([jnp.concatenate([w4, z4], axis=1), jnp.concatenate([z4, w4], axis=1)])
    wl = jnp.broadcast_to(wl[..., None], wl.shape + (DH,))
    sp = row2(jax.nn.softplus(-lru_lambda.astype(F32)))
    wa, wi = w_gate_a.astype(BF16), w_gate_i.astype(BF16)
    woa, wor = w_out[:D_ATTN].astype(BF16), w_out[D_ATTN:].astype(BF16)
    wr = jnp.pad(w_router, ((0, 0), (0, LANES - N_EXPERTS)))
    br = row2(jnp.pad(b_router, (0, LANES - N_EXPERTS)))
    mix = (row2(norm_mix_w), wq, wkv, wg, wxy, row2(q_norm_w), k_norm_w)
    lru_w = (conv_w, row2(conv_b), wa, row2(b_gate_a), wi, row2(b_gate_i), sp, row2(out_norm_rec))

    np_tok = bp * tp
    q_p, cmp_p, sel_p, win_p, kvb_p, gate_p, xr_p, yr_p = _inproj(xp.reshape(np_tok, d), *mix, tm=256)
    kce_p, kco_p = _pool_prompt(cmp_p, wl)
    gates_g = gate_p[:, :3 * N_HEADS].reshape(bp, tp, N_KV, 3 * GQA).transpose(0, 2, 3, 1)
    attn_p = _prompt_attention(slopes, q_p, kvb_p, kce_p, kco_p, gates_g, bp, tp)
    recn_p, hl_p = _lru_seq(xr_p.reshape(bp, tp, d_rec), yr_p.reshape(bp, tp, d_rec),
                            jnp.zeros((bp, CONV_W - 1, d_rec), F32), jnp.zeros((bp, 1, d_rec), F32), *lru_w, tt=256)
    post = (row2(out_norm_attn), woa, wor, row2(norm_ffn_w), wr, br)
    x1_p, h2_p, ti_p, tg_p, tp_p, cnt_p = _outproj_router(
        attn_p, recn_p.reshape(np_tok, d_rec), xp.reshape(np_tok, d), jnp.zeros((1, LANES), F32), *post, tm=256)

    ns_tok = bs * ts
    q_s, cmp_s, sel_s, win_s, kvb_s, gate_s, xr_s, yr_s = _inproj(xs.reshape(ns_tok, d), *mix, tm=QB)
    n_pool, page = cache_cmp_all.shape[1], cache_cmp_all.shape[2]
    page_view = lambda c: c.reshape(c.shape[0] * n_pool, page * KV_ROWS, DH)
    page_ids = (page_table + layer * n_pool).reshape(-1)
    kce_s, kco_s = _pool_pages(page_view(cache_cmp_all), page_ids, bs, wl)
    q_t = q_s[0].reshape(N_KV, GQA, bs, ts, DH).transpose(2, 0, 1, 3, 4)
    q_t = jnp.pad(q_t, ((0, 0), (0, 0), (0, 0), (0, TP - ts), (0, 0))).reshape(bs, N_KV, GQA * TP, DH)
    oc_s, selm_s = _sample_select(slopes, q_t, kce_s, kco_s, past)
    new_kvb = jnp.pad(kvb_s.reshape(bs, ts, 12 * DH), ((0, 0), (0, TP - ts), (0, 0)))
    g_t = gate_s[:, :3 * N_HEADS].reshape(bs, ts, N_KV, GQA, 3).transpose(0, 2, 3, 1, 4)
    g_t = jnp.pad(g_t, ((0, 0), (0, 0), (0, 0), (0, TP - ts), (0, 0))).reshape(bs, N_KV, GQA * TP, 3)
    win_view = cache_win_all.reshape(cache_win_all.shape[0] * bs, cache_win_all.shape[2] * KV_ROWS, DH)
    attn_s = _sample_attention(slopes, page_ids, page_view(cache_sel_all), q_t, selm_s, new_kvb, win_view, layer * bs,
                               oc_s, g_t, past, ts)
    attn_s = attn_s.reshape(bs, N_KV, GQA, TP, DH)[:, :, :, :ts].transpose(1, 2, 0, 3, 4).reshape(1, N_HEADS, QB, DH)
    tmaj = lambda a: a.reshape(bs, ts, d_rec).transpose(1, 0, 2)
    recn_s, hl_s = _lru_step(tmaj(xr_s), tmaj(yr_s), state_conv.transpose(1, 0, 2), state_h, *lru_w)
    x1_s, h2_s, ti_s, tg_s, tp_s, cnt = _outproj_router(
        attn_s, recn_s.transpose(1, 0, 2).reshape(ns_tok, d_rec), xs.reshape(ns_tok, d), cnt_p, *post, tm=QB)

    cat = lambda a, b: jnp.concatenate([a, b], axis=0)
    out_p, out_s = _moe(cat(h2_p, h2_s), cat(x1_p, x1_s), cat(ti_p, ti_s), cat(tg_p, tg_s), cat(tp_p, tp_s),
                        cnt[0, :N_EXPERTS].astype(I32), w_gate_up, b_gate_up, w_down, b_down, np_tok)
    y_p = out_p.reshape(bp, tp, d)
    y_s = out_s.reshape(bs, ts, d)

    kv5 = lambda a, b, t: a.reshape(b, t, 2, N_KV, DH)
    win_len_p = min(WINDOW, tp)
    st_p = (kv5(cmp_p, bp, tp), kv5(sel_p, bp, tp), kv5(win_p, bp, tp)[:, tp - win_len_p:],
            xr_p.reshape(bp, tp, d_rec)[:, tp - (CONV_W - 1):], hl_p.reshape(bp, d_rec))
    cache_win = cache_win_all[layer]
    win_all = jnp.concatenate([cache_win, kv5(win_s, bs, ts)], axis=1)
    xcat = jnp.concatenate([state_conv, xr_s.reshape(bs, ts, d_rec)], axis=1)
    st_s = (kv5(cmp_s, bs, ts), kv5(sel_s, bs, ts), win_all[:, win_all.shape[1] - cache_win.shape[1]:],
            xcat[:, ts:], hl_s)
    return y_p, y_s, st_p, st_s


def kernel(x_prompt, x_sample, cache_cmp_kv, cache_sel_kv, cache_win_kv, state_conv, state_h, page_table, norm_mix_w, w_in, q_norm_w, k_norm_w, w_cmp_k, w_cmp_v, conv_w, conv_b, w_gate_a, b_gate_a, w_gate_i, b_gate_i, lru_lambda, out_norm_attn, out_norm_rec, w_out, norm_ffn_w, w_router, b_router, w_gate_up, b_gate_up, w_down, b_down):
    depth = w_in.shape[0]
    xp, xs = x_prompt, x_sample
    st_ps, st_ss = [], []
    for l in range(depth):
        w = (norm_mix_w[l], w_in[l], q_norm_w[l], k_norm_w[l], w_cmp_k[l], w_cmp_v[l], conv_w[l], conv_b[l],
             w_gate_a[l], b_gate_a[l], w_gate_i[l], b_gate_i[l], lru_lambda[l], out_norm_attn[l], out_norm_rec[l],
             w_out[l], norm_ffn_w[l], w_router[l], b_router[l], w_gate_up[l], b_gate_up[l], w_down[l], b_down[l])
        xp, xs, st_p, st_s = _layer(l, xp, xs, cache_cmp_kv, cache_sel_kv, cache_win_kv, state_conv[l], state_h[l],
                                    page_table, w)
        st_ps.append(st_p)
        st_ss.append(st_s)
    stack = lambda sts, i: jnp.stack([s[i] for s in sts])
    return (xp, xs) + tuple(stack(st_ps, i) for i in range(5)) + tuple(stack(st_ss, i) for i in range(5))
```

```python
import functools

import jax
import jax.numpy as jnp
from jax import lax
from jax.experimental import pallas as pl
from jax.experimental.pallas import tpu as pltpu

F32 = jnp.float32
BF16 = jnp.bfloat16
I32 = jnp.int32

N_HEADS = 8
N_KV = 2
GQA = N_HEADS // N_KV
DH = 128
D_ATTN = N_HEADS * DH
CONV_W = 4
LRU_C = 8.0
L_CMP = 32
L_SEL = 64
N_SEL = 16
WINDOW = 512
QB = 128
N_EXPERTS = 32
TOP_K = 4
SWIGLU_LIMIT = 7.0
SWIGLU_ALPHA = 1.702
EPS = 1e-6
NEG = -1e30
FORCE = 1e9
Q_SCALE = DH ** -0.5
LANES = 128
TP = 8
KV_ROWS = 2 * N_KV
GATE_ROWS = 16
MOE_TM = 256
MOE_TN = 2048
PAGES_PER_STEP = 32
VMEM_LIMIT = 56 * 1024 * 1024


def _dot(a, b):
    return jnp.dot(a, b, preferred_element_type=F32)


def _dot_nt(a, b):
    return lax.dot_general(a, b, (((1,), (1,)), ((), ())), preferred_element_type=F32)


def _dot_tn(a, b):
    return lax.dot_general(a, b, (((0,), (0,)), ((), ())), preferred_element_type=F32)


def _iota(shape, dim):
    return lax.broadcasted_iota(I32, shape, dim)


def _rms(x, w):
    return x * lax.rsqrt(jnp.mean(x * x, axis=-1, keepdims=True) + EPS) * w


def _masked_softmax_rows(s, mask):
    s = jnp.where(mask, s, NEG)
    e = jnp.where(mask, jnp.exp(s - jnp.max(s, axis=-1, keepdims=True)), 0.0)
    return e / jnp.maximum(jnp.sum(e, axis=-1, keepdims=True), 1e-30)


def _masked_softmax_cols(s, mask):
    s = jnp.where(mask, s, NEG)
    e = jnp.where(mask, jnp.exp(s - jnp.max(s, axis=0, keepdims=True)), 0.0)
    return e / jnp.maximum(jnp.sum(e, axis=0, keepdims=True), 1e-30)


def _params(sem, vmem=None):
    return pltpu.CompilerParams(dimension_semantics=sem, vmem_limit_bytes=vmem or VMEM_LIMIT)


def _inproj_body(x_ref, nw_ref, wq_ref, wkv_ref, wg_ref, wxy_ref, qnw_ref, knw_ref,
                 q_ref, cmp_ref, sel_ref, win_ref, kvb_ref, gate_ref, xr_ref, yr_ref):
    x = x_ref[...]
    h = _rms(x, nw_ref[...]).astype(BF16)
    q = _dot(h, wq_ref[...])
    qnw = qnw_ref[...]
    nqb = q_ref.shape[0]
    for hd in range(N_HEADS):
        qn = (_rms(q[:, hd * DH:(hd + 1) * DH], qnw) * Q_SCALE).astype(BF16)
        for b in range(nqb):
            q_ref[b, hd] = qn[b * QB:(b + 1) * QB]
    kv = _dot(h, wkv_ref[...])
    tm = x.shape[0]
    outs = (cmp_ref, sel_ref, win_ref)
    for br in range(3):
        knw = knw_ref[br:br + 1, :]
        for g in range(N_KV):
            c0 = br * 4 * DH + g * DH
            kn = _rms(kv[:, c0:c0 + DH], knw)
            v = kv[:, c0 + 2 * DH:c0 + 3 * DH]
            outs[br][pl.ds(g, tm, KV_ROWS), :] = kn
            outs[br][pl.ds(2 + g, tm, KV_ROWS), :] = v
            kvb_ref[:, c0:c0 + DH] = kn.astype(BF16)
            kvb_ref[:, c0 + 2 * DH:c0 + 3 * DH] = v.astype(BF16)
    gate_ref[...] = jax.nn.sigmoid(_dot(h, wg_ref[...])).T
    xy = _dot(h, wxy_ref[...])
    d_rec = xr_ref.shape[1]
    xr_ref[...] = xy[:, :d_rec]
    yr_ref[...] = xy[:, d_rec:]


def _inproj(x2, nw, wq, wkv, wg, wxy, qnw, knw, tm):
    n, d = x2.shape
    d_rec = wxy.shape[1] // 2
    nqb = tm // QB
    row = lambda i: (i, 0)
    const = lambda i: (0, 0)
    wspec = lambda a: pl.BlockSpec(a.shape, const, pipeline_mode=pl.Buffered(1))
    return pl.pallas_call(
        _inproj_body,
        grid=(n // tm,),
        in_specs=[pl.BlockSpec((tm, d), row), wspec(nw), wspec(wq), wspec(wkv), wspec(wg), wspec(wxy),
                  wspec(qnw), wspec(knw)],
        out_specs=[pl.BlockSpec((nqb, N_HEADS, QB, DH), lambda i: (i, 0, 0, 0)),
                   pl.BlockSpec((tm * KV_ROWS, DH), row), pl.BlockSpec((tm * KV_ROWS, DH), row),
                   pl.BlockSpec((tm * KV_ROWS, DH), row),
                   pl.BlockSpec((tm, 12 * DH), row), pl.BlockSpec((LANES, tm), lambda i: (0, i)),
                   pl.BlockSpec((tm, d_rec), row), pl.BlockSpec((tm, d_rec), row)],
        out_shape=[jax.ShapeDtypeStruct((n // QB, N_HEADS, QB, DH), BF16),
                   jax.ShapeDtypeStruct((n * KV_ROWS, DH), F32), jax.ShapeDtypeStruct((n * KV_ROWS, DH), F32),
                   jax.ShapeDtypeStruct((n * KV_ROWS, DH), F32), jax.ShapeDtypeStruct((n, 12 * DH), BF16),
                   jax.ShapeDtypeStruct((LANES, n), F32),
                   jax.ShapeDtypeStruct((n, d_rec), F32), jax.ShapeDtypeStruct((n, d_rec), F32)],
        compiler_params=_params(("parallel",)),
        name="inproj",
    )(x2, nw, wq, wkv, wg, wxy, qnw, knw)


def _pool_combo(rows, wl_ref, combo):
    r = rows.shape[0] // (2 * L_CMP)
    x3 = rows.reshape(r, 2 * L_CMP, rows.shape[1])
    even = jnp.sum(x3 * wl_ref[0, combo][None], axis=1)
    odd = jnp.sum(x3 * wl_ref[1, combo][None], axis=1)
    return even.astype(BF16), odd.astype(BF16)


def _pool_body(x_ref, wl_ref, e_ref, o_ref):
    n_tok = x_ref.shape[0] // KV_ROWS
    for combo in range(KV_ROWS):
        even, odd = _pool_combo(x_ref[pl.ds(combo, n_tok, KV_ROWS), :], wl_ref, combo)
        e_ref[:, combo * DH:(combo + 1) * DH] = even
        o_ref[:, combo * DH:(combo + 1) * DH] = odd


def _pool_prompt(cmp4, wl):
    n = cmp4.shape[0] // KV_ROWS
    toks = 1024
    ob = toks // (2 * L_CMP)
    c = KV_ROWS * DH
    return pl.pallas_call(
        _pool_body,
        grid=(n // toks,),
        in_specs=[pl.BlockSpec((toks * KV_ROWS, DH), lambda i: (i, 0)), pl.BlockSpec(wl.shape, lambda i: (0, 0, 0, 0))],
        out_specs=[pl.BlockSpec((ob, c), lambda i: (i, 0)), pl.BlockSpec((ob, c), lambda i: (i, 0))],
        out_shape=[jax.ShapeDtypeStruct((n // (2 * L_CMP), c), BF16)] * 2,
        compiler_params=_params(("parallel",)),
        name="pool_prompt",
    )(cmp4, wl)


def _page_rows(pages, combo):
    n_tok = pages[0].shape[1] // KV_ROWS
    return jnp.concatenate([p[0, pl.ds(combo, n_tok, KV_ROWS), :] for p in pages], axis=0)


def _pool_pages_body(pt_ref, *refs):
    pages = refs[:PAGES_PER_STEP]
    wl_ref, e_ref, o_ref = refs[PAGES_PER_STEP:]
    for combo in range(KV_ROWS):
        even, odd = _pool_combo(_page_rows(pages, combo), wl_ref, combo)
        e_ref[0, :, combo * DH:(combo + 1) * DH] = even
        o_ref[0, :, combo * DH:(combo + 1) * DH] = odd


def _pool_pages(cache3, page_ids, bsz, wl):
    rows = cache3.shape[1]
    n_pages = page_ids.shape[0] // bsz
    n_steps = n_pages // PAGES_PER_STEP
    ob = PAGES_PER_STEP * (rows // KV_ROWS) // (2 * L_CMP)
    c = KV_ROWS * DH
    page_spec = lambda i: pl.BlockSpec((1, rows, DH), lambda b, s, pt: (pt[b * n_pages + s * PAGES_PER_STEP + i], 0, 0))
    grid_spec = pltpu.PrefetchScalarGridSpec(
        num_scalar_prefetch=1,
        grid=(bsz, n_steps),
        in_specs=[page_spec(i) for i in range(PAGES_PER_STEP)] + [pl.BlockSpec(wl.shape, lambda b, s, pt: (0, 0, 0, 0))],
        out_specs=[pl.BlockSpec((1, ob, c), lambda b, s, pt: (b, s, 0))] * 2,
    )
    return pl.pallas_call(
        _pool_pages_body,
        grid_spec=grid_spec,
        out_shape=[jax.ShapeDtypeStruct((bsz, n_steps * ob, c), BF16)] * 2,
        compiler_params=_params(("parallel", "parallel")),
        name="pool_pages",
    )(page_ids, *([cache3] * PAGES_PER_STEP), wl)


def _select_blocks_cols(score, blk, n_pick, n_blk):
    sel = jnp.zeros(score.shape, F32)
    for _ in range(n_pick):
        m = jnp.max(score, axis=0, keepdims=True)
        idx = jnp.min(jnp.where(score == m, blk, n_blk), axis=0, keepdims=True)
        hit = blk == idx
        sel = jnp.where(hit, 1.0, sel)
        score = jnp.where(hit, -jnp.inf, score)
    return sel


def _pattn_body(slopes_ref, q_ref, ksel_ref, vsel_ref, kwin_ref, vwin_ref, kce_ref, kco_ref, vce_ref, vco_ref,
                gt_ref, o_ref, m_scr, l_scr, acc_scr):
    g = pl.program_id(1)
    i = pl.program_id(2)
    rows = GQA * QB
    q = q_ref[0].reshape(rows, DH)
    col = _iota((1, rows), 1)
    qpos = i * QB + (col & (QB - 1))
    slope = jnp.zeros((1, rows), F32)
    slope_r = jnp.zeros((rows, 1), F32)
    for r in range(GQA):
        slope = jnp.where((col >> 7) == r, slopes_ref[g * GQA + r], slope)
        slope_r = jnp.where((_iota((rows, 1), 0) >> 7) == r, slopes_ref[g * GQA + r], slope_r)

    kc = jnp.concatenate([kce_ref[...], kco_ref[...]], axis=0)
    vc = jnp.concatenate([vce_ref[...], vco_ref[...]], axis=0)
    nb = kc.shape[0]
    half = nb // 2
    brow = _iota((nb, rows), 0)
    blk_c = jnp.where(brow < half, 2 * brow, 2 * (brow - half) + 1)
    dist = qpos - (blk_c * L_CMP + (L_CMP - 1))
    s = _dot_nt(kc, q) - slope * dist.astype(F32)
    p = _masked_softmax_cols(s, dist >= 0)
    o_c = _dot_tn(vc, p.astype(BF16))
    imp_t = p[:, 0:QB]
    for r in range(1, GQA):
        imp_t = imp_t + p[:, r * QB:(r + 1) * QB]
    pair = imp_t[:half] + imp_t[half:]
    ns = half
    blk = _iota((ns, QB), 0)
    qp = i * QB + _iota((ns, QB), 1)
    cur = qp >> 6
    forced = (blk == 0) | (blk == cur) | (blk == cur - 1)
    visible = blk * L_SEL <= qp
    score = jnp.where(visible, jnp.where(forced, FORCE, pair), NEG)
    sel = _select_blocks_cols(score, blk, min(N_SEL, ns), ns)

    unpicked = jnp.where(visible, 1.0 - sel, 1.0)
    unpicked = jnp.concatenate([unpicked, jnp.zeros((LANES - ns, QB), F32)], axis=0)
    tail = jnp.concatenate([unpicked.T] * GQA, axis=0)
    lane_t = _iota((rows, LANES), 1)
    q_first = (i * QB).astype(F32)
    tail = tail + jnp.where(lane_t == ns, slope_r * L_SEL,
                            jnp.where(lane_t == ns + 1, slope_r, jnp.where(lane_t == ns + 2, -slope_r * q_first, 0.0)))
    q_aug = jnp.concatenate([q, tail.astype(BF16)], axis=1)

    m_scr[...] = jnp.full(m_scr.shape, NEG, F32)
    l_scr[...] = jnp.zeros(l_scr.shape, F32)
    acc_scr[...] = jnp.zeros(acc_scr.shape, F32)
    ck = 512

    def attend(k0, causal):
        sc = _dot_nt(ksel_ref[pl.ds(k0, ck), :], q_aug)
        if causal:
            sc = jnp.where(qpos >= k0 + _iota((ck, rows), 0), sc, NEG)
        m_old = m_scr[...]
        m_new = jnp.maximum(m_old, jnp.max(sc, axis=0, keepdims=True))
        alpha = jnp.exp(m_old - m_new)
        e = jnp.exp(sc - m_new)
        l_scr[...] = alpha * l_scr[...] + jnp.sum(e, axis=0, keepdims=True)
        acc_scr[...] = alpha * acc_scr[...] + _dot_tn(vsel_ref[pl.ds(k0, ck), :], e.astype(BF16))
        m_scr[...] = m_new

    def chunk(c, carry):
        attend(pl.multiple_of(c * ck, ck), causal=False)
        return carry

    lax.fori_loop(0, i >> 2, chunk, 0)
    attend(pl.multiple_of((i >> 2) * ck, ck), causal=True)
    o_s = acc_scr[...] / jnp.maximum(l_scr[...], 1e-30)

    span = WINDOW + QB
    start = pl.multiple_of(jnp.maximum(i * QB - WINDOW, 0), QB)
    vw = vwin_ref[pl.ds(start, span), :]
    dist_w = qpos - (start + _iota((span, rows), 0))
    mask_w = jnp.where(dist_w >= 0, dist_w, WINDOW) < WINDOW
    s_w = _dot_nt(kwin_ref[pl.ds(start, span), :], q_aug)
    o_w = _dot_tn(vw, _masked_softmax_cols(s_w, mask_w).astype(BF16))

    gt = gt_ref[pl.ds(pl.multiple_of(g * GATE_ROWS, GATE_ROWS), GATE_ROWS), :]
    for r in range(GQA):
        sl = slice(r * QB, (r + 1) * QB)
        o_t = (gt[3 * r:3 * r + 1, :] * o_c[:, sl] + gt[3 * r + 1:3 * r + 2, :] * o_s[:, sl]
               + gt[3 * r + 2:3 * r + 3, :] * o_w[:, sl])
        o_ref[0, r] = o_t.T


MASK_BIAS = -(2.0 ** 100)


def _augmented_keys(kvb, bsz, t_len):
    ns = t_len // L_SEL
    pos = jnp.arange(t_len, dtype=I32)[:, None]
    lane = jnp.arange(LANES, dtype=I32)[None, :]
    feat = jnp.where(lane == ns, pos >> 6, jnp.where(lane == ns + 1, pos & (L_SEL - 1), jnp.where(lane == ns + 2, 1, 0)))
    feat = feat.astype(F32)
    tails = (jnp.where(lane == (pos >> 6), MASK_BIAS, 0.0) + feat, feat)
    kv4 = kvb.reshape(bsz, t_len, 12, DH)

    def aug(col0, tail):
        t = jnp.broadcast_to(tail.astype(BF16)[None, :, None, :], (bsz, t_len, N_KV, LANES))
        return jnp.concatenate([kv4[:, :, col0:col0 + N_KV], t], axis=-1).reshape(bsz * t_len, N_KV * 2 * DH)

    return aug(4, tails[0]), aug(8, tails[1])


def _prompt_attention(slopes, q_blk, kvb, kce, kco, gates_g, bsz, t_len):
    nq = t_len // QB
    nb = t_len // L_CMP
    assert nb % (2 * LANES) == 0 or nb == LANES, "compressed blocks must fill whole lane tiles"
    assert t_len >= WINDOW + QB and t_len // L_SEL + 3 <= LANES
    half = nb // 2
    rows = GQA * QB
    ksel_aug, kwin_aug = _augmented_keys(kvb, bsz, t_len)
    kv_spec = lambda col: pl.BlockSpec((t_len, DH), lambda b, g, i, col=col: (b, col + g))
    ka_spec = pl.BlockSpec((t_len, 2 * DH), lambda b, g, i: (b, g))
    kc_spec = lambda col: pl.BlockSpec((half, DH), lambda b, g, i, col=col: (b, col + g))
    return pl.pallas_call(
        _pattn_body,
        grid=(bsz, N_KV, nq),
        in_specs=[pl.BlockSpec(memory_space=pltpu.SMEM),
                  pl.BlockSpec((1, GQA, QB, DH), lambda b, g, i: (b * nq + i, g, 0, 0)),
                  ka_spec, kv_spec(6), ka_spec, kv_spec(10),
                  kc_spec(0), kc_spec(0), kc_spec(2), kc_spec(2),
                  pl.BlockSpec((LANES, QB), lambda b, g, i: (0, b * nq + i))],
        out_specs=pl.BlockSpec((1, GQA, QB, DH), lambda b, g, i: (b * nq + i, g, 0, 0)),
        out_shape=jax.ShapeDtypeStruct((bsz * nq, N_HEADS, QB, DH), F32),
        scratch_shapes=[pltpu.VMEM((1, rows), F32), pltpu.VMEM((1, rows), F32), pltpu.VMEM((DH, rows), F32)],
        compiler_params=_params(("parallel", "parallel", "arbitrary")),
        name="prompt_attention",
    )(slopes, q_blk, ksel_aug, kvb, kwin_aug, kvb, kce, kco, kce, kco, gates_g)


def _sample_select_body(slopes_ref, q_ref, kce_ref, kco_ref, oc_ref, sel_ref, *, past):
    rows = GQA * TP
    row = _iota((rows, 1), 0)
    qpos = past + (row & (TP - 1))
    r_of_row = row >> 3
    half = kce_ref.shape[1]
    nb = 2 * half
    n_b = q_ref.shape[0]
    lane = _iota((rows, nb), 1)
    blk_c = jnp.where(lane < half, 2 * lane, 2 * (lane - half) + 1)
    dist = qpos - (blk_c * L_CMP + (L_CMP - 1))
    pairs = []
    for b in range(n_b):
        for g in range(N_KV):
            slope = jnp.zeros((rows, 1), F32)
            for r in range(GQA):
                slope = jnp.where(r_of_row == r, slopes_ref[g * GQA + r], slope)
            q = q_ref[b, g]
            kc = jnp.concatenate([kce_ref[b, :, g * DH:(g + 1) * DH], kco_ref[b, :, g * DH:(g + 1) * DH]], axis=0)
            vc = jnp.concatenate([kce_ref[b, :, (2 + g) * DH:(3 + g) * DH], kco_ref[b, :, (2 + g) * DH:(3 + g) * DH]],
                                 axis=0)
            s = _dot_nt(q, kc) - slope * dist.astype(F32)
            p = _masked_softmax_rows(s, dist >= 0)
            oc_ref[b, g] = _dot(p.astype(BF16), vc)
            imp = p[0:TP]
            for r in range(1, GQA):
                imp = imp + p[r * TP:(r + 1) * TP]
            pairs.append(imp[:, :half] + imp[:, half:])
    pair = jnp.concatenate(pairs, axis=0)
    n_rows = pair.shape[0]
    blk = _iota((n_rows, half), 1)
    qp = past + (_iota((n_rows, half), 0) & (TP - 1))
    cur = qp >> 6
    forced = (blk == 0) | (blk == cur) | (blk == cur - 1)
    score = jnp.where(blk * L_SEL <= qp, jnp.where(forced, FORCE, pair), NEG)
    sel = jnp.zeros((n_rows, half), F32)
    for _ in range(N_SEL - 1):
        m = jnp.max(score, axis=1, keepdims=True)
        idx = jnp.min(jnp.where(score == m, blk, half), axis=1, keepdims=True)
        hit = blk == idx
        sel = jnp.where(hit, 1.0, sel)
        score = jnp.where(hit, -jnp.inf, score)
    sel = sel.astype(BF16)
    for b in range(n_b):
        for g in range(N_KV):
            i0 = (b * N_KV + g) * TP
            sel_ref[b, g] = jnp.concatenate([sel[i0:i0 + TP]] * GQA, axis=0)


SELECT_BATCH = 8


def _sample_select(slopes, q_s, kce, kco, past):
    bsz = q_s.shape[0]
    half = kce.shape[1]
    assert half == LANES, "past selection blocks must fill one lane tile"
    rows = GQA * TP
    nbt = SELECT_BATCH
    return pl.pallas_call(
        functools.partial(_sample_select_body, past=past),
        grid=(bsz // nbt,),
        in_specs=[pl.BlockSpec(memory_space=pltpu.SMEM),
                  pl.BlockSpec((nbt, N_KV, rows, DH), lambda b: (b, 0, 0, 0)),
                  pl.BlockSpec((nbt, half, 4 * DH), lambda b: (b, 0, 0)),
                  pl.BlockSpec((nbt, half, 4 * DH), lambda b: (b, 0, 0))],
        out_specs=[pl.BlockSpec((nbt, N_KV, rows, DH), lambda b: (b, 0, 0, 0)),
                   pl.BlockSpec((nbt, N_KV, rows, half), lambda b: (b, 0, 0, 0))],
        out_shape=[jax.ShapeDtypeStruct((bsz, N_KV, rows, DH), F32),
                   jax.ShapeDtypeStruct((bsz, N_KV, rows, half), BF16)],
        compiler_params=_params(("parallel",)),
        name="sample_select",
    )(slopes, q_s, kce, kco)


def _online_update(m_ref, l_ref, acc_ref, g, sc, msk, v):
    m_old = m_ref[g]
    m_new = jnp.maximum(m_old, jnp.max(sc, axis=-1, keepdims=True))
    alpha = jnp.exp(m_old - m_new)
    e = jnp.where(msk, jnp.exp(sc - m_new), 0.0)
    l_ref[g] = alpha * l_ref[g] + jnp.sum(e, axis=-1, keepdims=True)
    acc_ref[g] = alpha * acc_ref[g] + _dot(e.astype(BF16), v)
    m_ref[g] = m_new


def _sample_attn_body(pt_ref, slopes_ref, *refs, past, t_new):
    pages = refs[:PAGES_PER_STEP]
    q_ref, sel_ref, new_ref, win_ref, oc_ref, gt_ref, o_ref, m_scr, l_scr, acc_scr = refs[PAGES_PER_STEP:]
    c = pl.program_id(1)
    rows = GQA * TP
    row = _iota((rows, 1), 0)
    qpos = past + (row & (TP - 1))
    r_of_row = row >> 3
    ck = PAGES_PER_STEP * pages[0].shape[1] // KV_ROWS

    @pl.when(c == 0)
    def _():
        m_scr[...] = jnp.full(m_scr.shape, NEG, F32)
        l_scr[...] = jnp.zeros(l_scr.shape, F32)
        acc_scr[...] = jnp.zeros(acc_scr.shape, F32)

    dist_s = qpos - (c * ck + _iota((rows, ck), 1))
    expand = jnp.where((_iota((LANES, ck), 1) >> 6) + c * (ck // L_SEL) == _iota((LANES, ck), 0), 1.0, 0.0).astype(BF16)
    slopes = []
    for g in range(N_KV):
        slope = jnp.zeros((rows, 1), F32)
        for r in range(GQA):
            slope = jnp.where(r_of_row == r, slopes_ref[g * GQA + r], slope)
        slopes.append(slope)
        q = q_ref[0, g]
        kch = _page_rows(pages, g).astype(BF16)
        vch = _page_rows(pages, 2 + g).astype(BF16)
        picked = _dot(sel_ref[0, g], expand)
        msk = jnp.where(dist_s >= 0, picked, 0.0) > 0.5
        sc = jnp.where(msk, _dot_nt(q, kch) - slope * dist_s.astype(F32), NEG)
        _online_update(m_scr, l_scr, acc_scr, g, sc, msk, vch)

    @pl.when(c == pl.num_programs(1) - 1)
    def _():
        col = _iota((rows, TP), 1)
        dist_n = qpos - (past + col)
        mask_n = jnp.where(col < t_new, dist_n, -1) >= 0
        dist_c = qpos - (past - WINDOW + _iota((rows, WINDOW), 1))
        mask_c = jnp.where(dist_c >= 0, dist_c, WINDOW) < WINDOW
        mask_wn = jnp.where(mask_n, dist_n, WINDOW) < WINDOW
        for g in range(N_KV):
            slope = slopes[g]
            q = q_ref[0, g]
            kn = new_ref[0, :, (4 + g) * DH:(5 + g) * DH]
            vn = new_ref[0, :, (6 + g) * DH:(7 + g) * DH]
            sc = jnp.where(mask_n, _dot_nt(q, kn) - slope * dist_n.astype(F32), NEG)
            _online_update(m_scr, l_scr, acc_scr, g, sc, mask_n, vn)
            o_s = acc_scr[g] / jnp.maximum(l_scr[g], 1e-30)
            kwc = win_ref[0, pl.ds(g, WINDOW, KV_ROWS), :].astype(BF16)
            vwc = win_ref[0, pl.ds(2 + g, WINDOW, KV_ROWS), :].astype(BF16)
            kwn = new_ref[0, :, (8 + g) * DH:(9 + g) * DH]
            vwn = new_ref[0, :, (10 + g) * DH:(11 + g) * DH]
            s1 = jnp.where(mask_c, _dot_nt(q, kwc) - slope * dist_c.astype(F32), NEG)
            s2 = jnp.where(mask_wn, _dot_nt(q, kwn) - slope * dist_n.astype(F32), NEG)
            mx = jnp.maximum(jnp.max(s1, axis=-1, keepdims=True), jnp.max(s2, axis=-1, keepdims=True))
            e1 = jnp.where(mask_c, jnp.exp(s1 - mx), 0.0)
            e2 = jnp.where(mask_wn, jnp.exp(s2 - mx), 0.0)
            den = jnp.maximum(jnp.sum(e1, axis=-1, keepdims=True) + jnp.sum(e2, axis=-1, keepdims=True), 1e-30)
            o_w = _dot((e1 / den).astype(BF16), vwc) + _dot((e2 / den).astype(BF16), vwn)
            gt = gt_ref[0, g]
            o_ref[0, g] = gt[:, 0:1] * oc_ref[0, g] + gt[:, 1:2] * o_s + gt[:, 2:3] * o_w


def _sample_attention(slopes, page_ids, cache_sel3, q_s, sel_s, new_kvb, cache_win3, win_base, o_c, gates_s, past,
                      t_new):
    page_rows = cache_sel3.shape[1]
    bsz = q_s.shape[0]
    n_pages = page_ids.shape[0] // bsz
    n_steps = n_pages // PAGES_PER_STEP
    rows = GQA * TP
    assert cache_win3.shape[1] == WINDOW * KV_ROWS and past % L_SEL == 0 and t_new <= TP
    page_spec = lambda i: pl.BlockSpec((1, page_rows, DH),
                                       lambda b, s, pt: (pt[b * n_pages + s * PAGES_PER_STEP + i], 0, 0))
    per_b = lambda shape: pl.BlockSpec((1,) + shape, lambda b, s, pt: (b,) + (0,) * len(shape))
    grid_spec = pltpu.PrefetchScalarGridSpec(
        num_scalar_prefetch=1,
        grid=(bsz, n_steps),
        in_specs=[pl.BlockSpec(memory_space=pltpu.SMEM)] + [page_spec(i) for i in range(PAGES_PER_STEP)]
        + [per_b((N_KV, rows, DH)), per_b((N_KV, rows, LANES)), per_b((TP, 12 * DH)),
           pl.BlockSpec((1, WINDOW * KV_ROWS, DH), lambda b, s, pt: (win_base + b, 0, 0)),
           per_b((N_KV, rows, DH)), per_b((N_KV, rows, 3))],
        out_specs=per_b((N_KV, rows, DH)),
        scratch_shapes=[pltpu.VMEM((N_KV, rows, 1), F32), pltpu.VMEM((N_KV, rows, 1), F32),
                        pltpu.VMEM((N_KV, rows, DH), F32)],
    )
    return pl.pallas_call(
        functools.partial(_sample_attn_body, past=past, t_new=t_new),
        grid_spec=grid_spec,
        out_shape=jax.ShapeDtypeStruct((bsz, N_KV, rows, DH), F32),
        compiler_params=_params(("parallel", "arbitrary")),
        name="sample_attention",
    )(page_ids, slopes, *([cache_sel3] * PAGES_PER_STEP), q_s, sel_s, new_kvb, cache_win3, o_c, gates_s)


def _gelu_tanh(x):
    return 0.5 * x * (1.0 + jnp.tanh(0.7978845608028654 * (x + 0.044715 * (x * x * x))))


def _lru_coeffs(conv, wa_ref, ba_ref, wi_ref, bi_ref, sp_ref):
    cb = conv.astype(BF16)
    n_blk, blk = wa_ref.shape[0], wa_ref.shape[1]
    ra = jnp.concatenate([_dot(cb[:, n * blk:(n + 1) * blk], wa_ref[n]) for n in range(n_blk)], axis=1)
    ri = jnp.concatenate([_dot(cb[:, n * blk:(n + 1) * blk], wi_ref[n]) for n in range(n_blk)], axis=1)
    r = jax.nn.sigmoid(ra + ba_ref[...])
    gi = jax.nn.sigmoid(ri + bi_ref[...])
    log_a = -LRU_C * r * sp_ref[...]
    a = jnp.exp(log_a)
    b = jnp.sqrt(-jnp.tanh(log_a) * (a * a + 1.0)) * (gi * conv)
    return a, b


def _lru_seq_body(xr_ref, yr_ref, cs_ref, h0_ref, cw_ref, cb_ref, wa_ref, ba_ref, wi_ref, bi_ref, sp_ref, onw_ref,
                  rec_ref, hl_ref, xbuf, h_scr):
    k = pl.program_id(1)
    tt = xr_ref.shape[1]
    pad = 8

    @pl.when(k == 0)
    def _():
        xbuf[0:pad, :] = jnp.zeros((pad, xbuf.shape[1]), F32)
        xbuf[pad - (CONV_W - 1):pad, :] = cs_ref[0]
        h_scr[...] = h0_ref[0]

    x = xr_ref[0]
    xbuf[pad:pad + tt, :] = x
    conv = cb_ref[...] + cw_ref[CONV_W - 1:CONV_W, :] * x
    for j in range(CONV_W - 1):
        conv = conv + cw_ref[j:j + 1, :] * xbuf[pad - (CONV_W - 1) + j:pad - (CONV_W - 1) + j + tt, :]
    xbuf[0:pad, :] = x[tt - pad:tt]
    a, b = _lru_coeffs(conv, wa_ref, ba_ref, wi_ref, bi_ref, sp_ref)
    row = _iota((tt, 1), 0)
    s = 1
    while s < tt:
        keep = row >= s
        a_sh = jnp.where(keep, pltpu.roll(a, s, 0), 1.0)
        b_sh = jnp.where(keep, pltpu.roll(b, s, 0), 0.0)
        b = a * b_sh + b
        a = a * a_sh
        s *= 2
    hs = a * h_scr[...] + b
    h_scr[...] = hs[tt - 1:tt]
    hl_ref[0] = hs[tt - 1:tt]
    rec = hs * _gelu_tanh(yr_ref[0])
    rec_ref[0] = _rms(rec, onw_ref[...]).astype(BF16)


def _lru_seq(xr3, yr3, cs, h0, cw, cb, wa, ba, wi, bi, sp, onw, tt):
    bsz, t_len, d = xr3.shape
    seq = pl.BlockSpec((1, tt, d), lambda b, k: (b, k, 0))
    full = lambda a: pl.BlockSpec(a.shape, lambda b, k: (0,) * a.ndim)
    return pl.pallas_call(
        _lru_seq_body,
        grid=(bsz, t_len // tt),
        in_specs=[seq, seq, pl.BlockSpec((1, CONV_W - 1, d), lambda b, k: (b, 0, 0)),
                  pl.BlockSpec((1, 1, d), lambda b, k: (b, 0, 0)),
                  full(cw), full(cb), full(wa), full(ba), full(wi), full(bi), full(sp), full(onw)],
        out_specs=[seq, pl.BlockSpec((1, 1, d), lambda b, k: (b, 0, 0))],
        out_shape=[jax.ShapeDtypeStruct((bsz, t_len, d), BF16), jax.ShapeDtypeStruct((bsz, 1, d), F32)],
        scratch_shapes=[pltpu.VMEM((tt + 8, d), F32), pltpu.VMEM((1, d), F32)],
        compiler_params=_params(("parallel", "arbitrary")),
        name="lru_seq",
    )(xr3, yr3, cs, h0, cw, cb, wa, ba, wi, bi, sp, onw)


def _lru_step_body(xr_ref, yr_ref, cs_ref, h0_ref, cw_ref, cb_ref, wa_ref, ba_ref, wi_ref, bi_ref, sp_ref, onw_ref,
                   rec_ref, hl_ref):
    t_len, bsz = xr_ref.shape[0], xr_ref.shape[1]
    xs = [cs_ref[j] for j in range(CONV_W - 1)] + [xr_ref[t] for t in range(t_len)]
    convs = []
    for t in range(t_len):
        conv = cb_ref[...] + cw_ref[0:1, :] * xs[t]
        for j in range(1, CONV_W):
            conv = conv + cw_ref[j:j + 1, :] * xs[t + j]
        convs.append(conv)
    a, b = _lru_coeffs(jnp.concatenate(convs, axis=0), wa_ref, ba_ref, wi_ref, bi_ref, sp_ref)
    h = h0_ref[...]
    for t in range(t_len):
        h = a[t * bsz:(t + 1) * bsz] * h + b[t * bsz:(t + 1) * bsz]
        rec = h * _gelu_tanh(yr_ref[t])
        rec_ref[t] = _rms(rec, onw_ref[...]).astype(BF16)
    hl_ref[...] = h


def _lru_step(xr_t, yr_t, cs_t, h0, cw, cb, wa, ba, wi, bi, sp, onw):
    t_len, bsz, d = xr_t.shape
    return pl.pallas_call(
        _lru_step_body,
        out_shape=[jax.ShapeDtypeStruct((t_len, bsz, d), BF16), jax.ShapeDtypeStruct((bsz, d), F32)],
        compiler_params=pltpu.CompilerParams(vmem_limit_bytes=VMEM_LIMIT),
        name="lru_step",
    )(xr_t, yr_t, cs_t, h0, cw, cb, wa, ba, wi, bi, sp, onw)


def _outproj_body(attn_ref, rec_ref, x_ref, cnt0_ref, anw_ref, woa_ref, wor_ref, fnw_ref, wr_ref, br_ref,
                  x1_ref, h2_ref, ti_ref, tg_ref, tp_ref, cnt_ref, carry):
    step = pl.program_id(0)
    tm = x_ref.shape[0]

    @pl.when(step == 0)
    def _():
        carry[...] = cnt0_ref[...]

    attn = jnp.concatenate(
        [jnp.concatenate([attn_ref[b, hd] for hd in range(N_HEADS)], axis=1) for b in range(attn_ref.shape[0])], axis=0)
    an = _rms(attn, anw_ref[...]).astype(BF16)
    x1 = x_ref[...] + (_dot(an, woa_ref[...]) + _dot(rec_ref[...], wor_ref[...]))
    x1_ref[...] = x1
    h2 = _rms(x1, fnw_ref[...])
    dc = h2.shape[1] // LANES
    for c in range(dc):
        h2_ref[pl.ds(c, tm, dc), :] = h2[:, c * LANES:(c + 1) * LANES]
    lane = _iota((tm, LANES), 1)
    h_hi = h2.astype(BF16)
    h_lo = (h2 - h_hi.astype(F32)).astype(BF16)
    part_hi = _dot(h_hi, wr_ref[...])
    part_lo = _dot(h_lo, wr_ref[...])
    logits = part_hi[:, :LANES] + part_hi[:, LANES:] + part_lo[:, :LANES] + br_ref[...]
    lg = jnp.where(lane < N_EXPERTS, logits, -jnp.inf)
    vals, idxs = [], []
    for _ in range(TOP_K):
        m = jnp.max(lg, axis=-1, keepdims=True)
        ix = jnp.min(jnp.where(lg == m, lane, LANES), axis=-1, keepdims=True)
        vals.append(m)
        idxs.append(ix)
        lg = jnp.where(lane == ix, -jnp.inf, lg)
    es = [jnp.exp(v - vals[0]) for v in vals]
    den = es[0]
    for e in es[1:]:
        den = den + e
    onehot = jnp.zeros((tm, LANES), F32)
    for ix in idxs:
        onehot = jnp.where(lane == ix, 1.0, onehot)
    lower = jnp.where(_iota((tm, tm), 0) > _iota((tm, tm), 1), 1.0, 0.0).astype(BF16)
    rank = carry[...] + _dot(lower, onehot.astype(BF16))
    carry[...] = carry[...] + jnp.sum(onehot, axis=0, keepdims=True)
    ti = jnp.zeros((tm, LANES), I32)
    tg = jnp.zeros((tm, LANES), F32)
    tp = jnp.zeros((tm, LANES), I32)
    for k in range(TOP_K):
        pos = jnp.sum(jnp.where(lane == idxs[k], rank, 0.0), axis=-1, keepdims=True).astype(I32)
        ti = jnp.where(lane == k, idxs[k], ti)
        tg = jnp.where(lane == k, es[k] / den, tg)
        tp = jnp.where(lane == k, pos, tp)
    ti_ref[...] = ti
    tg_ref[...] = tg
    tp_ref[...] = tp
    cnt_ref[...] = carry[...]


def _outproj_router(attn_blk, recn, x2, cnt0, anw, woa, wor, fnw, wr, br, tm):
    n, d = x2.shape
    nqb = tm // QB
    row = lambda i: (i, 0)
    wspec = lambda a: pl.BlockSpec(a.shape, lambda i: (0,) * a.ndim, pipeline_mode=pl.Buffered(1))
    lanes_out = lambda dt: jax.ShapeDtypeStruct((n, LANES), dt)
    return pl.pallas_call(
        _outproj_body,
        grid=(n // tm,),
        in_specs=[pl.BlockSpec((nqb, N_HEADS, QB, DH), lambda i: (i, 0, 0, 0)),
                  pl.BlockSpec((tm, recn.shape[1]), row), pl.BlockSpec((tm, d), row),
                  wspec(cnt0), wspec(anw), wspec(woa), wspec(wor), wspec(fnw), wspec(wr), wspec(br)],
        out_specs=[pl.BlockSpec((tm, d), row), pl.BlockSpec((tm * (d // LANES), LANES), row),
                   pl.BlockSpec((tm, LANES), row),
                   pl.BlockSpec((tm, LANES), row), pl.BlockSpec((tm, LANES), row), pl.BlockSpec((1, LANES), lambda i: (0, 0))],
        out_shape=[jax.ShapeDtypeStruct((n, d), F32), jax.ShapeDtypeStruct((n * (d // LANES), LANES), F32),
                   lanes_out(I32), lanes_out(F32), lanes_out(I32), jax.ShapeDtypeStruct((1, LANES), F32)],
        scratch_shapes=[pltpu.VMEM((1, LANES), F32)],
        compiler_params=_params(("arbitrary",)),
        name="outproj_router",
    )(attn_blk, recn, x2, cnt0, anw, woa, wor, fnw, wr, br)


ISSUE_UNROLL = 8


def _gather_body(rowtok_ref, nused_ref, h2_hbm, out_ref, buf, sem):
    j = pl.program_id(0)
    tm = out_ref.shape[0]
    dc = buf.shape[1] // tm
    n_used = nused_ref[0]

    def issue_block(blk, slot):
        def body(it, c):
            for u in range(ISSUE_UNROLL):
                r = it * ISSUE_UNROLL + u
                src = pl.multiple_of(rowtok_ref[blk * tm + r] * dc, dc)
                pltpu.make_async_copy(h2_hbm.at[pl.ds(src, dc), :], buf.at[slot, pl.ds(r * dc, dc), :],
                                      sem.at[slot]).start(priority=u % 2)
            return c

        lax.fori_loop(0, tm // ISSUE_UNROLL, body, 0)

    @pl.when((j == 0) & (n_used > 0))
    def _():
        issue_block(0, 0)

    @pl.when(j + 1 < n_used)
    def _():
        issue_block(j + 1, (j + 1) % 2)

    @pl.when(j < n_used)
    def _():
        slot = j % 2
        pltpu.make_async_copy(h2_hbm.at[pl.ds(0, tm * dc), :], buf.at[slot], sem.at[slot]).wait()
        for c in range(dc):
            out_ref[:, c * LANES:(c + 1) * LANES] = buf[slot, pl.ds(c, tm, dc), :].astype(BF16)

    @pl.when(j >= n_used)
    def _():
        out_ref[...] = jnp.zeros(out_ref.shape, BF16)


def _gather_rows(row_tok, n_used, h2, n_blocks, d):
    tm = MOE_TM
    dc = d // LANES
    grid_spec = pltpu.PrefetchScalarGridSpec(
        num_scalar_prefetch=2,
        grid=(n_blocks,),
        in_specs=[pl.BlockSpec(memory_space=pl.ANY)],
        out_specs=pl.BlockSpec((tm, d), lambda j, rt, nu: (j, 0)),
        scratch_shapes=[pltpu.VMEM((2, tm * dc, LANES), F32), pltpu.SemaphoreType.DMA((2,))],
    )
    return pl.pallas_call(
        _gather_body,
        grid_spec=grid_spec,
        out_shape=jax.ShapeDtypeStruct((n_blocks * tm, d), BF16),
        compiler_params=_params(("arbitrary",)),
        name="moe_gather",
    )(row_tok, n_used, h2)


COPY_SPLIT = 4


def _expert_blocks(first_row, n_blk, n_used, n_blocks, x_hbm, out_rows, xbuf, obuf, xsem, osem, prepare, compute):
    tm = xbuf.shape[1]
    rc = tm // COPY_SPLIT
    orc = obuf.shape[1] // COPY_SPLIT

    def x_copies(b, slot):
        r = pl.multiple_of(first_row + b * tm, tm)
        return [pltpu.make_async_copy(x_hbm.at[pl.ds(r + s * rc, rc), :], xbuf.at[slot, pl.ds(s * rc, rc), :],
                                      xsem.at[slot]) for s in range(COPY_SPLIT)]

    def o_copies(r, slot):
        r = pl.multiple_of(r, tm)
        return [pltpu.make_async_copy(obuf.at[slot, pl.ds(s * orc, orc), :], out_rows(r + s * rc, rc), osem.at[slot])
                for s in range(COPY_SPLIT)]

    def start(copies):
        for cp in copies:
            cp.start(priority=1)

    def wait(copies):
        for cp in copies:
            cp.wait()

    @pl.when(n_blk > 0)
    def _():
        start(x_copies(0, 0))
        prepare()

        def body(b, c):
            slot = b % 2

            @pl.when(b + 1 < n_blk)
            def _():
                start(x_copies(b + 1, 1 - slot))

            wait(x_copies(b, slot))

            @pl.when(b >= 2)
            def _():
                wait(o_copies(first_row + (b - 2) * tm, slot))

            compute(xbuf[slot], slot)
            start(o_copies(first_row + b * tm, slot))
            return c

        lax.fori_loop(0, n_blk, body, 0)

        @pl.when(n_blk >= 2)
        def _():
            wait(o_copies(first_row + (n_blk - 2) * tm, n_blk % 2))

        wait(o_copies(first_row + (n_blk - 1) * tm, (n_blk - 1) % 2))

    @pl.when(pl.program_id(0) == pl.num_programs(0) - 1)
    def _():
        obuf[0] = jnp.zeros(obuf.shape[1:], obuf.dtype)

        def fill(t, c):
            start(o_copies(t * tm, 0))
            wait(o_copies(t * tm, 0))
            return c

        lax.fori_loop(n_used, n_blocks, fill, 0)


W_SPLIT = 4


def _moe_up_body(row0_ref, nblk_ref, nu_ref, xs_hbm, *refs, n_blocks):
    w_refs = refs[:W_SPLIT]
    b_ref, perm_ref, h_hbm, wbf, xbuf, obuf, xsem, osem = refs[W_SPLIT:]
    e = pl.program_id(0)
    tn = wbf.shape[1]
    pw = perm_ref.shape[0]
    hw = pw // 2
    kc = w_refs[0].shape[1]

    def prepare():
        for q, w_ref in enumerate(w_refs):
            for c in range(tn // pw):
                w = w_ref[0, :, c * pw:(c + 1) * pw].astype(BF16)
                wbf[q * kc:(q + 1) * kc, c * pw:(c + 1) * pw] = _dot(w, perm_ref[...]).astype(BF16)

    def compute(x, slot):
        gu = _dot(x, wbf[...]) + b_ref[0]
        for c in range(tn // pw):
            glu = jnp.minimum(gu[:, c * pw:c * pw + hw], SWIGLU_LIMIT)
            lin = jnp.clip(gu[:, c * pw + hw:(c + 1) * pw], -SWIGLU_LIMIT, SWIGLU_LIMIT)
            obuf[slot, :, c * hw:(c + 1) * hw] = (glu * jax.nn.sigmoid(SWIGLU_ALPHA * glu) * (lin + 1.0)).astype(BF16)

    col0 = pl.multiple_of(pl.program_id(1) * (tn // 2), tn // 2)
    out_rows = lambda r, n: h_hbm.at[pl.ds(r, n), pl.ds(col0, tn // 2)]
    _expert_blocks(row0_ref[e], nblk_ref[e], nu_ref[0], n_blocks, xs_hbm, out_rows, xbuf, obuf, xsem, osem,
                   prepare, compute)


def _w_specs(k, tn):
    return [pl.BlockSpec((1, k // W_SPLIT, tn), lambda e, n, *_, q=q: (e, q, n)) for q in range(W_SPLIT)]


def _moe_up(row0, nblk, n_used, xs, w_gu, b_gu, perm):
    n_rows, d = xs.shape
    n_e, _, f2 = w_gu.shape
    tm, tn = MOE_TM, MOE_TN
    grid_spec = pltpu.PrefetchScalarGridSpec(
        num_scalar_prefetch=3,
        grid=(n_e, f2 // tn),
        in_specs=[pl.BlockSpec(memory_space=pl.ANY)] + _w_specs(d, tn)
        + [pl.BlockSpec((1, 1, tn), lambda e, n, *_: (e, 0, n)),
           pl.BlockSpec(perm.shape, lambda e, n, *_: (0, 0))],
        out_specs=pl.BlockSpec(memory_space=pl.ANY),
        scratch_shapes=[pltpu.VMEM((d, tn), BF16), pltpu.VMEM((2, tm, d), BF16), pltpu.VMEM((2, tm, tn // 2), BF16),
                        pltpu.SemaphoreType.DMA((2,)), pltpu.SemaphoreType.DMA((2,))],
    )
    return pl.pallas_call(
        functools.partial(_moe_up_body, n_blocks=n_rows // tm),
        grid_spec=grid_spec,
        out_shape=jax.ShapeDtypeStruct((n_rows, f2 // 2), BF16),
        compiler_params=_params(("arbitrary", "arbitrary")),
        name="moe_up",
    )(row0, nblk, n_used, xs, *([w_gu] * W_SPLIT), b_gu, perm)


def _moe_down_body(row0_ref, nblk_ref, nu_ref, h_hbm, *refs, n_blocks):
    w_refs = refs[:W_SPLIT]
    b_ref, y_hbm, wbf, xbuf, obuf, xsem, osem = refs[W_SPLIT:]
    e = pl.program_id(0)
    tn = wbf.shape[1]
    kc = w_refs[0].shape[1]

    def prepare():
        for q, w_ref in enumerate(w_refs):
            wbf[q * kc:(q + 1) * kc, :] = w_ref[0].astype(BF16)

    dc = tn // LANES
    tm = xbuf.shape[1]

    def compute(h, slot):
        y = _dot(h, wbf[...]) + b_ref[0]
        for c in range(dc):
            obuf[slot, pl.ds(c, tm, dc), :] = y[:, c * LANES:(c + 1) * LANES]

    out_rows = lambda r, n: y_hbm.at[pl.ds(pl.multiple_of(r * dc, dc), n * dc), :]
    _expert_blocks(row0_ref[e], nblk_ref[e], nu_ref[0], n_blocks, h_hbm, out_rows, xbuf, obuf, xsem, osem,
                   prepare, compute)


def _moe_down(row0, nblk, n_used, h, w_d, b_d):
    n_rows, f = h.shape
    n_e, _, d = w_d.shape
    tm, tn = MOE_TM, MOE_TN
    assert tn == d, "a grid step writes whole output rows"
    grid_spec = pltpu.PrefetchScalarGridSpec(
        num_scalar_prefetch=3,
        grid=(n_e, d // tn),
        in_specs=[pl.BlockSpec(memory_space=pl.ANY)] + _w_specs(f, tn)
        + [pl.BlockSpec((1, 1, tn), lambda e, n, *_: (e, 0, n))],
        out_specs=pl.BlockSpec(memory_space=pl.ANY),
        scratch_shapes=[pltpu.VMEM((f, tn), BF16), pltpu.VMEM((2, tm, f), BF16),
                        pltpu.VMEM((2, tm * (d // LANES), LANES), F32),
                        pltpu.SemaphoreType.DMA((2,)), pltpu.SemaphoreType.DMA((2,))],
    )
    return pl.pallas_call(
        functools.partial(_moe_down_body, n_blocks=n_rows // tm),
        grid_spec=grid_spec,
        out_shape=jax.ShapeDtypeStruct((n_rows * (d // LANES), LANES), F32),
        compiler_params=_params(("arbitrary", "arbitrary")),
        name="moe_down",
    )(row0, nblk, n_used, h, *([w_d] * W_SPLIT), b_d)


def _combine_body(dest_hbm, gate_ref, x1_ref, y_hbm, outp_ref, outs_ref, dsm, buf, sem_d, sem, *, np_blocks):
    j = pl.program_id(0)
    n = pl.num_programs(0)
    tk = x1_ref.shape[0]
    dc = x1_ref.shape[1] // LANES
    toks = ISSUE_UNROLL // TOP_K

    def idx_copy(step, slot):
        return pltpu.make_async_copy(dest_hbm.at[step, 0], dsm.at[slot], sem_d.at[slot])

    def issue_rows(slot):
        def body(it, c):
            for u in range(toks):
                t = it * toks + u
                for k in range(TOP_K):
                    src = pl.multiple_of(dsm[slot, t * TOP_K + k] * dc, dc)
                    pltpu.make_async_copy(y_hbm.at[pl.ds(src, dc), :], buf.at[slot, k, pl.ds(t * dc, dc), :],
                                          sem.at[slot]).start(priority=k % 2)
            return c

        lax.fori_loop(0, tk // toks, body, 0)

    @pl.when(j == 0)
    def _():
        idx_copy(0, 0).start()
        idx_copy(0, 0).wait()
        issue_rows(0)

        @pl.when(1 < n)
        def _():
            idx_copy(1, 1).start()

    @pl.when(j + 1 < n)
    def _():
        slot = (j + 1) % 2
        idx_copy(j + 1, slot).wait()
        issue_rows(slot)

    @pl.when(j + 2 < n)
    def _():
        idx_copy(j + 2, j % 2).start()

    slot = j % 2
    for k in range(TOP_K):
        pltpu.make_async_copy(y_hbm.at[pl.ds(0, tk * dc), :], buf.at[slot, k], sem.at[slot]).wait()
    g = gate_ref[...]
    gk = [jnp.broadcast_to(g[:, k:k + 1], (tk, LANES)) for k in range(TOP_K)]

    def result(c):
        acc = gk[0] * buf[slot, 0, pl.ds(c, tk, dc), :]
        for k in range(1, TOP_K):
            acc = acc + gk[k] * buf[slot, k, pl.ds(c, tk, dc), :]
        return x1_ref[:, c * LANES:(c + 1) * LANES] + acc

    @pl.when(j < np_blocks)
    def _():
        for c in range(dc):
            outp_ref[:, c * LANES:(c + 1) * LANES] = result(c)

    @pl.when(j >= np_blocks)
    def _():
        for c in range(dc):
            outs_ref[:, c * LANES:(c + 1) * LANES] = result(c)


def _combine(dest3, gates, x1, y, n_prompt):
    n, d = x1.shape
    tk = dest3.shape[2] // TOP_K
    np_blocks = n_prompt // tk
    row = lambda j: (j, 0)
    return pl.pallas_call(
        functools.partial(_combine_body, np_blocks=np_blocks),
        grid=(n // tk,),
        in_specs=[pl.BlockSpec(memory_space=pl.ANY), pl.BlockSpec((tk, LANES), row), pl.BlockSpec((tk, d), row),
                  pl.BlockSpec(memory_space=pl.ANY)],
        out_specs=[pl.BlockSpec((tk, d), lambda j: (jnp.minimum(j, np_blocks - 1), 0)),
                   pl.BlockSpec((tk, d), lambda j: (jnp.maximum(j - np_blocks, 0), 0))],
        out_shape=[jax.ShapeDtypeStruct((n_prompt, d), F32), jax.ShapeDtypeStruct((n - n_prompt, d), F32)],
        scratch_shapes=[pltpu.SMEM((2, tk * TOP_K), I32), pltpu.VMEM((2, TOP_K, tk * (d // LANES), LANES), F32),
                        pltpu.SemaphoreType.DMA((2,)), pltpu.SemaphoreType.DMA((2,))],
        compiler_params=_params(("arbitrary",)),
        name="moe_combine",
    )(dest3, gates, x1, y)


def _moe(h2, x1, topi, topg, topp, counts, w_gu, b_gu, w_d, b_d, n_prompt):
    n, d = x1.shape
    tm = MOE_TM
    n_asg = n * TOP_K
    n_blocks = n_asg // tm + N_EXPERTS
    padded = (counts + tm - 1) // tm * tm
    pad_end = jnp.cumsum(padded)
    pad_start = pad_end - padded
    dest = pad_start[topi[:, :TOP_K]] + topp[:, :TOP_K]
    tok = jnp.broadcast_to(jnp.arange(n, dtype=I32)[:, None], (n, TOP_K))
    row_tok = jnp.zeros((n_blocks * tm,), I32).at[dest.reshape(-1)].set(tok.reshape(-1))
    n_used = (pad_end[-1:] // tm).astype(I32)
    row0 = pad_start.astype(I32)
    nblk = (padded // tm).astype(I32)
    f2 = w_gu.shape[2]
    pw = 2 * LANES
    src = jnp.arange(pw)
    perm = (jnp.arange(pw)[:, None] == jnp.where(src < LANES, 2 * src, 2 * (src - LANES) + 1)[None, :]).astype(BF16)
    b_gu_p = b_gu.reshape(N_EXPERTS, f2 // pw, LANES, 2).transpose(0, 1, 3, 2).reshape(N_EXPERTS, 1, f2)
    xs = _gather_rows(row_tok, n_used, h2, n_blocks, d)
    h = _moe_up(row0, nblk, n_used, xs, w_gu, b_gu_p, perm)
    y = _moe_down(row0, nblk, n_used, h, w_d, b_d.reshape(N_EXPERTS, 1, d))
    tk = 128
    dest3 = dest.astype(I32).reshape(n // tk, 1, tk * TOP_K)
    return _combine(dest3, topg, x1, y, n_prompt)


def _layer(layer, xp, xs, cache_cmp_all, cache_sel_all, cache_win_all, state_conv, state_h, page_table, w):
    (norm_mix_w, w_in, q_norm_w, k_norm_w, w_cmp_k, w_cmp_v, conv_w, conv_b, w_gate_a, b_gate_a, w_gate_i, b_gate_i,
     lru_lambda, out_norm_attn, out_norm_rec, w_out, norm_ffn_w, w_router, b_router, w_gate_up, b_gate_up, w_down,
     b_down) = w
    bp, tp, d = xp.shape
    bs, ts, _ = xs.shape
    d_rec = d - D_ATTN
    past = page_table.shape[1] * cache_cmp_all.shape[2]
    assert bs * ts == QB and ts <= TP

    o1, o2 = D_ATTN, D_ATTN + 12 * DH
    o3 = o2 + 3 * N_HEADS
    wq = w_in[:, :o1].astype(BF16)
    wkv = w_in[:, o1:o2].astype(BF16)
    wg = jnp.pad(w_in[:, o2:o3].reshape(d, N_KV, 3 * GQA), ((0, 0), (0, 0), (0, GATE_ROWS - 3 * GQA)))
    wg = jnp.pad(wg.reshape(d, N_KV * GATE_ROWS), ((0, 0), (0, LANES - N_KV * GATE_ROWS))).astype(BF16)
    wxy = w_in[:, o3:].astype(BF16)
    row2 = lambda v: v.reshape(1, -1)
    slopes = jnp.exp2(-8.0 * jnp.arange(1, N_HEADS + 1, dtype=F32) / N_HEADS)
    w4 = jnp.concatenate([w_cmp_k, w_cmp_v], axis=0)
    z4 = jnp.zeros_like(w4)
    wl = jnp.stack([jnp.concatenate([w4, z4], axis=1), jnp.concatenate([z4, w4], axis=1)])
    wl = jnp.broadcast_to(wl[..., None], wl.shape + (DH,))
    sp = row2(jax.nn.softplus(-lru_lambda.astype(F32)))
    wa, wi = w_gate_a.astype(BF16), w_gate_i.astype(BF16)
    woa, wor = w_out[:D_ATTN].astype(BF16), w_out[D_ATTN:].astype(BF16)
    wr = jnp.pad(w_router, ((0, 0), (0, LANES - N_EXPERTS)))
    wr_hi = wr.astype(BF16)
    wr = jnp.concatenate([wr_hi, (wr - wr_hi.astype(F32)).astype(BF16)], axis=1)
    br = row2(jnp.pad(b_router, (0, LANES - N_EXPERTS)))
    mix = (row2(norm_mix_w), wq, wkv, wg, wxy, row2(q_norm_w), k_norm_w)
    lru_w = (conv_w, row2(conv_b), wa, row2(b_gate_a), wi, row2(b_gate_i), sp, row2(out_norm_rec))

    np_tok = bp * tp
    q_p, cmp_p, sel_p, win_p, kvb_p, gate_p, xr_p, yr_p = _inproj(xp.reshape(np_tok, d), *mix, tm=256)
    kce_p, kco_p = _pool_prompt(cmp_p, wl)
    attn_p = _prompt_attention(slopes, q_p, kvb_p, kce_p, kco_p, gate_p, bp, tp)
    recn_p, hl_p = _lru_seq(xr_p.reshape(bp, tp, d_rec), yr_p.reshape(bp, tp, d_rec),
                            jnp.zeros((bp, CONV_W - 1, d_rec), F32), jnp.zeros((bp, 1, d_rec), F32), *lru_w, tt=256)
    post = (row2(out_norm_attn), woa, wor, row2(norm_ffn_w), wr, br)
    x1_p, h2_p, ti_p, tg_p, tp_p, cnt_p = _outproj_router(
        attn_p, recn_p.reshape(np_tok, d_rec), xp.reshape(np_tok, d), jnp.zeros((1, LANES), F32), *post, tm=256)

    ns_tok = bs * ts
    q_s, cmp_s, sel_s, win_s, kvb_s, gate_s, xr_s, yr_s = _inproj(xs.reshape(ns_tok, d), *mix, tm=QB)
    n_pool, page = cache_cmp_all.shape[1], cache_cmp_all.shape[2]
    page_view = lambda c: c.reshape(c.shape[0] * n_pool, page * KV_ROWS, DH)
    page_ids = (page_table + layer * n_pool).reshape(-1)
    kce_s, kco_s = _pool_pages(page_view(cache_cmp_all), page_ids, bs, wl)
    q_t = q_s[0].reshape(N_KV, GQA, bs, ts, DH).transpose(2, 0, 1, 3, 4)
    q_t = jnp.pad(q_t, ((0, 0), (0, 0), (0, 0), (0, TP - ts), (0, 0))).reshape(bs, N_KV, GQA * TP, DH)
    oc_s, selm_s = _sample_select(slopes, q_t, kce_s, kco_s, past)
    new_kvb = jnp.pad(kvb_s.reshape(bs, ts, 12 * DH), ((0, 0), (0, TP - ts), (0, 0)))
    g_t = gate_s.T[:, :N_KV * GATE_ROWS].reshape(bs, ts, N_KV, GATE_ROWS)[..., :3 * GQA]
    g_t = g_t.reshape(bs, ts, N_KV, GQA, 3).transpose(0, 2, 3, 1, 4)
    g_t = jnp.pad(g_t, ((0, 0), (0, 0), (0, 0), (0, TP - ts), (0, 0))).reshape(bs, N_KV, GQA * TP, 3)
    win_view = cache_win_all.reshape(cache_win_all.shape[0] * bs, cache_win_all.shape[2] * KV_ROWS, DH)
    attn_s = _sample_attention(slopes, page_ids, page_view(cache_sel_all), q_t, selm_s, new_kvb, win_view, layer * bs,
                               oc_s, g_t, past, ts)
    attn_s = attn_s.reshape(bs, N_KV, GQA, TP, DH)[:, :, :, :ts].transpose(1, 2, 0, 3, 4).reshape(1, N_HEADS, QB, DH)
    tmaj = lambda a: a.reshape(bs, ts, d_rec).transpose(1, 0, 2)
    recn_s, hl_s = _lru_step(tmaj(xr_s), tmaj(yr_s), state_conv.transpose(1, 0, 2), state_h, *lru_w)
    x1_s, h2_s, ti_s, tg_s, tp_s, cnt = _outproj_router(
        attn_s, recn_s.transpose(1, 0, 2).reshape(ns_tok, d_rec), xs.reshape(ns_tok, d), cnt_p, *post, tm=QB)

    cat = lambda a, b: jnp.concatenate([a, b], axis=0)
    out_p, out_s = _moe(cat(h2_p, h2_s), cat(x1_p, x1_s), cat(ti_p, ti_s), cat(tg_p, tg_s), cat(tp_p, tp_s),
                        cnt[0, :N_EXPERTS].astype(I32), w_gate_up, b_gate_up, w_down, b_down, np_tok)
    y_p = out_p.reshape(bp, tp, d)
    y_s = out_s.reshape(bs, ts, d)

    kv5 = lambda a, b, t: a.reshape(b, t, 2, N_KV, DH)
    win_len_p = min(WINDOW, tp)
    st_p = (kv5(cmp_p, bp, tp), kv5(sel_p, bp, tp), kv5(win_p, bp, tp)[:, tp - win_len_p:],
            xr_p.reshape(bp, tp, d_rec)[:, tp - (CONV_W - 1):], hl_p.reshape(bp, d_rec))
    cache_win = cache_win_all[layer]
    win_all = jnp.concatenate([cache_win, kv5(win_s, bs, ts)], axis=1)
    xcat = jnp.concatenate([state_conv, xr_s.reshape(bs, ts, d_rec)], axis=1)
    st_s = (kv5(cmp_s, bs, ts), kv5(sel_s, bs, ts), win_all[:, win_all.shape[1] - cache_win.shape[1]:],
            xcat[:, ts:], hl_s)
    return y_p, y_s, st_p, st_s


def kernel(x_prompt, x_sample, cache_cmp_kv, cache_sel_kv, cache_win_kv, state_conv, state_h, page_table, norm_mix_w, w_in, q_norm_w, k_norm_w, w_cmp_k, w_cmp_v, conv_w, conv_b, w_gate_a, b_gate_a, w_gate_i, b_gate_i, lru_lambda, out_norm_attn, out_norm_rec, w_out, norm_ffn_w, w_router, b_router, w_gate_up, b_gate_up, w_down, b_down):
    depth = w_in.shape[0]
    xp, xs = x_prompt, x_sample
    st_ps, st_ss = [], []
    for l in range(depth):
        w = (norm_mix_w[l], w_in[l], q_norm_w[l], k_norm_w[l], w_cmp_k[l], w_cmp_v[l], conv_w[l], conv_b[l],
             w_gate_a[l], b_gate_a[l], w_gate_i[l], b_gate_i[l], lru_lambda[l], out_norm_attn[l], out_norm_rec[l],
             w_out[l], norm_ffn_w[l], w_router[l], b_router[l], w_gate_up[l], b_gate_up[l], w_down[l], b_down[l])
        xp, xs, st_p, st_s = _layer(l, xp, xs, cache_cmp_kv, cache_sel_kv, cache_win_kv, state_conv[l], state_h[l],
                                    page_table, w)
        st_ps.append(st_p)
        st_ss.append(st_s)
    stack = lambda sts, i: jnp.stack([s[i] for s in sts])
    return (xp, xs) + tuple(stack(st_ps, i) for i in range(5)) + tuple(stack(st_ss, i) for i in range(5))
```

```python
import functools

import jax
import jax.numpy as jnp
from jax import lax
from jax.experimental import pallas as pl
from jax.experimental.pallas import tpu as pltpu

F32 = jnp.float32
BF16 = jnp.bfloat16
I32 = jnp.int32

N_HEADS = 8
N_KV = 2
GQA = N_HEADS // N_KV
DH = 128
D_ATTN = N_HEADS * DH
CONV_W = 4
LRU_C = 8.0
L_CMP = 32
L_SEL = 64
N_SEL = 16
WINDOW = 512
QB = 128
N_EXPERTS = 32
TOP_K = 4
SWIGLU_LIMIT = 7.0
SWIGLU_ALPHA = 1.702
EPS = 1e-6
NEG = -1e30
FORCE = 1e9
Q_SCALE = DH ** -0.5
LANES = 128
TP = 8
KV_ROWS = 2 * N_KV
GATE_ROWS = 16
MOE_TM = 256
MOE_TN = 2048
PAGES_PER_STEP = 32
VMEM_LIMIT = 56 * 1024 * 1024


def _dot(a, b):
    return jnp.dot(a, b, preferred_element_type=F32)


def _dot_nt(a, b):
    return lax.dot_general(a, b, (((1,), (1,)), ((), ())), preferred_element_type=F32)


def _dot_tn(a, b):
    return lax.dot_general(a, b, (((0,), (0,)), ((), ())), preferred_element_type=F32)


def _iota(shape, dim):
    return lax.broadcasted_iota(I32, shape, dim)


def _rms(x, w):
    return x * lax.rsqrt(jnp.mean(x * x, axis=-1, keepdims=True) + EPS) * w


def _masked_softmax_rows(s, mask):
    s = jnp.where(mask, s, NEG)
    e = jnp.where(mask, jnp.exp(s - jnp.max(s, axis=-1, keepdims=True)), 0.0)
    return e / jnp.maximum(jnp.sum(e, axis=-1, keepdims=True), 1e-30)


def _masked_softmax_cols(s, mask):
    s = jnp.where(mask, s, NEG)
    e = jnp.where(mask, jnp.exp(s - jnp.max(s, axis=0, keepdims=True)), 0.0)
    return e / jnp.maximum(jnp.sum(e, axis=0, keepdims=True), 1e-30)


def _params(sem, vmem=None):
    return pltpu.CompilerParams(dimension_semantics=sem, vmem_limit_bytes=vmem or VMEM_LIMIT)


def _inproj_body(x_ref, nw_ref, wq_ref, wkv_ref, wg_ref, wxy_ref, qnw_ref, knw_ref, tsel_ref, twin_ref,
                 q_ref, cmp_ref, sel_ref, win_ref, kvb_ref, kaug_ref, gate_ref, xr_ref, yr_ref):
    x = x_ref[...]
    h = _rms(x, nw_ref[...]).astype(BF16)
    q = _dot(h, wq_ref[...])
    qnw = qnw_ref[...]
    nqb = q_ref.shape[0]
    for hd in range(N_HEADS):
        qn = (_rms(q[:, hd * DH:(hd + 1) * DH], qnw) * Q_SCALE).astype(BF16)
        for b in range(nqb):
            q_ref[b, hd] = qn[b * QB:(b + 1) * QB]
    kv = _dot(h, wkv_ref[...])
    tm = x.shape[0]
    outs = (cmp_ref, sel_ref, win_ref)
    for br in range(3):
        knw = knw_ref[br:br + 1, :]
        for g in range(N_KV):
            c0 = br * 4 * DH + g * DH
            kn = _rms(kv[:, c0:c0 + DH], knw)
            v = kv[:, c0 + 2 * DH:c0 + 3 * DH]
            outs[br][pl.ds(g, tm, KV_ROWS), :] = kn
            outs[br][pl.ds(2 + g, tm, KV_ROWS), :] = v
            kvb_ref[:, c0:c0 + DH] = kn.astype(BF16)
            kvb_ref[:, c0 + 2 * DH:c0 + 3 * DH] = v.astype(BF16)
            if br > 0:
                a0 = ((br - 1) * N_KV + g) * 2 * DH
                kaug_ref[:, a0:a0 + DH] = kn.astype(BF16)
                kaug_ref[:, a0 + DH:a0 + 2 * DH] = (tsel_ref if br == 1 else twin_ref)[...]
    gate_ref[...] = jax.nn.sigmoid(_dot(h, wg_ref[...])).T
    xy = _dot(h, wxy_ref[...])
    d_rec = xr_ref.shape[1]
    xr_ref[...] = xy[:, :d_rec]
    yr_ref[...] = xy[:, d_rec:]


def _inproj(x2, nw, wq, wkv, wg, wxy, qnw, knw, tail_sel, tail_win, tm):
    n, d = x2.shape
    d_rec = wxy.shape[1] // 2
    nqb = tm // QB
    seq_tiles = tail_sel.shape[0] // tm
    row = lambda i: (i, 0)
    const = lambda i: (0, 0)
    wspec = lambda a: pl.BlockSpec(a.shape, const, pipeline_mode=pl.Buffered(1))
    tail_spec = pl.BlockSpec((tm, LANES), lambda i: (i % seq_tiles, 0))
    return pl.pallas_call(
        _inproj_body,
        grid=(n // tm,),
        in_specs=[pl.BlockSpec((tm, d), row), wspec(nw), wspec(wq), wspec(wkv), wspec(wg), wspec(wxy),
                  wspec(qnw), wspec(knw), tail_spec, tail_spec],
        out_specs=[pl.BlockSpec((nqb, N_HEADS, QB, DH), lambda i: (i, 0, 0, 0)),
                   pl.BlockSpec((tm * KV_ROWS, DH), row), pl.BlockSpec((tm * KV_ROWS, DH), row),
                   pl.BlockSpec((tm * KV_ROWS, DH), row),
                   pl.BlockSpec((tm, 12 * DH), row), pl.BlockSpec((tm, 8 * DH), row),
                   pl.BlockSpec((LANES, tm), lambda i: (0, i)),
                   pl.BlockSpec((tm, d_rec), row), pl.BlockSpec((tm, d_rec), row)],
        out_shape=[jax.ShapeDtypeStruct((n // QB, N_HEADS, QB, DH), BF16),
                   jax.ShapeDtypeStruct((n * KV_ROWS, DH), F32), jax.ShapeDtypeStruct((n * KV_ROWS, DH), F32),
                   jax.ShapeDtypeStruct((n * KV_ROWS, DH), F32), jax.ShapeDtypeStruct((n, 12 * DH), BF16),
                   jax.ShapeDtypeStruct((n, 8 * DH), BF16), jax.ShapeDtypeStruct((LANES, n), F32),
                   jax.ShapeDtypeStruct((n, d_rec), F32), jax.ShapeDtypeStruct((n, d_rec), F32)],
        compiler_params=_params(("parallel",)),
        name="inproj",
    )(x2, nw, wq, wkv, wg, wxy, qnw, knw, tail_sel, tail_win)


def _pool_combo(rows, wl_ref, combo):
    r = rows.shape[0] // (2 * L_CMP)
    x3 = rows.reshape(r, 2 * L_CMP, rows.shape[1])
    even = jnp.sum(x3 * wl_ref[0, combo][None], axis=1)
    odd = jnp.sum(x3 * wl_ref[1, combo][None], axis=1)
    return even.astype(BF16), odd.astype(BF16)


def _pool_body(x_ref, wl_ref, e_ref, o_ref):
    n_tok = x_ref.shape[0] // KV_ROWS
    for combo in range(KV_ROWS):
        even, odd = _pool_combo(x_ref[pl.ds(combo, n_tok, KV_ROWS), :], wl_ref, combo)
        e_ref[:, combo * DH:(combo + 1) * DH] = even
        o_ref[:, combo * DH:(combo + 1) * DH] = odd


def _pool_prompt(cmp4, wl):
    n = cmp4.shape[0] // KV_ROWS
    toks = 1024
    ob = toks // (2 * L_CMP)
    c = KV_ROWS * DH
    return pl.pallas_call(
        _pool_body,
        grid=(n // toks,),
        in_specs=[pl.BlockSpec((toks * KV_ROWS, DH), lambda i: (i, 0)), pl.BlockSpec(wl.shape, lambda i: (0, 0, 0, 0))],
        out_specs=[pl.BlockSpec((ob, c), lambda i: (i, 0)), pl.BlockSpec((ob, c), lambda i: (i, 0))],
        out_shape=[jax.ShapeDtypeStruct((n // (2 * L_CMP), c), BF16)] * 2,
        compiler_params=_params(("parallel",)),
        name="pool_prompt",
    )(cmp4, wl)


def _page_rows(pages, combo):
    n_tok = pages[0].shape[1] // KV_ROWS
    return jnp.concatenate([p[0, pl.ds(combo, n_tok, KV_ROWS), :] for p in pages], axis=0)


def _pool_pages_body(pt_ref, *refs):
    pages = refs[:PAGES_PER_STEP]
    wl_ref, e_ref, o_ref = refs[PAGES_PER_STEP:]
    for combo in range(KV_ROWS):
        even, odd = _pool_combo(_page_rows(pages, combo), wl_ref, combo)
        e_ref[0, :, combo * DH:(combo + 1) * DH] = even
        o_ref[0, :, combo * DH:(combo + 1) * DH] = odd


def _pool_pages(cache3, page_ids, bsz, wl):
    rows = cache3.shape[1]
    n_pages = page_ids.shape[0] // bsz
    n_steps = n_pages // PAGES_PER_STEP
    ob = PAGES_PER_STEP * (rows // KV_ROWS) // (2 * L_CMP)
    c = KV_ROWS * DH
    page_spec = lambda i: pl.BlockSpec((1, rows, DH), lambda b, s, pt: (pt[b * n_pages + s * PAGES_PER_STEP + i], 0, 0))
    grid_spec = pltpu.PrefetchScalarGridSpec(
        num_scalar_prefetch=1,
        grid=(bsz, n_steps),
        in_specs=[page_spec(i) for i in range(PAGES_PER_STEP)] + [pl.BlockSpec(wl.shape, lambda b, s, pt: (0, 0, 0, 0))],
        out_specs=[pl.BlockSpec((1, ob, c), lambda b, s, pt: (b, s, 0))] * 2,
    )
    return pl.pallas_call(
        _pool_pages_body,
        grid_spec=grid_spec,
        out_shape=[jax.ShapeDtypeStruct((bsz, n_steps * ob, c), BF16)] * 2,
        compiler_params=_params(("parallel", "parallel")),
        name="pool_pages",
    )(page_ids, *([cache3] * PAGES_PER_STEP), wl)


def _select_blocks_cols(score, blk, n_pick, n_blk):
    sel = jnp.zeros(score.shape, F32)
    for _ in range(n_pick):
        m = jnp.max(score, axis=0, keepdims=True)
        idx = jnp.min(jnp.where(score == m, blk, n_blk), axis=0, keepdims=True)
        hit = blk == idx
        sel = jnp.where(hit, 1.0, sel)
        score = jnp.where(hit, -jnp.inf, score)
    return sel


def _pattn_body(slopes_ref, q_ref, ksel_ref, vsel_ref, kwin_ref, vwin_ref, kce_ref, kco_ref, vce_ref, vco_ref,
                gt_ref, o_ref, m_scr, l_scr, acc_scr):
    g = pl.program_id(1)
    i = pl.program_id(2)
    rows = GQA * QB
    q = q_ref[0].reshape(rows, DH)
    col = _iota((1, rows), 1)
    qpos = i * QB + (col & (QB - 1))
    slope = jnp.zeros((1, rows), F32)
    slope_r = jnp.zeros((rows, 1), F32)
    for r in range(GQA):
        slope = jnp.where((col >> 7) == r, slopes_ref[g * GQA + r], slope)
        slope_r = jnp.where((_iota((rows, 1), 0) >> 7) == r, slopes_ref[g * GQA + r], slope_r)

    kc = jnp.concatenate([kce_ref[...], kco_ref[...]], axis=0)
    vc = jnp.concatenate([vce_ref[...], vco_ref[...]], axis=0)
    nb = kc.shape[0]
    half = nb // 2
    brow = _iota((nb, rows), 0)
    blk_c = jnp.where(brow < half, 2 * brow, 2 * (brow - half) + 1)
    dist = qpos - (blk_c * L_CMP + (L_CMP - 1))
    s = _dot_nt(kc, q) - slope * dist.astype(F32)
    p = _masked_softmax_cols(s, dist >= 0)
    o_c = _dot_tn(vc, p.astype(BF16))
    imp_t = p[:, 0:QB]
    for r in range(1, GQA):
        imp_t = imp_t + p[:, r * QB:(r + 1) * QB]
    pair = imp_t[:half] + imp_t[half:]
    ns = half
    blk = _iota((ns, QB), 0)
    qp = i * QB + _iota((ns, QB), 1)
    cur = qp >> 6
    forced = (blk == 0) | (blk == cur) | (blk == cur - 1)
    visible = blk * L_SEL <= qp
    score = jnp.where(visible, jnp.where(forced, FORCE, pair), NEG)
    sel = _select_blocks_cols(score, blk, min(N_SEL, ns), ns)

    unpicked = jnp.where(visible, 1.0 - sel, 1.0)
    unpicked = jnp.concatenate([unpicked, jnp.zeros((LANES - ns, QB), F32)], axis=0)
    tail = jnp.concatenate([unpicked.T] * GQA, axis=0)
    lane_t = _iota((rows, LANES), 1)
    q_first = (i * QB).astype(F32)
    tail = tail + jnp.where(lane_t == ns, slope_r * L_SEL,
                            jnp.where(lane_t == ns + 1, slope_r, jnp.where(lane_t == ns + 2, -slope_r * q_first, 0.0)))
    q_aug = jnp.concatenate([q, tail.astype(BF16)], axis=1)

    m_scr[...] = jnp.full(m_scr.shape, NEG, F32)
    l_scr[...] = jnp.zeros(l_scr.shape, F32)
    acc_scr[...] = jnp.zeros(acc_scr.shape, F32)
    ck = 512

    def attend(k0, causal):
        sc = _dot_nt(ksel_ref[pl.ds(k0, ck), :], q_aug)
        if causal:
            sc = jnp.where(qpos >= k0 + _iota((ck, rows), 0), sc, NEG)
        m_old = m_scr[...]
        m_new = jnp.maximum(m_old, jnp.max(sc, axis=0, keepdims=True))
        alpha = jnp.exp(m_old - m_new)
        e = jnp.exp(sc - m_new)
        l_scr[...] = alpha * l_scr[...] + jnp.sum(e, axis=0, keepdims=True)
        acc_scr[...] = alpha * acc_scr[...] + _dot_tn(vsel_ref[pl.ds(k0, ck), :], e.astype(BF16))
        m_scr[...] = m_new

    def chunk(c, carry):
        attend(pl.multiple_of(c * ck, ck), causal=False)
        return carry

    lax.fori_loop(0, i >> 2, chunk, 0)
    attend(pl.multiple_of((i >> 2) * ck, ck), causal=True)
    o_s = acc_scr[...] / jnp.maximum(l_scr[...], 1e-30)

    span = WINDOW + QB
    start = pl.multiple_of(jnp.maximum(i * QB - WINDOW, 0), QB)
    vw = vwin_ref[pl.ds(start, span), :]
    dist_w = qpos - (start + _iota((span, rows), 0))
    mask_w = jnp.where(dist_w >= 0, dist_w, WINDOW) < WINDOW
    s_w = _dot_nt(kwin_ref[pl.ds(start, span), :], q_aug)
    o_w = _dot_tn(vw, _masked_softmax_cols(s_w, mask_w).astype(BF16))

    gt = gt_ref[pl.ds(pl.multiple_of(g * GATE_ROWS, GATE_ROWS), GATE_ROWS), :]
    for r in range(GQA):
        sl = slice(r * QB, (r + 1) * QB)
        o_t = (gt[3 * r:3 * r + 1, :] * o_c[:, sl] + gt[3 * r + 1:3 * r + 2, :] * o_s[:, sl]
               + gt[3 * r + 2:3 * r + 3, :] * o_w[:, sl])
        o_ref[0, r] = o_t.T


MASK_BIAS = -(2.0 ** 100)


def _key_tails(t_len):
    ns = t_len // L_SEL
    pos = jnp.arange(t_len, dtype=I32)[:, None]
    lane = jnp.arange(LANES, dtype=I32)[None, :]
    feat = jnp.where(lane == ns, pos >> 6, jnp.where(lane == ns + 1, pos & (L_SEL - 1), jnp.where(lane == ns + 2, 1, 0)))
    feat = feat.astype(F32)
    return (jnp.where(lane == (pos >> 6), MASK_BIAS, 0.0) + feat).astype(BF16), feat.astype(BF16)


def _prompt_attention(slopes, q_blk, kvb, kaug, kce, kco, gates_g, bsz, t_len):
    nq = t_len // QB
    nb = t_len // L_CMP
    assert nb % (2 * LANES) == 0 or nb == LANES, "compressed blocks must fill whole lane tiles"
    assert t_len >= WINDOW + QB and t_len // L_SEL + 3 <= LANES
    half = nb // 2
    rows = GQA * QB
    kv_spec = lambda col: pl.BlockSpec((t_len, DH), lambda b, g, i, col=col: (b, col + g))
    ka_spec = lambda col: pl.BlockSpec((t_len, 2 * DH), lambda b, g, i, col=col: (b, col + g))
    kc_spec = lambda col: pl.BlockSpec((half, DH), lambda b, g, i, col=col: (b, col + g))
    return pl.pallas_call(
        _pattn_body,
        grid=(bsz, N_KV, nq),
        in_specs=[pl.BlockSpec(memory_space=pltpu.SMEM),
                  pl.BlockSpec((1, GQA, QB, DH), lambda b, g, i: (b * nq + i, g, 0, 0)),
                  ka_spec(0), kv_spec(6), ka_spec(N_KV), kv_spec(10),
                  kc_spec(0), kc_spec(0), kc_spec(2), kc_spec(2),
                  pl.BlockSpec((LANES, QB), lambda b, g, i: (0, b * nq + i))],
        out_specs=pl.BlockSpec((1, GQA, QB, DH), lambda b, g, i: (b * nq + i, g, 0, 0)),
        out_shape=jax.ShapeDtypeStruct((bsz * nq, N_HEADS, QB, DH), F32),
        scratch_shapes=[pltpu.VMEM((1, rows), F32), pltpu.VMEM((1, rows), F32), pltpu.VMEM((DH, rows), F32)],
        compiler_params=_params(("parallel", "parallel", "arbitrary")),
        name="prompt_attention",
    )(slopes, q_blk, kaug, kvb, kaug, kvb, kce, kco, kce, kco, gates_g)


def _sample_select_body(slopes_ref, q_ref, kce_ref, kco_ref, oc_ref, sel_ref, *, past):
    rows = GQA * TP
    row = _iota((rows, 1), 0)
    qpos = past + (row & (TP - 1))
    r_of_row = row >> 3
    half = kce_ref.shape[1]
    nb = 2 * half
    n_b = q_ref.shape[0]
    lane = _iota((rows, nb), 1)
    blk_c = jnp.where(lane < half, 2 * lane, 2 * (lane - half) + 1)
    dist = qpos - (blk_c * L_CMP + (L_CMP - 1))
    pairs = []
    for b in range(n_b):
        for g in range(N_KV):
            slope = jnp.zeros((rows, 1), F32)
            for r in range(GQA):
                slope = jnp.where(r_of_row == r, slopes_ref[g * GQA + r], slope)
            q = q_ref[b, g]
            kc = jnp.concatenate([kce_ref[b, :, g * DH:(g + 1) * DH], kco_ref[b, :, g * DH:(g + 1) * DH]], axis=0)
            vc = jnp.concatenate([kce_ref[b, :, (2 + g) * DH:(3 + g) * DH], kco_ref[b, :, (2 + g) * DH:(3 + g) * DH]],
                                 axis=0)
            s = _dot_nt(q, kc) - slope * dist.astype(F32)
            p = _masked_softmax_rows(s, dist >= 0)
            oc_ref[b, g] = _dot(p.astype(BF16), vc)
            imp = p[0:TP]
            for r in range(1, GQA):
                imp = imp + p[r * TP:(r + 1) * TP]
            pairs.append(imp[:, :half] + imp[:, half:])
    pair = jnp.concatenate(pairs, axis=0)
    n_rows = pair.shape[0]
    blk = _iota((n_rows, half), 1)
    qp = past + (_iota((n_rows, half), 0) & (TP - 1))
    cur = qp >> 6
    forced = (blk == 0) | (blk == cur) | (blk == cur - 1)
    score = jnp.where(blk * L_SEL <= qp, jnp.where(forced, FORCE, pair), NEG)
    sel = jnp.zeros((n_rows, half), F32)
    for _ in range(N_SEL - 1):
        m = jnp.max(score, axis=1, keepdims=True)
        idx = jnp.min(jnp.where(score == m, blk, half), axis=1, keepdims=True)
        hit = blk == idx
        sel = jnp.where(hit, 1.0, sel)
        score = jnp.where(hit, -jnp.inf, score)
    sel = sel.astype(BF16)
    for b in range(n_b):
        for g in range(N_KV):
            i0 = (b * N_KV + g) * TP
            sel_ref[b, g] = jnp.concatenate([sel[i0:i0 + TP]] * GQA, axis=0)


SELECT_BATCH = 8


def _sample_select(slopes, q_s, kce, kco, past):
    bsz = q_s.shape[0]
    half = kce.shape[1]
    assert half == LANES, "past selection blocks must fill one lane tile"
    rows = GQA * TP
    nbt = SELECT_BATCH
    return pl.pallas_call(
        functools.partial(_sample_select_body, past=past),
        grid=(bsz // nbt,),
        in_specs=[pl.BlockSpec(memory_space=pltpu.SMEM),
                  pl.BlockSpec((nbt, N_KV, rows, DH), lambda b: (b, 0, 0, 0)),
                  pl.BlockSpec((nbt, half, 4 * DH), lambda b: (b, 0, 0)),
                  pl.BlockSpec((nbt, half, 4 * DH), lambda b: (b, 0, 0))],
        out_specs=[pl.BlockSpec((nbt, N_KV, rows, DH), lambda b: (b, 0, 0, 0)),
                   pl.BlockSpec((nbt, N_KV, rows, half), lambda b: (b, 0, 0, 0))],
        out_shape=[jax.ShapeDtypeStruct((bsz, N_KV, rows, DH), F32),
                   jax.ShapeDtypeStruct((bsz, N_KV, rows, half), BF16)],
        compiler_params=_params(("parallel",)),
        name="sample_select",
    )(slopes, q_s, kce, kco)


def _online_update(m_ref, l_ref, acc_ref, g, sc, msk, v):
    m_old = m_ref[g]
    m_new = jnp.maximum(m_old, jnp.max(sc, axis=-1, keepdims=True))
    alpha = jnp.exp(m_old - m_new)
    e = jnp.where(msk, jnp.exp(sc - m_new), 0.0)
    l_ref[g] = alpha * l_ref[g] + jnp.sum(e, axis=-1, keepdims=True)
    acc_ref[g] = alpha * acc_ref[g] + _dot(e.astype(BF16), v)
    m_ref[g] = m_new


def _sample_attn_body(pt_ref, slopes_ref, *refs, past, t_new):
    pages = refs[:PAGES_PER_STEP]
    q_ref, sel_ref, new_ref, win_ref, oc_ref, gt_ref, o_ref, m_scr, l_scr, acc_scr = refs[PAGES_PER_STEP:]
    c = pl.program_id(1)
    rows = GQA * TP
    row = _iota((rows, 1), 0)
    qpos = past + (row & (TP - 1))
    r_of_row = row >> 3
    ck = PAGES_PER_STEP * pages[0].shape[1] // KV_ROWS

    @pl.when(c == 0)
    def _():
        m_scr[...] = jnp.full(m_scr.shape, NEG, F32)
        l_scr[...] = jnp.zeros(l_scr.shape, F32)
        acc_scr[...] = jnp.zeros(acc_scr.shape, F32)

    dist_s = qpos - (c * ck + _iota((rows, ck), 1))
    expand = jnp.where((_iota((LANES, ck), 1) >> 6) + c * (ck // L_SEL) == _iota((LANES, ck), 0), 1.0, 0.0).astype(BF16)
    slopes = []
    for g in range(N_KV):
        slope = jnp.zeros((rows, 1), F32)
        for r in range(GQA):
            slope = jnp.where(r_of_row == r, slopes_ref[g * GQA + r], slope)
        slopes.append(slope)
        q = q_ref[0, g]
        kch = _page_rows(pages, g).astype(BF16)
        vch = _page_rows(pages, 2 + g).astype(BF16)
        picked = _dot(sel_ref[0, g], expand)
        msk = jnp.where(dist_s >= 0, picked, 0.0) > 0.5
        sc = jnp.where(msk, _dot_nt(q, kch) - slope * dist_s.astype(F32), NEG)
        _online_update(m_scr, l_scr, acc_scr, g, sc, msk, vch)

    @pl.when(c == pl.num_programs(1) - 1)
    def _():
        col = _iota((rows, TP), 1)
        dist_n = qpos - (past + col)
        mask_n = jnp.where(col < t_new, dist_n, -1) >= 0
        dist_c = qpos - (past - WINDOW + _iota((rows, WINDOW), 1))
        mask_c = jnp.where(dist_c >= 0, dist_c, WINDOW) < WINDOW
        mask_wn = jnp.where(mask_n, dist_n, WINDOW) < WINDOW
        for g in range(N_KV):
            slope = slopes[g]
            q = q_ref[0, g]
            kn = new_ref[0, :, (4 + g) * DH:(5 + g) * DH]
            vn = new_ref[0, :, (6 + g) * DH:(7 + g) * DH]
            sc = jnp.where(mask_n, _dot_nt(q, kn) - slope * dist_n.astype(F32), NEG)
            _online_update(m_scr, l_scr, acc_scr, g, sc, mask_n, vn)
            o_s = acc_scr[g] / jnp.maximum(l_scr[g], 1e-30)
            kwc = win_ref[0, pl.ds(g, WINDOW, KV_ROWS), :].astype(BF16)
            vwc = win_ref[0, pl.ds(2 + g, WINDOW, KV_ROWS), :].astype(BF16)
            kwn = new_ref[0, :, (8 + g) * DH:(9 + g) * DH]
            vwn = new_ref[0, :, (10 + g) * DH:(11 + g) * DH]
            s1 = jnp.where(mask_c, _dot_nt(q, kwc) - slope * dist_c.astype(F32), NEG)
            s2 = jnp.where(mask_wn, _dot_nt(q, kwn) - slope * dist_n.astype(F32), NEG)
            mx = jnp.maximum(jnp.max(s1, axis=-1, keepdims=True), jnp.max(s2, axis=-1, keepdims=True))
            e1 = jnp.where(mask_c, jnp.exp(s1 - mx), 0.0)
            e2 = jnp.where(mask_wn, jnp.exp(s2 - mx), 0.0)
            den = jnp.maximum(jnp.sum(e1, axis=-1, keepdims=True) + jnp.sum(e2, axis=-1, keepdims=True), 1e-30)
            o_w = _dot((e1 / den).astype(BF16), vwc) + _dot((e2 / den).astype(BF16), vwn)
            gt = gt_ref[0, g]
            o_ref[0, g] = gt[:, 0:1] * oc_ref[0, g] + gt[:, 1:2] * o_s + gt[:, 2:3] * o_w


def _sample_attention(slopes, page_ids, cache_sel3, q_s, sel_s, new_kvb, cache_win3, win_base, o_c, gates_s, past,
                      t_new):
    page_rows = cache_sel3.shape[1]
    bsz = q_s.shape[0]
    n_pages = page_ids.shape[0] // bsz
    n_steps = n_pages // PAGES_PER_STEP
    rows = GQA * TP
    assert cache_win3.shape[1] == WINDOW * KV_ROWS and past % L_SEL == 0 and t_new <= TP
    page_spec = lambda i: pl.BlockSpec((1, page_rows, DH),
                                       lambda b, s, pt: (pt[b * n_pages + s * PAGES_PER_STEP + i], 0, 0))
    per_b = lambda shape: pl.BlockSpec((1,) + shape, lambda b, s, pt: (b,) + (0,) * len(shape))
    grid_spec = pltpu.PrefetchScalarGridSpec(
        num_scalar_prefetch=1,
        grid=(bsz, n_steps),
        in_specs=[pl.BlockSpec(memory_space=pltpu.SMEM)] + [page_spec(i) for i in range(PAGES_PER_STEP)]
        + [per_b((N_KV, rows, DH)), per_b((N_KV, rows, LANES)), per_b((TP, 12 * DH)),
           pl.BlockSpec((1, WINDOW * KV_ROWS, DH), lambda b, s, pt: (win_base + b, 0, 0)),
           per_b((N_KV, rows, DH)), per_b((N_KV, rows, 3))],
        out_specs=per_b((N_KV, rows, DH)),
        scratch_shapes=[pltpu.VMEM((N_KV, rows, 1), F32), pltpu.VMEM((N_KV, rows, 1), F32),
                        pltpu.VMEM((N_KV, rows, DH), F32)],
    )
    return pl.pallas_call(
        functools.partial(_sample_attn_body, past=past, t_new=t_new),
        grid_spec=grid_spec,
        out_shape=jax.ShapeDtypeStruct((bsz, N_KV, rows, DH), F32),
        compiler_params=_params(("parallel", "arbitrary")),
        name="sample_attention",
    )(page_ids, slopes, *([cache_sel3] * PAGES_PER_STEP), q_s, sel_s, new_kvb, cache_win3, o_c, gates_s)


def _gelu_tanh(x):
    return 0.5 * x * (1.0 + jnp.tanh(0.7978845608028654 * (x + 0.044715 * (x * x * x))))


def _lru_coeffs(conv, wa_ref, ba_ref, wi_ref, bi_ref, sp_ref):
    cb = conv.astype(BF16)
    n_blk, blk = wa_ref.shape[0], wa_ref.shape[1]
    ra = jnp.concatenate([_dot(cb[:, n * blk:(n + 1) * blk], wa_ref[n]) for n in range(n_blk)], axis=1)
    ri = jnp.concatenate([_dot(cb[:, n * blk:(n + 1) * blk], wi_ref[n]) for n in range(n_blk)], axis=1)
    r = jax.nn.sigmoid(ra + ba_ref[...])
    gi = jax.nn.sigmoid(ri + bi_ref[...])
    log_a = -LRU_C * r * sp_ref[...]
    a = jnp.exp(log_a)
    b = jnp.sqrt(-jnp.tanh(log_a) * (a * a + 1.0)) * (gi * conv)
    return a, b


def _lru_seq_body(xr_ref, yr_ref, cs_ref, h0_ref, cw_ref, cb_ref, wa_ref, ba_ref, wi_ref, bi_ref, sp_ref, onw_ref,
                  rec_ref, hl_ref, xbuf, h_scr):
    k = pl.program_id(1)
    tt = xr_ref.shape[1]
    pad = 8

    @pl.when(k == 0)
    def _():
        xbuf[0:pad, :] = jnp.zeros((pad, xbuf.shape[1]), F32)
        xbuf[pad - (CONV_W - 1):pad, :] = cs_ref[0]
        h_scr[...] = h0_ref[0]

    x = xr_ref[0]
    xbuf[pad:pad + tt, :] = x
    conv = cb_ref[...] + cw_ref[CONV_W - 1:CONV_W, :] * x
    for j in range(CONV_W - 1):
        conv = conv + cw_ref[j:j + 1, :] * xbuf[pad - (CONV_W - 1) + j:pad - (CONV_W - 1) + j + tt, :]
    xbuf[0:pad, :] = x[tt - pad:tt]
    a, b = _lru_coeffs(conv, wa_ref, ba_ref, wi_ref, bi_ref, sp_ref)
    row = _iota((tt, 1), 0)
    s = 1
    while s < tt:
        keep = row >= s
        a_sh = jnp.where(keep, pltpu.roll(a, s, 0), 1.0)
        b_sh = jnp.where(keep, pltpu.roll(b, s, 0), 0.0)
        b = a * b_sh + b
        a = a * a_sh
        s *= 2
    hs = a * h_scr[...] + b
    h_scr[...] = hs[tt - 1:tt]
    hl_ref[0] = hs[tt - 1:tt]
    rec = hs * _gelu_tanh(yr_ref[0])
    rec_ref[0] = _rms(rec, onw_ref[...]).astype(BF16)


def _lru_seq(xr3, yr3, cs, h0, cw, cb, wa, ba, wi, bi, sp, onw, tt):
    bsz, t_len, d = xr3.shape
    seq = pl.BlockSpec((1, tt, d), lambda b, k: (b, k, 0))
    full = lambda a: pl.BlockSpec(a.shape, lambda b, k: (0,) * a.ndim)
    return pl.pallas_call(
        _lru_seq_body,
        grid=(bsz, t_len // tt),
        in_specs=[seq, seq, pl.BlockSpec((1, CONV_W - 1, d), lambda b, k: (b, 0, 0)),
                  pl.BlockSpec((1, 1, d), lambda b, k: (b, 0, 0)),
                  full(cw), full(cb), full(wa), full(ba), full(wi), full(bi), full(sp), full(onw)],
        out_specs=[seq, pl.BlockSpec((1, 1, d), lambda b, k: (b, 0, 0))],
        out_shape=[jax.ShapeDtypeStruct((bsz, t_len, d), BF16), jax.ShapeDtypeStruct((bsz, 1, d), F32)],
        scratch_shapes=[pltpu.VMEM((tt + 8, d), F32), pltpu.VMEM((1, d), F32)],
        compiler_params=_params(("parallel", "arbitrary")),
        name="lru_seq",
    )(xr3, yr3, cs, h0, cw, cb, wa, ba, wi, bi, sp, onw)


def _lru_step_body(xr_ref, yr_ref, cs_ref, h0_ref, cw_ref, cb_ref, wa_ref, ba_ref, wi_ref, bi_ref, sp_ref, onw_ref,
                   rec_ref, hl_ref):
    t_len, bsz = xr_ref.shape[0], xr_ref.shape[1]
    xs = [cs_ref[j] for j in range(CONV_W - 1)] + [xr_ref[t] for t in range(t_len)]
    convs = []
    for t in range(t_len):
        conv = cb_ref[...] + cw_ref[0:1, :] * xs[t]
        for j in range(1, CONV_W):
            conv = conv + cw_ref[j:j + 1, :] * xs[t + j]
        convs.append(conv)
    a, b = _lru_coeffs(jnp.concatenate(convs, axis=0), wa_ref, ba_ref, wi_ref, bi_ref, sp_ref)
    h = h0_ref[...]
    for t in range(t_len):
        h = a[t * bsz:(t + 1) * bsz] * h + b[t * bsz:(t + 1) * bsz]
        rec = h * _gelu_tanh(yr_ref[t])
        rec_ref[t] = _rms(rec, onw_ref[...]).astype(BF16)
    hl_ref[...] = h


def _lru_step(xr_t, yr_t, cs_t, h0, cw, cb, wa, ba, wi, bi, sp, onw):
    t_len, bsz, d = xr_t.shape
    return pl.pallas_call(
        _lru_step_body,
        out_shape=[jax.ShapeDtypeStruct((t_len, bsz, d), BF16), jax.ShapeDtypeStruct((bsz, d), F32)],
        compiler_params=pltpu.CompilerParams(vmem_limit_bytes=VMEM_LIMIT),
        name="lru_step",
    )(xr_t, yr_t, cs_t, h0, cw, cb, wa, ba, wi, bi, sp, onw)


def _outproj_body(attn_ref, rec_ref, x_ref, cnt0_ref, anw_ref, woa_ref, wor_ref, fnw_ref, wr_ref, br_ref,
                  x1_ref, h2_ref, ti_ref, tg_ref, tp_ref, cnt_ref, carry):
    step = pl.program_id(0)
    tm = x_ref.shape[0]

    @pl.when(step == 0)
    def _():
        carry[...] = cnt0_ref[...]

    attn = jnp.concatenate(
        [jnp.concatenate([attn_ref[b, hd] for hd in range(N_HEADS)], axis=1) for b in range(attn_ref.shape[0])], axis=0)
    an = _rms(attn, anw_ref[...]).astype(BF16)
    x1 = x_ref[...] + (_dot(an, woa_ref[...]) + _dot(rec_ref[...], wor_ref[...]))
    x1_ref[...] = x1
    h2 = _rms(x1, fnw_ref[...])
    dc = h2.shape[1] // LANES
    for c in range(dc):
        h2_ref[pl.ds(c, tm, dc), :] = h2[:, c * LANES:(c + 1) * LANES]
    lane = _iota((tm, LANES), 1)
    h_hi = h2.astype(BF16)
    h_lo = (h2 - h_hi.astype(F32)).astype(BF16)
    part_hi = _dot(h_hi, wr_ref[...])
    part_lo = _dot(h_lo, wr_ref[...])
    logits = part_hi[:, :LANES] + part_hi[:, LANES:] + part_lo[:, :LANES] + br_ref[...]
    lg = jnp.where(lane < N_EXPERTS, logits, -jnp.inf)
    vals, idxs = [], []
    for _ in range(TOP_K):
        m = jnp.max(lg, axis=-1, keepdims=True)
        ix = jnp.min(jnp.where(lg == m, lane, LANES), axis=-1, keepdims=True)
        vals.append(m)
        idxs.append(ix)
        lg = jnp.where(lane == ix, -jnp.inf, lg)
    es = [jnp.exp(v - vals[0]) for v in vals]
    den = es[0]
    for e in es[1:]:
        den = den + e
    onehot = jnp.zeros((tm, LANES), F32)
    for ix in idxs:
        onehot = jnp.where(lane == ix, 1.0, onehot)
    lower = jnp.where(_iota((tm, tm), 0) > _iota((tm, tm), 1), 1.0, 0.0).astype(BF16)
    rank = carry[...] + _dot(lower, onehot.astype(BF16))
    carry[...] = carry[...] + jnp.sum(onehot, axis=0, keepdims=True)
    ti = jnp.zeros((tm, LANES), I32)
    tg = jnp.zeros((tm, LANES), F32)
    tp = jnp.zeros((tm, LANES), I32)
    for k in range(TOP_K):
        pos = jnp.sum(jnp.where(lane == idxs[k], rank, 0.0), axis=-1, keepdims=True).astype(I32)
        ti = jnp.where(lane == k, idxs[k], ti)
        tg = jnp.where(lane == k, es[k] / den, tg)
        tp = jnp.where(lane == k, pos, tp)
    ti_ref[...] = ti
    tg_ref[...] = tg
    tp_ref[...] = tp
    cnt_ref[...] = carry[...]


def _outproj_router(attn_blk, recn, x2, cnt0, anw, woa, wor, fnw, wr, br, tm):
    n, d = x2.shape
    nqb = tm // QB
    row = lambda i: (i, 0)
    wspec = lambda a: pl.BlockSpec(a.shape, lambda i: (0,) * a.ndim, pipeline_mode=pl.Buffered(1))
    lanes_out = lambda dt: jax.ShapeDtypeStruct((n, LANES), dt)
    return pl.pallas_call(
        _outproj_body,
        grid=(n // tm,),
        in_specs=[pl.BlockSpec((nqb, N_HEADS, QB, DH), lambda i: (i, 0, 0, 0)),
                  pl.BlockSpec((tm, recn.shape[1]), row), pl.BlockSpec((tm, d), row),
                  wspec(cnt0), wspec(anw), wspec(woa), wspec(wor), wspec(fnw), wspec(wr), wspec(br)],
        out_specs=[pl.BlockSpec((tm, d), row), pl.BlockSpec((tm * (d // LANES), LANES), row),
                   pl.BlockSpec((tm, LANES), row),
                   pl.BlockSpec((tm, LANES), row), pl.BlockSpec((tm, LANES), row), pl.BlockSpec((1, LANES), lambda i: (0, 0))],
        out_shape=[jax.ShapeDtypeStruct((n, d), F32), jax.ShapeDtypeStruct((n * (d // LANES), LANES), F32),
                   lanes_out(I32), lanes_out(F32), lanes_out(I32), jax.ShapeDtypeStruct((1, LANES), F32)],
        scratch_shapes=[pltpu.VMEM((1, LANES), F32)],
        compiler_params=_params(("arbitrary",)),
        name="outproj_router",
    )(attn_blk, recn, x2, cnt0, anw, woa, wor, fnw, wr, br)


ISSUE_UNROLL = 8


def _gather_body(rowtok_ref, nused_ref, h2_hbm, out_ref, buf, sem):
    j = pl.program_id(0)
    tm = out_ref.shape[0]
    dc = buf.shape[1] // tm
    n_used = nused_ref[0]

    def issue_block(blk, slot):
        def body(it, c):
            for u in range(ISSUE_UNROLL):
                r = it * ISSUE_UNROLL + u
                src = pl.multiple_of(rowtok_ref[blk * tm + r] * dc, dc)
                pltpu.make_async_copy(h2_hbm.at[pl.ds(src, dc), :], buf.at[slot, pl.ds(r * dc, dc), :],
                                      sem.at[slot]).start(priority=u % 2)
            return c

        lax.fori_loop(0, tm // ISSUE_UNROLL, body, 0)

    @pl.when((j == 0) & (n_used > 0))
    def _():
        issue_block(0, 0)

    @pl.when(j + 1 < n_used)
    def _():
        issue_block(j + 1, (j + 1) % 2)

    @pl.when(j < n_used)
    def _():
        slot = j % 2
        pltpu.make_async_copy(h2_hbm.at[pl.ds(0, tm * dc), :], buf.at[slot], sem.at[slot]).wait()
        for c in range(dc):
            out_ref[:, c * LANES:(c + 1) * LANES] = buf[slot, pl.ds(c, tm, dc), :].astype(BF16)

    @pl.when(j >= n_used)
    def _():
        out_ref[...] = jnp.zeros(out_ref.shape, BF16)


def _gather_rows(row_tok, n_used, h2, n_blocks, d):
    tm = MOE_TM
    dc = d // LANES
    grid_spec = pltpu.PrefetchScalarGridSpec(
        num_scalar_prefetch=2,
        grid=(n_blocks,),
        in_specs=[pl.BlockSpec(memory_space=pl.ANY)],
        out_specs=pl.BlockSpec((tm, d), lambda j, rt, nu: (j, 0)),
        scratch_shapes=[pltpu.VMEM((2, tm * dc, LANES), F32), pltpu.SemaphoreType.DMA((2,))],
    )
    return pl.pallas_call(
        _gather_body,
        grid_spec=grid_spec,
        out_shape=jax.ShapeDtypeStruct((n_blocks * tm, d), BF16),
        compiler_params=_params(("arbitrary",)),
        name="moe_gather",
    )(row_tok, n_used, h2)


COPY_SPLIT = 4


def _expert_blocks(first_row, n_blk, n_used, n_blocks, x_hbm, out_rows, xbuf, obuf, xsem, osem, prepare, compute):
    tm = xbuf.shape[1]
    rc = tm // COPY_SPLIT
    orc = obuf.shape[1] // COPY_SPLIT

    def x_copies(b, slot):
        r = pl.multiple_of(first_row + b * tm, tm)
        return [pltpu.make_async_copy(x_hbm.at[pl.ds(r + s * rc, rc), :], xbuf.at[slot, pl.ds(s * rc, rc), :],
                                      xsem.at[slot]) for s in range(COPY_SPLIT)]

    def o_copies(r, slot):
        r = pl.multiple_of(r, tm)
        return [pltpu.make_async_copy(obuf.at[slot, pl.ds(s * orc, orc), :], out_rows(r + s * rc, rc), osem.at[slot])
                for s in range(COPY_SPLIT)]

    def start(copies):
        for cp in copies:
            cp.start(priority=1)

    def wait(copies):
        for cp in copies:
            cp.wait()

    @pl.when(n_blk > 0)
    def _():
        start(x_copies(0, 0))
        prepare()

        def body(b, c):
            slot = b % 2

            @pl.when(b + 1 < n_blk)
            def _():
                start(x_copies(b + 1, 1 - slot))

            wait(x_copies(b, slot))

            @pl.when(b >= 2)
            def _():
                wait(o_copies(first_row + (b - 2) * tm, slot))

            compute(xbuf[slot], slot)
            start(o_copies(first_row + b * tm, slot))
            return c

        lax.fori_loop(0, n_blk, body, 0)

        @pl.when(n_blk >= 2)
        def _():
            wait(o_copies(first_row + (n_blk - 2) * tm, n_blk % 2))

        wait(o_copies(first_row + (n_blk - 1) * tm, (n_blk - 1) % 2))

    @pl.when(pl.program_id(0) == pl.num_programs(0) - 1)
    def _():
        obuf[0] = jnp.zeros(obuf.shape[1:], obuf.dtype)

        def fill(t, c):
            start(o_copies(t * tm, 0))
            wait(o_copies(t * tm, 0))
            return c

        lax.fori_loop(n_used, n_blocks, fill, 0)


W_SPLIT = 4


def _moe_up_body(row0_ref, nblk_ref, nu_ref, xs_hbm, *refs, n_blocks):
    w_refs = refs[:W_SPLIT]
    b_ref, perm_ref, h_hbm, wbf, xbuf, obuf, xsem, osem = refs[W_SPLIT:]
    e = pl.program_id(0)
    tn = wbf.shape[1]
    pw = perm_ref.shape[0]
    hw = pw // 2
    kc = w_refs[0].shape[1]

    def prepare():
        for q, w_ref in enumerate(w_refs):
            for c in range(tn // pw):
                w = w_ref[0, :, c * pw:(c + 1) * pw].astype(BF16)
                wbf[q * kc:(q + 1) * kc, c * pw:(c + 1) * pw] = _dot(w, perm_ref[...]).astype(BF16)

    def compute(x, slot):
        gu = _dot(x, wbf[...]) + b_ref[0]
        for c in range(tn // pw):
            glu = jnp.minimum(gu[:, c * pw:c * pw + hw], SWIGLU_LIMIT)
            lin = jnp.clip(gu[:, c * pw + hw:(c + 1) * pw], -SWIGLU_LIMIT, SWIGLU_LIMIT)
            obuf[slot, :, c * hw:(c + 1) * hw] = (glu * jax.nn.sigmoid(SWIGLU_ALPHA * glu) * (lin + 1.0)).astype(BF16)

    col0 = pl.multiple_of(pl.program_id(1) * (tn // 2), tn // 2)
    out_rows = lambda r, n: h_hbm.at[pl.ds(r, n), pl.ds(col0, tn // 2)]
    _expert_blocks(row0_ref[e], nblk_ref[e], nu_ref[0], n_blocks, xs_hbm, out_rows, xbuf, obuf, xsem, osem,
                   prepare, compute)


def _w_specs(k, tn):
    return [pl.BlockSpec((1, k // W_SPLIT, tn), lambda e, n, *_, q=q: (e, q, n)) for q in range(W_SPLIT)]


def _moe_up(row0, nblk, n_used, xs, w_gu, b_gu, perm):
    n_rows, d = xs.shape
    n_e, _, f2 = w_gu.shape
    tm, tn = MOE_TM, MOE_TN
    grid_spec = pltpu.PrefetchScalarGridSpec(
        num_scalar_prefetch=3,
        grid=(n_e, f2 // tn),
        in_specs=[pl.BlockSpec(memory_space=pl.ANY)] + _w_specs(d, tn)
        + [pl.BlockSpec((1, 1, tn), lambda e, n, *_: (e, 0, n)),
           pl.BlockSpec(perm.shape, lambda e, n, *_: (0, 0))],
        out_specs=pl.BlockSpec(memory_space=pl.ANY),
        scratch_shapes=[pltpu.VMEM((d, tn), BF16), pltpu.VMEM((2, tm, d), BF16), pltpu.VMEM((2, tm, tn // 2), BF16),
                        pltpu.SemaphoreType.DMA((2,)), pltpu.SemaphoreType.DMA((2,))],
    )
    return pl.pallas_call(
        functools.partial(_moe_up_body, n_blocks=n_rows // tm),
        grid_spec=grid_spec,
        out_shape=jax.ShapeDtypeStruct((n_rows, f2 // 2), BF16),
        compiler_params=_params(("arbitrary", "arbitrary")),
        name="moe_up",
    )(row0, nblk, n_used, xs, *([w_gu] * W_SPLIT), b_gu, perm)


def _moe_down_body(row0_ref, nblk_ref, nu_ref, h_hbm, *refs, n_blocks):
    w_refs = refs[:W_SPLIT]
    b_ref, y_hbm, wbf, xbuf, obuf, xsem, osem = refs[W_SPLIT:]
    e = pl.program_id(0)
    tn = wbf.shape[1]
    kc = w_refs[0].shape[1]

    def prepare():
        for q, w_ref in enumerate(w_refs):
            wbf[q * kc:(q + 1) * kc, :] = w_ref[0].astype(BF16)

    dc = tn // LANES
    tm = xbuf.shape[1]

    def compute(h, slot):
        y = _dot(h, wbf[...]) + b_ref[0]
        for c in range(dc):
            obuf[slot, pl.ds(c, tm, dc), :] = y[:, c * LANES:(c + 1) * LANES]

    out_rows = lambda r, n: y_hbm.at[pl.ds(pl.multiple_of(r * dc, dc), n * dc), :]
    _expert_blocks(row0_ref[e], nblk_ref[e], nu_ref[0], n_blocks, h_hbm, out_rows, xbuf, obuf, xsem, osem,
                   prepare, compute)


def _moe_down(row0, nblk, n_used, h, w_d, b_d):
    n_rows, f = h.shape
    n_e, _, d = w_d.shape
    tm, tn = MOE_TM, MOE_TN
    assert tn == d, "a grid step writes whole output rows"
    grid_spec = pltpu.PrefetchScalarGridSpec(
        num_scalar_prefetch=3,
        grid=(n_e, d // tn),
        in_specs=[pl.BlockSpec(memory_space=pl.ANY)] + _w_specs(f, tn)
        + [pl.BlockSpec((1, 1, tn), lambda e, n, *_: (e, 0, n))],
        out_specs=pl.BlockSpec(memory_space=pl.ANY),
        scratch_shapes=[pltpu.VMEM((f, tn), BF16), pltpu.VMEM((2, tm, f), BF16),
                        pltpu.VMEM((2, tm * (d // LANES), LANES), F32),
                        pltpu.SemaphoreType.DMA((2,)), pltpu.SemaphoreType.DMA((2,))],
    )
    return pl.pallas_call(
        functools.partial(_moe_down_body, n_blocks=n_rows // tm),
        grid_spec=grid_spec,
        out_shape=jax.ShapeDtypeStruct((n_rows * (d // LANES), LANES), F32),
        compiler_params=_params(("arbitrary", "arbitrary")),
        name="moe_down",
    )(row0, nblk, n_used, h, *([w_d] * W_SPLIT), b_d)


def _combine_body(dest_hbm, gate_ref, x1p_ref, x1s_ref, y_hbm, outp_ref, outs_ref, dsm, buf, sem_d, sem, *, np_blocks):
    j = pl.program_id(0)
    n = pl.num_programs(0)
    tk = x1p_ref.shape[0]
    dc = x1p_ref.shape[1] // LANES
    toks = ISSUE_UNROLL // TOP_K

    def idx_copy(step, slot):
        return pltpu.make_async_copy(dest_hbm.at[step, 0], dsm.at[slot], sem_d.at[slot])

    def issue_rows(slot):
        def body(it, c):
            for u in range(toks):
                t = it * toks + u
                for k in range(TOP_K):
                    src = pl.multiple_of(dsm[slot, t * TOP_K + k] * dc, dc)
                    pltpu.make_async_copy(y_hbm.at[pl.ds(src, dc), :], buf.at[slot, k, pl.ds(t * dc, dc), :],
                                          sem.at[slot]).start(priority=k % 2)
            return c

        lax.fori_loop(0, tk // toks, body, 0)

    @pl.when(j == 0)
    def _():
        idx_copy(0, 0).start()
        idx_copy(0, 0).wait()
        issue_rows(0)

        @pl.when(1 < n)
        def _():
            idx_copy(1, 1).start()

    @pl.when(j + 1 < n)
    def _():
        slot = (j + 1) % 2
        idx_copy(j + 1, slot).wait()
        issue_rows(slot)

    @pl.when(j + 2 < n)
    def _():
        idx_copy(j + 2, j % 2).start()

    slot = j % 2
    for k in range(TOP_K):
        pltpu.make_async_copy(y_hbm.at[pl.ds(0, tk * dc), :], buf.at[slot, k], sem.at[slot]).wait()
    g = gate_ref[...]
    gk = [jnp.broadcast_to(g[:, k:k + 1], (tk, LANES)) for k in range(TOP_K)]

    def result(x1_ref, c):
        acc = gk[0] * buf[slot, 0, pl.ds(c, tk, dc), :]
        for k in range(1, TOP_K):
            acc = acc + gk[k] * buf[slot, k, pl.ds(c, tk, dc), :]
        return x1_ref[:, c * LANES:(c + 1) * LANES] + acc

    @pl.when(j < np_blocks)
    def _():
        for c in range(dc):
            outp_ref[:, c * LANES:(c + 1) * LANES] = result(x1p_ref, c)

    @pl.when(j >= np_blocks)
    def _():
        for c in range(dc):
            outs_ref[:, c * LANES:(c + 1) * LANES] = result(x1s_ref, c)


def _combine(dest3, gates, x1_p, x1_s, y):
    n_prompt, d = x1_p.shape
    n = n_prompt + x1_s.shape[0]
    tk = dest3.shape[2] // TOP_K
    np_blocks = n_prompt // tk
    row = lambda j: (j, 0)
    p_tile = lambda j: (jnp.minimum(j, np_blocks - 1), 0)
    s_tile = lambda j: (jnp.maximum(j - np_blocks, 0), 0)
    return pl.pallas_call(
        functools.partial(_combine_body, np_blocks=np_blocks),
        grid=(n // tk,),
        in_specs=[pl.BlockSpec(memory_space=pl.ANY), pl.BlockSpec((tk, LANES), row),
                  pl.BlockSpec((tk, d), p_tile), pl.BlockSpec((tk, d), s_tile), pl.BlockSpec(memory_space=pl.ANY)],
        out_specs=[pl.BlockSpec((tk, d), p_tile), pl.BlockSpec((tk, d), s_tile)],
        out_shape=[jax.ShapeDtypeStruct((n_prompt, d), F32), jax.ShapeDtypeStruct((n - n_prompt, d), F32)],
        scratch_shapes=[pltpu.SMEM((2, tk * TOP_K), I32), pltpu.VMEM((2, TOP_K, tk * (d // LANES), LANES), F32),
                        pltpu.SemaphoreType.DMA((2,)), pltpu.SemaphoreType.DMA((2,))],
        compiler_params=_params(("arbitrary",)),
        name="moe_combine",
    )(dest3, gates, x1_p, x1_s, y)


def _moe(h2, x1_p, x1_s, topi, topg, topp, counts, w_gu, b_gu, w_d, b_d):
    d = x1_p.shape[1]
    n = x1_p.shape[0] + x1_s.shape[0]
    tm = MOE_TM
    n_asg = n * TOP_K
    n_blocks = n_asg // tm + N_EXPERTS
    padded = (counts + tm - 1) // tm * tm
    pad_end = jnp.cumsum(padded)
    pad_start = pad_end - padded
    dest = pad_start[topi[:, :TOP_K]] + topp[:, :TOP_K]
    tok = jnp.broadcast_to(jnp.arange(n, dtype=I32)[:, None], (n, TOP_K))
    row_tok = jnp.zeros((n_blocks * tm,), I32).at[dest.reshape(-1)].set(tok.reshape(-1))
    n_used = (pad_end[-1:] // tm).astype(I32)
    row0 = pad_start.astype(I32)
    nblk = (padded // tm).astype(I32)
    f2 = w_gu.shape[2]
    pw = 2 * LANES
    src = jnp.arange(pw)
    perm = (jnp.arange(pw)[:, None] == jnp.where(src < LANES, 2 * src, 2 * (src - LANES) + 1)[None, :]).astype(BF16)
    b_gu_p = b_gu.reshape(N_EXPERTS, f2 // pw, LANES, 2).transpose(0, 1, 3, 2).reshape(N_EXPERTS, 1, f2)
    xs = _gather_rows(row_tok, n_used, h2, n_blocks, d)
    h = _moe_up(row0, nblk, n_used, xs, w_gu, b_gu_p, perm)
    y = _moe_down(row0, nblk, n_used, h, w_d, b_d.reshape(N_EXPERTS, 1, d))
    tk = 128
    dest3 = dest.astype(I32).reshape(n // tk, 1, tk * TOP_K)
    return _combine(dest3, topg, x1_p, x1_s, y)


def _layer(layer, xp, xs, cache_cmp_all, cache_sel_all, cache_win_all, state_conv, state_h, page_table, w):
    (norm_mix_w, w_in, q_norm_w, k_norm_w, w_cmp_k, w_cmp_v, conv_w, conv_b, w_gate_a, b_gate_a, w_gate_i, b_gate_i,
     lru_lambda, out_norm_attn, out_norm_rec, w_out, norm_ffn_w, w_router, b_router, w_gate_up, b_gate_up, w_down,
     b_down) = w
    bp, tp, d = xp.shape
    bs, ts, _ = xs.shape
    d_rec = d - D_ATTN
    past = page_table.shape[1] * cache_cmp_all.shape[2]
    assert bs * ts == QB and ts <= TP

    o1, o2 = D_ATTN, D_ATTN + 12 * DH
    o3 = o2 + 3 * N_HEADS
    wq = w_in[:, :o1].astype(BF16)
    wkv = w_in[:, o1:o2].astype(BF16)
    wg = jnp.pad(w_in[:, o2:o3].reshape(d, N_KV, 3 * GQA), ((0, 0), (0, 0), (0, GATE_ROWS - 3 * GQA)))
    wg = jnp.pad(wg.reshape(d, N_KV * GATE_ROWS), ((0, 0), (0, LANES - N_KV * GATE_ROWS))).astype(BF16)
    wxy = w_in[:, o3:].astype(BF16)
    row2 = lambda v: v.reshape(1, -1)
    slopes = jnp.exp2(-8.0 * jnp.arange(1, N_HEADS + 1, dtype=F32) / N_HEADS)
    w4 = jnp.concatenate([w_cmp_k, w_cmp_v], axis=0)
    z4 = jnp.zeros_like(w4)
    wl = jnp.stack([jnp.concatenate([w4, z4], axis=1), jnp.concatenate([z4, w4], axis=1)])
    wl = jnp.broadcast_to(wl[..., None], wl.shape + (DH,))
    sp = row2(jax.nn.softplus(-lru_lambda.astype(F32)))
    wa, wi = w_gate_a.astype(BF16), w_gate_i.astype(BF16)
    woa, wor = w_out[:D_ATTN].astype(BF16), w_out[D_ATTN:].astype(BF16)
    wr = jnp.pad(w_router, ((0, 0), (0, LANES - N_EXPERTS)))
    wr_hi = wr.astype(BF16)
    wr = jnp.concatenate([wr_hi, (wr - wr_hi.astype(F32)).astype(BF16)], axis=1)
    br = row2(jnp.pad(b_router, (0, LANES - N_EXPERTS)))
    mix = (row2(norm_mix_w), wq, wkv, wg, wxy, row2(q_norm_w), k_norm_w)
    lru_w = (conv_w, row2(conv_b), wa, row2(b_gate_a), wi, row2(b_gate_i), sp, row2(out_norm_rec))

    np_tok = bp * tp
    q_p, cmp_p, sel_p, win_p, kvb_p, kaug_p, gate_p, xr_p, yr_p = _inproj(xp.reshape(np_tok, d), *mix,
                                                                          *_key_tails(tp), tm=256)
    kce_p, kco_p = _pool_prompt(cmp_p, wl)
    attn_p = _prompt_attention(slopes, q_p, kvb_p, kaug_p, kce_p, kco_p, gate_p, bp, tp)
    recn_p, hl_p = _lru_seq(xr_p.reshape(bp, tp, d_rec), yr_p.reshape(bp, tp, d_rec),
                            jnp.zeros((bp, CONV_W - 1, d_rec), F32), jnp.zeros((bp, 1, d_rec), F32), *lru_w, tt=256)
    post = (row2(out_norm_attn), woa, wor, row2(norm_ffn_w), wr, br)
    x1_p, h2_p, ti_p, tg_p, tp_p, cnt_p = _outproj_router(
        attn_p, recn_p.reshape(np_tok, d_rec), xp.reshape(np_tok, d), jnp.zeros((1, LANES), F32), *post, tm=256)

    ns_tok = bs * ts
    no_tail = jnp.zeros((QB, LANES), BF16)
    q_s, cmp_s, sel_s, win_s, kvb_s, _, gate_s, xr_s, yr_s = _inproj(xs.reshape(ns_tok, d), *mix, no_tail, no_tail,
                                                                      tm=QB)
    n_pool, page = cache_cmp_all.shape[1], cache_cmp_all.shape[2]
    page_view = lambda c: c.reshape(c.shape[0] * n_pool, page * KV_ROWS, DH)
    page_ids = (page_table + layer * n_pool).reshape(-1)
    kce_s, kco_s = _pool_pages(page_view(cache_cmp_all), page_ids, bs, wl)
    q_t = q_s[0].reshape(N_KV, GQA, bs, ts, DH).transpose(2, 0, 1, 3, 4)
    q_t = jnp.pad(q_t, ((0, 0), (0, 0), (0, 0), (0, TP - ts), (0, 0))).reshape(bs, N_KV, GQA * TP, DH)
    oc_s, selm_s = _sample_select(slopes, q_t, kce_s, kco_s, past)
    new_kvb = jnp.pad(kvb_s.reshape(bs, ts, 12 * DH), ((0, 0), (0, TP - ts), (0, 0)))
    g_t = gate_s.T[:, :N_KV * GATE_ROWS].reshape(bs, ts, N_KV, GATE_ROWS)[..., :3 * GQA]
    g_t = g_t.reshape(bs, ts, N_KV, GQA, 3).transpose(0, 2, 3, 1, 4)
    g_t = jnp.pad(g_t, ((0, 0), (0, 0), (0, 0), (0, TP - ts), (0, 0))).reshape(bs, N_KV, GQA * TP, 3)
    win_view = cache_win_all.reshape(cache_win_all.shape[0] * bs, cache_win_all.shape[2] * KV_ROWS, DH)
    attn_s = _sample_attention(slopes, page_ids, page_view(cache_sel_all), q_t, selm_s, new_kvb, win_view, layer * bs,
                               oc_s, g_t, past, ts)
    attn_s = attn_s.reshape(bs, N_KV, GQA, TP, DH)[:, :, :, :ts].transpose(1, 2, 0, 3, 4).reshape(1, N_HEADS, QB, DH)
    tmaj = lambda a: a.reshape(bs, ts, d_rec).transpose(1, 0, 2)
    recn_s, hl_s = _lru_step(tmaj(xr_s), tmaj(yr_s), state_conv.transpose(1, 0, 2), state_h, *lru_w)
    x1_s, h2_s, ti_s, tg_s, tp_s, cnt = _outproj_router(
        attn_s, recn_s.transpose(1, 0, 2).reshape(ns_tok, d_rec), xs.reshape(ns_tok, d), cnt_p, *post, tm=QB)

    cat = lambda a, b: jnp.concatenate([a, b], axis=0)
    out_p, out_s = _moe(cat(h2_p, h2_s), x1_p, x1_s, cat(ti_p, ti_s), cat(tg_p, tg_s), cat(tp_p, tp_s),
                        cnt[0, :N_EXPERTS].astype(I32), w_gate_up, b_gate_up, w_down, b_down)
    y_p = out_p.reshape(bp, tp, d)
    y_s = out_s.reshape(bs, ts, d)

    kv5 = lambda a, b, t: a.reshape(b, t, 2, N_KV, DH)
    win_len_p = min(WINDOW, tp)
    st_p = (kv5(cmp_p, bp, tp), kv5(sel_p, bp, tp), kv5(win_p, bp, tp)[:, tp - win_len_p:],
            xr_p.reshape(bp, tp, d_rec)[:, tp - (CONV_W - 1):], hl_p.reshape(bp, d_rec))
    cache_win = cache_win_all[layer]
    win_all = jnp.concatenate([cache_win, kv5(win_s, bs, ts)], axis=1)
    xcat = jnp.concatenate([state_conv, xr_s.reshape(bs, ts, d_rec)], axis=1)
    st_s = (kv5(cmp_s, bs, ts), kv5(sel_s, bs, ts), win_all[:, win_all.shape[1] - cache_win.shape[1]:],
            xcat[:, ts:], hl_s)
    return y_p, y_s, st_p, st_s


def kernel(x_prompt, x_sample, cache_cmp_kv, cache_sel_kv, cache_win_kv, state_conv, state_h, page_table, norm_mix_w, w_in, q_norm_w, k_norm_w, w_cmp_k, w_cmp_v, conv_w, conv_b, w_gate_a, b_gate_a, w_gate_i, b_gate_i, lru_lambda, out_norm_attn, out_norm_rec, w_out, norm_ffn_w, w_router, b_router, w_gate_up, b_gate_up, w_down, b_down):
    depth = w_in.shape[0]
    xp, xs = x_prompt, x_sample
    st_ps, st_ss = [], []
    for l in range(depth):
        w = (norm_mix_w[l], w_in[l], q_norm_w[l], k_norm_w[l], w_cmp_k[l], w_cmp_v[l], conv_w[l], conv_b[l],
             w_gate_a[l], b_gate_a[l], w_gate_i[l], b_gate_i[l], lru_lambda[l], out_norm_attn[l], out_norm_rec[l],
             w_out[l], norm_ffn_w[l], w_router[l], b_router[l], w_gate_up[l], b_gate_up[l], w_down[l], b_down[l])
        xp, xs, st_p, st_s = _layer(l, xp, xs, cache_cmp_kv, cache_sel_kv, cache_win_kv, state_conv[l], state_h[l],
                                    page_table, w)
        st_ps.append(st_p)
        st_ss.append(st_s)
    stack = lambda sts, i: jnp.stack([s[i] for s in sts])
    return (xp, xs) + tuple(stack(st_ps, i) for i in range(5)) + tuple(stack(st_ss, i) for i in range(5))
```

```python
import functools

import jax
import jax.numpy as jnp
from jax import lax
from jax.experimental import pallas as pl
from jax.experimental.pallas import tpu as pltpu

F32 = jnp.float32
BF16 = jnp.bfloat16
I32 = jnp.int32

N_HEADS = 8
N_KV = 2
GQA = N_HEADS // N_KV
DH = 128
D_ATTN = N_HEADS * DH
CONV_W = 4
LRU_C = 8.0
L_CMP = 32
L_SEL = 64
N_SEL = 16
WINDOW = 512
QB = 128
N_EXPERTS = 32
TOP_K = 4
SWIGLU_LIMIT = 7.0
SWIGLU_ALPHA = 1.702
EPS = 1e-6
NEG = -1e30
FORCE = 1e9
Q_SCALE = DH ** -0.5
LANES = 128
TP = 8
KV_ROWS = 2 * N_KV
GATE_ROWS = 16
MOE_TM = 256
MOE_TN = 2048
PAGES_PER_STEP = 32
POOL_PAGES_PER_STEP = 64
VMEM_LIMIT = 56 * 1024 * 1024


def _dot(a, b):
    return jnp.dot(a, b, preferred_element_type=F32)


def _dot_nt(a, b):
    return lax.dot_general(a, b, (((1,), (1,)), ((), ())), preferred_element_type=F32)


def _dot_tn(a, b):
    return lax.dot_general(a, b, (((0,), (0,)), ((), ())), preferred_element_type=F32)


def _iota(shape, dim):
    return lax.broadcasted_iota(I32, shape, dim)


def _rms(x, w):
    return x * lax.rsqrt(jnp.mean(x * x, axis=-1, keepdims=True) + EPS) * w


def _masked_softmax_rows(s, mask):
    s = jnp.where(mask, s, NEG)
    e = jnp.where(mask, jnp.exp(s - jnp.max(s, axis=-1, keepdims=True)), 0.0)
    return e / jnp.maximum(jnp.sum(e, axis=-1, keepdims=True), 1e-30)


def _masked_softmax_cols(s, mask):
    s = jnp.where(mask, s, NEG)
    e = jnp.where(mask, jnp.exp(s - jnp.max(s, axis=0, keepdims=True)), 0.0)
    return e / jnp.maximum(jnp.sum(e, axis=0, keepdims=True), 1e-30)


def _params(sem, vmem=None):
    return pltpu.CompilerParams(dimension_semantics=sem, vmem_limit_bytes=vmem or VMEM_LIMIT)


def _inproj_body(x_ref, nw_ref, wq_ref, wkv_ref, wg_ref, wxy_ref, qnw_ref, knw_ref, tsel_ref, twin_ref,
                 q_ref, cmp_ref, sel_ref, win_ref, kvb_ref, kaug_ref, gate_ref, xr_ref, yr_ref):
    x = x_ref[...]
    h = _rms(x, nw_ref[...]).astype(BF16)
    q = _dot(h, wq_ref[...])
    qnw = qnw_ref[...]
    nqb = q_ref.shape[0]
    for hd in range(N_HEADS):
        qn = (_rms(q[:, hd * DH:(hd + 1) * DH], qnw) * Q_SCALE).astype(BF16)
        for b in range(nqb):
            q_ref[b, hd] = qn[b * QB:(b + 1) * QB]
    kv = _dot(h, wkv_ref[...])
    tm = x.shape[0]
    outs = (cmp_ref, sel_ref, win_ref)
    for br in range(3):
        knw = knw_ref[br:br + 1, :]
        for g in range(N_KV):
            c0 = br * 4 * DH + g * DH
            kn = _rms(kv[:, c0:c0 + DH], knw)
            v = kv[:, c0 + 2 * DH:c0 + 3 * DH]
            outs[br][pl.ds(g, tm, KV_ROWS), :] = kn
            outs[br][pl.ds(2 + g, tm, KV_ROWS), :] = v
            kvb_ref[:, c0:c0 + DH] = kn.astype(BF16)
            kvb_ref[:, c0 + 2 * DH:c0 + 3 * DH] = v.astype(BF16)
            if br > 0:
                a0 = ((br - 1) * N_KV + g) * 2 * DH
                kaug_ref[:, a0:a0 + DH] = kn.astype(BF16)
                kaug_ref[:, a0 + DH:a0 + 2 * DH] = (tsel_ref if br == 1 else twin_ref)[...]
    gate_ref[...] = jax.nn.sigmoid(_dot(h, wg_ref[...])).T
    xy = _dot(h, wxy_ref[...])
    d_rec = xr_ref.shape[1]
    xr_ref[...] = xy[:, :d_rec]
    yr_ref[...] = xy[:, d_rec:]


def _inproj(x2, nw, wq, wkv, wg, wxy, qnw, knw, tail_sel, tail_win, tm):
    n, d = x2.shape
    d_rec = wxy.shape[1] // 2
    nqb = tm // QB
    seq_tiles = tail_sel.shape[0] // tm
    row = lambda i: (i, 0)
    const = lambda i: (0, 0)
    wspec = lambda a: pl.BlockSpec(a.shape, const, pipeline_mode=pl.Buffered(1))
    tail_spec = pl.BlockSpec((tm, LANES), lambda i: (i % seq_tiles, 0))
    return pl.pallas_call(
        _inproj_body,
        grid=(n // tm,),
        in_specs=[pl.BlockSpec((tm, d), row), wspec(nw), wspec(wq), wspec(wkv), wspec(wg), wspec(wxy),
                  wspec(qnw), wspec(knw), tail_spec, tail_spec],
        out_specs=[pl.BlockSpec((nqb, N_HEADS, QB, DH), lambda i: (i, 0, 0, 0)),
                   pl.BlockSpec((tm * KV_ROWS, DH), row), pl.BlockSpec((tm * KV_ROWS, DH), row),
                   pl.BlockSpec((tm * KV_ROWS, DH), row),
                   pl.BlockSpec((tm, 12 * DH), row), pl.BlockSpec((tm, 8 * DH), row),
                   pl.BlockSpec((LANES, tm), lambda i: (0, i)),
                   pl.BlockSpec((tm, d_rec), row), pl.BlockSpec((tm, d_rec), row)],
        out_shape=[jax.ShapeDtypeStruct((n // QB, N_HEADS, QB, DH), BF16),
                   jax.ShapeDtypeStruct((n * KV_ROWS, DH), F32), jax.ShapeDtypeStruct((n * KV_ROWS, DH), F32),
                   jax.ShapeDtypeStruct((n * KV_ROWS, DH), F32), jax.ShapeDtypeStruct((n, 12 * DH), BF16),
                   jax.ShapeDtypeStruct((n, 8 * DH), BF16), jax.ShapeDtypeStruct((LANES, n), F32),
                   jax.ShapeDtypeStruct((n, d_rec), F32), jax.ShapeDtypeStruct((n, d_rec), F32)],
        compiler_params=_params(("parallel",)),
        name="inproj",
    )(x2, nw, wq, wkv, wg, wxy, qnw, knw, tail_sel, tail_win)


def _pool_combo(rows, wl_ref, combo):
    r = rows.shape[0] // (2 * L_CMP)
    x3 = rows.reshape(r, 2 * L_CMP, rows.shape[1])
    even = jnp.sum(x3 * wl_ref[0, combo][None], axis=1)
    odd = jnp.sum(x3 * wl_ref[1, combo][None], axis=1)
    return even.astype(BF16), odd.astype(BF16)


def _pool_body(x_ref, wl_ref, e_ref, o_ref):
    n_tok = x_ref.shape[0] // KV_ROWS
    for combo in range(KV_ROWS):
        even, odd = _pool_combo(x_ref[pl.ds(combo, n_tok, KV_ROWS), :], wl_ref, combo)
        e_ref[:, combo * DH:(combo + 1) * DH] = even
        o_ref[:, combo * DH:(combo + 1) * DH] = odd


def _pool_prompt(cmp4, wl):
    n = cmp4.shape[0] // KV_ROWS
    toks = 1024
    ob = toks // (2 * L_CMP)
    c = KV_ROWS * DH
    return pl.pallas_call(
        _pool_body,
        grid=(n // toks,),
        in_specs=[pl.BlockSpec((toks * KV_ROWS, DH), lambda i: (i, 0)), pl.BlockSpec(wl.shape, lambda i: (0, 0, 0, 0))],
        out_specs=[pl.BlockSpec((ob, c), lambda i: (i, 0)), pl.BlockSpec((ob, c), lambda i: (i, 0))],
        out_shape=[jax.ShapeDtypeStruct((n // (2 * L_CMP), c), BF16)] * 2,
        compiler_params=_params(("parallel",)),
        name="pool_prompt",
    )(cmp4, wl)


def _page_rows(pages, combo):
    n_tok = pages[0].shape[1] // KV_ROWS
    return jnp.concatenate([p[0, pl.ds(combo, n_tok, KV_ROWS), :] for p in pages], axis=0)


def _pool_pages_body(pt_ref, *refs):
    pages = refs[:POOL_PAGES_PER_STEP]
    wl_ref, e_ref, o_ref = refs[POOL_PAGES_PER_STEP:]
    for combo in range(KV_ROWS):
        even, odd = _pool_combo(_page_rows(pages, combo), wl_ref, combo)
        e_ref[0, :, combo * DH:(combo + 1) * DH] = even
        o_ref[0, :, combo * DH:(combo + 1) * DH] = odd


def _pool_pages(cache3, page_ids, bsz, wl):
    rows = cache3.shape[1]
    n_pages = page_ids.shape[0] // bsz
    pps = POOL_PAGES_PER_STEP
    n_steps = n_pages // pps
    ob = pps * (rows // KV_ROWS) // (2 * L_CMP)
    c = KV_ROWS * DH
    page_spec = lambda i: pl.BlockSpec((1, rows, DH), lambda b, s, pt: (pt[b * n_pages + s * pps + i], 0, 0))
    grid_spec = pltpu.PrefetchScalarGridSpec(
        num_scalar_prefetch=1,
        grid=(bsz, n_steps),
        in_specs=[page_spec(i) for i in range(pps)] + [pl.BlockSpec(wl.shape, lambda b, s, pt: (0, 0, 0, 0))],
        out_specs=[pl.BlockSpec((1, ob, c), lambda b, s, pt: (b, s, 0))] * 2,
    )
    return pl.pallas_call(
        _pool_pages_body,
        grid_spec=grid_spec,
        out_shape=[jax.ShapeDtypeStruct((bsz, n_steps * ob, c), BF16)] * 2,
        compiler_params=_params(("parallel", "parallel")),
        name="pool_pages",
    )(page_ids, *([cache3] * POOL_PAGES_PER_STEP), wl)


def _select_blocks_cols(score, blk, n_pick, n_blk):
    sel = jnp.zeros(score.shape, F32)
    for _ in range(n_pick):
        m = jnp.max(score, axis=0, keepdims=True)
        idx = jnp.min(jnp.where(score == m, blk, n_blk), axis=0, keepdims=True)
        hit = blk == idx
        sel = jnp.where(hit, 1.0, sel)
        score = jnp.where(hit, -jnp.inf, score)
    return sel


def _pattn_body(slopes_ref, q_ref, ksel_ref, vsel_ref, kwin_ref, vwin_ref, kce_ref, kco_ref, vce_ref, vco_ref,
                gt_ref, o_ref, m_scr, l_scr, acc_scr):
    g = pl.program_id(1)
    i = pl.program_id(2)
    rows = GQA * QB
    q = q_ref[0].reshape(rows, DH)
    col = _iota((1, rows), 1)
    qpos = i * QB + (col & (QB - 1))
    slope = jnp.zeros((1, rows), F32)
    slope_r = jnp.zeros((rows, 1), F32)
    for r in range(GQA):
        slope = jnp.where((col >> 7) == r, slopes_ref[g * GQA + r], slope)
        slope_r = jnp.where((_iota((rows, 1), 0) >> 7) == r, slopes_ref[g * GQA + r], slope_r)

    kc = jnp.concatenate([kce_ref[...], kco_ref[...]], axis=0)
    vc = jnp.concatenate([vce_ref[...], vco_ref[...]], axis=0)
    nb = kc.shape[0]
    half = nb // 2
    brow = _iota((nb, rows), 0)
    blk_c = jnp.where(brow < half, 2 * brow, 2 * (brow - half) + 1)
    dist = qpos - (blk_c * L_CMP + (L_CMP - 1))
    s = _dot_nt(kc, q) - slope * dist.astype(F32)
    p = _masked_softmax_cols(s, dist >= 0)
    o_c = _dot_tn(vc, p.astype(BF16))
    imp_t = p[:, 0:QB]
    for r in range(1, GQA):
        imp_t = imp_t + p[:, r * QB:(r + 1) * QB]
    pair = imp_t[:half] + imp_t[half:]
    ns = half
    blk = _iota((ns, QB), 0)
    qp = i * QB + _iota((ns, QB), 1)
    cur = qp >> 6
    forced = (blk == 0) | (blk == cur) | (blk == cur - 1)
    visible = blk * L_SEL <= qp
    score = jnp.where(visible, jnp.where(forced, FORCE, pair), NEG)
    sel = _select_blocks_cols(score, blk, min(N_SEL, ns), ns)

    unpicked = jnp.where(visible, 1.0 - sel, 1.0)
    unpicked = jnp.concatenate([unpicked, jnp.zeros((LANES - ns, QB), F32)], axis=0)
    tail = jnp.concatenate([unpicked.T] * GQA, axis=0)
    lane_t = _iota((rows, LANES), 1)
    q_first = (i * QB).astype(F32)
    tail = tail + jnp.where(lane_t == ns, slope_r * L_SEL,
                            jnp.where(lane_t == ns + 1, slope_r, jnp.where(lane_t == ns + 2, -slope_r * q_first, 0.0)))
    q_aug = jnp.concatenate([q, tail.astype(BF16)], axis=1)

    m_scr[...] = jnp.full(m_scr.shape, NEG, F32)
    l_scr[...] = jnp.zeros(l_scr.shape, F32)
    acc_scr[...] = jnp.zeros(acc_scr.shape, F32)
    ck = 512

    def attend(k0, causal):
        sc = _dot_nt(ksel_ref[pl.ds(k0, ck), :], q_aug)
        if causal:
            sc = jnp.where(qpos >= k0 + _iota((ck, rows), 0), sc, NEG)
        m_old = m_scr[...]
        m_new = jnp.maximum(m_old, jnp.max(sc, axis=0, keepdims=True))
        alpha = jnp.exp(m_old - m_new)
        e = jnp.exp(sc - m_new)
        l_scr[...] = alpha * l_scr[...] + jnp.sum(e, axis=0, keepdims=True)
        acc_scr[...] = alpha * acc_scr[...] + _dot_tn(vsel_ref[pl.ds(k0, ck), :], e.astype(BF16))
        m_scr[...] = m_new

    def chunk(c, carry):
        attend(pl.multiple_of(c * ck, ck), causal=False)
        return carry

    lax.fori_loop(0, i >> 2, chunk, 0)
    attend(pl.multiple_of((i >> 2) * ck, ck), causal=True)
    o_s = acc_scr[...] / jnp.maximum(l_scr[...], 1e-30)

    span = WINDOW + QB
    start = pl.multiple_of(jnp.maximum(i * QB - WINDOW, 0), QB)
    vw = vwin_ref[pl.ds(start, span), :]
    dist_w = qpos - (start + _iota((span, rows), 0))
    mask_w = jnp.where(dist_w >= 0, dist_w, WINDOW) < WINDOW
    s_w = _dot_nt(kwin_ref[pl.ds(start, span), :], q_aug)
    o_w = _dot_tn(vw, _masked_softmax_cols(s_w, mask_w).astype(BF16))

    gt = gt_ref[pl.ds(pl.multiple_of(g * GATE_ROWS, GATE_ROWS), GATE_ROWS), :]
    for r in range(GQA):
        sl = slice(r * QB, (r + 1) * QB)
        o_t = (gt[3 * r:3 * r + 1, :] * o_c[:, sl] + gt[3 * r + 1:3 * r + 2, :] * o_s[:, sl]
               + gt[3 * r + 2:3 * r + 3, :] * o_w[:, sl])
        o_ref[0, r] = o_t.T


MASK_BIAS = -(2.0 ** 100)


def _key_tails(t_len):
    ns = t_len // L_SEL
    pos = jnp.arange(t_len, dtype=I32)[:, None]
    lane = jnp.arange(LANES, dtype=I32)[None, :]
    feat = jnp.where(lane == ns, pos >> 6, jnp.where(lane == ns + 1, pos & (L_SEL - 1), jnp.where(lane == ns + 2, 1, 0)))
    feat = feat.astype(F32)
    return (jnp.where(lane == (pos >> 6), MASK_BIAS, 0.0) + feat).astype(BF16), feat.astype(BF16)


def _prompt_attention(slopes, q_blk, kvb, kaug, kce, kco, gates_g, bsz, t_len):
    nq = t_len // QB
    nb = t_len // L_CMP
    assert nb % (2 * LANES) == 0 or nb == LANES, "compressed blocks must fill whole lane tiles"
    assert t_len >= WINDOW + QB and t_len // L_SEL + 3 <= LANES
    half = nb // 2
    rows = GQA * QB
    kv_spec = lambda col: pl.BlockSpec((t_len, DH), lambda b, g, i, col=col: (b, col + g))
    ka_spec = lambda col: pl.BlockSpec((t_len, 2 * DH), lambda b, g, i, col=col: (b, col + g))
    kc_spec = lambda col: pl.BlockSpec((half, DH), lambda b, g, i, col=col: (b, col + g))
    return pl.pallas_call(
        _pattn_body,
        grid=(bsz, N_KV, nq),
        in_specs=[pl.BlockSpec(memory_space=pltpu.SMEM),
                  pl.BlockSpec((1, GQA, QB, DH), lambda b, g, i: (b * nq + i, g, 0, 0)),
                  ka_spec(0), kv_spec(6), ka_spec(N_KV), kv_spec(10),
                  kc_spec(0), kc_spec(0), kc_spec(2), kc_spec(2),
                  pl.BlockSpec((LANES, QB), lambda b, g, i: (0, b * nq + i))],
        out_specs=pl.BlockSpec((1, GQA, QB, DH), lambda b, g, i: (b * nq + i, g, 0, 0)),
        out_shape=jax.ShapeDtypeStruct((bsz * nq, N_HEADS, QB, DH), F32),
        scratch_shapes=[pltpu.VMEM((1, rows), F32), pltpu.VMEM((1, rows), F32), pltpu.VMEM((DH, rows), F32)],
        compiler_params=_params(("parallel", "parallel", "arbitrary")),
        name="prompt_attention",
    )(slopes, q_blk, kaug, kvb, kaug, kvb, kce, kco, kce, kco, gates_g)


def _sample_select_body(slopes_ref, q_ref, kce_ref, kco_ref, oc_ref, sel_ref, *, past):
    rows = GQA * TP
    row = _iota((rows, 1), 0)
    qpos = past + (row & (TP - 1))
    r_of_row = row >> 3
    half = kce_ref.shape[1]
    nb = 2 * half
    n_b = q_ref.shape[0]
    lane = _iota((rows, nb), 1)
    blk_c = jnp.where(lane < half, 2 * lane, 2 * (lane - half) + 1)
    dist = qpos - (blk_c * L_CMP + (L_CMP - 1))
    pairs = []
    for b in range(n_b):
        for g in range(N_KV):
            slope = jnp.zeros((rows, 1), F32)
            for r in range(GQA):
                slope = jnp.where(r_of_row == r, slopes_ref[g * GQA + r], slope)
            q = q_ref[b, g]
            kc = jnp.concatenate([kce_ref[b, :, g * DH:(g + 1) * DH], kco_ref[b, :, g * DH:(g + 1) * DH]], axis=0)
            vc = jnp.concatenate([kce_ref[b, :, (2 + g) * DH:(3 + g) * DH], kco_ref[b, :, (2 + g) * DH:(3 + g) * DH]],
                                 axis=0)
            s = _dot_nt(q, kc) - slope * dist.astype(F32)
            p = _masked_softmax_rows(s, dist >= 0)
            oc_ref[b, g] = _dot(p.astype(BF16), vc)
            imp = p[0:TP]
            for r in range(1, GQA):
                imp = imp + p[r * TP:(r + 1) * TP]
            pairs.append(imp[:, :half] + imp[:, half:])
    pair = jnp.concatenate(pairs, axis=0)
    n_rows = pair.shape[0]
    blk = _iota((n_rows, half), 1)
    qp = past + (_iota((n_rows, half), 0) & (TP - 1))
    cur = qp >> 6
    forced = (blk == 0) | (blk == cur) | (blk == cur - 1)
    score = jnp.where(blk * L_SEL <= qp, jnp.where(forced, FORCE, pair), NEG)
    sel = jnp.zeros((n_rows, half), F32)
    for _ in range(N_SEL - 1):
        m = jnp.max(score, axis=1, keepdims=True)
        idx = jnp.min(jnp.where(score == m, blk, half), axis=1, keepdims=True)
        hit = blk == idx
        sel = jnp.where(hit, 1.0, sel)
        score = jnp.where(hit, -jnp.inf, score)
    sel = sel.astype(BF16)
    for b in range(n_b):
        for g in range(N_KV):
            i0 = (b * N_KV + g) * TP
            sel_ref[b, g] = jnp.concatenate([sel[i0:i0 + TP]] * GQA, axis=0)


SELECT_BATCH = 8


def _sample_select(slopes, q_s, kce, kco, past):
    bsz = q_s.shape[0]
    half = kce.shape[1]
    assert half == LANES, "past selection blocks must fill one lane tile"
    rows = GQA * TP
    nbt = SELECT_BATCH
    return pl.pallas_call(
        functools.partial(_sample_select_body, past=past),
        grid=(bsz // nbt,),
        in_specs=[pl.BlockSpec(memory_space=pltpu.SMEM),
                  pl.BlockSpec((nbt, N_KV, rows, DH), lambda b: (b, 0, 0, 0)),
                  pl.BlockSpec((nbt, half, 4 * DH), lambda b: (b, 0, 0)),
                  pl.BlockSpec((nbt, half, 4 * DH), lambda b: (b, 0, 0))],
        out_specs=[pl.BlockSpec((nbt, N_KV, rows, DH), lambda b: (b, 0, 0, 0)),
                   pl.BlockSpec((nbt, N_KV, rows, half), lambda b: (b, 0, 0, 0))],
        out_shape=[jax.ShapeDtypeStruct((bsz, N_KV, rows, DH), F32),
                   jax.ShapeDtypeStruct((bsz, N_KV, rows, half), BF16)],
        compiler_params=_params(("parallel",)),
        name="sample_select",
    )(slopes, q_s, kce, kco)


def _online_update(m_ref, l_ref, acc_ref, g, sc, msk, v):
    m_old = m_ref[g]
    m_new = jnp.maximum(m_old, jnp.max(sc, axis=-1, keepdims=True))
    alpha = jnp.exp(m_old - m_new)
    e = jnp.where(msk, jnp.exp(sc - m_new), 0.0)
    l_ref[g] = alpha * l_ref[g] + jnp.sum(e, axis=-1, keepdims=True)
    acc_ref[g] = alpha * acc_ref[g] + _dot(e.astype(BF16), v)
    m_ref[g] = m_new


def _sample_attn_body(pt_ref, slopes_ref, *refs, past, t_new):
    pages = refs[:PAGES_PER_STEP]
    q_ref, sel_ref, new_ref, win_ref, oc_ref, gt_ref, o_ref, m_scr, l_scr, acc_scr = refs[PAGES_PER_STEP:]
    c = pl.program_id(1)
    rows = GQA * TP
    row = _iota((rows, 1), 0)
    qpos = past + (row & (TP - 1))
    r_of_row = row >> 3
    ck = PAGES_PER_STEP * pages[0].shape[1] // KV_ROWS

    @pl.when(c == 0)
    def _():
        m_scr[...] = jnp.full(m_scr.shape, NEG, F32)
        l_scr[...] = jnp.zeros(l_scr.shape, F32)
        acc_scr[...] = jnp.zeros(acc_scr.shape, F32)

    dist_s = qpos - (c * ck + _iota((rows, ck), 1))
    expand = jnp.where((_iota((LANES, ck), 1) >> 6) + c * (ck // L_SEL) == _iota((LANES, ck), 0), 1.0, 0.0).astype(BF16)
    slopes = []
    for g in range(N_KV):
        slope = jnp.zeros((rows, 1), F32)
        for r in range(GQA):
            slope = jnp.where(r_of_row == r, slopes_ref[g * GQA + r], slope)
        slopes.append(slope)
        q = q_ref[0, g]
        kch = _page_rows(pages, g).astype(BF16)
        vch = _page_rows(pages, 2 + g).astype(BF16)
        picked = _dot(sel_ref[0, g], expand)
        msk = jnp.where(dist_s >= 0, picked, 0.0) > 0.5
        sc = jnp.where(msk, _dot_nt(q, kch) - slope * dist_s.astype(F32), NEG)
        _online_update(m_scr, l_scr, acc_scr, g, sc, msk, vch)

    @pl.when(c == pl.num_programs(1) - 1)
    def _():
        col = _iota((rows, TP), 1)
        dist_n = qpos - (past + col)
        mask_n = jnp.where(col < t_new, dist_n, -1) >= 0
        dist_c = qpos - (past - WINDOW + _iota((rows, WINDOW), 1))
        mask_c = jnp.where(dist_c >= 0, dist_c, WINDOW) < WINDOW
        mask_wn = jnp.where(mask_n, dist_n, WINDOW) < WINDOW
        for g in range(N_KV):
            slope = slopes[g]
            q = q_ref[0, g]
            kn = new_ref[0, :, (4 + g) * DH:(5 + g) * DH]
            vn = new_ref[0, :, (6 + g) * DH:(7 + g) * DH]
            sc = jnp.where(mask_n, _dot_nt(q, kn) - slope * dist_n.astype(F32), NEG)
            _online_update(m_scr, l_scr, acc_scr, g, sc, mask_n, vn)
            o_s = acc_scr[g] / jnp.maximum(l_scr[g], 1e-30)
            kwc = win_ref[0, pl.ds(g, WINDOW, KV_ROWS), :].astype(BF16)
            vwc = win_ref[0, pl.ds(2 + g, WINDOW, KV_ROWS), :].astype(BF16)
            kwn = new_ref[0, :, (8 + g) * DH:(9 + g) * DH]
            vwn = new_ref[0, :, (10 + g) * DH:(11 + g) * DH]
            s1 = jnp.where(mask_c, _dot_nt(q, kwc) - slope * dist_c.astype(F32), NEG)
            s2 = jnp.where(mask_wn, _dot_nt(q, kwn) - slope * dist_n.astype(F32), NEG)
            mx = jnp.maximum(jnp.max(s1, axis=-1, keepdims=True), jnp.max(s2, axis=-1, keepdims=True))
            e1 = jnp.where(mask_c, jnp.exp(s1 - mx), 0.0)
            e2 = jnp.where(mask_wn, jnp.exp(s2 - mx), 0.0)
            den = jnp.maximum(jnp.sum(e1, axis=-1, keepdims=True) + jnp.sum(e2, axis=-1, keepdims=True), 1e-30)
            o_w = _dot((e1 / den).astype(BF16), vwc) + _dot((e2 / den).astype(BF16), vwn)
            gt = gt_ref[0, g]
            o_ref[0, g] = gt[:, 0:1] * oc_ref[0, g] + gt[:, 1:2] * o_s + gt[:, 2:3] * o_w


def _sample_attention(slopes, page_ids, cache_sel3, q_s, sel_s, new_kvb, cache_win3, win_base, o_c, gates_s, past,
                      t_new):
    page_rows = cache_sel3.shape[1]
    bsz = q_s.shape[0]
    n_pages = page_ids.shape[0] // bsz
    n_steps = n_pages // PAGES_PER_STEP
    rows = GQA * TP
    assert cache_win3.shape[1] == WINDOW * KV_ROWS and past % L_SEL == 0 and t_new <= TP
    page_spec = lambda i: pl.BlockSpec((1, page_rows, DH),
                                       lambda b, s, pt: (pt[b * n_pages + s * PAGES_PER_STEP + i], 0, 0))
    per_b = lambda shape: pl.BlockSpec((1,) + shape, lambda b, s, pt: (b,) + (0,) * len(shape))
    grid_spec = pltpu.PrefetchScalarGridSpec(
        num_scalar_prefetch=1,
        grid=(bsz, n_steps),
        in_specs=[pl.BlockSpec(memory_space=pltpu.SMEM)] + [page_spec(i) for i in range(PAGES_PER_STEP)]
        + [per_b((N_KV, rows, DH)), per_b((N_KV, rows, LANES)), per_b((TP, 12 * DH)),
           pl.BlockSpec((1, WINDOW * KV_ROWS, DH), lambda b, s, pt: (win_base + b, 0, 0)),
           per_b((N_KV, rows, DH)), per_b((N_KV, rows, 3))],
        out_specs=per_b((N_KV, rows, DH)),
        scratch_shapes=[pltpu.VMEM((N_KV, rows, 1), F32), pltpu.VMEM((N_KV, rows, 1), F32),
                        pltpu.VMEM((N_KV, rows, DH), F32)],
    )
    return pl.pallas_call(
        functools.partial(_sample_attn_body, past=past, t_new=t_new),
        grid_spec=grid_spec,
        out_shape=jax.ShapeDtypeStruct((bsz, N_KV, rows, DH), F32),
        compiler_params=_params(("parallel", "arbitrary")),
        name="sample_attention",
    )(page_ids, slopes, *([cache_sel3] * PAGES_PER_STEP), q_s, sel_s, new_kvb, cache_win3, o_c, gates_s)


def _gelu_tanh(x):
    return 0.5 * x * (1.0 + jnp.tanh(0.7978845608028654 * (x + 0.044715 * (x * x * x))))


def _lru_coeffs(conv, wa_ref, ba_ref, wi_ref, bi_ref, sp_ref):
    cb = conv.astype(BF16)
    n_blk, blk = wa_ref.shape[0], wa_ref.shape[1]
    ra = jnp.concatenate([_dot(cb[:, n * blk:(n + 1) * blk], wa_ref[n]) for n in range(n_blk)], axis=1)
    ri = jnp.concatenate([_dot(cb[:, n * blk:(n + 1) * blk], wi_ref[n]) for n in range(n_blk)], axis=1)
    r = jax.nn.sigmoid(ra + ba_ref[...])
    gi = jax.nn.sigmoid(ri + bi_ref[...])
    log_a = -LRU_C * r * sp_ref[...]
    a = jnp.exp(log_a)
    b = jnp.sqrt(-jnp.tanh(log_a) * (a * a + 1.0)) * (gi * conv)
    return a, b


def _lru_seq_body(xr_ref, yr_ref, cs_ref, h0_ref, cw_ref, cb_ref, wa_ref, ba_ref, wi_ref, bi_ref, sp_ref, onw_ref,
                  rec_ref, hl_ref, xbuf, h_scr):
    k = pl.program_id(1)
    tt = xr_ref.shape[1]
    pad = 8

    @pl.when(k == 0)
    def _():
        xbuf[0:pad, :] = jnp.zeros((pad, xbuf.shape[1]), F32)
        xbuf[pad - (CONV_W - 1):pad, :] = cs_ref[0]
        h_scr[...] = h0_ref[0]

    x = xr_ref[0]
    xbuf[pad:pad + tt, :] = x
    conv = cb_ref[...] + cw_ref[CONV_W - 1:CONV_W, :] * x
    for j in range(CONV_W - 1):
        conv = conv + cw_ref[j:j + 1, :] * xbuf[pad - (CONV_W - 1) + j:pad - (CONV_W - 1) + j + tt, :]
    xbuf[0:pad, :] = x[tt - pad:tt]
    a, b = _lru_coeffs(conv, wa_ref, ba_ref, wi_ref, bi_ref, sp_ref)
    grp = 8
    row_in_grp = _iota((tt, 1), 0) & (grp - 1)
    s = 1
    while s < grp:
        keep = row_in_grp >= s
        a_sh = jnp.where(keep, pltpu.roll(a, s, 0), 1.0)
        b_sh = jnp.where(keep, pltpu.roll(b, s, 0), 0.0)
        b = a * b_sh + b
        a = a * a_sh
        s *= 2
    h = h_scr[...]
    groups = []
    for j in range(tt // grp):
        hg = a[j * grp:(j + 1) * grp] * h + b[j * grp:(j + 1) * grp]
        groups.append(hg)
        h = hg[grp - 1:grp]
    hs = jnp.concatenate(groups, axis=0)
    h_scr[...] = h
    hl_ref[0] = h
    rec = hs * _gelu_tanh(yr_ref[0])
    rec_ref[0] = _rms(rec, onw_ref[...]).astype(BF16)


def _lru_seq(xr3, yr3, cs, h0, cw, cb, wa, ba, wi, bi, sp, onw, tt):
    bsz, t_len, d = xr3.shape
    seq = pl.BlockSpec((1, tt, d), lambda b, k: (b, k, 0))
    full = lambda a: pl.BlockSpec(a.shape, lambda b, k: (0,) * a.ndim)
    return pl.pallas_call(
        _lru_seq_body,
        grid=(bsz, t_len // tt),
        in_specs=[seq, seq, pl.BlockSpec((1, CONV_W - 1, d), lambda b, k: (b, 0, 0)),
                  pl.BlockSpec((1, 1, d), lambda b, k: (b, 0, 0)),
                  full(cw), full(cb), full(wa), full(ba), full(wi), full(bi), full(sp), full(onw)],
        out_specs=[seq, pl.BlockSpec((1, 1, d), lambda b, k: (b, 0, 0))],
        out_shape=[jax.ShapeDtypeStruct((bsz, t_len, d), BF16), jax.ShapeDtypeStruct((bsz, 1, d), F32)],
        scratch_shapes=[pltpu.VMEM((tt + 8, d), F32), pltpu.VMEM((1, d), F32)],
        compiler_params=_params(("parallel", "arbitrary")),
        name="lru_seq",
    )(xr3, yr3, cs, h0, cw, cb, wa, ba, wi, bi, sp, onw)


def _lru_step_body(xr_ref, yr_ref, cs_ref, h0_ref, cw_ref, cb_ref, wa_ref, ba_ref, wi_ref, bi_ref, sp_ref, onw_ref,
                   rec_ref, hl_ref):
    t_len, bsz = xr_ref.shape[0], xr_ref.shape[1]
    xs = [cs_ref[j] for j in range(CONV_W - 1)] + [xr_ref[t] for t in range(t_len)]
    convs = []
    for t in range(t_len):
        conv = cb_ref[...] + cw_ref[0:1, :] * xs[t]
        for j in range(1, CONV_W):
            conv = conv + cw_ref[j:j + 1, :] * xs[t + j]
        convs.append(conv)
    a, b = _lru_coeffs(jnp.concatenate(convs, axis=0), wa_ref, ba_ref, wi_ref, bi_ref, sp_ref)
    h = h0_ref[...]
    for t in range(t_len):
        h = a[t * bsz:(t + 1) * bsz] * h + b[t * bsz:(t + 1) * bsz]
        rec = h * _gelu_tanh(yr_ref[t])
        rec_ref[t] = _rms(rec, onw_ref[...]).astype(BF16)
    hl_ref[...] = h


def _lru_step(xr_t, yr_t, cs_t, h0, cw, cb, wa, ba, wi, bi, sp, onw):
    t_len, bsz, d = xr_t.shape
    return pl.pallas_call(
        _lru_step_body,
        out_shape=[jax.ShapeDtypeStruct((t_len, bsz, d), BF16), jax.ShapeDtypeStruct((bsz, d), F32)],
        compiler_params=pltpu.CompilerParams(vmem_limit_bytes=VMEM_LIMIT),
        name="lru_step",
    )(xr_t, yr_t, cs_t, h0, cw, cb, wa, ba, wi, bi, sp, onw)


def _outproj_body(attn_ref, rec_ref, x_ref, cnt0_ref, anw_ref, woa_ref, wor_ref, fnw_ref, wr_ref, br_ref,
                  x1_ref, h2_ref, ti_ref, tg_ref, tp_ref, cnt_ref, carry):
    step = pl.program_id(0)
    tm = x_ref.shape[0]

    @pl.when(step == 0)
    def _():
        carry[...] = cnt0_ref[...]

    attn = jnp.concatenate(
        [jnp.concatenate([attn_ref[b, hd] for hd in range(N_HEADS)], axis=1) for b in range(attn_ref.shape[0])], axis=0)
    an = _rms(attn, anw_ref[...]).astype(BF16)
    x1 = x_ref[...] + (_dot(an, woa_ref[...]) + _dot(rec_ref[...], wor_ref[...]))
    x1_ref[...] = x1
    h2 = _rms(x1, fnw_ref[...])
    dc = h2.shape[1] // LANES
    for c in range(dc):
        h2_ref[pl.ds(c, tm, dc), :] = h2[:, c * LANES:(c + 1) * LANES]
    lane = _iota((tm, LANES), 1)
    h_hi = h2.astype(BF16)
    h_lo = (h2 - h_hi.astype(F32)).astype(BF16)
    part_hi = _dot(h_hi, wr_ref[...])
    part_lo = _dot(h_lo, wr_ref[...])
    logits = part_hi[:, :LANES] + part_hi[:, LANES:] + part_lo[:, :LANES] + br_ref[...]
    lg = jnp.where(lane < N_EXPERTS, logits, -jnp.inf)
    vals, idxs = [], []
    for _ in range(TOP_K):
        m = jnp.max(lg, axis=-1, keepdims=True)
        ix = jnp.min(jnp.where(lg == m, lane, LANES), axis=-1, keepdims=True)
        vals.append(m)
        idxs.append(ix)
        lg = jnp.where(lane == ix, -jnp.inf, lg)
    es = [jnp.exp(v - vals[0]) for v in vals]
    den = es[0]
    for e in es[1:]:
        den = den + e
    onehot = jnp.zeros((tm, LANES), F32)
    for ix in idxs:
        onehot = jnp.where(lane == ix, 1.0, onehot)
    lower = jnp.where(_iota((tm, tm), 0) > _iota((tm, tm), 1), 1.0, 0.0).astype(BF16)
    rank = carry[...] + _dot(lower, onehot.astype(BF16))
    carry[...] = carry[...] + jnp.sum(onehot, axis=0, keepdims=True)
    ti = jnp.zeros((tm, LANES), I32)
    tg = jnp.zeros((tm, LANES), F32)
    tp = jnp.zeros((tm, LANES), I32)
    for k in range(TOP_K):
        pos = jnp.sum(jnp.where(lane == idxs[k], rank, 0.0), axis=-1, keepdims=True).astype(I32)
        ti = jnp.where(lane == k, idxs[k], ti)
        tg = jnp.where(lane == k, es[k] / den, tg)
        tp = jnp.where(lane == k, pos, tp)
    ti_ref[...] = ti
    tg_ref[...] = tg
    tp_ref[...] = tp
    cnt_ref[...] = carry[...]


def _outproj_router(attn_blk, recn, x2, cnt0, anw, woa, wor, fnw, wr, br, tm):
    n, d = x2.shape
    nqb = tm // QB
    row = lambda i: (i, 0)
    wspec = lambda a: pl.BlockSpec(a.shape, lambda i: (0,) * a.ndim, pipeline_mode=pl.Buffered(1))
    lanes_out = lambda dt: jax.ShapeDtypeStruct((n, LANES), dt)
    return pl.pallas_call(
        _outproj_body,
        grid=(n // tm,),
        in_specs=[pl.BlockSpec((nqb, N_HEADS, QB, DH), lambda i: (i, 0, 0, 0)),
                  pl.BlockSpec((tm, recn.shape[1]), row), pl.BlockSpec((tm, d), row),
                  wspec(cnt0), wspec(anw), wspec(woa), wspec(wor), wspec(fnw), wspec(wr), wspec(br)],
        out_specs=[pl.BlockSpec((tm, d), row), pl.BlockSpec((tm * (d // LANES), LANES), row),
                   pl.BlockSpec((tm, LANES), row),
                   pl.BlockSpec((tm, LANES), row), pl.BlockSpec((tm, LANES), row), pl.BlockSpec((1, LANES), lambda i: (0, 0))],
        out_shape=[jax.ShapeDtypeStruct((n, d), F32), jax.ShapeDtypeStruct((n * (d // LANES), LANES), F32),
                   lanes_out(I32), lanes_out(F32), lanes_out(I32), jax.ShapeDtypeStruct((1, LANES), F32)],
        scratch_shapes=[pltpu.VMEM((1, LANES), F32)],
        compiler_params=_params(("arbitrary",)),
        name="outproj_router",
    )(attn_blk, recn, x2, cnt0, anw, woa, wor, fnw, wr, br)


ISSUE_UNROLL = 8


def _gather_body(rowtok_ref, nused_ref, h2_hbm, out_ref, buf, sem):
    j = pl.program_id(0)
    tm = out_ref.shape[0]
    dc = buf.shape[1] // tm
    n_used = nused_ref[0]

    def issue_block(blk, slot):
        def body(it, c):
            for u in range(ISSUE_UNROLL):
                r = it * ISSUE_UNROLL + u
                src = pl.multiple_of(rowtok_ref[blk * tm + r] * dc, dc)
                pltpu.make_async_copy(h2_hbm.at[pl.ds(src, dc), :], buf.at[slot, pl.ds(r * dc, dc), :],
                                      sem.at[slot]).start(priority=u % 2)
            return c

        lax.fori_loop(0, tm // ISSUE_UNROLL, body, 0)

    @pl.when((j == 0) & (n_used > 0))
    def _():
        issue_block(0, 0)

    @pl.when(j + 1 < n_used)
    def _():
        issue_block(j + 1, (j + 1) % 2)

    @pl.when(j < n_used)
    def _():
        slot = j % 2
        pltpu.make_async_copy(h2_hbm.at[pl.ds(0, tm * dc), :], buf.at[slot], sem.at[slot]).wait()
        for c in range(dc):
            out_ref[:, c * LANES:(c + 1) * LANES] = buf[slot, pl.ds(c, tm, dc), :].astype(BF16)

    @pl.when(j >= n_used)
    def _():
        out_ref[...] = jnp.zeros(out_ref.shape, BF16)


def _gather_rows(row_tok, n_used, h2, n_blocks, d):
    tm = MOE_TM
    dc = d // LANES
    grid_spec = pltpu.PrefetchScalarGridSpec(
        num_scalar_prefetch=2,
        grid=(n_blocks,),
        in_specs=[pl.BlockSpec(memory_space=pl.ANY)],
        out_specs=pl.BlockSpec((tm, d), lambda j, rt, nu: (j, 0)),
        scratch_shapes=[pltpu.VMEM((2, tm * dc, LANES), F32), pltpu.SemaphoreType.DMA((2,))],
    )
    return pl.pallas_call(
        _gather_body,
        grid_spec=grid_spec,
        out_shape=jax.ShapeDtypeStruct((n_blocks * tm, d), BF16),
        compiler_params=_params(("arbitrary",)),
        name="moe_gather",
    )(row_tok, n_used, h2)


COPY_SPLIT = 4


def _expert_blocks(first_row, n_blk, n_used, n_blocks, x_hbm, out_rows, xbuf, obuf, xsem, osem, prepare, compute):
    tm = xbuf.shape[1]
    rc = tm // COPY_SPLIT
    orc = obuf.shape[1] // COPY_SPLIT

    def x_copies(b, slot):
        r = pl.multiple_of(first_row + b * tm, tm)
        return [pltpu.make_async_copy(x_hbm.at[pl.ds(r + s * rc, rc), :], xbuf.at[slot, pl.ds(s * rc, rc), :],
                                      xsem.at[slot]) for s in range(COPY_SPLIT)]

    def o_copies(r, slot):
        r = pl.multiple_of(r, tm)
        return [pltpu.make_async_copy(obuf.at[slot, pl.ds(s * orc, orc), :], out_rows(r + s * rc, rc), osem.at[slot])
                for s in range(COPY_SPLIT)]

    def start(copies):
        for cp in copies:
            cp.start(priority=1)

    def wait(copies):
        for cp in copies:
            cp.wait()

    @pl.when(n_blk > 0)
    def _():
        start(x_copies(0, 0))
        prepare()

        def body(b, c):
            slot = b % 2

            @pl.when(b + 1 < n_blk)
            def _():
                start(x_copies(b + 1, 1 - slot))

            wait(x_copies(b, slot))

            @pl.when(b >= 2)
            def _():
                wait(o_copies(first_row + (b - 2) * tm, slot))

            compute(xbuf[slot], slot)
            start(o_copies(first_row + b * tm, slot))
            return c

        lax.fori_loop(0, n_blk, body, 0)

        @pl.when(n_blk >= 2)
        def _():
            wait(o_copies(first_row + (n_blk - 2) * tm, n_blk % 2))

        wait(o_copies(first_row + (n_blk - 1) * tm, (n_blk - 1) % 2))

    @pl.when(pl.program_id(0) == pl.num_programs(0) - 1)
    def _():
        obuf[0] = jnp.zeros(obuf.shape[1:], obuf.dtype)

        def fill(t, c):
            start(o_copies(t * tm, 0))
            wait(o_copies(t * tm, 0))
            return c

        lax.fori_loop(n_used, n_blocks, fill, 0)


W_SPLIT = 4


def _moe_up_body(row0_ref, nblk_ref, nu_ref, xs_hbm, *refs, n_blocks):
    w_refs = refs[:W_SPLIT]
    b_ref, perm_ref, h_hbm, wbf, xbuf, obuf, xsem, osem = refs[W_SPLIT:]
    e = pl.program_id(0)
    tn = wbf.shape[1]
    pw = perm_ref.shape[0]
    hw = pw // 2
    kc = w_refs[0].shape[1]

    def prepare():
        for q, w_ref in enumerate(w_refs):
            for c in range(tn // pw):
                w = w_ref[0, :, c * pw:(c + 1) * pw].astype(BF16)
                wbf[q * kc:(q + 1) * kc, c * pw:(c + 1) * pw] = _dot(w, perm_ref[...]).astype(BF16)

    def compute(x, slot):
        gu = _dot(x, wbf[...]) + b_ref[0]
        for c in range(tn // pw):
            glu = jnp.minimum(gu[:, c * pw:c * pw + hw], SWIGLU_LIMIT)
            lin = jnp.clip(gu[:, c * pw + hw:(c + 1) * pw], -SWIGLU_LIMIT, SWIGLU_LIMIT)
            obuf[slot, :, c * hw:(c + 1) * hw] = (glu * jax.nn.sigmoid(SWIGLU_ALPHA * glu) * (lin + 1.0)).astype(BF16)

    col0 = pl.multiple_of(pl.program_id(1) * (tn // 2), tn // 2)
    out_rows = lambda r, n: h_hbm.at[pl.ds(r, n), pl.ds(col0, tn // 2)]
    _expert_blocks(row0_ref[e], nblk_ref[e], nu_ref[0], n_blocks, xs_hbm, out_rows, xbuf, obuf, xsem, osem,
                   prepare, compute)


def _w_specs(k, tn):
    return [pl.BlockSpec((1, k // W_SPLIT, tn), lambda e, n, *_, q=q: (e, q, n)) for q in range(W_SPLIT)]


def _moe_up(row0, nblk, n_used, xs, w_gu, b_gu, perm):
    n_rows, d = xs.shape
    n_e, _, f2 = w_gu.shape
    tm, tn = MOE_TM, MOE_TN
    grid_spec = pltpu.PrefetchScalarGridSpec(
        num_scalar_prefetch=3,
        grid=(n_e, f2 // tn),
        in_specs=[pl.BlockSpec(memory_space=pl.ANY)] + _w_specs(d, tn)
        + [pl.BlockSpec((1, 1, tn), lambda e, n, *_: (e, 0, n)),
           pl.BlockSpec(perm.shape, lambda e, n, *_: (0, 0))],
        out_specs=pl.BlockSpec(memory_space=pl.ANY),
        scratch_shapes=[pltpu.VMEM((d, tn), BF16), pltpu.VMEM((2, tm, d), BF16), pltpu.VMEM((2, tm, tn // 2), BF16),
                        pltpu.SemaphoreType.DMA((2,)), pltpu.SemaphoreType.DMA((2,))],
    )
    return pl.pallas_call(
        functools.partial(_moe_up_body, n_blocks=n_rows // tm),
        grid_spec=grid_spec,
        out_shape=jax.ShapeDtypeStruct((n_rows, f2 // 2), BF16),
        compiler_params=_params(("arbitrary", "arbitrary")),
        name="moe_up",
    )(row0, nblk, n_used, xs, *([w_gu] * W_SPLIT), b_gu, perm)


def _moe_down_body(row0_ref, nblk_ref, nu_ref, h_hbm, *refs, n_blocks):
    w_refs = refs[:W_SPLIT]
    b_ref, y_hbm, wbf, xbuf, obuf, xsem, osem = refs[W_SPLIT:]
    e = pl.program_id(0)
    tn = wbf.shape[1]
    kc = w_refs[0].shape[1]

    def prepare():
        for q, w_ref in enumerate(w_refs):
            wbf[q * kc:(q + 1) * kc, :] = w_ref[0].astype(BF16)

    dc = tn // LANES
    tm = xbuf.shape[1]

    def compute(h, slot):
        y = _dot(h, wbf[...]) + b_ref[0]
        for c in range(dc):
            obuf[slot, pl.ds(c, tm, dc), :] = y[:, c * LANES:(c + 1) * LANES]

    out_rows = lambda r, n: y_hbm.at[pl.ds(pl.multiple_of(r * dc, dc), n * dc), :]
    _expert_blocks(row0_ref[e], nblk_ref[e], nu_ref[0], n_blocks, h_hbm, out_rows, xbuf, obuf, xsem, osem,
                   prepare, compute)


def _moe_down(row0, nblk, n_used, h, w_d, b_d):
    n_rows, f = h.shape
    n_e, _, d = w_d.shape
    tm, tn = MOE_TM, MOE_TN
    assert tn == d, "a grid step writes whole output rows"
    grid_spec = pltpu.PrefetchScalarGridSpec(
        num_scalar_prefetch=3,
        grid=(n_e, d // tn),
        in_specs=[pl.BlockSpec(memory_space=pl.ANY)] + _w_specs(f, tn)
        + [pl.BlockSpec((1, 1, tn), lambda e, n, *_: (e, 0, n))],
        out_specs=pl.BlockSpec(memory_space=pl.ANY),
        scratch_shapes=[pltpu.VMEM((f, tn), BF16), pltpu.VMEM((2, tm, f), BF16),
                        pltpu.VMEM((2, tm * (d // LANES), LANES), F32),
                        pltpu.SemaphoreType.DMA((2,)), pltpu.SemaphoreType.DMA((2,))],
    )
    return pl.pallas_call(
        functools.partial(_moe_down_body, n_blocks=n_rows // tm),
        grid_spec=grid_spec,
        out_shape=jax.ShapeDtypeStruct((n_rows * (d // LANES), LANES), F32),
        compiler_params=_params(("arbitrary", "arbitrary")),
        name="moe_down",
    )(row0, nblk, n_used, h, *([w_d] * W_SPLIT), b_d)


def _combine_body(dest_hbm, gate_ref, x1p_ref, x1s_ref, y_hbm, outp_ref, outs_ref, dsm, buf, sem_d, sem, *, np_blocks):
    j = pl.program_id(0)
    n = pl.num_programs(0)
    tk = x1p_ref.shape[0]
    dc = x1p_ref.shape[1] // LANES
    toks = ISSUE_UNROLL // TOP_K

    def idx_copy(step, slot):
        return pltpu.make_async_copy(dest_hbm.at[step, 0], dsm.at[slot], sem_d.at[slot])

    def issue_rows(slot):
        def body(it, c):
            for u in range(toks):
                t = it * toks + u
                for k in range(TOP_K):
                    src = pl.multiple_of(dsm[slot, t * TOP_K + k] * dc, dc)
                    pltpu.make_async_copy(y_hbm.at[pl.ds(src, dc), :], buf.at[slot, k, pl.ds(t * dc, dc), :],
                                          sem.at[slot]).start(priority=k % 2)
            return c

        lax.fori_loop(0, tk // toks, body, 0)

    @pl.when(j == 0)
    def _():
        idx_copy(0, 0).start()
        idx_copy(0, 0).wait()
        issue_rows(0)

        @pl.when(1 < n)
        def _():
            idx_copy(1, 1).start()

    @pl.when(j + 1 < n)
    def _():
        slot = (j + 1) % 2
        idx_copy(j + 1, slot).wait()
        issue_rows(slot)

    @pl.when(j + 2 < n)
    def _():
        idx_copy(j + 2, j % 2).start()

    slot = j % 2
    for k in range(TOP_K):
        pltpu.make_async_copy(y_hbm.at[pl.ds(0, tk * dc), :], buf.at[slot, k], sem.at[slot]).wait()
    g = gate_ref[...]
    gk = [jnp.broadcast_to(g[:, k:k + 1], (tk, LANES)) for k in range(TOP_K)]

    def result(x1_ref, c):
        acc = gk[0] * buf[slot, 0, pl.ds(c, tk, dc), :]
        for k in range(1, TOP_K):
            acc = acc + gk[k] * buf[slot, k, pl.ds(c, tk, dc), :]
        return x1_ref[:, c * LANES:(c + 1) * LANES] + acc

    @pl.when(j < np_blocks)
    def _():
        for c in range(dc):
            outp_ref[:, c * LANES:(c + 1) * LANES] = result(x1p_ref, c)

    @pl.when(j >= np_blocks)
    def _():
        for c in range(dc):
            outs_ref[:, c * LANES:(c + 1) * LANES] = result(x1s_ref, c)


def _combine(dest3, gates, x1_p, x1_s, y):
    n_prompt, d = x1_p.shape
    n = n_prompt + x1_s.shape[0]
    tk = dest3.shape[2] // TOP_K
    np_blocks = n_prompt // tk
    row = lambda j: (j, 0)
    p_tile = lambda j: (jnp.minimum(j, np_blocks - 1), 0)
    s_tile = lambda j: (jnp.maximum(j - np_blocks, 0), 0)
    return pl.pallas_call(
        functools.partial(_combine_body, np_blocks=np_blocks),
        grid=(n // tk,),
        in_specs=[pl.BlockSpec(memory_space=pl.ANY), pl.BlockSpec((tk, LANES), row),
                  pl.BlockSpec((tk, d), p_tile), pl.BlockSpec((tk, d), s_tile), pl.BlockSpec(memory_space=pl.ANY)],
        out_specs=[pl.BlockSpec((tk, d), p_tile), pl.BlockSpec((tk, d), s_tile)],
        out_shape=[jax.ShapeDtypeStruct((n_prompt, d), F32), jax.ShapeDtypeStruct((n - n_prompt, d), F32)],
        scratch_shapes=[pltpu.SMEM((2, tk * TOP_K), I32), pltpu.VMEM((2, TOP_K, tk * (d // LANES), LANES), F32),
                        pltpu.SemaphoreType.DMA((2,)), pltpu.SemaphoreType.DMA((2,))],
        compiler_params=_params(("arbitrary",)),
        name="moe_combine",
    )(dest3, gates, x1_p, x1_s, y)


def _moe(h2, x1_p, x1_s, topi, topg, topp, counts, w_gu, b_gu, w_d, b_d):
    d = x1_p.shape[1]
    n = x1_p.shape[0] + x1_s.shape[0]
    tm = MOE_TM
    n_asg = n * TOP_K
    n_blocks = n_asg // tm + N_EXPERTS
    padded = (counts + tm - 1) // tm * tm
    pad_end = jnp.cumsum(padded)
    pad_start = pad_end - padded
    dest = pad_start[topi[:, :TOP_K]] + topp[:, :TOP_K]
    tok = jnp.broadcast_to(jnp.arange(n, dtype=I32)[:, None], (n, TOP_K))
    row_tok = jnp.zeros((n_blocks * tm,), I32).at[dest.reshape(-1)].set(tok.reshape(-1))
    n_used = (pad_end[-1:] // tm).astype(I32)
    row0 = pad_start.astype(I32)
    nblk = (padded // tm).astype(I32)
    f2 = w_gu.shape[2]
    pw = 2 * LANES
    src = jnp.arange(pw)
    perm = (jnp.arange(pw)[:, None] == jnp.where(src < LANES, 2 * src, 2 * (src - LANES) + 1)[None, :]).astype(BF16)
    b_gu_p = b_gu.reshape(N_EXPERTS, f2 // pw, LANES, 2).transpose(0, 1, 3, 2).reshape(N_EXPERTS, 1, f2)
    xs = _gather_rows(row_tok, n_used, h2, n_blocks, d)
    h = _moe_up(row0, nblk, n_used, xs, w_gu, b_gu_p, perm)
    y = _moe_down(row0, nblk, n_used, h, w_d, b_d.reshape(N_EXPERTS, 1, d))
    tk = 128
    dest3 = dest.astype(I32).reshape(n // tk, 1, tk * TOP_K)
    return _combine(dest3, topg, x1_p, x1_s, y)


def _layer(layer, xp, xs, cache_cmp_all, cache_sel_all, cache_win_all, state_conv, state_h, page_table, w):
    (norm_mix_w, w_in, q_norm_w, k_norm_w, w_cmp_k, w_cmp_v, conv_w, conv_b, w_gate_a, b_gate_a, w_gate_i, b_gate_i,
     lru_lambda, out_norm_attn, out_norm_rec, w_out, norm_ffn_w, w_router, b_router, w_gate_up, b_gate_up, w_down,
     b_down) = w
    bp, tp, d = xp.shape
    bs, ts, _ = xs.shape
    d_rec = d - D_ATTN
    past = page_table.shape[1] * cache_cmp_all.shape[2]
    assert bs * ts == QB and ts <= TP

    o1, o2 = D_ATTN, D_ATTN + 12 * DH
    o3 = o2 + 3 * N_HEADS
    wq = w_in[:, :o1].astype(BF16)
    wkv = w_in[:, o1:o2].astype(BF16)
    wg = jnp.pad(w_in[:, o2:o3].reshape(d, N_KV, 3 * GQA), ((0, 0), (0, 0), (0, GATE_ROWS - 3 * GQA)))
    wg = jnp.pad(wg.reshape(d, N_KV * GATE_ROWS), ((0, 0), (0, LANES - N_KV * GATE_ROWS))).astype(BF16)
    wxy = w_in[:, o3:].astype(BF16)
    row2 = lambda v: v.reshape(1, -1)
    slopes = jnp.exp2(-8.0 * jnp.arange(1, N_HEADS + 1, dtype=F32) / N_HEADS)
    w4 = jnp.concatenate([w_cmp_k, w_cmp_v], axis=0)
    z4 = jnp.zeros_like(w4)
    wl = jnp.stack([jnp.concatenate([w4, z4], axis=1), jnp.concatenate([z4, w4], axis=1)])
    wl = jnp.broadcast_to(wl[..., None], wl.shape + (DH,))
    sp = row2(jax.nn.softplus(-lru_lambda.astype(F32)))
    wa, wi = w_gate_a.astype(BF16), w_gate_i.astype(BF16)
    woa, wor = w_out[:D_ATTN].astype(BF16), w_out[D_ATTN:].astype(BF16)
    wr = jnp.pad(w_router, ((0, 0), (0, LANES - N_EXPERTS)))
    wr_hi = wr.astype(BF16)
    wr = jnp.concatenate([wr_hi, (wr - wr_hi.astype(F32)).astype(BF16)], axis=1)
    br = row2(jnp.pad(b_router, (0, LANES - N_EXPERTS)))
    mix = (row2(norm_mix_w), wq, wkv, wg, wxy, row2(q_norm_w), k_norm_w)
    lru_w = (conv_w, row2(conv_b), wa, row2(b_gate_a), wi, row2(b_gate_i), sp, row2(out_norm_rec))

    np_tok = bp * tp
    q_p, cmp_p, sel_p, win_p, kvb_p, kaug_p, gate_p, xr_p, yr_p = _inproj(xp.reshape(np_tok, d), *mix,
                                                                          *_key_tails(tp), tm=256)
    kce_p, kco_p = _pool_prompt(cmp_p, wl)
    attn_p = _prompt_attention(slopes, q_p, kvb_p, kaug_p, kce_p, kco_p, gate_p, bp, tp)
    recn_p, hl_p = _lru_seq(xr_p.reshape(bp, tp, d_rec), yr_p.reshape(bp, tp, d_rec),
                            jnp.zeros((bp, CONV_W - 1, d_rec), F32), jnp.zeros((bp, 1, d_rec), F32), *lru_w, tt=256)
    post = (row2(out_norm_attn), woa, wor, row2(norm_ffn_w), wr, br)
    x1_p, h2_p, ti_p, tg_p, tp_p, cnt_p = _outproj_router(
        attn_p, recn_p.reshape(np_tok, d_rec), xp.reshape(np_tok, d), jnp.zeros((1, LANES), F32), *post, tm=256)

    ns_tok = bs * ts
    no_tail = jnp.zeros((QB, LANES), BF16)
    q_s, cmp_s, sel_s, win_s, kvb_s, _, gate_s, xr_s, yr_s = _inproj(xs.reshape(ns_tok, d), *mix, no_tail, no_tail,
                                                                      tm=QB)
    n_pool, page = cache_cmp_all.shape[1], cache_cmp_all.shape[2]
    page_view = lambda c: c.reshape(c.shape[0] * n_pool, page * KV_ROWS, DH)
    page_ids = (page_table + layer * n_pool).reshape(-1)
    kce_s, kco_s = _pool_pages(page_view(cache_cmp_all), page_ids, bs, wl)
    q_t = q_s[0].reshape(N_KV, GQA, bs, ts, DH).transpose(2, 0, 1, 3, 4)
    q_t = jnp.pad(q_t, ((0, 0), (0, 0), (0, 0), (0, TP - ts), (0, 0))).reshape(bs, N_KV, GQA * TP, DH)
    oc_s, selm_s = _sample_select(slopes, q_t, kce_s, kco_s, past)
    new_kvb = jnp.pad(kvb_s.reshape(bs, ts, 12 * DH), ((0, 0), (0, TP - ts), (0, 0)))
    g_t = gate_s.T[:, :N_KV * GATE_ROWS].reshape(bs, ts, N_KV, GATE_ROWS)[..., :3 * GQA]
    g_t = g_t.reshape(bs, ts, N_KV, GQA, 3).transpose(0, 2, 3, 1, 4)
    g_t = jnp.pad(g_t, ((0, 0), (0, 0), (0, 0), (0, TP - ts), (0, 0))).reshape(bs, N_KV, GQA * TP, 3)
    win_view = cache_win_all.reshape(cache_win_all.shape[0] * bs, cache_win_all.shape[2] * KV_ROWS, DH)
    attn_s = _sample_attention(slopes, page_ids, page_view(cache_sel_all), q_t, selm_s, new_kvb, win_view, layer * bs,
                               oc_s, g_t, past, ts)
    attn_s = attn_s.reshape(bs, N_KV, GQA, TP, DH)[:, :, :, :ts].transpose(1, 2, 0, 3, 4).reshape(1, N_HEADS, QB, DH)
    tmaj = lambda a: a.reshape(bs, ts, d_rec).transpose(1, 0, 2)
    recn_s, hl_s = _lru_step(tmaj(xr_s), tmaj(yr_s), state_conv.transpose(1, 0, 2), state_h, *lru_w)
    x1_s, h2_s, ti_s, tg_s, tp_s, cnt = _outproj_router(
        attn_s, recn_s.transpose(1, 0, 2).reshape(ns_tok, d_rec), xs.reshape(ns_tok, d), cnt_p, *post, tm=QB)

    cat = lambda a, b: jnp.concatenate([a, b], axis=0)
    out_p, out_s = _moe(cat(h2_p, h2_s), x1_p, x1_s, cat(ti_p, ti_s), cat(tg_p, tg_s), cat(tp_p, tp_s),
                        cnt[0, :N_EXPERTS].astype(I32), w_gate_up, b_gate_up, w_down, b_down)
    y_p = out_p.reshape(bp, tp, d)
    y_s = out_s.reshape(bs, ts, d)

    kv5 = lambda a, b, t: a.reshape(b, t, 2, N_KV, DH)
    win_len_p = min(WINDOW, tp)
    st_p = (kv5(cmp_p, bp, tp), kv5(sel_p, bp, tp), kv5(win_p, bp, tp)[:, tp - win_len_p:],
            xr_p.reshape(bp, tp, d_rec)[:, tp - (CONV_W - 1):], hl_p.reshape(bp, d_rec))
    cache_win = cache_win_all[layer]
    win_all = jnp.concatenate([cache_win, kv5(win_s, bs, ts)], axis=1)
    xcat = jnp.concatenate([state_conv, xr_s.reshape(bs, ts, d_rec)], axis=1)
    st_s = (kv5(cmp_s, bs, ts), kv5(sel_s, bs, ts), win_all[:, win_all.shape[1] - cache_win.shape[1]:],
            xcat[:, ts:], hl_s)
    return y_p, y_s, st_p, st_s


def kernel(x_prompt, x_sample, cache_cmp_kv, cache_sel_kv, cache_win_kv, state_conv, state_h, page_table, norm_mix_w, w_in, q_norm_w, k_norm_w, w_cmp_k, w_cmp_v, conv_w, conv_b, w_gate_a, b_gate_a, w_gate_i, b_gate_i, lru_lambda, out_norm_attn, out_norm_rec, w_out, norm_ffn_w, w_router, b_router, w_gate_up, b_gate_up, w_down, b_down):
    depth = w_in.shape[0]
    xp, xs = x_prompt, x_sample
    st_ps, st_ss = [], []
    for l in range(depth):
        w = (norm_mix_w[l], w_in[l], q_norm_w[l], k_norm_w[l], w_cmp_k[l], w_cmp_v[l], conv_w[l], conv_b[l],
             w_gate_a[l], b_gate_a[l], w_gate_i[l], b_gate_i[l], lru_lambda[l], out_norm_attn[l], out_norm_rec[l],
             w_out[l], norm_ffn_w[l], w_router[l], b_router[l], w_gate_up[l], b_gate_up[l], w_down[l], b_down[l])
        xp, xs, st_p, st_s = _layer(l, xp, xs, cache_cmp_kv, cache_sel_kv, cache_win_kv, state_conv[l], state_h[l],
                                    page_table, w)
        st_ps.append(st_p)
        st_ss.append(st_s)
    stack = lambda sts, i: jnp.stack([s[i] for s in sts])
    return (xp, xs) + tuple(stack(st_ps, i) for i in range(5)) + tuple(stack(st_ss, i) for i in range(5))
```
